```python
import jax, jax.numpy as jnp
from jax import lax
import numpy as np

D_MODEL = 1024
BATCH = 4
SEQ = 8192
DEPTH = 1

HEAD_DIM = 64
FOX_HEADS = 8
SWA_HEADS = 8
SWA_KV_HEADS = 2
SWA_GROUP = SWA_HEADS // SWA_KV_HEADS
MIX_WIDTH = (FOX_HEADS + SWA_HEADS) * HEAD_DIM
Q_BLOCK = 128
WINDOW = 128
FOX_Q_W = FOX_HEADS * HEAD_DIM
FOX_K_W = FOX_HEADS * HEAD_DIM
FOX_V_W = FOX_HEADS * HEAD_DIM
FOX_F_W = FOX_HEADS
SWA_Q_W = SWA_HEADS * HEAD_DIM
SWA_K_W = SWA_KV_HEADS * HEAD_DIM
SWA_V_W = SWA_KV_HEADS * HEAD_DIM
IN_WIDTH = FOX_Q_W + FOX_K_W + FOX_V_W + FOX_F_W + SWA_Q_W + SWA_K_W + SWA_V_W
PEER_HEADS = 8
PEER_QUERY_DIM = 256
PEER_HALF = PEER_QUERY_DIM // 2
N_KEYS = 128
N_EXPERTS = N_KEYS * N_KEYS
PEER_TOPK = 16
PEER_CHUNK = 128
N_MOD = 6
RMS_EPS = 1e-6
NEG_INF = -1e30

kernel_name = "hymba_fox_swa_peer_block"


def rmsnorm(x, g):
    xf = x.astype(jnp.float32)
    y = xf * lax.rsqrt(jnp.mean(xf * xf, axis=-1, keepdims=True) + RMS_EPS)
    return y.astype(x.dtype) * g


def alibi_slopes(n):
    return jnp.exp2(-8.0 * jnp.arange(1, n + 1, dtype=jnp.float32) / n)


def fox_attention(q, k, v, f_logit):
    B, S, H, Dh = q.shape
    nb = S // Q_BLOCK
    F = jnp.cumsum(jax.nn.log_sigmoid(f_logit.astype(jnp.float32)), axis=1)
    Fk = F.transpose(0, 2, 1)
    kpos = jnp.arange(S)
    scale = Dh ** -0.5

    def block(i):
        start = i * Q_BLOCK
        qb = lax.dynamic_slice_in_dim(q, start, Q_BLOCK, axis=1)
        Fq = lax.dynamic_slice_in_dim(Fk, start, Q_BLOCK, axis=2)
        s = jnp.einsum('bqhd,bkhd->bhqk', qb, k).astype(jnp.float32) * scale
        s = s + Fq[..., :, None] - Fk[..., None, :]
        qpos = start + jnp.arange(Q_BLOCK)
        s = jnp.where(kpos[None, :] <= qpos[:, None], s, NEG_INF)
        p = jax.nn.softmax(s, axis=-1).astype(v.dtype)
        return jnp.einsum('bhqk,bkhd->bqhd', p, v)

    out = lax.map(block, jnp.arange(nb))
    return out.transpose(1, 0, 2, 3, 4).reshape(B, S, H * Dh)


def swa_attention(q, k, v, sinks):
    B, S = q.shape[0], q.shape[1]
    nb = S // WINDOW
    scale = HEAD_DIM ** -0.5
    qb = q.reshape(B, nb, WINDOW, SWA_KV_HEADS, SWA_GROUP, HEAD_DIM)

    def band(t):
        tp = jnp.pad(t, ((0, 0), (WINDOW, 0), (0, 0), (0, 0)))
        prev = tp[:, :S].reshape(B, nb, WINDOW, SWA_KV_HEADS, HEAD_DIM)
        cur = t.reshape(B, nb, WINDOW, SWA_KV_HEADS, HEAD_DIM)
        return jnp.concatenate([prev, cur], axis=2)

    kb, vb = band(k), band(v)
    s = jnp.einsum('bnqkgd,bnjkd->bnkgqj', qb, kb).astype(jnp.float32) * scale
    r = jnp.arange(WINDOW)[:, None]
    j = jnp.arange(2 * WINDOW)[None, :]
    dist = r + WINDOW - j
    valid = (dist >= 0) & (dist < WINDOW)
    not_pad = (jnp.arange(nb)[:, None] * WINDOW - WINDOW + jnp.arange(2 * WINDOW)[None, :]) >= 0
    mask = valid[None] & not_pad[:, None, :]
    slopes = alibi_slopes(SWA_HEADS).reshape(SWA_KV_HEADS, SWA_GROUP)
    s = s - slopes[:, :, None, None] * dist.astype(jnp.float32)
    s = jnp.where(mask[None, :, None, None], s, NEG_INF)
    sink = jnp.broadcast_to(sinks.astype(jnp.float32).reshape(SWA_KV_HEADS, SWA_GROUP, 1, 1),
                            s.shape[:-1] + (1,))
    p = jax.nn.softmax(jnp.concatenate([s, sink], axis=-1), axis=-1)[..., :-1]
    out = jnp.einsum('bnkgqj,bnjkd->bnqkgd', p.astype(v.dtype), vb)
    return out.reshape(B, S, SWA_HEADS * HEAD_DIM)


def peer_ffn(h, w_query, sub_keys, w_u, w_v):
    B, S, D = h.shape
    T = B * S
    hf = h.reshape(T, D)
    q = (hf @ w_query).reshape(T, PEER_HEADS, 2, PEER_HALF)
    sc = jnp.einsum('thpd,hpkd->thpk', q, sub_keys).astype(jnp.float32)
    sv, si = lax.top_k(sc, PEER_TOPK)
    cand = sv[:, :, 0, :, None] + sv[:, :, 1, None, :]
    cv, ci = lax.top_k(cand.reshape(T, PEER_HEADS, PEER_TOPK * PEER_TOPK), PEER_TOPK)
    i1 = jnp.take_along_axis(si[:, :, 0, :], ci // PEER_TOPK, axis=-1)
    i2 = jnp.take_along_axis(si[:, :, 1, :], ci % PEER_TOPK, axis=-1)
    experts = i1 * N_KEYS + i2
    gates = jax.nn.softmax(cv, axis=-1)
    nc = T // PEER_CHUNK
    experts = experts.reshape(nc, PEER_CHUNK, PEER_HEADS * PEER_TOPK)
    gates = gates.reshape(nc, PEER_CHUNK, PEER_HEADS * PEER_TOPK).astype(h.dtype)

    def chunk(args):
        hc, ec, gc = args
        u = jnp.take(w_u, ec, axis=0)
        a = jax.nn.gelu(jnp.einsum('cd,ced->ce', hc, u)) * gc
        vv = jnp.take(w_v, ec, axis=0)
        return jnp.einsum('ce,ced->cd', a, vv)

    out = lax.map(chunk, (hf.reshape(nc, PEER_CHUNK, D), experts, gates))
    return out.reshape(B, S, D)


def setup_inputs(seed: int = 0) -> dict:
    key = jax.random.key(seed)
    ks = jax.random.split(key, 16)
    D = D_MODEL
    nrm = jax.random.normal
    f32 = jnp.float32
    return {
        "x": nrm(ks[0], (BATCH, SEQ, D), f32),
        "c": nrm(ks[1], (BATCH, D), f32),
        "w_ada": nrm(ks[2], (DEPTH, D, N_MOD * D), f32) * (0.5 * D ** -0.5),
        "b_ada": nrm(ks[3], (DEPTH, N_MOD * D), f32) * 0.1,
        "g_pre_mix": 1.0 + 0.05 * nrm(ks[4], (DEPTH, D), f32),
        "g_post_mix": 1.0 + 0.05 * nrm(ks[5], (DEPTH, D), f32),
        "g_pre_ffn": 1.0 + 0.05 * nrm(ks[6], (DEPTH, D), f32),
        "g_post_ffn": 1.0 + 0.05 * nrm(ks[7], (DEPTH, D), f32),
        "w_in": nrm(ks[8], (DEPTH, D, IN_WIDTH), f32) * D ** -0.5,
        "b_fgate": 3.0 + 0.5 * nrm(ks[9], (DEPTH, FOX_HEADS), f32),
        "swa_sinks": 0.5 * nrm(ks[10], (DEPTH, SWA_HEADS), f32),
        "w_out": nrm(ks[11], (DEPTH, MIX_WIDTH, D), f32) * MIX_WIDTH ** -0.5,
        "w_query": nrm(ks[12], (DEPTH, D, PEER_HEADS * PEER_QUERY_DIM), f32) * D ** -0.5,
        "sub_keys": nrm(ks[13], (DEPTH, PEER_HEADS, 2, N_KEYS, PEER_HALF), f32) * PEER_HALF ** -0.5,
        "w_u": nrm(ks[14], (DEPTH, N_EXPERTS, D), f32) * D ** -0.5,
        "w_v": nrm(ks[15], (DEPTH, N_EXPERTS, D), f32) * D ** -0.5,
    }


def reference(x, c, w_ada, b_ada, g_pre_mix, g_post_mix, g_pre_ffn, g_post_ffn,
              w_in, b_fgate, swa_sinks, w_out, w_query, sub_keys, w_u, w_v):
    B, S, _ = x.shape
    s1 = FOX_Q_W
    s2 = s1 + FOX_K_W
    s3 = s2 + FOX_V_W
    s4 = s3 + FOX_F_W
    s5 = s4 + SWA_Q_W
    s6 = s5 + SWA_K_W
    for l in range(DEPTH):
        mod = jax.nn.silu(c) @ w_ada[l] + b_ada[l]
        sh1, sc1, gt1, sh2, sc2, gt2 = [m[:, None, :] for m in jnp.split(mod, N_MOD, axis=-1)]

        h = rmsnorm(x, g_pre_mix[l]) * (1 + sc1) + sh1
        proj = h @ w_in[l]
        fq, fk, fv, ff, sq, sk, sv = jnp.split(proj, [s1, s2, s3, s4, s5, s6], axis=-1)
        fox = fox_attention(fq.reshape(B, S, FOX_HEADS, HEAD_DIM),
                            fk.reshape(B, S, FOX_HEADS, HEAD_DIM),
                            fv.reshape(B, S, FOX_HEADS, HEAD_DIM),
                            ff + b_fgate[l])
        swa = swa_attention(sq.reshape(B, S, SWA_HEADS, HEAD_DIM),
                            sk.reshape(B, S, SWA_KV_HEADS, HEAD_DIM),
                            sv.reshape(B, S, SWA_KV_HEADS, HEAD_DIM),
                            swa_sinks[l])
        y = jnp.concatenate([fox, swa], axis=-1) @ w_out[l]
        x = x + gt1 * rmsnorm(y, g_post_mix[l])

        h = rmsnorm(x, g_pre_ffn[l]) * (1 + sc2) + sh2
        y = peer_ffn(h, w_query[l], sub_keys[l], w_u[l], w_v[l])
        x = x + gt2 * rmsnorm(y, g_post_ffn[l])
    return x
```

```python
import functools

import numpy as np
import jax
import jax.numpy as jnp
from jax import lax
from jax.experimental import pallas as pl
from jax.experimental.pallas import tpu as pltpu

F32 = jnp.float32
BF16 = jnp.bfloat16

D_MODEL = 1024
HEAD_DIM = 64
HEAD_PAD = 128
FOX_HEADS = 8
SWA_HEADS = 8
SWA_KV_HEADS = 2
SWA_GROUP = SWA_HEADS // SWA_KV_HEADS
WINDOW = 128
PEER_HEADS = 8
PEER_HALF = 128
N_KEYS = 128
N_EXPERTS = N_KEYS * N_KEYS
PEER_TOPK = 16
PEER_PICKS = PEER_HEADS * PEER_TOPK
N_MOD = 6
RMS_EPS = 1e-6
NEG_INF = -1e30

_F_LANE = HEAD_DIM
_ONE_LANE = HEAD_DIM

ROW_TILE = 512
FOX_TILE = 512
SWA_TILE = 512
ROUTE_TILE = 256
PEER_TILE = 16
TABLE_ROWS = 16
TABLE_COLS = PEER_PICKS * TABLE_ROWS

_VMEM_LIMIT = 56 * 1024 * 1024


def _dot(a, b):
    return jnp.dot(a, b, preferred_element_type=F32)


def _dot_nt(a, b):
    return lax.dot_general(a, b, (((1,), (1,)), ((), ())), preferred_element_type=F32)


def _split3(x):
    hi = x.astype(BF16)
    r = x - hi.astype(F32)
    mid = r.astype(BF16)
    lo = (r - mid.astype(F32)).astype(BF16)
    return hi, mid, lo


def _rms(x):
    return x * lax.rsqrt(jnp.mean(x * x, axis=-1, keepdims=True) + RMS_EPS)


def _params(*sem):
    return pltpu.CompilerParams(dimension_semantics=sem, vmem_limit_bytes=_VMEM_LIMIT)


def _ada_kernel(c_ref, w_ref, b_ref, o_ref):
    c = c_ref[...]
    s = (c * jax.nn.sigmoid(c)).astype(BF16)
    o_ref[...] = _dot(s, w_ref[...].astype(BF16)) + b_ref[...]


def _ada_call(c8, w_ada, b_ada):
    n = w_ada.shape[1]
    tn = 1536
    return pl.pallas_call(
        _ada_kernel,
        grid=(n // tn,),
        in_specs=[pl.BlockSpec((8, D_MODEL), lambda j: (0, 0)),
                  pl.BlockSpec((D_MODEL, tn), lambda j: (0, j)),
                  pl.BlockSpec((1, tn), lambda j: (0, j))],
        out_specs=pl.BlockSpec((8, tn), lambda j: (0, j)),
        out_shape=jax.ShapeDtypeStruct((8, n), F32),
        compiler_params=_params("arbitrary"),
        name="ada",
    )(c8, w_ada, b_ada.reshape(1, n))


_NQ = FOX_HEADS * HEAD_PAD
_NS = SWA_HEADS * HEAD_PAD
_NKV = SWA_KV_HEADS * HEAD_PAD
_IN_COLS = 3 * _NQ + HEAD_PAD + _NS + 2 * _NKV


def _inproj_kernel(x_ref, mod_ref, g_ref, w_ref, bf_ref, tri_ref, pq_ref, pk_ref, cst_ref,
                   qp_ref, kp_ref, vp_ref, sq_ref, sk_ref, sv_ref, carry_ref):
    i = pl.program_id(1)

    @pl.when(i == 0)
    def _():
        carry_ref[...] = jnp.zeros_like(carry_ref)

    x = x_ref[0]
    sh1 = mod_ref[0, 0:1, :]
    sc1 = mod_ref[0, 1:2, :]
    h = _rms(x) * g_ref[...] * (1.0 + sc1) + sh1
    proj = _dot(h.astype(BF16), w_ref[...])

    z = proj[:, 3 * _NQ:3 * _NQ + HEAD_PAD] + bf_ref[...]
    ls = jnp.minimum(z, 0.0) - jnp.log(1.0 + jnp.exp(-jnp.abs(z)))
    tri = tri_ref[...]
    hi, mid, lo = _split3(ls)
    fcum = _dot(tri, hi) + _dot(tri, mid) + _dot(tri, lo) + carry_ref[...]
    carry_ref[...] = fcum[fcum.shape[0] - 1:, :]

    fh, fm, fl = _split3(fcum)
    eq = _dot(fh, pq_ref[0]) + _dot(fm, pq_ref[1]) + _dot(fl, pq_ref[2]) + cst_ref[0:1, :]
    ek = _dot(fh, pk_ref[0]) + _dot(fm, pk_ref[1]) + _dot(fl, pk_ref[2]) + cst_ref[1:2, :]
    qp_ref[0] = (proj[:, 0:_NQ] + eq).astype(BF16)
    kp_ref[0] = (proj[:, _NQ:2 * _NQ] + ek).astype(BF16)
    vp_ref[0] = (proj[:, 2 * _NQ:3 * _NQ] + cst_ref[2:3, :]).astype(BF16)
    o = 3 * _NQ + HEAD_PAD
    sq_ref[0] = proj[:, o:o + _NS].astype(BF16)
    sk_ref[0] = proj[:, o + _NS:o + _NS + _NKV].astype(BF16)
    sv_ref[0] = proj[:, o + _NS + _NKV:o + _NS + 2 * _NKV].astype(BF16)


def _inproj_consts(tr):
    tri = np.tril(np.ones((tr, tr), np.float32))
    pq = np.zeros((3, HEAD_PAD, _NQ), np.float32)
    pk = np.zeros((3, HEAD_PAD, _NQ), np.float32)
    cst = np.zeros((8, _NQ), np.float32)
    for h in range(FOX_HEADS):
        b = h * HEAD_PAD + _F_LANE
        for j in range(3):
            pq[j, h, b + j] = 1.0
            pk[j, h, b + 3 + j] = -1.0
            cst[0, b + 3 + j] = 1.0
            cst[1, b + j] = 1.0
        cst[2, h * HEAD_PAD + _ONE_LANE] = 1.0
    return (jnp.asarray(tri, BF16), jnp.asarray(pq, BF16), jnp.asarray(pk, BF16), jnp.asarray(cst, F32))


def _inproj_call(x, mod, g_pre, w_all, bf_pad):
    B, S, _ = x.shape
    tr = min(ROW_TILE, S)
    tri, pq, pk, cst = _inproj_consts(tr)
    row = lambda w: pl.BlockSpec((1, tr, w), lambda b, i: (b, i, 0))
    full = lambda a: pl.BlockSpec(a.shape, lambda b, i: (0,) * a.ndim)
    outs = [jax.ShapeDtypeStruct((B, S, w), BF16) for w in (_NQ, _NQ, _NQ, _NS, _NKV, _NKV)]
    return pl.pallas_call(
        _inproj_kernel,
        grid=(B, S // tr),
        in_specs=[row(D_MODEL),
                  pl.BlockSpec((1, N_MOD, D_MODEL), lambda b, i: (b, 0, 0)),
                  full(g_pre), full(w_all), full(bf_pad), full(tri), full(pq), full(pk), full(cst)],
        out_specs=[row(_NQ), row(_NQ), row(_NQ), row(_NS), row(_NKV), row(_NKV)],
        out_shape=outs,
        scratch_shapes=[pltpu.VMEM((1, HEAD_PAD), F32)],
        compiler_params=_params("arbitrary", "arbitrary"),
        name="inproj",
    )(x, mod, g_pre, w_all, bf_pad, tri, pq, pk, cst)


def _fox_kernel(q_ref, k_ref, v_ref, o_ref, m_ref, acc_ref):
    i = pl.program_id(2)
    t = q_ref.shape[1]
    q = q_ref[0]
    m_ref[...] = jnp.full_like(m_ref, NEG_INF)
    acc_ref[...] = jnp.zeros_like(acc_ref)

    def block(j, masked):
        off = pl.multiple_of(j * t, t)
        k = k_ref[0, pl.ds(off, t), :]
        v = v_ref[0, pl.ds(off, t), :]
        s = _dot_nt(q, k)
        if masked:
            r = lax.broadcasted_iota(jnp.int32, s.shape, 0)
            c = lax.broadcasted_iota(jnp.int32, s.shape, 1)
            s = jnp.where(c <= r, s, NEG_INF)
        m_prev = m_ref[...]
        m_new = jnp.maximum(m_prev, jnp.max(s, axis=1, keepdims=True))
        p = jnp.exp(s - m_new)
        acc_ref[...] = jnp.exp(m_prev - m_new) * acc_ref[...] + _dot(p.astype(BF16), v)
        m_ref[...] = m_new

    def body(j, carry):
        block(j, False)
        return carry

    lax.fori_loop(0, i, body, 0)
    block(i, True)
    acc = acc_ref[...]
    o_ref[0] = (acc / acc[:, _ONE_LANE:_ONE_LANE + 1]).astype(BF16)


def _fox_call(qp, kp, vp):
    B, S, _ = qp.shape
    t = min(FOX_TILE, S)
    return pl.pallas_call(
        _fox_kernel,
        grid=(B, FOX_HEADS, S // t),
        in_specs=[pl.BlockSpec((1, t, HEAD_PAD), lambda b, h, i: (b, i, h)),
                  pl.BlockSpec((1, S, HEAD_PAD), lambda b, h, i: (b, 0, h)),
                  pl.BlockSpec((1, S, HEAD_PAD), lambda b, h, i: (b, 0, h))],
        out_specs=pl.BlockSpec((1, t, HEAD_PAD), lambda b, h, i: (b, i, h)),
        out_shape=jax.ShapeDtypeStruct((B, S, _NQ), BF16),
        scratch_shapes=[pltpu.VMEM((t, 1), F32), pltpu.VMEM((t, HEAD_PAD), F32)],
        compiler_params=_params("arbitrary", "arbitrary", "arbitrary"),
        name="fox",
    )(qp, kp, vp)


def _swa_kernel(sink_ref, q_ref, kc_ref, kp_ref, vc_ref, vp_ref, o_ref):
    i = pl.program_id(1)
    nsub = q_ref.shape[1] // WINDOW
    r = lax.broadcasted_iota(jnp.int32, (WINDOW, 2 * WINDOW), 0)
    j = lax.broadcasted_iota(jnp.int32, (WINDOW, 2 * WINDOW), 1)
    dist = r + WINDOW - j
    valid = (dist >= 0) & (dist < WINDOW)
    distf = dist.astype(F32)
    for qb in range(nsub):
        rows = slice(qb * WINDOW, (qb + 1) * WINDOW)
        if qb == 0:
            ok = valid & ((j >= WINDOW) | (i > 0))
        else:
            ok = valid
        for g in range(SWA_KV_HEADS):
            lanes = slice(g * HEAD_PAD, (g + 1) * HEAD_PAD)
            if qb == 0:
                kprev, vprev = kp_ref[0, :, lanes], vp_ref[0, :, lanes]
            else:
                prev = slice((qb - 1) * WINDOW, qb * WINDOW)
                kprev, vprev = kc_ref[0, prev, lanes], vc_ref[0, prev, lanes]
            kk = jnp.concatenate([kprev, kc_ref[0, rows, lanes]], axis=0)
            vv = jnp.concatenate([vprev, vc_ref[0, rows, lanes]], axis=0)
            for u in range(SWA_GROUP):
                hq = g * SWA_GROUP + u
                slope = 2.0 ** (-8.0 * (hq + 1) / SWA_HEADS)
                q = q_ref[0, rows, hq * HEAD_PAD:(hq + 1) * HEAD_PAD]
                s = _dot_nt(q, kk) - slope * distf
                s = jnp.where(ok, s, NEG_INF)
                sink = sink_ref[hq]
                m = jnp.maximum(jnp.max(s, axis=1, keepdims=True), sink)
                p = jnp.exp(s - m)
                den = jnp.sum(p, axis=1, keepdims=True) + jnp.exp(sink - m)
                o = _dot(p.astype(BF16), vv) / den
                o_ref[0, rows, hq * HEAD_PAD:(hq + 1) * HEAD_PAD] = o.astype(BF16)


def _swa_call(sinks, sq, sk, sv):
    B, S, _ = sq.shape
    t = min(SWA_TILE, S)
    per = t // WINDOW
    cur = lambda w: pl.BlockSpec((1, t, w), lambda b, i: (b, i, 0))
    prv = lambda w: pl.BlockSpec((1, WINDOW, w), lambda b, i: (b, jnp.maximum(i * per - 1, 0), 0))
    return pl.pallas_call(
        _swa_kernel,
        grid=(B, S // t),
        in_specs=[pl.BlockSpec(memory_space=pltpu.SMEM),
                  cur(_NS), cur(_NKV), prv(_NKV), cur(_NKV), prv(_NKV)],
        out_specs=cur(_NS),
        out_shape=jax.ShapeDtypeStruct((B, S, _NS), BF16),
        compiler_params=_params("arbitrary", "arbitrary"),
        name="swa",
    )(sinks, sq, sk, sk, sv, sv)


def _topk_rows(s, k, val_ref, idx_ref):
    n = s.shape[0]
    iota = lax.broadcasted_iota(jnp.int32, s.shape, 0)
    for r in range(k):
        m = jnp.max(s, axis=0, keepdims=True)
        i = jnp.min(jnp.where(s == m, iota, n), axis=0, keepdims=True)
        val_ref[pl.ds(r, 1), :] = m
        idx_ref[pl.ds(r, 1), :] = i
        s = jnp.where(iota == i, -jnp.inf, s)


def _route_kernel(fo_ref, so_ref, x_ref, mod_ref, gpost_ref, gpre_ref, wof_ref, wos_ref, wq_ref, keys_ref,
                  x1_ref, h2_ref, idx_ref, gate_ref, qs_ref, sv_ref, si_ref, et_ref, gt_ref):
    x = x_ref[0]
    gt1 = mod_ref[0, 2:3, :]
    sh2 = mod_ref[0, 3:4, :]
    sc2 = mod_ref[0, 4:5, :]
    y = _dot(fo_ref[0], wof_ref[...]) + _dot(so_ref[0], wos_ref[...])
    x1 = x + gt1 * (_rms(y) * gpost_ref[...])
    x1_ref[0] = x1
    h2 = _rms(x1) * gpre_ref[...] * (1.0 + sc2) + sh2
    h2_ref[0] = h2
    qp = _dot(h2.astype(BF16), wq_ref[...])
    nhp = 2 * PEER_HEADS
    for hp in range(nhp):
        qs_ref[hp] = qp[:, hp * PEER_HALF:(hp + 1) * PEER_HALF].astype(BF16)

    def half(hp, carry):
        sc = _dot_nt(keys_ref[hp], qs_ref[hp])
        _topk_rows(sc, PEER_TOPK, sv_ref.at[hp], si_ref.at[hp])
        return carry

    lax.fori_loop(0, nhp, half, 0)

    def head(h, carry):
        v0, v1 = sv_ref[2 * h], sv_ref[2 * h + 1]
        i0, i1 = si_ref[2 * h], si_ref[2 * h + 1]
        cand = jnp.concatenate([v0[a:a + 1, :] + v1 for a in range(PEER_TOPK)], axis=0)
        _topk_rows(cand, PEER_TOPK, gt_ref.at[h], et_ref.at[h])
        cv, ci = gt_ref[h], et_ref[h]
        ca, cb = ci >> 4, ci & (PEER_TOPK - 1)
        e1 = jnp.zeros_like(ci)
        e2 = jnp.zeros_like(ci)
        for a in range(PEER_TOPK):
            e1 = jnp.where(ca == a, i0[a:a + 1, :], e1)
            e2 = jnp.where(cb == a, i1[a:a + 1, :], e2)
        et_ref[h] = e1 * N_KEYS + e2
        ex = jnp.exp(cv - cv[0:1, :])
        gt_ref[h] = ex / jnp.sum(ex, axis=0, keepdims=True)
        return carry

    lax.fori_loop(0, PEER_HEADS, head, 0)
    tt = et_ref.shape[2]
    et = lax.bitcast_convert_type(et_ref[...].reshape(PEER_PICKS, tt), F32)
    idx_ref[...] = lax.bitcast_convert_type(et.T, jnp.int32)
    gate_ref[...] = gt_ref[...].reshape(PEER_PICKS, tt).T


def _route_call(fo, so, x, mod, g_post, g_pre, wof, wos, wq, keys):
    B, S, _ = x.shape
    tt = min(ROUTE_TILE, S)
    per = S // tt
    row = lambda w: pl.BlockSpec((1, tt, w), lambda b, i: (b, i, 0))
    full = lambda a: pl.BlockSpec(a.shape, lambda b, i: (0,) * a.ndim)
    tok = pl.BlockSpec((tt, PEER_PICKS), lambda b, i: (b * per + i, 0))
    return pl.pallas_call(
        _route_kernel,
        grid=(B, per),
        in_specs=[row(_NQ), row(_NS), row(D_MODEL),
                  pl.BlockSpec((1, N_MOD, D_MODEL), lambda b, i: (b, 0, 0)),
                  full(g_post), full(g_pre), full(wof), full(wos), full(wq), full(keys)],
        out_specs=[row(D_MODEL), row(D_MODEL), tok, tok],
        out_shape=[jax.ShapeDtypeStruct((B, S, D_MODEL), F32),
                   jax.ShapeDtypeStruct((B, S, D_MODEL), F32),
                   jax.ShapeDtypeStruct((B * S, PEER_PICKS), jnp.int32),
                   jax.ShapeDtypeStruct((B * S, PEER_PICKS), F32)],
        scratch_shapes=[pltpu.VMEM((2 * PEER_HEADS, tt, PEER_HALF), BF16),
                        pltpu.VMEM((2 * PEER_HEADS, PEER_TOPK, tt), F32),
                        pltpu.VMEM((2 * PEER_HEADS, PEER_TOPK, tt), jnp.int32),
                        pltpu.VMEM((PEER_HEADS, PEER_TOPK, tt), jnp.int32),
                        pltpu.VMEM((PEER_HEADS, PEER_TOPK, tt), F32)],
        compiler_params=_params("arbitrary", "arbitrary"),
        name="route",
    )(fo, so, x, mod, g_post, g_pre, wof, wos, wq, keys)


def _peer_consts():
    etu = np.zeros((TABLE_COLS, PEER_PICKS), np.float32)
    ev = np.zeros((PEER_PICKS, TABLE_COLS), np.float32)
    half = TABLE_ROWS // 2
    for k in range(PEER_PICKS):
        for s in range(half):
            etu[k * TABLE_ROWS + s, k] = 1.0
            ev[k, k * TABLE_ROWS + half + s] = 1.0
    return jnp.asarray(etu, BF16), jnp.asarray(ev, BF16)


def _peer_kernel(idx_hbm, tab_hbm, hs_ref, g_ref, etu_ref, ev_ref, y_ref,
                 buf, idx_sm, zs_ref, wr_ref, sem_g, sem_i):
    s = pl.program_id(0)
    n = pl.num_programs(0)
    tt = hs_ref.shape[0]
    npt = tt * PEER_PICKS
    slot = lax.rem(s, 2)
    nslot = 1 - slot
    last = n - 1

    def idx_copy(tile, sl):
        return pltpu.make_async_copy(idx_hbm.at[pl.ds(tile * npt, npt)], idx_sm.at[sl], sem_i.at[sl])

    def gather_all(sl):
        return pltpu.make_async_copy(tab_hbm.at[pl.ds(0, npt)], buf.at[sl], sem_g.at[sl])

    def issue_token(t, sl):
        base = t * PEER_PICKS
        for k in range(PEER_PICKS):
            e = idx_sm[sl, base + k]
            pltpu.make_async_copy(tab_hbm.at[e], buf.at[sl, base + k], sem_g.at[sl]).start()

    @pl.when(s == 0)
    def _():
        first = idx_copy(0, 0)
        first.start()
        first.wait()

        def body(t, carry):
            issue_token(t, 0)
            return carry

        lax.fori_loop(0, tt, body, 0)
        idx_copy(jnp.minimum(1, last), 1).start()

    idx_copy(jnp.minimum(s + 1, last), nslot).wait()
    idx_copy(jnp.minimum(s + 2, last), slot).start()
    gather_all(slot).wait()

    half = TABLE_ROWS // 2
    row8 = lax.broadcasted_iota(jnp.int32, (half, TABLE_COLS), 0)
    col8 = lax.broadcasted_iota(jnp.int32, (half, TABLE_COLS), 1) % TABLE_ROWS
    mask_u = col8 == row8
    mask_v = col8 == row8 + half

    def ustage(t, carry):
        issue_token(t, nslot)
        wb = buf[slot, pl.ds(t * PEER_PICKS, PEER_PICKS)].reshape(TABLE_COLS, HEAD_PAD)
        y = _dot_nt(hs_ref[t].astype(BF16), wb)
        zs_ref[pl.ds(t, 1), :] = jnp.sum(jnp.where(mask_u, y, 0.0), axis=0, keepdims=True)
        return carry

    lax.fori_loop(0, tt, ustage, 0)

    etu = etu_ref[...]
    zh, zm, zl = _split3(zs_ref[...])
    a = _dot(zh, etu) + _dot(zm, etu) + _dot(zl, etu)
    w = jax.nn.gelu(a) * g_ref[...]
    wr_ref[...] = _dot(w.astype(BF16), ev_ref[...])

    def vstage(t, carry):
        wrow = jnp.broadcast_to(wr_ref[pl.ds(t, 1), :], (half, TABLE_COLS))
        wexp = jnp.where(mask_v, wrow, 0.0).astype(BF16)
        wb = buf[slot, pl.ds(t * PEER_PICKS, PEER_PICKS)].reshape(TABLE_COLS, HEAD_PAD)
        y_ref[t] = _dot(wexp, wb)
        return carry

    lax.fori_loop(0, tt, vstage, 0)

    @pl.when(s == last)
    def _():
        idx_copy(last, slot).wait()
        gather_all(nslot).wait()


def _peer_call(idx_flat, table, hs3, gates):
    T = hs3.shape[0]
    tt = PEER_TILE
    npt = tt * PEER_PICKS
    etu, ev = _peer_consts()
    full = lambda a: pl.BlockSpec(a.shape, lambda s: (0,) * a.ndim)
    return pl.pallas_call(
        _peer_kernel,
        grid=(T // tt,),
        in_specs=[pl.BlockSpec(memory_space=pl.ANY),
                  pl.BlockSpec(memory_space=pl.ANY),
                  pl.BlockSpec((tt, 8, HEAD_PAD), lambda s: (s, 0, 0)),
                  pl.BlockSpec((tt, PEER_PICKS), lambda s: (s, 0)),
                  full(etu), full(ev)],
        out_specs=pl.BlockSpec((tt, 8, HEAD_PAD), lambda s: (s, 0, 0)),
        out_shape=jax.ShapeDtypeStruct((T, 8, HEAD_PAD), F32),
        scratch_shapes=[pltpu.VMEM((2, npt, TABLE_ROWS, HEAD_PAD), BF16),
                        pltpu.SMEM((2, npt), jnp.int32),
                        pltpu.VMEM((tt, TABLE_COLS), F32),
                        pltpu.VMEM((tt, TABLE_COLS), F32),
                        pltpu.SemaphoreType.DMA((2,)),
                        pltpu.SemaphoreType.DMA((2,))],
        compiler_params=_params("arbitrary"),
        name="peer",
    )(idx_flat, table, hs3, gates, etu, ev)


def _final_kernel(x1_ref, y_ref, mod_ref, g_ref, o_ref):
    gt2 = mod_ref[0, 5:6, :]
    o_ref[0] = x1_ref[0] + gt2 * (_rms(y_ref[0]) * g_ref[...])


def _final_call(x1, y, mod, g_post):
    B, S, _ = x1.shape
    tr = min(ROW_TILE, S)
    row = pl.BlockSpec((1, tr, D_MODEL), lambda b, i: (b, i, 0))
    return pl.pallas_call(
        _final_kernel,
        grid=(B, S // tr),
        in_specs=[row, row, pl.BlockSpec((1, N_MOD, D_MODEL), lambda b, i: (b, 0, 0)),
                  pl.BlockSpec((1, D_MODEL), lambda b, i: (0, 0))],
        out_specs=row,
        out_shape=jax.ShapeDtypeStruct((B, S, D_MODEL), F32),
        compiler_params=_params("arbitrary", "arbitrary"),
        name="final",
    )(x1, y, mod, g_post)


def _pad_heads_cols(w, nh, scale=1.0):
    k = w.shape[0]
    w = (w * scale).reshape(k, nh, HEAD_DIM)
    return jnp.pad(w, ((0, 0), (0, 0), (0, HEAD_PAD - HEAD_DIM))).reshape(k, nh * HEAD_PAD)


def _pad_heads_rows(w, nh):
    n = w.shape[1]
    w = w.reshape(nh, HEAD_DIM, n)
    return jnp.pad(w, ((0, 0), (0, HEAD_PAD - HEAD_DIM), (0, 0))).reshape(nh * HEAD_PAD, n)


def _layer(x, c8, w_ada, b_ada, g_pre_mix, g_post_mix, g_pre_ffn, g_post_ffn,
           w_in, b_fgate, swa_sinks, w_out, w_query, sub_keys, w_u, w_v):
    B, S, D = x.shape
    T = B * S
    scale = HEAD_DIM ** -0.5
    mod = _ada_call(c8, w_ada, b_ada)[:B].reshape(B, N_MOD, D)

    o = 0
    parts = []
    for nh, sc in ((FOX_HEADS, scale), (FOX_HEADS, 1.0), (FOX_HEADS, 1.0)):
        parts.append(_pad_heads_cols(w_in[:, o:o + nh * HEAD_DIM], nh, sc))
        o += nh * HEAD_DIM
    parts.append(jnp.pad(w_in[:, o:o + FOX_HEADS], ((0, 0), (0, HEAD_PAD - FOX_HEADS))))
    o += FOX_HEADS
    for nh, sc in ((SWA_HEADS, scale), (SWA_KV_HEADS, 1.0), (SWA_KV_HEADS, 1.0)):
        parts.append(_pad_heads_cols(w_in[:, o:o + nh * HEAD_DIM], nh, sc))
        o += nh * HEAD_DIM
    w_all = jnp.concatenate(parts, axis=1).astype(BF16)
    bf_pad = jnp.pad(b_fgate, (0, HEAD_PAD - FOX_HEADS)).reshape(1, HEAD_PAD)

    qp, kp, vp, sq, sk, sv = _inproj_call(x, mod, g_pre_mix.reshape(1, D), w_all, bf_pad)
    fo = _fox_call(qp, kp, vp)
    so = _swa_call(swa_sinks, sq, sk, sv)

    nf = FOX_HEADS * HEAD_DIM
    wof = _pad_heads_rows(w_out[:nf], FOX_HEADS).astype(BF16)
    wos = _pad_heads_rows(w_out[nf:], SWA_HEADS).astype(BF16)
    keys = sub_keys.reshape(2 * PEER_HEADS, N_KEYS, PEER_HALF).astype(BF16)
    x1, h2, idx, gates = _route_call(fo, so, x, mod, g_post_mix.reshape(1, D), g_pre_ffn.reshape(1, D),
                                     wof, wos, w_query.astype(BF16), keys)

    table = jnp.concatenate([w_u.reshape(N_EXPERTS, 8, HEAD_PAD), w_v.reshape(N_EXPERTS, 8, HEAD_PAD)],
                            axis=1).astype(BF16)
    y3 = _peer_call(idx.reshape(T * PEER_PICKS), table, h2.reshape(T, 8, HEAD_PAD), gates)
    return _final_call(x1, y3.reshape(B, S, D), mod, g_post_ffn.reshape(1, D))


def kernel(x, c, w_ada, b_ada, g_pre_mix, g_post_mix, g_pre_ffn, g_post_ffn, w_in, b_fgate, swa_sinks, w_out,
           w_query, sub_keys, w_u, w_v):
    B = x.shape[0]
    c8 = jnp.pad(c, ((0, 8 - B), (0, 0)))
    for l in range(w_ada.shape[0]):
        x = _layer(x, c8, w_ada[l], b_ada[l], g_pre_mix[l], g_post_mix[l], g_pre_ffn[l], g_post_ffn[l],
                   w_in[l], b_fgate[l], swa_sinks[l], w_out[l], w_query[l], sub_keys[l], w_u[l], w_v[l])
    return x
```

```python
import functools

import numpy as np
import jax
import jax.numpy as jnp
from jax import lax
from jax.experimental import pallas as pl
from jax.experimental.pallas import tpu as pltpu

F32 = jnp.float32
BF16 = jnp.bfloat16

D_MODEL = 1024
HEAD_DIM = 64
HEAD_PAD = 128
FOX_HEADS = 8
SWA_HEADS = 8
SWA_KV_HEADS = 2
SWA_GROUP = SWA_HEADS // SWA_KV_HEADS
WINDOW = 128
PEER_HEADS = 8
PEER_HALF = 128
N_KEYS = 128
N_EXPERTS = N_KEYS * N_KEYS
PEER_TOPK = 16
PEER_PICKS = PEER_HEADS * PEER_TOPK
N_MOD = 6
RMS_EPS = 1e-6
NEG_INF = -1e30

_F_LANE = HEAD_DIM
_ONE_LANE = HEAD_DIM

ROW_TILE = 512
FOX_TILE = 512
SWA_TILE = 512
ROUTE_TILE = 256
PEER_TILE = 16
PEER_UNROLL = 4
TABLE_ROWS = 16
TABLE_COLS = PEER_PICKS * TABLE_ROWS

_VMEM_LIMIT = 56 * 1024 * 1024


def _dot(a, b):
    return jnp.dot(a, b, preferred_element_type=F32)


def _dot_nt(a, b):
    return lax.dot_general(a, b, (((1,), (1,)), ((), ())), preferred_element_type=F32)


def _split3(x):
    hi = x.astype(BF16)
    r = x - hi.astype(F32)
    mid = r.astype(BF16)
    lo = (r - mid.astype(F32)).astype(BF16)
    return hi, mid, lo


def _rms(x):
    return x * lax.rsqrt(jnp.mean(x * x, axis=-1, keepdims=True) + RMS_EPS)


def _params(*sem):
    return pltpu.CompilerParams(dimension_semantics=sem, vmem_limit_bytes=_VMEM_LIMIT)


def _ada_kernel(c_ref, w_ref, b_ref, o_ref):
    c = c_ref[...]
    s = (c * jax.nn.sigmoid(c)).astype(BF16)
    o_ref[...] = _dot(s, w_ref[...].astype(BF16)) + b_ref[...]


def _ada_call(c8, w_ada, b_ada):
    n = w_ada.shape[1]
    tn = 1536
    return pl.pallas_call(
        _ada_kernel,
        grid=(n // tn,),
        in_specs=[pl.BlockSpec((8, D_MODEL), lambda j: (0, 0)),
                  pl.BlockSpec((D_MODEL, tn), lambda j: (0, j)),
                  pl.BlockSpec((1, tn), lambda j: (0, j))],
        out_specs=pl.BlockSpec((8, tn), lambda j: (0, j)),
        out_shape=jax.ShapeDtypeStruct((8, n), F32),
        compiler_params=_params("arbitrary"),
        name="ada",
    )(c8, w_ada, b_ada.reshape(1, n))


_NQ = FOX_HEADS * HEAD_PAD
_NS = SWA_HEADS * HEAD_PAD
_NKV = SWA_KV_HEADS * HEAD_PAD
_IN_COLS = 3 * _NQ + HEAD_PAD + _NS + 2 * _NKV


def _inproj_kernel(x_ref, mod_ref, g_ref, w_ref, bf_ref, tri_ref, pq_ref, pk_ref, cst_ref,
                   qp_ref, kp_ref, vp_ref, sq_ref, sk_ref, sv_ref, carry_ref):
    i = pl.program_id(1)

    @pl.when(i == 0)
    def _():
        carry_ref[...] = jnp.zeros_like(carry_ref)

    x = x_ref[0]
    sh1 = mod_ref[0, 0:1, :]
    sc1 = mod_ref[0, 1:2, :]
    h = _rms(x) * g_ref[...] * (1.0 + sc1) + sh1
    proj = _dot(h.astype(BF16), w_ref[...])

    z = proj[:, 3 * _NQ:3 * _NQ + HEAD_PAD] + bf_ref[...]
    ls = jnp.minimum(z, 0.0) - jnp.log(1.0 + jnp.exp(-jnp.abs(z)))
    tri = tri_ref[...]
    hi, mid, lo = _split3(ls)
    fcum = _dot(tri, hi) + _dot(tri, mid) + _dot(tri, lo) + carry_ref[...]
    carry_ref[...] = fcum[fcum.shape[0] - 1:, :]

    fh, fm, fl = _split3(fcum)
    eq = _dot(fh, pq_ref[0]) + _dot(fm, pq_ref[1]) + _dot(fl, pq_ref[2]) + cst_ref[0:1, :]
    ek = _dot(fh, pk_ref[0]) + _dot(fm, pk_ref[1]) + _dot(fl, pk_ref[2]) + cst_ref[1:2, :]
    qp_ref[0] = (proj[:, 0:_NQ] + eq).astype(BF16)
    kp_ref[0] = (proj[:, _NQ:2 * _NQ] + ek).astype(BF16)
    vp_ref[0] = (proj[:, 2 * _NQ:3 * _NQ] + cst_ref[2:3, :]).astype(BF16)
    o = 3 * _NQ + HEAD_PAD
    sq_ref[0] = proj[:, o:o + _NS].astype(BF16)
    sk_ref[0] = proj[:, o + _NS:o + _NS + _NKV].astype(BF16)
    sv_ref[0] = proj[:, o + _NS + _NKV:o + _NS + 2 * _NKV].astype(BF16)


def _inproj_consts(tr):
    tri = np.tril(np.ones((tr, tr), np.float32))
    pq = np.zeros((3, HEAD_PAD, _NQ), np.float32)
    pk = np.zeros((3, HEAD_PAD, _NQ), np.float32)
    cst = np.zeros((8, _NQ), np.float32)
    for h in range(FOX_HEADS):
        b = h * HEAD_PAD + _F_LANE
        for j in range(3):
            pq[j, h, b + j] = 1.0
            pk[j, h, b + 3 + j] = -1.0
            cst[0, b + 3 + j] = 1.0
            cst[1, b + j] = 1.0
        cst[2, h * HEAD_PAD + _ONE_LANE] = 1.0
    return (jnp.asarray(tri, BF16), jnp.asarray(pq, BF16), jnp.asarray(pk, BF16), jnp.asarray(cst, F32))


def _inproj_call(x, mod, g_pre, w_all, bf_pad):
    B, S, _ = x.shape
    tr = min(ROW_TILE, S)
    tri, pq, pk, cst = _inproj_consts(tr)
    row = lambda w: pl.BlockSpec((1, tr, w), lambda b, i: (b, i, 0))
    full = lambda a: pl.BlockSpec(a.shape, lambda b, i: (0,) * a.ndim)
    outs = [jax.ShapeDtypeStruct((B, S, w), BF16) for w in (_NQ, _NQ, _NQ, _NS, _NKV, _NKV)]
    return pl.pallas_call(
        _inproj_kernel,
        grid=(B, S // tr),
        in_specs=[row(D_MODEL),
                  pl.BlockSpec((1, N_MOD, D_MODEL), lambda b, i: (b, 0, 0)),
                  full(g_pre), full(w_all), full(bf_pad), full(tri), full(pq), full(pk), full(cst)],
        out_specs=[row(_NQ), row(_NQ), row(_NQ), row(_NS), row(_NKV), row(_NKV)],
        out_shape=outs,
        scratch_shapes=[pltpu.VMEM((1, HEAD_PAD), F32)],
        compiler_params=_params("arbitrary", "arbitrary"),
        name="inproj",
    )(x, mod, g_pre, w_all, bf_pad, tri, pq, pk, cst)


def _fox_kernel(q_ref, k_ref, v_ref, o_ref, m_ref, acc_ref):
    i = pl.program_id(2)
    t = q_ref.shape[1]
    q = q_ref[0]
    m_ref[...] = jnp.full_like(m_ref, NEG_INF)
    acc_ref[...] = jnp.zeros_like(acc_ref)

    def block(j, masked):
        off = pl.multiple_of(j * t, t)
        k = k_ref[0, pl.ds(off, t), :]
        v = v_ref[0, pl.ds(off, t), :]
        s = _dot_nt(q, k)
        if masked:
            r = lax.broadcasted_iota(jnp.int32, s.shape, 0)
            c = lax.broadcasted_iota(jnp.int32, s.shape, 1)
            s = jnp.where(c <= r, s, NEG_INF)
        m_prev = m_ref[...]
        m_new = jnp.maximum(m_prev, jnp.max(s, axis=1, keepdims=True))
        p = jnp.exp(s - m_new)
        acc_ref[...] = jnp.exp(m_prev - m_new) * acc_ref[...] + _dot(p.astype(BF16), v)
        m_ref[...] = m_new

    def body(j, carry):
        block(j, False)
        return carry

    lax.fori_loop(0, i, body, 0)
    block(i, True)
    acc = acc_ref[...]
    o_ref[0] = (acc / acc[:, _ONE_LANE:_ONE_LANE + 1]).astype(BF16)


def _fox_call(qp, kp, vp):
    B, S, _ = qp.shape
    t = min(FOX_TILE, S)
    return pl.pallas_call(
        _fox_kernel,
        grid=(B, FOX_HEADS, S // t),
        in_specs=[pl.BlockSpec((1, t, HEAD_PAD), lambda b, h, i: (b, i, h)),
                  pl.BlockSpec((1, S, HEAD_PAD), lambda b, h, i: (b, 0, h)),
                  pl.BlockSpec((1, S, HEAD_PAD), lambda b, h, i: (b, 0, h))],
        out_specs=pl.BlockSpec((1, t, HEAD_PAD), lambda b, h, i: (b, i, h)),
        out_shape=jax.ShapeDtypeStruct((B, S, _NQ), BF16),
        scratch_shapes=[pltpu.VMEM((t, 1), F32), pltpu.VMEM((t, HEAD_PAD), F32)],
        compiler_params=_params("arbitrary", "arbitrary", "arbitrary"),
        name="fox",
    )(qp, kp, vp)


def _swa_kernel(sink_ref, q_ref, kc_ref, kp_ref, vc_ref, vp_ref, o_ref):
    i = pl.program_id(1)
    nsub = q_ref.shape[1] // WINDOW
    r = lax.broadcasted_iota(jnp.int32, (WINDOW, 2 * WINDOW), 0)
    j = lax.broadcasted_iota(jnp.int32, (WINDOW, 2 * WINDOW), 1)
    dist = r + WINDOW - j
    valid = (dist >= 0) & (dist < WINDOW)
    distf = dist.astype(F32)
    for qb in range(nsub):
        rows = slice(qb * WINDOW, (qb + 1) * WINDOW)
        if qb == 0:
            ok = valid & ((j >= WINDOW) | (i > 0))
        else:
            ok = valid
        for g in range(SWA_KV_HEADS):
            lanes = slice(g * HEAD_PAD, (g + 1) * HEAD_PAD)
            if qb == 0:
                kprev, vprev = kp_ref[0, :, lanes], vp_ref[0, :, lanes]
            else:
                prev = slice((qb - 1) * WINDOW, qb * WINDOW)
                kprev, vprev = kc_ref[0, prev, lanes], vc_ref[0, prev, lanes]
            kk = jnp.concatenate([kprev, kc_ref[0, rows, lanes]], axis=0)
            vv = jnp.concatenate([vprev, vc_ref[0, rows, lanes]], axis=0)
            for u in range(SWA_GROUP):
                hq = g * SWA_GROUP + u
                slope = 2.0 ** (-8.0 * (hq + 1) / SWA_HEADS)
                q = q_ref[0, rows, hq * HEAD_PAD:(hq + 1) * HEAD_PAD]
                s = _dot_nt(q, kk) - slope * distf
                s = jnp.where(ok, s, NEG_INF)
                sink = sink_ref[hq]
                m = jnp.maximum(jnp.max(s, axis=1, keepdims=True), sink)
                p = jnp.exp(s - m)
                den = jnp.sum(p, axis=1, keepdims=True) + jnp.exp(sink - m)
                o = _dot(p.astype(BF16), vv) / den
                o_ref[0, rows, hq * HEAD_PAD:(hq + 1) * HEAD_PAD] = o.astype(BF16)


def _swa_call(sinks, sq, sk, sv):
    B, S, _ = sq.shape
    t = min(SWA_TILE, S)
    per = t // WINDOW
    cur = lambda w: pl.BlockSpec((1, t, w), lambda b, i: (b, i, 0))
    prv = lambda w: pl.BlockSpec((1, WINDOW, w), lambda b, i: (b, jnp.maximum(i * per - 1, 0), 0))
    return pl.pallas_call(
        _swa_kernel,
        grid=(B, S // t),
        in_specs=[pl.BlockSpec(memory_space=pltpu.SMEM),
                  cur(_NS), cur(_NKV), prv(_NKV), cur(_NKV), prv(_NKV)],
        out_specs=cur(_NS),
        out_shape=jax.ShapeDtypeStruct((B, S, _NS), BF16),
        compiler_params=_params("arbitrary", "arbitrary"),
        name="swa",
    )(sinks, sq, sk, sk, sv, sv)


def _topk_rows(s, k, val_ref, idx_ref):
    n = s.shape[0]
    iota = lax.broadcasted_iota(jnp.int32, s.shape, 0)
    for r in range(k):
        m = jnp.max(s, axis=0, keepdims=True)
        i = jnp.min(jnp.where(s == m, iota, n), axis=0, keepdims=True)
        val_ref[pl.ds(r, 1), :] = m
        idx_ref[pl.ds(r, 1), :] = i
        s = jnp.where(iota == i, -jnp.inf, s)


def _route_kernel(fo_ref, so_ref, x_ref, mod_ref, gpost_ref, gpre_ref, wof_ref, wos_ref, wq_ref, keys_ref,
                  x1_ref, h2_ref, idx_ref, gate_ref, qs_ref, sv_ref, si_ref, et_ref, gt_ref):
    x = x_ref[0]
    gt1 = mod_ref[0, 2:3, :]
    sh2 = mod_ref[0, 3:4, :]
    sc2 = mod_ref[0, 4:5, :]
    y = _dot(fo_ref[0], wof_ref[...]) + _dot(so_ref[0], wos_ref[...])
    x1 = x + gt1 * (_rms(y) * gpost_ref[...])
    x1_ref[0] = x1
    h2 = _rms(x1) * gpre_ref[...] * (1.0 + sc2) + sh2
    h2_ref[0] = h2
    qp = _dot(h2.astype(BF16), wq_ref[...])
    nhp = 2 * PEER_HEADS
    for hp in range(nhp):
        qs_ref[hp] = qp[:, hp * PEER_HALF:(hp + 1) * PEER_HALF].astype(BF16)

    def half(hp, carry):
        sc = _dot_nt(keys_ref[hp], qs_ref[hp])
        _topk_rows(sc, PEER_TOPK, sv_ref.at[hp], si_ref.at[hp])
        return carry

    lax.fori_loop(0, nhp, half, 0)

    def head(h, carry):
        v0, v1 = sv_ref[2 * h], sv_ref[2 * h + 1]
        i0, i1 = si_ref[2 * h], si_ref[2 * h + 1]
        cand = jnp.concatenate([v0[a:a + 1, :] + v1 for a in range(PEER_TOPK)], axis=0)
        _topk_rows(cand, PEER_TOPK, gt_ref.at[h], et_ref.at[h])
        cv, ci = gt_ref[h], et_ref[h]
        ca, cb = ci >> 4, ci & (PEER_TOPK - 1)
        e1 = jnp.zeros_like(ci)
        e2 = jnp.zeros_like(ci)
        for a in range(PEER_TOPK):
            e1 = jnp.where(ca == a, i0[a:a + 1, :], e1)
            e2 = jnp.where(cb == a, i1[a:a + 1, :], e2)
        et_ref[h] = e1 * N_KEYS + e2
        ex = jnp.exp(cv - cv[0:1, :])
        gt_ref[h] = ex / jnp.sum(ex, axis=0, keepdims=True)
        return carry

    lax.fori_loop(0, PEER_HEADS, head, 0)
    tt = et_ref.shape[2]
    et = lax.bitcast_convert_type(et_ref[...].reshape(PEER_PICKS, tt), F32)
    idx_ref[...] = lax.bitcast_convert_type(et.T, jnp.int32)
    gate_ref[...] = gt_ref[...].reshape(PEER_PICKS, tt).T


def _route_call(fo, so, x, mod, g_post, g_pre, wof, wos, wq, keys):
    B, S, _ = x.shape
    tt = min(ROUTE_TILE, S)
    per = S // tt
    row = lambda w: pl.BlockSpec((1, tt, w), lambda b, i: (b, i, 0))
    full = lambda a: pl.BlockSpec(a.shape, lambda b, i: (0,) * a.ndim)
    tok = pl.BlockSpec((tt, PEER_PICKS), lambda b, i: (b * per + i, 0))
    return pl.pallas_call(
        _route_kernel,
        grid=(B, per),
        in_specs=[row(_NQ), row(_NS), row(D_MODEL),
                  pl.BlockSpec((1, N_MOD, D_MODEL), lambda b, i: (b, 0, 0)),
                  full(g_post), full(g_pre), full(wof), full(wos), full(wq), full(keys)],
        out_specs=[row(D_MODEL), row(D_MODEL), tok, tok],
        out_shape=[jax.ShapeDtypeStruct((B, S, D_MODEL), F32),
                   jax.ShapeDtypeStruct((B, S, D_MODEL), F32),
                   jax.ShapeDtypeStruct((B * S, PEER_PICKS), jnp.int32),
                   jax.ShapeDtypeStruct((B * S, PEER_PICKS), F32)],
        scratch_shapes=[pltpu.VMEM((2 * PEER_HEADS, tt, PEER_HALF), BF16),
                        pltpu.VMEM((2 * PEER_HEADS, PEER_TOPK, tt), F32),
                        pltpu.VMEM((2 * PEER_HEADS, PEER_TOPK, tt), jnp.int32),
                        pltpu.VMEM((PEER_HEADS, PEER_TOPK, tt), jnp.int32),
                        pltpu.VMEM((PEER_HEADS, PEER_TOPK, tt), F32)],
        compiler_params=_params("arbitrary", "arbitrary"),
        name="route",
    )(fo, so, x, mod, g_post, g_pre, wof, wos, wq, keys)


def _peer_consts():
    etu = np.zeros((TABLE_COLS, PEER_PICKS), np.float32)
    ev = np.zeros((PEER_PICKS, TABLE_COLS), np.float32)
    half = TABLE_ROWS // 2
    for k in range(PEER_PICKS):
        for s in range(half):
            etu[k * TABLE_ROWS + s, k] = 1.0
            ev[k, k * TABLE_ROWS + half + s] = 1.0
    return jnp.asarray(etu, BF16), jnp.asarray(ev, BF16)


def _peer_kernel(idx_hbm, tab_hbm, hs_ref, g_ref, etu_ref, ev_ref, y_ref,
                 buf0, buf1, ibuf, zs_ref, wr_ref, sem_g, sem_i):
    s = pl.program_id(0)
    last = pl.num_programs(0) - 1
    tt = PEER_TILE
    npt = tt * PEER_PICKS
    par = lax.rem(s, 2)
    half = TABLE_ROWS // 2

    def chunk_copy(step, p):
        return pltpu.make_async_copy(idx_hbm.at[pl.ds((2 * step + 1) * npt, 2 * npt)],
                                     ibuf.at[pl.ds(p * 2 * npt, 2 * npt)], sem_i.at[p])

    def gather_all(buf, sem):
        return pltpu.make_async_copy(tab_hbm.at[pl.ds(0, npt)], buf, sem)

    def issue_token(ioff, t, buf, sem):
        ids = ibuf.at[pl.ds(ioff, PEER_PICKS)]
        rows = buf.at[pl.ds(t * PEER_PICKS, PEER_PICKS)]
        for k in range(PEER_PICKS):
            pltpu.make_async_copy(tab_hbm.at[ids[k]], rows.at[k], sem).start()

    @pl.when(s == 0)
    def _():
        first = pltpu.make_async_copy(idx_hbm.at[pl.ds(0, npt)], ibuf.at[pl.ds(2 * npt, npt)], sem_i.at[1])
        first.start()
        first.wait()

        def body(t, carry):
            issue_token(2 * npt + t * PEER_PICKS, t, buf0, sem_g.at[0])
            return carry

        lax.fori_loop(0, tt, body, 0)
        chunk_copy(0, 0).start()

    chunk_copy(s, par).wait()

    @pl.when(s < last)
    def _():
        chunk_copy(s + 1, 1 - par).start()

    row8 = lax.broadcasted_iota(jnp.int32, (half, TABLE_COLS), 0)
    col8 = lax.broadcasted_iota(jnp.int32, (half, TABLE_COLS), 1) % TABLE_ROWS
    mask_u = col8 == row8
    mask_v = col8 == row8 + half
    etu = etu_ref[...]

    def tile(cur, cur_sem, nxt, nxt_sem, ioff, tok0):
        gather_all(cur, cur_sem).wait()

        def ubody(it, carry):
            for u in range(PEER_UNROLL):
                t = it * PEER_UNROLL + u
                issue_token(ioff + t * PEER_PICKS, t, nxt, nxt_sem)
                wb = cur[pl.ds(t * PEER_PICKS, PEER_PICKS)].reshape(TABLE_COLS, HEAD_PAD)
                y = _dot_nt(hs_ref[tok0 + t].astype(BF16), wb)
                zs_ref[pl.ds(t, 1), :] = jnp.sum(jnp.where(mask_u, y, 0.0), axis=0, keepdims=True)
            return carry

        lax.fori_loop(0, tt // PEER_UNROLL, ubody, 0)

        zh, zm, zl = _split3(zs_ref[...])
        a = _dot(zh, etu) + _dot(zm, etu) + _dot(zl, etu)
        w = jax.nn.gelu(a) * g_ref[tok0:tok0 + tt, :]
        wr_ref[...] = _dot(w.astype(BF16), ev_ref[...])

        def vbody(it, carry):
            for u in range(PEER_UNROLL):
                t = it * PEER_UNROLL + u
                wrow = jnp.broadcast_to(wr_ref[pl.ds(t, 1), :], (half, TABLE_COLS))
                wexp = jnp.where(mask_v, wrow, 0.0).astype(BF16)
                wb = cur[pl.ds(t * PEER_PICKS, PEER_PICKS)].reshape(TABLE_COLS, HEAD_PAD)
                y_ref[tok0 + t] = _dot(wexp, wb)
            return carry

        lax.fori_loop(0, tt // PEER_UNROLL, vbody, 0)

    base = par * 2 * npt
    tile(buf0, sem_g.at[0], buf1, sem_g.at[1], base, 0)
    tile(buf1, sem_g.at[1], buf0, sem_g.at[0], base + npt, tt)

    @pl.when(s == last)
    def _():
        gather_all(buf0, sem_g.at[0]).wait()


def _peer_call(idx_flat, table, hs3, gates):
    T = hs3.shape[0]
    tt = PEER_TILE
    npt = tt * PEER_PICKS
    etu, ev = _peer_consts()
    idx_pad = jnp.pad(idx_flat, (0, npt))
    full = lambda a: pl.BlockSpec(a.shape, lambda s: (0,) * a.ndim)
    return pl.pallas_call(
        _peer_kernel,
        grid=(T // (2 * tt),),
        in_specs=[pl.BlockSpec(memory_space=pl.ANY),
                  pl.BlockSpec(memory_space=pl.ANY),
                  pl.BlockSpec((2 * tt, 8, HEAD_PAD), lambda s: (s, 0, 0)),
                  pl.BlockSpec((2 * tt, PEER_PICKS), lambda s: (s, 0)),
                  full(etu), full(ev)],
        out_specs=pl.BlockSpec((2 * tt, 8, HEAD_PAD), lambda s: (s, 0, 0)),
        out_shape=jax.ShapeDtypeStruct((T, 8, HEAD_PAD), F32),
        scratch_shapes=[pltpu.VMEM((npt, TABLE_ROWS, HEAD_PAD), BF16),
                        pltpu.VMEM((npt, TABLE_ROWS, HEAD_PAD), BF16),
                        pltpu.SMEM((4 * npt,), jnp.int32),
                        pltpu.VMEM((tt, TABLE_COLS), F32),
                        pltpu.VMEM((tt, TABLE_COLS), F32),
                        pltpu.SemaphoreType.DMA((2,)),
                        pltpu.SemaphoreType.DMA((2,))],
        compiler_params=_params("arbitrary"),
        name="peer",
    )(idx_pad, table, hs3, gates, etu, ev)


def _final_kernel(x1_ref, y_ref, mod_ref, g_ref, o_ref):
    gt2 = mod_ref[0, 5:6, :]
    o_ref[0] = x1_ref[0] + gt2 * (_rms(y_ref[0]) * g_ref[...])


def _final_call(x1, y, mod, g_post):
    B, S, _ = x1.shape
    tr = min(ROW_TILE, S)
    row = pl.BlockSpec((1, tr, D_MODEL), lambda b, i: (b, i, 0))
    return pl.pallas_call(
        _final_kernel,
        grid=(B, S // tr),
        in_specs=[row, row, pl.BlockSpec((1, N_MOD, D_MODEL), lambda b, i: (b, 0, 0)),
                  pl.BlockSpec((1, D_MODEL), lambda b, i: (0, 0))],
        out_specs=row,
        out_shape=jax.ShapeDtypeStruct((B, S, D_MODEL), F32),
        compiler_params=_params("arbitrary", "arbitrary"),
        name="final",
    )(x1, y, mod, g_post)


def _pad_heads_cols(w, nh, scale=1.0):
    k = w.shape[0]
    w = (w * scale).reshape(k, nh, HEAD_DIM)
    return jnp.pad(w, ((0, 0), (0, 0), (0, HEAD_PAD - HEAD_DIM))).reshape(k, nh * HEAD_PAD)


def _pad_heads_rows(w, nh):
    n = w.shape[1]
    w = w.reshape(nh, HEAD_DIM, n)
    return jnp.pad(w, ((0, 0), (0, HEAD_PAD - HEAD_DIM), (0, 0))).reshape(nh * HEAD_PAD, n)


def _layer(x, c8, w_ada, b_ada, g_pre_mix, g_post_mix, g_pre_ffn, g_post_ffn,
           w_in, b_fgate, swa_sinks, w_out, w_query, sub_keys, w_u, w_v):
    B, S, D = x.shape
    T = B * S
    scale = HEAD_DIM ** -0.5
    mod = _ada_call(c8, w_ada, b_ada)[:B].reshape(B, N_MOD, D)

    o = 0
    parts = []
    for nh, sc in ((FOX_HEADS, scale), (FOX_HEADS, 1.0), (FOX_HEADS, 1.0)):
        parts.append(_pad_heads_cols(w_in[:, o:o + nh * HEAD_DIM], nh, sc))
        o += nh * HEAD_DIM
    parts.append(jnp.pad(w_in[:, o:o + FOX_HEADS], ((0, 0), (0, HEAD_PAD - FOX_HEADS))))
    o += FOX_HEADS
    for nh, sc in ((SWA_HEADS, scale), (SWA_KV_HEADS, 1.0), (SWA_KV_HEADS, 1.0)):
        parts.append(_pad_heads_cols(w_in[:, o:o + nh * HEAD_DIM], nh, sc))
        o += nh * HEAD_DIM
    w_all = jnp.concatenate(parts, axis=1).astype(BF16)
    bf_pad = jnp.pad(b_fgate, (0, HEAD_PAD - FOX_HEADS)).reshape(1, HEAD_PAD)

    qp, kp, vp, sq, sk, sv = _inproj_call(x, mod, g_pre_mix.reshape(1, D), w_all, bf_pad)
    fo = _fox_call(qp, kp, vp)
    so = _swa_call(swa_sinks, sq, sk, sv)

    nf = FOX_HEADS * HEAD_DIM
    wof = _pad_heads_rows(w_out[:nf], FOX_HEADS).astype(BF16)
    wos = _pad_heads_rows(w_out[nf:], SWA_HEADS).astype(BF16)
    keys = sub_keys.reshape(2 * PEER_HEADS, N_KEYS, PEER_HALF).astype(BF16)
    x1, h2, idx, gates = _route_call(fo, so, x, mod, g_post_mix.reshape(1, D), g_pre_ffn.reshape(1, D),
                                     wof, wos, w_query.astype(BF16), keys)

    table = jnp.concatenate([w_u.reshape(N_EXPERTS, 8, HEAD_PAD), w_v.reshape(N_EXPERTS, 8, HEAD_PAD)],
                            axis=1).astype(BF16)
    y3 = _peer_call(idx.reshape(T * PEER_PICKS), table, h2.reshape(T, 8, HEAD_PAD), gates)
    return _final_call(x1, y3.reshape(B, S, D), mod, g_post_ffn.reshape(1, D))


def kernel(x, c, w_ada, b_ada, g_pre_mix, g_post_mix, g_pre_ffn, g_post_ffn, w_in, b_fgate, swa_sinks, w_out,
           w_query, sub_keys, w_u, w_v):
    B = x.shape[0]
    c8 = jnp.pad(c, ((0, 8 - B), (0, 0)))
    for l in range(w_ada.shape[0]):
        x = _layer(x, c8, w_ada[l], b_ada[l], g_pre_mix[l], g_post_mix[l], g_pre_ffn[l], g_post_ffn[l],
                   w_in[l], b_fgate[l], swa_sinks[l], w_out[l], w_query[l], sub_keys[l], w_u[l], w_v[l])
    return x
```

```python
import numpy as np
import jax
import jax.numpy as jnp
from jax import lax
from jax.experimental import pallas as pl
from jax.experimental.pallas import tpu as pltpu

F32 = jnp.float32
BF16 = jnp.bfloat16

D_MODEL = 1024
HEAD_DIM = 64
HEAD_PAD = 128
FOX_HEADS = 8
SWA_HEADS = 8
SWA_KV_HEADS = 2
SWA_GROUP = SWA_HEADS // SWA_KV_HEADS
WINDOW = 128
PEER_HEADS = 8
PEER_HALF = 128
N_KEYS = 128
N_EXPERTS = N_KEYS * N_KEYS
PEER_TOPK = 16
PEER_PICKS = PEER_HEADS * PEER_TOPK
N_MOD = 6
RMS_EPS = 1e-6
NEG_INF = -1e30

_F_LANE = HEAD_DIM
_ONE_LANE = HEAD_DIM

ROW_TILE = 512
FOX_TILE = 512
SWA_TILE = 512
ROUTE_TILE = 256
PEER_TILE = 16
PEER_UNROLL = 8
PEER_GROUP = 16
TABLE_ROWS = 16
TABLE_COLS = PEER_PICKS * TABLE_ROWS

_VMEM_LIMIT = 56 * 1024 * 1024


def _dot(a, b):
    return jnp.dot(a, b, preferred_element_type=F32)


def _dot_nt(a, b):
    return lax.dot_general(a, b, (((1,), (1,)), ((), ())), preferred_element_type=F32)


def _split3(x):
    hi = x.astype(BF16)
    r = x - hi.astype(F32)
    mid = r.astype(BF16)
    lo = (r - mid.astype(F32)).astype(BF16)
    return hi, mid, lo


def _rms(x):
    return x * lax.rsqrt(jnp.mean(x * x, axis=-1, keepdims=True) + RMS_EPS)


def _params(*sem):
    return pltpu.CompilerParams(dimension_semantics=sem, vmem_limit_bytes=_VMEM_LIMIT)


def _ada_kernel(c_ref, w_ref, b_ref, o_ref):
    c = c_ref[...]
    s = (c * jax.nn.sigmoid(c)).astype(BF16)
    o_ref[...] = _dot(s, w_ref[...].astype(BF16)) + b_ref[...]


def _ada_call(c8, w_ada, b_ada):
    n = w_ada.shape[1]
    tn = 1536
    return pl.pallas_call(
        _ada_kernel,
        grid=(n // tn,),
        in_specs=[pl.BlockSpec((8, D_MODEL), lambda j: (0, 0)),
                  pl.BlockSpec((D_MODEL, tn), lambda j: (0, j)),
                  pl.BlockSpec((1, tn), lambda j: (0, j))],
        out_specs=pl.BlockSpec((8, tn), lambda j: (0, j)),
        out_shape=jax.ShapeDtypeStruct((8, n), F32),
        compiler_params=_params("arbitrary"),
        name="ada",
    )(c8, w_ada, b_ada.reshape(1, n))


_NQ = FOX_HEADS * HEAD_PAD
_NS = SWA_HEADS * HEAD_PAD
_NKV = SWA_KV_HEADS * HEAD_PAD
_IN_COLS = 3 * _NQ + HEAD_PAD + _NS + 2 * _NKV


def _inproj_kernel(x_ref, mod_ref, g_ref, w_ref, bf_ref, tri_ref, pq_ref, pk_ref, cst_ref,
                   qt_ref, kp_ref, vt_ref, sq_ref, sk_ref, sv_ref, carry_ref):
    i = pl.program_id(1)

    @pl.when(i == 0)
    def _():
        carry_ref[...] = jnp.zeros_like(carry_ref)

    x = x_ref[0]
    sh1 = mod_ref[0, 0:1, :]
    sc1 = mod_ref[0, 1:2, :]
    h = _rms(x) * g_ref[...] * (1.0 + sc1) + sh1
    proj = _dot(h.astype(BF16), w_ref[...])

    z = proj[:, 3 * _NQ:3 * _NQ + HEAD_PAD] + bf_ref[...]
    ls = jnp.minimum(z, 0.0) - jnp.log(1.0 + jnp.exp(-jnp.abs(z)))
    tri = tri_ref[...]
    hi, mid, lo = _split3(ls)
    fcum = _dot(tri, hi) + _dot(tri, mid) + _dot(tri, lo) + carry_ref[...]
    carry_ref[...] = fcum[fcum.shape[0] - 1:, :]

    fh, fm, fl = _split3(fcum)
    eq = _dot(fh, pq_ref[0]) + _dot(fm, pq_ref[1]) + _dot(fl, pq_ref[2]) + cst_ref[0:1, :]
    ek = _dot(fh, pk_ref[0]) + _dot(fm, pk_ref[1]) + _dot(fl, pk_ref[2]) + cst_ref[1:2, :]
    qt_ref[0, 0] = (proj[:, 0:_NQ] + eq).T.astype(BF16)
    kp_ref[0] = (proj[:, _NQ:2 * _NQ] + ek).astype(BF16)
    vt_ref[0, 0] = (proj[:, 2 * _NQ:3 * _NQ] + cst_ref[2:3, :]).T.astype(BF16)
    o = 3 * _NQ + HEAD_PAD
    sq_ref[0] = proj[:, o:o + _NS].astype(BF16)
    sk_ref[0] = proj[:, o + _NS:o + _NS + _NKV].astype(BF16)
    sv_ref[0] = proj[:, o + _NS + _NKV:o + _NS + 2 * _NKV].astype(BF16)


def _inproj_consts(tr):
    tri = np.tril(np.ones((tr, tr), np.float32))
    pq = np.zeros((3, HEAD_PAD, _NQ), np.float32)
    pk = np.zeros((3, HEAD_PAD, _NQ), np.float32)
    cst = np.zeros((8, _NQ), np.float32)
    for h in range(FOX_HEADS):
        b = h * HEAD_PAD + _F_LANE
        for j in range(3):
            pq[j, h, b + j] = 1.0
            pk[j, h, b + 3 + j] = -1.0
            cst[0, b + 3 + j] = 1.0
            cst[1, b + j] = 1.0
        cst[2, h * HEAD_PAD + _ONE_LANE] = 1.0
    return (jnp.asarray(tri, BF16), jnp.asarray(pq, BF16), jnp.asarray(pk, BF16), jnp.asarray(cst, F32))


def _inproj_call(x, mod, g_pre, w_all, bf_pad):
    B, S, _ = x.shape
    tr = min(ROW_TILE, S)
    n = S // tr
    tri, pq, pk, cst = _inproj_consts(tr)
    row = lambda w: pl.BlockSpec((1, tr, w), lambda b, i: (b, i, 0))
    slab = pl.BlockSpec((1, 1, _NQ, tr), lambda b, i: (b, i, 0, 0))
    full = lambda a: pl.BlockSpec(a.shape, lambda b, i: (0,) * a.ndim)
    outs = [jax.ShapeDtypeStruct((B, n, _NQ, tr), BF16), jax.ShapeDtypeStruct((B, S, _NQ), BF16),
            jax.ShapeDtypeStruct((B, n, _NQ, tr), BF16), jax.ShapeDtypeStruct((B, S, _NS), BF16),
            jax.ShapeDtypeStruct((B, S, _NKV), BF16), jax.ShapeDtypeStruct((B, S, _NKV), BF16)]
    return pl.pallas_call(
        _inproj_kernel,
        grid=(B, n),
        in_specs=[row(D_MODEL),
                  pl.BlockSpec((1, N_MOD, D_MODEL), lambda b, i: (b, 0, 0)),
                  full(g_pre), full(w_all), full(bf_pad), full(tri), full(pq), full(pk), full(cst)],
        out_specs=[slab, row(_NQ), slab, row(_NS), row(_NKV), row(_NKV)],
        out_shape=outs,
        scratch_shapes=[pltpu.VMEM((1, HEAD_PAD), F32)],
        compiler_params=_params("arbitrary", "arbitrary"),
        name="inproj",
    )(x, mod, g_pre, w_all, bf_pad, tri, pq, pk, cst)


def _fox_kernel(qt_ref, k_ref, vt_ref, o_ref, m_ref, acc_ref):
    i = pl.program_id(2)
    t = k_ref.shape[1] // vt_ref.shape[1]
    qt = qt_ref[0, 0]
    m_ref[...] = jnp.full_like(m_ref, NEG_INF)
    acc_ref[...] = jnp.zeros_like(acc_ref)

    def block(j, masked):
        off = pl.multiple_of(j * t, t)
        k = k_ref[0, pl.ds(off, t), :]
        s = _dot(k, qt)
        if masked:
            r = lax.broadcasted_iota(jnp.int32, s.shape, 0)
            c = lax.broadcasted_iota(jnp.int32, s.shape, 1)
            s = jnp.where(r <= c, s, NEG_INF)
        m_prev = m_ref[...]
        m_new = jnp.maximum(m_prev, jnp.max(s, axis=0, keepdims=True))
        p = jnp.exp(s - m_new)
        acc_ref[...] = jnp.exp(m_prev - m_new) * acc_ref[...] + _dot(vt_ref[0, j], p.astype(BF16))
        m_ref[...] = m_new

    def body(j, carry):
        block(j, False)
        return carry

    lax.fori_loop(0, i, body, 0)
    block(i, True)
    acc = acc_ref[...]
    o_ref[0] = (acc / acc[_ONE_LANE:_ONE_LANE + 1, :]).T.astype(BF16)


def _fox_call(qt, kp, vt):
    B, n, _, t = qt.shape
    S = n * t
    return pl.pallas_call(
        _fox_kernel,
        grid=(B, FOX_HEADS, n),
        in_specs=[pl.BlockSpec((1, 1, HEAD_PAD, t), lambda b, h, i: (b, i, h, 0)),
                  pl.BlockSpec((1, S, HEAD_PAD), lambda b, h, i: (b, 0, h)),
                  pl.BlockSpec((1, n, HEAD_PAD, t), lambda b, h, i: (b, 0, h, 0))],
        out_specs=pl.BlockSpec((1, t, HEAD_PAD), lambda b, h, i: (b, i, h)),
        out_shape=jax.ShapeDtypeStruct((B, S, _NQ), BF16),
        scratch_shapes=[pltpu.VMEM((1, t), F32), pltpu.VMEM((HEAD_PAD, t), F32)],
        compiler_params=_params("arbitrary", "arbitrary", "arbitrary"),
        name="fox",
    )(qt, kp, vt)


def _swa_kernel(sink_ref, q_ref, kc_ref, kp_ref, vc_ref, vp_ref, o_ref):
    i = pl.program_id(1)
    nsub = q_ref.shape[1] // WINDOW
    r = lax.broadcasted_iota(jnp.int32, (WINDOW, 2 * WINDOW), 0)
    j = lax.broadcasted_iota(jnp.int32, (WINDOW, 2 * WINDOW), 1)
    dist = r + WINDOW - j
    valid = (dist >= 0) & (dist < WINDOW)
    distf = dist.astype(F32)
    for qb in range(nsub):
        rows = slice(qb * WINDOW, (qb + 1) * WINDOW)
        if qb == 0:
            ok = valid & ((j >= WINDOW) | (i > 0))
        else:
            ok = valid
        for g in range(SWA_KV_HEADS):
            lanes = slice(g * HEAD_PAD, (g + 1) * HEAD_PAD)
            if qb == 0:
                kprev, vprev = kp_ref[0, :, lanes], vp_ref[0, :, lanes]
            else:
                prev = slice((qb - 1) * WINDOW, qb * WINDOW)
                kprev, vprev = kc_ref[0, prev, lanes], vc_ref[0, prev, lanes]
            kk = jnp.concatenate([kprev, kc_ref[0, rows, lanes]], axis=0)
            vv = jnp.concatenate([vprev, vc_ref[0, rows, lanes]], axis=0)
            for u in range(SWA_GROUP):
                hq = g * SWA_GROUP + u
                slope = 2.0 ** (-8.0 * (hq + 1) / SWA_HEADS)
                q = q_ref[0, rows, hq * HEAD_PAD:(hq + 1) * HEAD_PAD]
                s = _dot_nt(q, kk) - slope * distf
                s = jnp.where(ok, s, NEG_INF)
                sink = sink_ref[hq]
                m = jnp.maximum(jnp.max(s, axis=1, keepdims=True), sink)
                p = jnp.exp(s - m)
                den = jnp.sum(p, axis=1, keepdims=True) + jnp.exp(sink - m)
                o = _dot(p.astype(BF16), vv) / den
                o_ref[0, rows, hq * HEAD_PAD:(hq + 1) * HEAD_PAD] = o.astype(BF16)


def _swa_call(sinks, sq, sk, sv):
    B, S, _ = sq.shape
    t = min(SWA_TILE, S)
    per = t // WINDOW
    cur = lambda w: pl.BlockSpec((1, t, w), lambda b, i: (b, i, 0))
    prv = lambda w: pl.BlockSpec((1, WINDOW, w), lambda b, i: (b, jnp.maximum(i * per - 1, 0), 0))
    return pl.pallas_call(
        _swa_kernel,
        grid=(B, S // t),
        in_specs=[pl.BlockSpec(memory_space=pltpu.SMEM),
                  cur(_NS), cur(_NKV), prv(_NKV), cur(_NKV), prv(_NKV)],
        out_specs=cur(_NS),
        out_shape=jax.ShapeDtypeStruct((B, S, _NS), BF16),
        compiler_params=_params("arbitrary", "arbitrary"),
        name="swa",
    )(sinks, sq, sk, sk, sv, sv)


def _topk_rows(s, k, val_ref, idx_ref):
    n = s.shape[0]
    iota = lax.broadcasted_iota(jnp.int32, s.shape, 0)
    for r in range(k):
        m = jnp.max(s, axis=0, keepdims=True)
        i = jnp.min(jnp.where(s == m, iota, n), axis=0, keepdims=True)
        val_ref[pl.ds(r, 1), :] = m
        idx_ref[pl.ds(r, 1), :] = i
        s = jnp.where(iota == i, -jnp.inf, s)


def _route_kernel(fo_ref, so_ref, x_ref, mod_ref, gpost_ref, gpre_ref, wof_ref, wos_ref, wq_ref, keys_ref,
                  x1_ref, h2_ref, idx_ref, gate_ref, qs_ref, sv_ref, si_ref, et_ref, gt_ref):
    x = x_ref[0]
    gt1 = mod_ref[0, 2:3, :]
    sh2 = mod_ref[0, 3:4, :]
    sc2 = mod_ref[0, 4:5, :]
    y = _dot(fo_ref[0], wof_ref[...]) + _dot(so_ref[0], wos_ref[...])
    x1 = x + gt1 * (_rms(y) * gpost_ref[...])
    x1_ref[0] = x1
    h2 = _rms(x1) * gpre_ref[...] * (1.0 + sc2) + sh2
    h2_ref[0] = h2
    qp = _dot(h2.astype(BF16), wq_ref[...])
    nhp = 2 * PEER_HEADS
    for hp in range(nhp):
        qs_ref[hp] = qp[:, hp * PEER_HALF:(hp + 1) * PEER_HALF].astype(BF16)

    def half(hp, carry):
        sc = _dot_nt(keys_ref[hp], qs_ref[hp])
        _topk_rows(sc, PEER_TOPK, sv_ref.at[hp], si_ref.at[hp])
        return carry

    lax.fori_loop(0, nhp, half, 0)

    def head(h, carry):
        v0, v1 = sv_ref[2 * h], sv_ref[2 * h + 1]
        i0, i1 = si_ref[2 * h], si_ref[2 * h + 1]
        cand = jnp.concatenate([v0[a:a + 1, :] + v1 for a in range(PEER_TOPK)], axis=0)
        _topk_rows(cand, PEER_TOPK, gt_ref.at[h], et_ref.at[h])
        cv, ci = gt_ref[h], et_ref[h]
        ca, cb = ci >> 4, ci & (PEER_TOPK - 1)
        e1 = jnp.zeros_like(ci)
        e2 = jnp.zeros_like(ci)
        for a in range(PEER_TOPK):
            e1 = jnp.where(ca == a, i0[a:a + 1, :], e1)
            e2 = jnp.where(cb == a, i1[a:a + 1, :], e2)
        et_ref[h] = e1 * N_KEYS + e2
        ex = jnp.exp(cv - cv[0:1, :])
        gt_ref[h] = ex / jnp.sum(ex, axis=0, keepdims=True)
        return carry

    lax.fori_loop(0, PEER_HEADS, head, 0)
    tt = et_ref.shape[2]
    et = lax.bitcast_convert_type(et_ref[...].reshape(PEER_PICKS, tt), F32)
    idx_ref[...] = lax.bitcast_convert_type(et.T, jnp.int32)
    gate_ref[...] = gt_ref[...].reshape(PEER_PICKS, tt).T


def _route_call(fo, so, x, mod, g_post, g_pre, wof, wos, wq, keys):
    B, S, _ = x.shape
    tt = min(ROUTE_TILE, S)
    per = S // tt
    row = lambda w: pl.BlockSpec((1, tt, w), lambda b, i: (b, i, 0))
    full = lambda a: pl.BlockSpec(a.shape, lambda b, i: (0,) * a.ndim)
    tok = pl.BlockSpec((tt, PEER_PICKS), lambda b, i: (b * per + i, 0))
    return pl.pallas_call(
        _route_kernel,
        grid=(B, per),
        in_specs=[row(_NQ), row(_NS), row(D_MODEL),
                  pl.BlockSpec((1, N_MOD, D_MODEL), lambda b, i: (b, 0, 0)),
                  full(g_post), full(g_pre), full(wof), full(wos), full(wq), full(keys)],
        out_specs=[row(D_MODEL), row(D_MODEL), tok, tok],
        out_shape=[jax.ShapeDtypeStruct((B, S, D_MODEL), F32),
                   jax.ShapeDtypeStruct((B, S, D_MODEL), F32),
                   jax.ShapeDtypeStruct((B * S, PEER_PICKS), jnp.int32),
                   jax.ShapeDtypeStruct((B * S, PEER_PICKS), F32)],
        scratch_shapes=[pltpu.VMEM((2 * PEER_HEADS, tt, PEER_HALF), BF16),
                        pltpu.VMEM((2 * PEER_HEADS, PEER_TOPK, tt), F32),
                        pltpu.VMEM((2 * PEER_HEADS, PEER_TOPK, tt), jnp.int32),
                        pltpu.VMEM((PEER_HEADS, PEER_TOPK, tt), jnp.int32),
                        pltpu.VMEM((PEER_HEADS, PEER_TOPK, tt), F32)],
        compiler_params=_params("arbitrary", "arbitrary"),
        name="route",
    )(fo, so, x, mod, g_post, g_pre, wof, wos, wq, keys)


def _peer_consts():
    half = TABLE_ROWS // 2
    gsum = np.zeros((PEER_GROUP, PEER_GROUP * TABLE_ROWS), np.float32)
    ev = np.zeros((PEER_PICKS, TABLE_COLS), np.float32)
    for j in range(PEER_GROUP):
        gsum[j, j * TABLE_ROWS:j * TABLE_ROWS + half] = 1.0
    for k in range(PEER_PICKS):
        ev[k, k * TABLE_ROWS + half:(k + 1) * TABLE_ROWS] = 1.0
    return jnp.asarray(gsum, BF16), jnp.asarray(ev, BF16)


def _peer_kernel(idx_hbm, tab_hbm, hs_ref, g_ref, gsum_ref, ev_ref, y_ref,
                 buf0, buf1, ib_a, ib_b, wr_ref, sem_g, sem_i):
    s = pl.program_id(0)
    last = pl.num_programs(0) - 1
    tt = PEER_TILE
    npt = tt * PEER_PICKS
    half = TABLE_ROWS // 2
    ngrp = PEER_PICKS // PEER_GROUP
    grows = PEER_GROUP * TABLE_ROWS

    def idx_fetch(tile, ib, sem):
        return pltpu.make_async_copy(idx_hbm.at[pl.ds(tile * npt, npt)], ib, sem)

    def rows_done(buf, sem):
        return pltpu.make_async_copy(tab_hbm.at[pl.ds(0, npt)], buf, sem)

    def issue_rows(ib, buf, sem):
        for i in range(npt):
            pltpu.make_async_copy(tab_hbm.at[ib[i]], buf.at[i], sem).start(priority=i % 2)

    @pl.when(s == 0)
    def _():
        first = idx_fetch(0, ib_a, sem_i.at[1])
        first.start()
        first.wait()

        def body(i, carry):
            pltpu.make_async_copy(tab_hbm.at[ib_a[i]], buf0.at[i], sem_g.at[0]).start()
            return carry

        lax.fori_loop(0, npt, body, 0)
        idx_fetch(1, ib_b, sem_i.at[0]).start()

    row8 = lax.broadcasted_iota(jnp.int32, (half, TABLE_COLS), 0)
    col8 = lax.broadcasted_iota(jnp.int32, (half, TABLE_COLS), 1) % TABLE_ROWS
    mask_v = col8 == row8 + half
    lane = lax.broadcasted_iota(jnp.int32, (PEER_PICKS, HEAD_PAD), 1)
    gsum = gsum_ref[...]

    def evaluate(cur, tok0):
        def ubody(it, at):
            for u in range(PEER_UNROLL):
                t = it * PEER_UNROLL + u
                hrow = hs_ref[tok0 + t]
                h16 = jnp.concatenate([hrow, jnp.zeros_like(hrow)], axis=0).astype(BF16)
                tw = cur[pl.ds(t * PEER_PICKS, PEER_PICKS)]
                prod = (tw * h16[None]).reshape(TABLE_COLS, HEAD_PAD)
                parts = [_dot(gsum, prod[g * grows:(g + 1) * grows]) for g in range(ngrp)]
                z = jnp.sum(jnp.concatenate(parts, axis=0), axis=1, keepdims=True)
                at = jnp.where(lane == t, z, at)
            return at

        at = lax.fori_loop(0, tt // PEER_UNROLL, ubody, jnp.zeros((PEER_PICKS, HEAD_PAD), F32))
        a = at.T[:tt]
        w = jax.nn.gelu(a) * g_ref[tok0:tok0 + tt, :]
        wr_ref[...] = _dot(w.astype(BF16), ev_ref[...])

        def vbody(it, carry):
            for u in range(PEER_UNROLL):
                t = it * PEER_UNROLL + u
                wrow = jnp.broadcast_to(wr_ref[pl.ds(t, 1), :], (half, TABLE_COLS))
                wexp = jnp.where(mask_v, wrow, 0.0).astype(BF16)
                wb = cur[pl.ds(t * PEER_PICKS, PEER_PICKS)].reshape(TABLE_COLS, HEAD_PAD)
                y_ref[tok0 + t] = _dot(wexp, wb)
            return carry

        lax.fori_loop(0, tt // PEER_UNROLL, vbody, 0)

    idx_fetch(2 * s + 1, ib_b, sem_i.at[0]).wait()
    idx_fetch(2 * s + 2, ib_a, sem_i.at[1]).start()

    @pl.when(s >= 0)
    def _():
        issue_rows(ib_b, buf1, sem_g.at[1])

    rows_done(buf0, sem_g.at[0]).wait()
    evaluate(buf0, 0)

    idx_fetch(2 * s + 2, ib_a, sem_i.at[1]).wait()

    @pl.when(s < last)
    def _():
        idx_fetch(2 * s + 3, ib_b, sem_i.at[0]).start()

    @pl.when(s >= 0)
    def _():
        issue_rows(ib_a, buf0, sem_g.at[0])

    rows_done(buf1, sem_g.at[1]).wait()
    evaluate(buf1, tt)

    @pl.when(s == last)
    def _():
        rows_done(buf0, sem_g.at[0]).wait()


def _peer_call(idx_flat, table, hs3, gates):
    T = hs3.shape[0]
    tt = PEER_TILE
    npt = tt * PEER_PICKS
    assert tt % PEER_UNROLL == 0 and PEER_PICKS % PEER_GROUP == 0 and tt <= HEAD_PAD
    gsum, ev = _peer_consts()
    idx_pad = jnp.pad(idx_flat, (0, npt))
    full = lambda a: pl.BlockSpec(a.shape, lambda s: (0,) * a.ndim)
    return pl.pallas_call(
        _peer_kernel,
        grid=(T // (2 * tt),),
        in_specs=[pl.BlockSpec(memory_space=pl.ANY),
                  pl.BlockSpec(memory_space=pl.ANY),
                  pl.BlockSpec((2 * tt, 8, HEAD_PAD), lambda s: (s, 0, 0)),
                  pl.BlockSpec((2 * tt, PEER_PICKS), lambda s: (s, 0)),
                  full(gsum), full(ev)],
        out_specs=pl.BlockSpec((2 * tt, 8, HEAD_PAD), lambda s: (s, 0, 0)),
        out_shape=jax.ShapeDtypeStruct((T, 8, HEAD_PAD), F32),
        scratch_shapes=[pltpu.VMEM((npt, TABLE_ROWS, HEAD_PAD), BF16),
                        pltpu.VMEM((npt, TABLE_ROWS, HEAD_PAD), BF16),
                        pltpu.SMEM((npt,), jnp.int32),
                        pltpu.SMEM((npt,), jnp.int32),
                        pltpu.VMEM((tt, TABLE_COLS), F32),
                        pltpu.SemaphoreType.DMA((2,)),
                        pltpu.SemaphoreType.DMA((2,))],
        compiler_params=_params("arbitrary"),
        name="peer",
    )(idx_pad, table, hs3, gates, gsum, ev)


def _final_kernel(x1_ref, y_ref, mod_ref, g_ref, o_ref):
    gt2 = mod_ref[0, 5:6, :]
    o_ref[0] = x1_ref[0] + gt2 * (_rms(y_ref[0]) * g_ref[...])


def _final_call(x1, y, mod, g_post):
    B, S, _ = x1.shape
    tr = min(ROW_TILE, S)
    row = pl.BlockSpec((1, tr, D_MODEL), lambda b, i: (b, i, 0))
    return pl.pallas_call(
        _final_kernel,
        grid=(B, S // tr),
        in_specs=[row, row, pl.BlockSpec((1, N_MOD, D_MODEL), lambda b, i: (b, 0, 0)),
                  pl.BlockSpec((1, D_MODEL), lambda b, i: (0, 0))],
        out_specs=row,
        out_shape=jax.ShapeDtypeStruct((B, S, D_MODEL), F32),
        compiler_params=_params("arbitrary", "arbitrary"),
        name="final",
    )(x1, y, mod, g_post)


def _pad_heads_cols(w, nh, scale=1.0):
    k = w.shape[0]
    w = (w * scale).reshape(k, nh, HEAD_DIM)
    return jnp.pad(w, ((0, 0), (0, 0), (0, HEAD_PAD - HEAD_DIM))).reshape(k, nh * HEAD_PAD)


def _pad_heads_rows(w, nh):
    n = w.shape[1]
    w = w.reshape(nh, HEAD_DIM, n)
    return jnp.pad(w, ((0, 0), (0, HEAD_PAD - HEAD_DIM), (0, 0))).reshape(nh * HEAD_PAD, n)


def _layer(x, c8, w_ada, b_ada, g_pre_mix, g_post_mix, g_pre_ffn, g_post_ffn,
           w_in, b_fgate, swa_sinks, w_out, w_query, sub_keys, w_u, w_v):
    B, S, D = x.shape
    T = B * S
    scale = HEAD_DIM ** -0.5
    mod = _ada_call(c8, w_ada, b_ada)[:B].reshape(B, N_MOD, D)

    o = 0
    parts = []
    for nh, sc in ((FOX_HEADS, scale), (FOX_HEADS, 1.0), (FOX_HEADS, 1.0)):
        parts.append(_pad_heads_cols(w_in[:, o:o + nh * HEAD_DIM], nh, sc))
        o += nh * HEAD_DIM
    parts.append(jnp.pad(w_in[:, o:o + FOX_HEADS], ((0, 0), (0, HEAD_PAD - FOX_HEADS))))
    o += FOX_HEADS
    for nh, sc in ((SWA_HEADS, scale), (SWA_KV_HEADS, 1.0), (SWA_KV_HEADS, 1.0)):
        parts.append(_pad_heads_cols(w_in[:, o:o + nh * HEAD_DIM], nh, sc))
        o += nh * HEAD_DIM
    w_all = jnp.concatenate(parts, axis=1).astype(BF16)
    bf_pad = jnp.pad(b_fgate, (0, HEAD_PAD - FOX_HEADS)).reshape(1, HEAD_PAD)

    qt, kp, vt, sq, sk, sv = _inproj_call(x, mod, g_pre_mix.reshape(1, D), w_all, bf_pad)
    fo = _fox_call(qt, kp, vt)
    so = _swa_call(swa_sinks, sq, sk, sv)

    nf = FOX_HEADS * HEAD_DIM
    wof = _pad_heads_rows(w_out[:nf], FOX_HEADS).astype(BF16)
    wos = _pad_heads_rows(w_out[nf:], SWA_HEADS).astype(BF16)
    keys = sub_keys.reshape(2 * PEER_HEADS, N_KEYS, PEER_HALF).astype(BF16)
    x1, h2, idx, gates = _route_call(fo, so, x, mod, g_post_mix.reshape(1, D), g_pre_ffn.reshape(1, D),
                                     wof, wos, w_query.astype(BF16), keys)

    table = jnp.concatenate([w_u.reshape(N_EXPERTS, 8, HEAD_PAD), w_v.reshape(N_EXPERTS, 8, HEAD_PAD)],
                            axis=1).astype(BF16)
    y3 = _peer_call(idx.reshape(T * PEER_PICKS), table, h2.reshape(T, 8, HEAD_PAD), gates)
    return _final_call(x1, y3.reshape(B, S, D), mod, g_post_ffn.reshape(1, D))


def kernel(x, c, w_ada, b_ada, g_pre_mix, g_post_mix, g_pre_ffn, g_post_ffn, w_in, b_fgate, swa_sinks, w_out,
           w_query, sub_keys, w_u, w_v):
    B = x.shape[0]
    c8 = jnp.pad(c, ((0, 8 - B), (0, 0)))
    for l in range(w_ada.shape[0]):
        x = _layer(x, c8, w_ada[l], b_ada[l], g_pre_mix[l], g_post_mix[l], g_pre_ffn[l], g_post_ffn[l],
                   w_in[l], b_fgate[l], swa_sinks[l], w_out[l], w_query[l], sub_keys[l], w_u[l], w_v[l])
    return x
```

```python
import numpy as np
import jax
import jax.numpy as jnp
from jax import lax
from jax.experimental import pallas as pl
from jax.experimental.pallas import tpu as pltpu

F32 = jnp.float32
BF16 = jnp.bfloat16

D_MODEL = 1024
HEAD_DIM = 64
HEAD_PAD = 128
FOX_HEADS = 8
SWA_HEADS = 8
SWA_KV_HEADS = 2
SWA_GROUP = SWA_HEADS // SWA_KV_HEADS
WINDOW = 128
PEER_HEADS = 8
PEER_HALF = 128
N_KEYS = 128
N_EXPERTS = N_KEYS * N_KEYS
PEER_TOPK = 16
PEER_PICKS = PEER_HEADS * PEER_TOPK
N_MOD = 6
RMS_EPS = 1e-6
NEG_INF = -1e30

_F_LANE = HEAD_DIM
_ONE_LANE = HEAD_DIM

ROW_TILE = 512
FOX_TILE = 512
SWA_TILE = 512
ROUTE_TILE = 256
PEER_TILE = 8
PEER_PHASES = 4
PEER_UNROLL = 8
PEER_CHAIN = 8
PEER_GROUP = 16
TABLE_ROWS = 16
TABLE_COLS = PEER_PICKS * TABLE_ROWS

_VMEM_LIMIT = 56 * 1024 * 1024


def _dot(a, b):
    return jnp.dot(a, b, preferred_element_type=F32)


def _dot_nt(a, b):
    return lax.dot_general(a, b, (((1,), (1,)), ((), ())), preferred_element_type=F32)


def _split3(x):
    hi = x.astype(BF16)
    r = x - hi.astype(F32)
    mid = r.astype(BF16)
    lo = (r - mid.astype(F32)).astype(BF16)
    return hi, mid, lo


def _rms(x):
    return x * lax.rsqrt(jnp.mean(x * x, axis=-1, keepdims=True) + RMS_EPS)


def _params(*sem):
    return pltpu.CompilerParams(dimension_semantics=sem, vmem_limit_bytes=_VMEM_LIMIT)


def _ada_kernel(c_ref, w_ref, b_ref, o_ref):
    c = c_ref[...]
    s = (c * jax.nn.sigmoid(c)).astype(BF16)
    o_ref[...] = _dot(s, w_ref[...].astype(BF16)) + b_ref[...]


def _ada_call(c8, w_ada, b_ada):
    n = w_ada.shape[1]
    tn = 1536
    return pl.pallas_call(
        _ada_kernel,
        grid=(n // tn,),
        in_specs=[pl.BlockSpec((8, D_MODEL), lambda j: (0, 0)),
                  pl.BlockSpec((D_MODEL, tn), lambda j: (0, j)),
                  pl.BlockSpec((1, tn), lambda j: (0, j))],
        out_specs=pl.BlockSpec((8, tn), lambda j: (0, j)),
        out_shape=jax.ShapeDtypeStruct((8, n), F32),
        compiler_params=_params("arbitrary"),
        name="ada",
    )(c8, w_ada, b_ada.reshape(1, n))


_NQ = FOX_HEADS * HEAD_PAD
_NS = SWA_HEADS * HEAD_PAD
_NKV = SWA_KV_HEADS * HEAD_PAD
_IN_COLS = 3 * _NQ + HEAD_PAD + _NS + 2 * _NKV


def _inproj_kernel(x_ref, mod_ref, g_ref, w_ref, bf_ref, tri_ref, pq_ref, pk_ref, cst_ref,
                   qt_ref, kp_ref, vt_ref, sq_ref, sk_ref, sv_ref, carry_ref):
    i = pl.program_id(1)

    @pl.when(i == 0)
    def _():
        carry_ref[...] = jnp.zeros_like(carry_ref)

    x = x_ref[0]
    sh1 = mod_ref[0, 0:1, :]
    sc1 = mod_ref[0, 1:2, :]
    h = _rms(x) * g_ref[...] * (1.0 + sc1) + sh1
    proj = _dot(h.astype(BF16), w_ref[...])

    z = proj[:, 3 * _NQ:3 * _NQ + HEAD_PAD] + bf_ref[...]
    ls = jnp.minimum(z, 0.0) - jnp.log(1.0 + jnp.exp(-jnp.abs(z)))
    tri = tri_ref[...]
    hi, mid, lo = _split3(ls)
    fcum = _dot(tri, hi) + _dot(tri, mid) + _dot(tri, lo) + carry_ref[...]
    carry_ref[...] = fcum[fcum.shape[0] - 1:, :]

    fh, fm, fl = _split3(fcum)
    eq = _dot(fh, pq_ref[0]) + _dot(fm, pq_ref[1]) + _dot(fl, pq_ref[2]) + cst_ref[0:1, :]
    ek = _dot(fh, pk_ref[0]) + _dot(fm, pk_ref[1]) + _dot(fl, pk_ref[2]) + cst_ref[1:2, :]
    qt_ref[0, 0] = (proj[:, 0:_NQ] + eq).T.astype(BF16)
    kp_ref[0] = (proj[:, _NQ:2 * _NQ] + ek).astype(BF16)
    vt_ref[0, 0] = (proj[:, 2 * _NQ:3 * _NQ] + cst_ref[2:3, :]).T.astype(BF16)
    o = 3 * _NQ + HEAD_PAD
    sq_ref[0] = proj[:, o:o + _NS].astype(BF16)
    sk_ref[0] = proj[:, o + _NS:o + _NS + _NKV].astype(BF16)
    sv_ref[0] = proj[:, o + _NS + _NKV:o + _NS + 2 * _NKV].astype(BF16)


def _inproj_consts(tr):
    tri = np.tril(np.ones((tr, tr), np.float32))
    pq = np.zeros((3, HEAD_PAD, _NQ), np.float32)
    pk = np.zeros((3, HEAD_PAD, _NQ), np.float32)
    cst = np.zeros((8, _NQ), np.float32)
    for h in range(FOX_HEADS):
        b = h * HEAD_PAD + _F_LANE
        for j in range(3):
            pq[j, h, b + j] = 1.0
            pk[j, h, b + 3 + j] = -1.0
            cst[0, b + 3 + j] = 1.0
            cst[1, b + j] = 1.0
        cst[2, h * HEAD_PAD + _ONE_LANE] = 1.0
    return (jnp.asarray(tri, BF16), jnp.asarray(pq, BF16), jnp.asarray(pk, BF16), jnp.asarray(cst, F32))


def _inproj_call(x, mod, g_pre, w_all, bf_pad):
    B, S, _ = x.shape
    tr = min(ROW_TILE, S)
    n = S // tr
    tri, pq, pk, cst = _inproj_consts(tr)
    row = lambda w: pl.BlockSpec((1, tr, w), lambda b, i: (b, i, 0))
    slab = pl.BlockSpec((1, 1, _NQ, tr), lambda b, i: (b, i, 0, 0))
    full = lambda a: pl.BlockSpec(a.shape, lambda b, i: (0,) * a.ndim)
    outs = [jax.ShapeDtypeStruct((B, n, _NQ, tr), BF16), jax.ShapeDtypeStruct((B, S, _NQ), BF16),
            jax.ShapeDtypeStruct((B, n, _NQ, tr), BF16), jax.ShapeDtypeStruct((B, S, _NS), BF16),
            jax.ShapeDtypeStruct((B, S, _NKV), BF16), jax.ShapeDtypeStruct((B, S, _NKV), BF16)]
    return pl.pallas_call(
        _inproj_kernel,
        grid=(B, n),
        in_specs=[row(D_MODEL),
                  pl.BlockSpec((1, N_MOD, D_MODEL), lambda b, i: (b, 0, 0)),
                  full(g_pre), full(w_all), full(bf_pad), full(tri), full(pq), full(pk), full(cst)],
        out_specs=[slab, row(_NQ), slab, row(_NS), row(_NKV), row(_NKV)],
        out_shape=outs,
        scratch_shapes=[pltpu.VMEM((1, HEAD_PAD), F32)],
        compiler_params=_params("arbitrary", "arbitrary"),
        name="inproj",
    )(x, mod, g_pre, w_all, bf_pad, tri, pq, pk, cst)


def _fox_kernel(qt_ref, k_ref, vt_ref, o_ref, m_ref, acc_ref):
    i = pl.program_id(2)
    t = k_ref.shape[1] // vt_ref.shape[1]
    qt = qt_ref[0, 0]
    m_ref[...] = jnp.full_like(m_ref, NEG_INF)
    acc_ref[...] = jnp.zeros_like(acc_ref)

    def block(j, masked):
        off = pl.multiple_of(j * t, t)
        k = k_ref[0, pl.ds(off, t), :]
        s = _dot(k, qt)
        if masked:
            r = lax.broadcasted_iota(jnp.int32, s.shape, 0)
            c = lax.broadcasted_iota(jnp.int32, s.shape, 1)
            s = jnp.where(r <= c, s, NEG_INF)
        m_prev = m_ref[...]
        m_new = jnp.maximum(m_prev, jnp.max(s, axis=0, keepdims=True))
        p = jnp.exp(s - m_new)
        acc_ref[...] = jnp.exp(m_prev - m_new) * acc_ref[...] + _dot(vt_ref[0, j], p.astype(BF16))
        m_ref[...] = m_new

    def body(j, carry):
        block(j, False)
        return carry

    lax.fori_loop(0, i, body, 0)
    block(i, True)
    acc = acc_ref[...]
    o_ref[0] = (acc / acc[_ONE_LANE:_ONE_LANE + 1, :]).T.astype(BF16)


def _fox_call(qt, kp, vt):
    B, n, _, t = qt.shape
    S = n * t
    return pl.pallas_call(
        _fox_kernel,
        grid=(B, FOX_HEADS, n),
        in_specs=[pl.BlockSpec((1, 1, HEAD_PAD, t), lambda b, h, i: (b, i, h, 0)),
                  pl.BlockSpec((1, S, HEAD_PAD), lambda b, h, i: (b, 0, h)),
                  pl.BlockSpec((1, n, HEAD_PAD, t), lambda b, h, i: (b, 0, h, 0))],
        out_specs=pl.BlockSpec((1, t, HEAD_PAD), lambda b, h, i: (b, i, h)),
        out_shape=jax.ShapeDtypeStruct((B, S, _NQ), BF16),
        scratch_shapes=[pltpu.VMEM((1, t), F32), pltpu.VMEM((HEAD_PAD, t), F32)],
        compiler_params=_params("arbitrary", "arbitrary", "arbitrary"),
        name="fox",
    )(qt, kp, vt)


def _swa_kernel(sink_ref, q_ref, kc_ref, kp_ref, vc_ref, vp_ref, o_ref):
    i = pl.program_id(1)
    nsub = q_ref.shape[1] // WINDOW
    r = lax.broadcasted_iota(jnp.int32, (WINDOW, 2 * WINDOW), 0)
    j = lax.broadcasted_iota(jnp.int32, (WINDOW, 2 * WINDOW), 1)
    dist = r + WINDOW - j
    valid = (dist >= 0) & (dist < WINDOW)
    distf = dist.astype(F32)
    for qb in range(nsub):
        rows = slice(qb * WINDOW, (qb + 1) * WINDOW)
        if qb == 0:
            ok = valid & ((j >= WINDOW) | (i > 0))
        else:
            ok = valid
        for g in range(SWA_KV_HEADS):
            lanes = slice(g * HEAD_PAD, (g + 1) * HEAD_PAD)
            if qb == 0:
                kprev, vprev = kp_ref[0, :, lanes], vp_ref[0, :, lanes]
            else:
                prev = slice((qb - 1) * WINDOW, qb * WINDOW)
                kprev, vprev = kc_ref[0, prev, lanes], vc_ref[0, prev, lanes]
            kk = jnp.concatenate([kprev, kc_ref[0, rows, lanes]], axis=0)
            vv = jnp.concatenate([vprev, vc_ref[0, rows, lanes]], axis=0)
            for u in range(SWA_GROUP):
                hq = g * SWA_GROUP + u
                slope = 2.0 ** (-8.0 * (hq + 1) / SWA_HEADS)
                q = q_ref[0, rows, hq * HEAD_PAD:(hq + 1) * HEAD_PAD]
                s = _dot_nt(q, kk) - slope * distf
                s = jnp.where(ok, s, NEG_INF)
                sink = sink_ref[hq]
                m = jnp.maximum(jnp.max(s, axis=1, keepdims=True), sink)
                p = jnp.exp(s - m)
                den = jnp.sum(p, axis=1, keepdims=True) + jnp.exp(sink - m)
                o = _dot(p.astype(BF16), vv) / den
                o_ref[0, rows, hq * HEAD_PAD:(hq + 1) * HEAD_PAD] = o.astype(BF16)


def _swa_call(sinks, sq, sk, sv):
    B, S, _ = sq.shape
    t = min(SWA_TILE, S)
    per = t // WINDOW
    cur = lambda w: pl.BlockSpec((1, t, w), lambda b, i: (b, i, 0))
    prv = lambda w: pl.BlockSpec((1, WINDOW, w), lambda b, i: (b, jnp.maximum(i * per - 1, 0), 0))
    return pl.pallas_call(
        _swa_kernel,
        grid=(B, S // t),
        in_specs=[pl.BlockSpec(memory_space=pltpu.SMEM),
                  cur(_NS), cur(_NKV), prv(_NKV), cur(_NKV), prv(_NKV)],
        out_specs=cur(_NS),
        out_shape=jax.ShapeDtypeStruct((B, S, _NS), BF16),
        compiler_params=_params("arbitrary", "arbitrary"),
        name="swa",
    )(sinks, sq, sk, sk, sv, sv)


_NO_ID = 1 << 20


def _topk_rows(s, k, val_ref, idx_ref, ids=None):
    if ids is None:
        ids = lax.broadcasted_iota(jnp.int32, s.shape, 0)
    for r in range(k):
        m = jnp.max(s, axis=0, keepdims=True)
        i = jnp.min(jnp.where(s == m, ids, _NO_ID), axis=0, keepdims=True)
        val_ref[pl.ds(r, 1), :] = m
        idx_ref[pl.ds(r, 1), :] = i
        s = jnp.where(ids == i, -jnp.inf, s)


def _cand_counts():
    return [PEER_TOPK // (a + 1) for a in range(PEER_TOPK)]


_CAND_ROWS = 56


def _route_kernel(fo_ref, so_ref, x_ref, mod_ref, gpost_ref, gpre_ref, wof_ref, wos_ref, wq_ref, keys_ref, cid_ref,
                  x1_ref, h2_ref, idx_ref, gate_ref, qs_ref, sv_ref, si_ref, et_ref, gt_ref, cand_ref):
    x = x_ref[0]
    gt1 = mod_ref[0, 2:3, :]
    sh2 = mod_ref[0, 3:4, :]
    sc2 = mod_ref[0, 4:5, :]
    y = _dot(fo_ref[0], wof_ref[...]) + _dot(so_ref[0], wos_ref[...])
    x1 = x + gt1 * (_rms(y) * gpost_ref[...])
    x1_ref[0] = x1
    h2 = _rms(x1) * gpre_ref[...] * (1.0 + sc2) + sh2
    h2_ref[0] = h2
    qp = _dot(h2.astype(BF16), wq_ref[...])
    nhp = 2 * PEER_HEADS
    for hp in range(nhp):
        qs_ref[hp] = qp[:, hp * PEER_HALF:(hp + 1) * PEER_HALF].astype(BF16)

    def half(hp, carry):
        sc = _dot_nt(keys_ref[hp], qs_ref[hp])
        _topk_rows(sc, PEER_TOPK, sv_ref.at[hp], si_ref.at[hp])
        return carry

    lax.fori_loop(0, nhp, half, 0)

    counts = _cand_counts()
    used = sum(counts)
    cand_ref[pl.ds(used, _CAND_ROWS - used), :] = jnp.full((_CAND_ROWS - used, cand_ref.shape[1]), -jnp.inf, F32)
    cid = cid_ref[...]

    def head(h, carry):
        v0, v1 = sv_ref[2 * h], sv_ref[2 * h + 1]
        i0, i1 = si_ref[2 * h], si_ref[2 * h + 1]
        off = 0
        for a, nb in enumerate(counts):
            cand_ref[pl.ds(off, nb), :] = v0[a:a + 1, :] + v1[0:nb, :]
            off += nb
        _topk_rows(cand_ref[...], PEER_TOPK, gt_ref.at[h], et_ref.at[h], ids=cid)
        cv, ci = gt_ref[h], et_ref[h]
        ca, cb = ci >> 4, ci & (PEER_TOPK - 1)
        e1 = jnp.zeros_like(ci)
        e2 = jnp.zeros_like(ci)
        for a in range(PEER_TOPK):
            e1 = jnp.where(ca == a, i0[a:a + 1, :], e1)
            e2 = jnp.where(cb == a, i1[a:a + 1, :], e2)
        et_ref[h] = e1 * N_KEYS + e2
        ex = jnp.exp(cv - cv[0:1, :])
        gt_ref[h] = ex / jnp.sum(ex, axis=0, keepdims=True)
        return carry

    lax.fori_loop(0, PEER_HEADS, head, 0)
    tt = et_ref.shape[2]
    et = lax.bitcast_convert_type(et_ref[...].reshape(PEER_PICKS, tt), F32)
    idx_ref[...] = lax.bitcast_convert_type(et.T, jnp.int32)
    gate_ref[...] = gt_ref[...].reshape(PEER_PICKS, tt).T


def _route_call(fo, so, x, mod, g_post, g_pre, wof, wos, wq, keys):
    B, S, _ = x.shape
    tt = min(ROUTE_TILE, S)
    per = S // tt
    row = lambda w: pl.BlockSpec((1, tt, w), lambda b, i: (b, i, 0))
    full = lambda a: pl.BlockSpec(a.shape, lambda b, i: (0,) * a.ndim)
    tok = pl.BlockSpec((tt, PEER_PICKS), lambda b, i: (b * per + i, 0))
    flat = [a * PEER_TOPK + b for a, nb in enumerate(_cand_counts()) for b in range(nb)]
    flat += [_NO_ID] * (_CAND_ROWS - len(flat))
    cid = jnp.asarray(np.broadcast_to(np.asarray(flat, np.int32)[:, None], (_CAND_ROWS, tt)))
    return pl.pallas_call(
        _route_kernel,
        grid=(B, per),
        in_specs=[row(_NQ), row(_NS), row(D_MODEL),
                  pl.BlockSpec((1, N_MOD, D_MODEL), lambda b, i: (b, 0, 0)),
                  full(g_post), full(g_pre), full(wof), full(wos), full(wq), full(keys), full(cid)],
        out_specs=[row(D_MODEL), row(D_MODEL), tok, tok],
        out_shape=[jax.ShapeDtypeStruct((B, S, D_MODEL), F32),
                   jax.ShapeDtypeStruct((B, S, D_MODEL), F32),
                   jax.ShapeDtypeStruct((B * S, PEER_PICKS), jnp.int32),
                   jax.ShapeDtypeStruct((B * S, PEER_PICKS), F32)],
        scratch_shapes=[pltpu.VMEM((2 * PEER_HEADS, tt, PEER_HALF), BF16),
                        pltpu.VMEM((2 * PEER_HEADS, PEER_TOPK, tt), F32),
                        pltpu.VMEM((2 * PEER_HEADS, PEER_TOPK, tt), jnp.int32),
                        pltpu.VMEM((PEER_HEADS, PEER_TOPK, tt), jnp.int32),
                        pltpu.VMEM((PEER_HEADS, PEER_TOPK, tt), F32),
                        pltpu.VMEM((_CAND_ROWS, tt), F32)],
        compiler_params=_params("arbitrary", "arbitrary"),
        name="route",
    )(fo, so, x, mod, g_post, g_pre, wof, wos, wq, keys, cid)


def _peer_consts():
    half = TABLE_ROWS // 2
    gsum = np.zeros((PEER_GROUP, PEER_GROUP * TABLE_ROWS), np.float32)
    ev = np.zeros((PEER_PICKS, TABLE_COLS), np.float32)
    for j in range(PEER_GROUP):
        gsum[j, j * TABLE_ROWS:j * TABLE_ROWS + half] = 1.0
    for k in range(PEER_PICKS):
        ev[k, k * TABLE_ROWS + half:(k + 1) * TABLE_ROWS] = 1.0
    return jnp.asarray(gsum, BF16), jnp.asarray(ev, BF16)


def _peer_kernel(idx_hbm, tab_hbm, hs_ref, g_ref, gsum_ref, ev_ref, y_ref,
                 buf0, buf1, buf2, buf3, ib0, ib1, wr_ref, sem_g, sem_i):
    s = pl.program_id(0)
    last = pl.num_programs(0) - 1
    tt = PEER_TILE
    npt = tt * PEER_PICKS
    half = TABLE_ROWS // 2
    ngrp = PEER_PICKS // PEER_GROUP
    grows = PEER_GROUP * TABLE_ROWS
    bufs = (buf0, buf1, buf2, buf3)
    ibs = (ib0, ib1)

    def idx_fetch(tile, j):
        return pltpu.make_async_copy(idx_hbm.at[pl.ds(tile * npt, npt)], ibs[j], sem_i.at[j])

    def rows_done(j):
        return pltpu.make_async_copy(tab_hbm.at[pl.ds(0, npt)], bufs[j], sem_g.at[j])

    def issue_rows(ib, buf, sem):
        c = ib[0] >> 31
        for i in range(npt):
            e = ib[i] + c
            pltpu.make_async_copy(tab_hbm.at[e], buf.at[i], sem).start(priority=i % 2)
            if i % PEER_CHAIN == PEER_CHAIN - 1:
                c = e >> 31

    @pl.when(s == 0)
    def _():
        for j in range(2):
            first = idx_fetch(j, j)
            first.start()
            first.wait()

            def body(i, carry):
                pltpu.make_async_copy(tab_hbm.at[ibs[j][i]], bufs[j].at[i], sem_g.at[j]).start()
                return carry

            lax.fori_loop(0, npt, body, 0)
        idx_fetch(2, 0).start()

    row8 = lax.broadcasted_iota(jnp.int32, (half, TABLE_COLS), 0)
    col8 = lax.broadcasted_iota(jnp.int32, (half, TABLE_COLS), 1) % TABLE_ROWS
    mask_v = col8 == row8 + half
    lane = lax.broadcasted_iota(jnp.int32, (PEER_PICKS, HEAD_PAD), 1)
    gsum = gsum_ref[...]

    def evaluate(cur, tok0):
        def ubody(it, at):
            for u in range(PEER_UNROLL):
                t = it * PEER_UNROLL + u
                hrow = hs_ref[tok0 + t]
                h16 = jnp.concatenate([hrow, jnp.zeros_like(hrow)], axis=0).astype(BF16)
                tw = cur[pl.ds(t * PEER_PICKS, PEER_PICKS)]
                prod = (tw * h16[None]).reshape(TABLE_COLS, HEAD_PAD)
                parts = [_dot(gsum, prod[g * grows:(g + 1) * grows]) for g in range(ngrp)]
                z = jnp.sum(jnp.concatenate(parts, axis=0), axis=1, keepdims=True)
                at = jnp.where(lane == t, z, at)
            return at

        at = jnp.zeros((PEER_PICKS, HEAD_PAD), F32)
        for it in range(tt // PEER_UNROLL):
            at = ubody(it, at)
        a = at.T[:tt]
        w = jax.nn.gelu(a) * g_ref[tok0:tok0 + tt, :]
        wr_ref[...] = _dot(w.astype(BF16), ev_ref[...])

        def vbody(it, carry):
            for u in range(PEER_UNROLL):
                t = it * PEER_UNROLL + u
                wrow = jnp.broadcast_to(wr_ref[pl.ds(t, 1), :], (half, TABLE_COLS))
                wexp = jnp.where(mask_v, wrow, 0.0).astype(BF16)
                wb = cur[pl.ds(t * PEER_PICKS, PEER_PICKS)].reshape(TABLE_COLS, HEAD_PAD)
                y_ref[tok0 + t] = _dot(wexp, wb)
            return carry

        for it in range(tt // PEER_UNROLL):
            vbody(it, 0)

    for p in range(PEER_PHASES):
        k = PEER_PHASES * s + p
        idx_fetch(k + 2, p % 2).wait()
        idx_fetch(k + 3, (p + 1) % 2).start()
        rows_done(p).wait()
        issue_rows(ibs[p % 2], bufs[(p + 2) % PEER_PHASES], sem_g.at[(p + 2) % PEER_PHASES])
        evaluate(bufs[p], p * tt)

    @pl.when(s == last)
    def _():
        rows_done(0).wait()
        rows_done(1).wait()
        idx_fetch(0, 0).wait()


def _peer_call(idx_flat, table, hs3, gates):
    T = hs3.shape[0]
    tt = PEER_TILE
    npt = tt * PEER_PICKS
    assert tt % PEER_UNROLL == 0 and PEER_PICKS % PEER_GROUP == 0 and tt <= HEAD_PAD and PEER_PHASES == 4
    gsum, ev = _peer_consts()
    idx_pad = jnp.pad(idx_flat, (0, 3 * npt))
    step = PEER_PHASES * tt
    full = lambda a: pl.BlockSpec(a.shape, lambda s: (0,) * a.ndim)
    rows = pltpu.VMEM((npt, TABLE_ROWS, HEAD_PAD), BF16)
    return pl.pallas_call(
        _peer_kernel,
        grid=(T // step,),
        in_specs=[pl.BlockSpec(memory_space=pl.ANY),
                  pl.BlockSpec(memory_space=pl.ANY),
                  pl.BlockSpec((step, 8, HEAD_PAD), lambda s: (s, 0, 0)),
                  pl.BlockSpec((step, PEER_PICKS), lambda s: (s, 0)),
                  full(gsum), full(ev)],
        out_specs=pl.BlockSpec((step, 8, HEAD_PAD), lambda s: (s, 0, 0)),
        out_shape=jax.ShapeDtypeStruct((T, 8, HEAD_PAD), F32),
        scratch_shapes=[rows, rows, rows, rows,
                        pltpu.SMEM((npt,), jnp.int32),
                        pltpu.SMEM((npt,), jnp.int32),
                        pltpu.VMEM((tt, TABLE_COLS), F32),
                        pltpu.SemaphoreType.DMA((PEER_PHASES,)),
                        pltpu.SemaphoreType.DMA((2,))],
        compiler_params=_params("arbitrary"),
        name="peer",
    )(idx_pad, table, hs3, gates, gsum, ev)


def _final_kernel(x1_ref, y_ref, mod_ref, g_ref, o_ref):
    gt2 = mod_ref[0, 5:6, :]
    o_ref[0] = x1_ref[0] + gt2 * (_rms(y_ref[0]) * g_ref[...])


def _final_call(x1, y, mod, g_post):
    B, S, _ = x1.shape
    tr = min(ROW_TILE, S)
    row = pl.BlockSpec((1, tr, D_MODEL), lambda b, i: (b, i, 0))
    return pl.pallas_call(
        _final_kernel,
        grid=(B, S // tr),
        in_specs=[row, row, pl.BlockSpec((1, N_MOD, D_MODEL), lambda b, i: (b, 0, 0)),
                  pl.BlockSpec((1, D_MODEL), lambda b, i: (0, 0))],
        out_specs=row,
        out_shape=jax.ShapeDtypeStruct((B, S, D_MODEL), F32),
        compiler_params=_params("arbitrary", "arbitrary"),
        name="final",
    )(x1, y, mod, g_post)


def _pad_heads_cols(w, nh, scale=1.0):
    k = w.shape[0]
    w = (w * scale).reshape(k, nh, HEAD_DIM)
    return jnp.pad(w, ((0, 0), (0, 0), (0, HEAD_PAD - HEAD_DIM))).reshape(k, nh * HEAD_PAD)


def _pad_heads_rows(w, nh):
    n = w.shape[1]
    w = w.reshape(nh, HEAD_DIM, n)
    return jnp.pad(w, ((0, 0), (0, HEAD_PAD - HEAD_DIM), (0, 0))).reshape(nh * HEAD_PAD, n)


def _layer(x, c8, w_ada, b_ada, g_pre_mix, g_post_mix, g_pre_ffn, g_post_ffn,
           w_in, b_fgate, swa_sinks, w_out, w_query, sub_keys, w_u, w_v):
    B, S, D = x.shape
    T = B * S
    scale = HEAD_DIM ** -0.5
    mod = _ada_call(c8, w_ada, b_ada)[:B].reshape(B, N_MOD, D)

    o = 0
    parts = []
    for nh, sc in ((FOX_HEADS, scale), (FOX_HEADS, 1.0), (FOX_HEADS, 1.0)):
        parts.append(_pad_heads_cols(w_in[:, o:o + nh * HEAD_DIM], nh, sc))
        o += nh * HEAD_DIM
    parts.append(jnp.pad(w_in[:, o:o + FOX_HEADS], ((0, 0), (0, HEAD_PAD - FOX_HEADS))))
    o += FOX_HEADS
    for nh, sc in ((SWA_HEADS, scale), (SWA_KV_HEADS, 1.0), (SWA_KV_HEADS, 1.0)):
        parts.append(_pad_heads_cols(w_in[:, o:o + nh * HEAD_DIM], nh, sc))
        o += nh * HEAD_DIM
    w_all = jnp.concatenate(parts, axis=1).astype(BF16)
    bf_pad = jnp.pad(b_fgate, (0, HEAD_PAD - FOX_HEADS)).reshape(1, HEAD_PAD)

    qt, kp, vt, sq, sk, sv = _inproj_call(x, mod, g_pre_mix.reshape(1, D), w_all, bf_pad)
    fo = _fox_call(qt, kp, vt)
    so = _swa_call(swa_sinks, sq, sk, sv)

    nf = FOX_HEADS * HEAD_DIM
    wof = _pad_heads_rows(w_out[:nf], FOX_HEADS).astype(BF16)
    wos = _pad_heads_rows(w_out[nf:], SWA_HEADS).astype(BF16)
    keys = sub_keys.reshape(2 * PEER_HEADS, N_KEYS, PEER_HALF).astype(BF16)
    x1, h2, idx, gates = _route_call(fo, so, x, mod, g_post_mix.reshape(1, D), g_pre_ffn.reshape(1, D),
                                     wof, wos, w_query.astype(BF16), keys)

    table = jnp.concatenate([w_u.reshape(N_EXPERTS, 8, HEAD_PAD), w_v.reshape(N_EXPERTS, 8, HEAD_PAD)],
                            axis=1).astype(BF16)
    y3 = _peer_call(idx.reshape(T * PEER_PICKS), table, h2.reshape(T, 8, HEAD_PAD), gates)
    return _final_call(x1, y3.reshape(B, S, D), mod, g_post_ffn.reshape(1, D))


def kernel(x, c, w_ada, b_ada, g_pre_mix, g_post_mix, g_pre_ffn, g_post_ffn, w_in, b_fgate, swa_sinks, w_out,
           w_query, sub_keys, w_u, w_v):
    B = x.shape[0]
    c8 = jnp.pad(c, ((0, 8 - B), (0, 0)))
    for l in range(w_ada.shape[0]):
        x = _layer(x, c8, w_ada[l], b_ada[l], g_pre_mix[l], g_post_mix[l], g_pre_ffn[l], g_post_ffn[l],
                   w_in[l], b_fgate[l], swa_sinks[l], w_out[l], w_query[l], sub_keys[l], w_u[l], w_v[l])
    return x
```

```python
import numpy as np
import jax
import jax.numpy as jnp
from jax import lax
from jax.experimental import pallas as pl
from jax.experimental.pallas import tpu as pltpu

F32 = jnp.float32
BF16 = jnp.bfloat16

D_MODEL = 1024
HEAD_DIM = 64
HEAD_PAD = 128
FOX_HEADS = 8
SWA_HEADS = 8
SWA_KV_HEADS = 2
SWA_GROUP = SWA_HEADS // SWA_KV_HEADS
WINDOW = 128
PEER_HEADS = 8
PEER_HALF = 128
N_KEYS = 128
N_EXPERTS = N_KEYS * N_KEYS
PEER_TOPK = 16
PEER_PICKS = PEER_HEADS * PEER_TOPK
N_MOD = 6
RMS_EPS = 1e-6
NEG_INF = -1e30

_F_LANE = HEAD_DIM
_ONE_LANE = HEAD_DIM

ROW_TILE = 1024
FOX_TILE = 1024
FOX_PAR = 1
SWA_TILE = 512
ROUTE_TILE = 256
PEER_TILE = 8
PEER_PHASES = 4
PEER_UNROLL = 8
PEER_CHAIN = 8
PEER_GROUP = 16
TABLE_ROWS = 16
TABLE_COLS = PEER_PICKS * TABLE_ROWS

_VMEM_LIMIT = 56 * 1024 * 1024


def _dot(a, b):
    return jnp.dot(a, b, preferred_element_type=F32)


def _dot_nt(a, b):
    return lax.dot_general(a, b, (((1,), (1,)), ((), ())), preferred_element_type=F32)


def _split3(x):
    hi = x.astype(BF16)
    r = x - hi.astype(F32)
    mid = r.astype(BF16)
    lo = (r - mid.astype(F32)).astype(BF16)
    return hi, mid, lo


def _rms(x):
    return x * lax.rsqrt(jnp.mean(x * x, axis=-1, keepdims=True) + RMS_EPS)


def _params(*sem):
    return pltpu.CompilerParams(dimension_semantics=sem, vmem_limit_bytes=_VMEM_LIMIT)


def _ada_kernel(c_ref, w_ref, b_ref, o_ref):
    c = c_ref[...]
    s = (c * jax.nn.sigmoid(c)).astype(BF16)
    o_ref[...] = _dot(s, w_ref[...].astype(BF16)) + b_ref[...]


def _ada_call(c8, w_ada, b_ada):
    n = w_ada.shape[1]
    tn = 1536
    return pl.pallas_call(
        _ada_kernel,
        grid=(n // tn,),
        in_specs=[pl.BlockSpec((8, D_MODEL), lambda j: (0, 0)),
                  pl.BlockSpec((D_MODEL, tn), lambda j: (0, j)),
                  pl.BlockSpec((1, tn), lambda j: (0, j))],
        out_specs=pl.BlockSpec((8, tn), lambda j: (0, j)),
        out_shape=jax.ShapeDtypeStruct((8, n), F32),
        compiler_params=_params("arbitrary"),
        name="ada",
    )(c8, w_ada, b_ada.reshape(1, n))


_NQ = FOX_HEADS * HEAD_PAD
_NS = SWA_HEADS * HEAD_PAD
_NKV = SWA_KV_HEADS * HEAD_PAD
_IN_COLS = 3 * _NQ + HEAD_PAD + _NS + 2 * _NKV


def _inproj_kernel(x_ref, mod_ref, g_ref, w_ref, bf_ref, tri_ref, pq_ref, pk_ref, cst_ref,
                   qt_ref, kp_ref, vt_ref, sq_ref, sk_ref, sv_ref, carry_ref):
    i = pl.program_id(1)

    @pl.when(i == 0)
    def _():
        carry_ref[...] = jnp.zeros_like(carry_ref)

    x = x_ref[0]
    sh1 = mod_ref[0, 0:1, :]
    sc1 = mod_ref[0, 1:2, :]
    h = _rms(x) * g_ref[...] * (1.0 + sc1) + sh1
    proj = _dot(h.astype(BF16), w_ref[...])

    z = proj[:, 3 * _NQ:3 * _NQ + HEAD_PAD] + bf_ref[...]
    ls = jnp.minimum(z, 0.0) - jnp.log(1.0 + jnp.exp(-jnp.abs(z)))
    tri = tri_ref[...]
    hi, mid, lo = _split3(ls)
    fcum = _dot(tri, hi) + _dot(tri, mid) + _dot(tri, lo) + carry_ref[...]
    carry_ref[...] = fcum[fcum.shape[0] - 1:, :]

    fh, fm, fl = _split3(fcum)
    eq = _dot(fh, pq_ref[0]) + _dot(fm, pq_ref[1]) + _dot(fl, pq_ref[2]) + cst_ref[0:1, :]
    ek = _dot(fh, pk_ref[0]) + _dot(fm, pk_ref[1]) + _dot(fl, pk_ref[2]) + cst_ref[1:2, :]
    qt_ref[0, 0] = (proj[:, 0:_NQ] + eq).T.astype(BF16)
    kp_ref[0] = (proj[:, _NQ:2 * _NQ] + ek).astype(BF16)
    vt_ref[0, 0] = (proj[:, 2 * _NQ:3 * _NQ] + cst_ref[2:3, :]).T.astype(BF16)
    o = 3 * _NQ + HEAD_PAD
    sq_ref[0] = proj[:, o:o + _NS].astype(BF16)
    sk_ref[0] = proj[:, o + _NS:o + _NS + _NKV].astype(BF16)
    sv_ref[0] = proj[:, o + _NS + _NKV:o + _NS + 2 * _NKV].astype(BF16)


def _inproj_consts(tr):
    tri = np.tril(np.ones((tr, tr), np.float32))
    pq = np.zeros((3, HEAD_PAD, _NQ), np.float32)
    pk = np.zeros((3, HEAD_PAD, _NQ), np.float32)
    cst = np.zeros((8, _NQ), np.float32)
    for h in range(FOX_HEADS):
        b = h * HEAD_PAD + _F_LANE
        for j in range(3):
            pq[j, h, b + j] = 1.0
            pk[j, h, b + 3 + j] = -1.0
            cst[0, b + 3 + j] = 1.0
            cst[1, b + j] = 1.0
        cst[2, h * HEAD_PAD + _ONE_LANE] = 1.0
    return (jnp.asarray(tri, BF16), jnp.asarray(pq, BF16), jnp.asarray(pk, BF16), jnp.asarray(cst, F32))


def _inproj_call(x, mod, g_pre, w_all, bf_pad):
    B, S, _ = x.shape
    tr = min(ROW_TILE, S)
    n = S // tr
    tri, pq, pk, cst = _inproj_consts(tr)
    row = lambda w: pl.BlockSpec((1, tr, w), lambda b, i: (b, i, 0))
    slab = pl.BlockSpec((1, 1, _NQ, tr), lambda b, i: (b, i, 0, 0))
    full = lambda a: pl.BlockSpec(a.shape, lambda b, i: (0,) * a.ndim)
    outs = [jax.ShapeDtypeStruct((B, n, _NQ, tr), BF16), jax.ShapeDtypeStruct((B, S, _NQ), BF16),
            jax.ShapeDtypeStruct((B, n, _NQ, tr), BF16), jax.ShapeDtypeStruct((B, S, _NS), BF16),
            jax.ShapeDtypeStruct((B, S, _NKV), BF16), jax.ShapeDtypeStruct((B, S, _NKV), BF16)]
    return pl.pallas_call(
        _inproj_kernel,
        grid=(B, n),
        in_specs=[row(D_MODEL),
                  pl.BlockSpec((1, N_MOD, D_MODEL), lambda b, i: (b, 0, 0)),
                  full(g_pre), full(w_all), full(bf_pad), full(tri), full(pq), full(pk), full(cst)],
        out_specs=[slab, row(_NQ), slab, row(_NS), row(_NKV), row(_NKV)],
        out_shape=outs,
        scratch_shapes=[pltpu.VMEM((1, HEAD_PAD), F32)],
        compiler_params=_params("arbitrary", "arbitrary"),
        name="inproj",
    )(x, mod, g_pre, w_all, bf_pad, tri, pq, pk, cst)


def _fox_kernel(qt_ref, k_ref, vt_ref, o_ref, m_ref, acc_ref):
    i = pl.program_id(2)
    t = o_ref.shape[1]
    m_ref[...] = jnp.full_like(m_ref, NEG_INF)
    acc_ref[...] = jnp.zeros_like(acc_ref)

    def block(j, masked):
        off = pl.multiple_of(j * t, t)
        for hh in range(FOX_PAR):
            lanes = slice(hh * HEAD_PAD, (hh + 1) * HEAD_PAD)
            k = k_ref[0, pl.ds(off, t), lanes]
            s = _dot(k, qt_ref[0, 0, lanes, :])
            if masked:
                r = lax.broadcasted_iota(jnp.int32, s.shape, 0)
                c = lax.broadcasted_iota(jnp.int32, s.shape, 1)
                s = jnp.where(r <= c, s, NEG_INF)
            m_prev = m_ref[hh]
            m_new = jnp.maximum(m_prev, jnp.max(s, axis=0, keepdims=True))
            p = jnp.exp(s - m_new)
            acc_ref[hh] = jnp.exp(m_prev - m_new) * acc_ref[hh] + _dot(vt_ref[0, j, lanes, :], p.astype(BF16))
            m_ref[hh] = m_new

    def body(j, carry):
        block(j, False)
        return carry

    lax.fori_loop(0, i, body, 0)
    block(i, True)
    for hh in range(FOX_PAR):
        acc = acc_ref[hh]
        o_ref[0, :, hh * HEAD_PAD:(hh + 1) * HEAD_PAD] = (acc / acc[_ONE_LANE:_ONE_LANE + 1, :]).T.astype(BF16)


def _fox_call(qt, kp, vt):
    B, n, _, t = qt.shape
    S = n * t
    w = FOX_PAR * HEAD_PAD
    return pl.pallas_call(
        _fox_kernel,
        grid=(B, FOX_HEADS // FOX_PAR, n),
        in_specs=[pl.BlockSpec((1, 1, w, t), lambda b, h, i: (b, i, h, 0)),
                  pl.BlockSpec((1, S, w), lambda b, h, i: (b, 0, h)),
                  pl.BlockSpec((1, n, w, t), lambda b, h, i: (b, 0, h, 0))],
        out_specs=pl.BlockSpec((1, t, w), lambda b, h, i: (b, i, h)),
        out_shape=jax.ShapeDtypeStruct((B, S, _NQ), BF16),
        scratch_shapes=[pltpu.VMEM((FOX_PAR, 1, t), F32), pltpu.VMEM((FOX_PAR, HEAD_PAD, t), F32)],
        compiler_params=_params("arbitrary", "arbitrary", "arbitrary"),
        name="fox",
    )(qt, kp, vt)


def _swa_kernel(sink_ref, q_ref, kc_ref, kp_ref, vc_ref, vp_ref, o_ref):
    i = pl.program_id(1)
    nsub = q_ref.shape[1] // WINDOW
    r = lax.broadcasted_iota(jnp.int32, (WINDOW, 2 * WINDOW), 0)
    j = lax.broadcasted_iota(jnp.int32, (WINDOW, 2 * WINDOW), 1)
    dist = r + WINDOW - j
    valid = (dist >= 0) & (dist < WINDOW)
    distf = dist.astype(F32)
    for qb in range(nsub):
        rows = slice(qb * WINDOW, (qb + 1) * WINDOW)
        if qb == 0:
            ok = valid & ((j >= WINDOW) | (i > 0))
        else:
            ok = valid
        for g in range(SWA_KV_HEADS):
            lanes = slice(g * HEAD_PAD, (g + 1) * HEAD_PAD)
            if qb == 0:
                kprev, vprev = kp_ref[0, :, lanes], vp_ref[0, :, lanes]
            else:
                prev = slice((qb - 1) * WINDOW, qb * WINDOW)
                kprev, vprev = kc_ref[0, prev, lanes], vc_ref[0, prev, lanes]
            kk = jnp.concatenate([kprev, kc_ref[0, rows, lanes]], axis=0)
            vv = jnp.concatenate([vprev, vc_ref[0, rows, lanes]], axis=0)
            for u in range(SWA_GROUP):
                hq = g * SWA_GROUP + u
                slope = 2.0 ** (-8.0 * (hq + 1) / SWA_HEADS)
                q = q_ref[0, rows, hq * HEAD_PAD:(hq + 1) * HEAD_PAD]
                s = _dot_nt(q, kk) - slope * distf
                s = jnp.where(ok, s, NEG_INF)
                sink = sink_ref[hq]
                m = jnp.maximum(jnp.max(s, axis=1, keepdims=True), sink)
                p = jnp.exp(s - m)
                den = jnp.sum(p, axis=1, keepdims=True) + jnp.exp(sink - m)
                o = _dot(p.astype(BF16), vv) / den
                o_ref[0, rows, hq * HEAD_PAD:(hq + 1) * HEAD_PAD] = o.astype(BF16)


def _swa_call(sinks, sq, sk, sv):
    B, S, _ = sq.shape
    t = min(SWA_TILE, S)
    per = t // WINDOW
    cur = lambda w: pl.BlockSpec((1, t, w), lambda b, i: (b, i, 0))
    prv = lambda w: pl.BlockSpec((1, WINDOW, w), lambda b, i: (b, jnp.maximum(i * per - 1, 0), 0))
    return pl.pallas_call(
        _swa_kernel,
        grid=(B, S // t),
        in_specs=[pl.BlockSpec(memory_space=pltpu.SMEM),
                  cur(_NS), cur(_NKV), prv(_NKV), cur(_NKV), prv(_NKV)],
        out_specs=cur(_NS),
        out_shape=jax.ShapeDtypeStruct((B, S, _NS), BF16),
        compiler_params=_params("arbitrary", "arbitrary"),
        name="swa",
    )(sinks, sq, sk, sk, sv, sv)


_NO_ID = 1 << 20


def _topk_rows(s, k, val_ref, idx_ref, ids=None):
    if ids is None:
        ids = lax.broadcasted_iota(jnp.int32, s.shape, 0)
    for r in range(k):
        m = jnp.max(s, axis=0, keepdims=True)
        i = jnp.min(jnp.where(s == m, ids, _NO_ID), axis=0, keepdims=True)
        val_ref[pl.ds(r, 1), :] = m
        idx_ref[pl.ds(r, 1), :] = i
        s = jnp.where(ids == i, -jnp.inf, s)


def _cand_counts():
    return [PEER_TOPK // (a + 1) for a in range(PEER_TOPK)]


_CAND_ROWS = 56


def _route_kernel(fo_ref, so_ref, x_ref, mod_ref, gpost_ref, gpre_ref, wof_ref, wos_ref, wq_ref, keys_ref, cid_ref,
                  x1_ref, h2_ref, idx_ref, gate_ref, qs_ref, sv_ref, si_ref, et_ref, gt_ref, cand_ref):
    x = x_ref[0]
    gt1 = mod_ref[0, 2:3, :]
    sh2 = mod_ref[0, 3:4, :]
    sc2 = mod_ref[0, 4:5, :]
    y = _dot(fo_ref[0], wof_ref[...]) + _dot(so_ref[0], wos_ref[...])
    x1 = x + gt1 * (_rms(y) * gpost_ref[...])
    x1_ref[0] = x1
    h2 = _rms(x1) * gpre_ref[...] * (1.0 + sc2) + sh2
    h2_ref[0] = h2
    qp = _dot(h2.astype(BF16), wq_ref[...])
    nhp = 2 * PEER_HEADS
    for hp in range(nhp):
        qs_ref[hp] = qp[:, hp * PEER_HALF:(hp + 1) * PEER_HALF].astype(BF16)

    def half(hp, carry):
        sc = _dot_nt(keys_ref[hp], qs_ref[hp])
        _topk_rows(sc, PEER_TOPK, sv_ref.at[hp], si_ref.at[hp])
        return carry

    lax.fori_loop(0, nhp, half, 0)

    counts = _cand_counts()
    used = sum(counts)
    cand_ref[pl.ds(used, _CAND_ROWS - used), :] = jnp.full((_CAND_ROWS - used, cand_ref.shape[1]), -jnp.inf, F32)
    cid = cid_ref[...]

    def head(h, carry):
        v0, v1 = sv_ref[2 * h], sv_ref[2 * h + 1]
        i0, i1 = si_ref[2 * h], si_ref[2 * h + 1]
        off = 0
        for a, nb in enumerate(counts):
            cand_ref[pl.ds(off, nb), :] = v0[a:a + 1, :] + v1[0:nb, :]
            off += nb
        _topk_rows(cand_ref[...], PEER_TOPK, gt_ref.at[h], et_ref.at[h], ids=cid)
        cv, ci = gt_ref[h], et_ref[h]
        ca, cb = ci >> 4, ci & (PEER_TOPK - 1)
        e1 = jnp.zeros_like(ci)
        e2 = jnp.zeros_like(ci)
        for a in range(PEER_TOPK):
            e1 = jnp.where(ca == a, i0[a:a + 1, :], e1)
            e2 = jnp.where(cb == a, i1[a:a + 1, :], e2)
        et_ref[h] = e1 * N_KEYS + e2
        ex = jnp.exp(cv - cv[0:1, :])
        gt_ref[h] = ex / jnp.sum(ex, axis=0, keepdims=True)
        return carry

    lax.fori_loop(0, PEER_HEADS, head, 0)
    tt = et_ref.shape[2]
    et = lax.bitcast_convert_type(et_ref[...].reshape(PEER_PICKS, tt), F32)
    idx_ref[...] = lax.bitcast_convert_type(et.T, jnp.int32)
    gate_ref[...] = gt_ref[...].reshape(PEER_PICKS, tt).T


def _route_call(fo, so, x, mod, g_post, g_pre, wof, wos, wq, keys):
    B, S, _ = x.shape
    tt = min(ROUTE_TILE, S)
    per = S // tt
    row = lambda w: pl.BlockSpec((1, tt, w), lambda b, i: (b, i, 0))
    full = lambda a: pl.BlockSpec(a.shape, lambda b, i: (0,) * a.ndim)
    tok = pl.BlockSpec((tt, PEER_PICKS), lambda b, i: (b * per + i, 0))
    flat = [a * PEER_TOPK + b for a, nb in enumerate(_cand_counts()) for b in range(nb)]
    flat += [_NO_ID] * (_CAND_ROWS - len(flat))
    cid = jnp.asarray(np.broadcast_to(np.asarray(flat, np.int32)[:, None], (_CAND_ROWS, tt)))
    return pl.pallas_call(
        _route_kernel,
        grid=(B, per),
        in_specs=[row(_NQ), row(_NS), row(D_MODEL),
                  pl.BlockSpec((1, N_MOD, D_MODEL), lambda b, i: (b, 0, 0)),
                  full(g_post), full(g_pre), full(wof), full(wos), full(wq), full(keys), full(cid)],
        out_specs=[row(D_MODEL), row(D_MODEL), tok, tok],
        out_shape=[jax.ShapeDtypeStruct((B, S, D_MODEL), F32),
                   jax.ShapeDtypeStruct((B, S, D_MODEL), F32),
                   jax.ShapeDtypeStruct((B * S, PEER_PICKS), jnp.int32),
                   jax.ShapeDtypeStruct((B * S, PEER_PICKS), F32)],
        scratch_shapes=[pltpu.VMEM((2 * PEER_HEADS, tt, PEER_HALF), BF16),
                        pltpu.VMEM((2 * PEER_HEADS, PEER_TOPK, tt), F32),
                        pltpu.VMEM((2 * PEER_HEADS, PEER_TOPK, tt), jnp.int32),
                        pltpu.VMEM((PEER_HEADS, PEER_TOPK, tt), jnp.int32),
                        pltpu.VMEM((PEER_HEADS, PEER_TOPK, tt), F32),
                        pltpu.VMEM((_CAND_ROWS, tt), F32)],
        compiler_params=_params("arbitrary", "arbitrary"),
        name="route",
    )(fo, so, x, mod, g_post, g_pre, wof, wos, wq, keys, cid)


def _peer_consts():
    half = TABLE_ROWS // 2
    gsum = np.zeros((PEER_GROUP, PEER_GROUP * TABLE_ROWS), np.float32)
    ev = np.zeros((PEER_PICKS, TABLE_COLS), np.float32)
    for j in range(PEER_GROUP):
        gsum[j, j * TABLE_ROWS:j * TABLE_ROWS + half] = 1.0
    for k in range(PEER_PICKS):
        ev[k, k * TABLE_ROWS + half:(k + 1) * TABLE_ROWS] = 1.0
    return jnp.asarray(gsum, BF16), jnp.asarray(ev, BF16)


def _peer_kernel(idx_hbm, tab_hbm, hs_ref, g_ref, gsum_ref, ev_ref, y_ref,
                 buf0, buf1, buf2, buf3, ib0, ib1, wr_ref, sem_g, sem_i):
    s = pl.program_id(0)
    last = pl.num_programs(0) - 1
    tt = PEER_TILE
    npt = tt * PEER_PICKS
    half = TABLE_ROWS // 2
    ngrp = PEER_PICKS // PEER_GROUP
    grows = PEER_GROUP * TABLE_ROWS
    bufs = (buf0, buf1, buf2, buf3)
    ibs = (ib0, ib1)

    def idx_fetch(tile, j):
        return pltpu.make_async_copy(idx_hbm.at[pl.ds(tile * npt, npt)], ibs[j], sem_i.at[j])

    def rows_done(j):
        return pltpu.make_async_copy(tab_hbm.at[pl.ds(0, npt)], bufs[j], sem_g.at[j])

    def issue_rows(ib, buf, sem):
        c = ib[0] >> 31
        for i in range(npt):
            e = ib[i] + c
            pltpu.make_async_copy(tab_hbm.at[e], buf.at[i], sem).start(priority=i % 2)
            if i % PEER_CHAIN == PEER_CHAIN - 1:
                c = e >> 31

    @pl.when(s == 0)
    def _():
        for j in range(2):
            first = idx_fetch(j, j)
            first.start()
            first.wait()

            def body(i, carry):
                pltpu.make_async_copy(tab_hbm.at[ibs[j][i]], bufs[j].at[i], sem_g.at[j]).start()
                return carry

            lax.fori_loop(0, npt, body, 0)
        idx_fetch(2, 0).start()

    row8 = lax.broadcasted_iota(jnp.int32, (half, TABLE_COLS), 0)
    col8 = lax.broadcasted_iota(jnp.int32, (half, TABLE_COLS), 1) % TABLE_ROWS
    mask_v = col8 == row8 + half
    lane = lax.broadcasted_iota(jnp.int32, (PEER_PICKS, HEAD_PAD), 1)
    gsum = gsum_ref[...]

    def evaluate(cur, tok0):
        def ubody(it, at):
            for u in range(PEER_UNROLL):
                t = it * PEER_UNROLL + u
                hrow = hs_ref[tok0 + t]
                h16 = jnp.concatenate([hrow, jnp.zeros_like(hrow)], axis=0).astype(BF16)
                tw = cur[pl.ds(t * PEER_PICKS, PEER_PICKS)]
                prod = (tw * h16[None]).reshape(TABLE_COLS, HEAD_PAD)
                parts = [_dot(gsum, prod[g * grows:(g + 1) * grows]) for g in range(ngrp)]
                z = jnp.sum(jnp.concatenate(parts, axis=0), axis=1, keepdims=True)
                at = jnp.where(lane == t, z, at)
            return at

        at = jnp.zeros((PEER_PICKS, HEAD_PAD), F32)
        for it in range(tt // PEER_UNROLL):
            at = ubody(it, at)
        a = at.T[:tt]
        w = jax.nn.gelu(a) * g_ref[tok0:tok0 + tt, :]
        wr_ref[...] = _dot(w.astype(BF16), ev_ref[...])

        def vbody(it, carry):
            for u in range(PEER_UNROLL):
                t = it * PEER_UNROLL + u
                wrow = jnp.broadcast_to(wr_ref[pl.ds(t, 1), :], (half, TABLE_COLS))
                wexp = jnp.where(mask_v, wrow, 0.0).astype(BF16)
                wb = cur[pl.ds(t * PEER_PICKS, PEER_PICKS)].reshape(TABLE_COLS, HEAD_PAD)
                y_ref[tok0 + t] = _dot(wexp, wb)
            return carry

        for it in range(tt // PEER_UNROLL):
            vbody(it, 0)

    for p in range(PEER_PHASES):
        k = PEER_PHASES * s + p
        idx_fetch(k + 2, p % 2).wait()
        idx_fetch(k + 3, (p + 1) % 2).start()
        rows_done(p).wait()
        issue_rows(ibs[p % 2], bufs[(p + 2) % PEER_PHASES], sem_g.at[(p + 2) % PEER_PHASES])
        evaluate(bufs[p], p * tt)

    @pl.when(s == last)
    def _():
        rows_done(0).wait()
        rows_done(1).wait()
        idx_fetch(0, 0).wait()


def _peer_call(idx_flat, table, hs3, gates):
    T = hs3.shape[0]
    tt = PEER_TILE
    npt = tt * PEER_PICKS
    assert tt % PEER_UNROLL == 0 and PEER_PICKS % PEER_GROUP == 0 and tt <= HEAD_PAD and PEER_PHASES == 4
    gsum, ev = _peer_consts()
    idx_pad = jnp.pad(idx_flat, (0, 3 * npt))
    step = PEER_PHASES * tt
    full = lambda a: pl.BlockSpec(a.shape, lambda s: (0,) * a.ndim)
    rows = pltpu.VMEM((npt, TABLE_ROWS, HEAD_PAD), BF16)
    return pl.pallas_call(
        _peer_kernel,
        grid=(T // step,),
        in_specs=[pl.BlockSpec(memory_space=pl.ANY),
                  pl.BlockSpec(memory_space=pl.ANY),
                  pl.BlockSpec((step, 8, HEAD_PAD), lambda s: (s, 0, 0)),
                  pl.BlockSpec((step, PEER_PICKS), lambda s: (s, 0)),
                  full(gsum), full(ev)],
        out_specs=pl.BlockSpec((step, 8, HEAD_PAD), lambda s: (s, 0, 0)),
        out_shape=jax.ShapeDtypeStruct((T, 8, HEAD_PAD), F32),
        scratch_shapes=[rows, rows, rows, rows,
                        pltpu.SMEM((npt,), jnp.int32),
                        pltpu.SMEM((npt,), jnp.int32),
                        pltpu.VMEM((tt, TABLE_COLS), F32),
                        pltpu.SemaphoreType.DMA((PEER_PHASES,)),
                        pltpu.SemaphoreType.DMA((2,))],
        compiler_params=_params("arbitrary"),
        name="peer",
    )(idx_pad, table, hs3, gates, gsum, ev)


def _final_kernel(x1_ref, y_ref, mod_ref, g_ref, o_ref):
    gt2 = mod_ref[0, 5:6, :]
    o_ref[0] = x1_ref[0] + gt2 * (_rms(y_ref[0]) * g_ref[...])


def _final_call(x1, y, mod, g_post):
    B, S, _ = x1.shape
    tr = min(ROW_TILE, S)
    row = pl.BlockSpec((1, tr, D_MODEL), lambda b, i: (b, i, 0))
    return pl.pallas_call(
        _final_kernel,
        grid=(B, S // tr),
        in_specs=[row, row, pl.BlockSpec((1, N_MOD, D_MODEL), lambda b, i: (b, 0, 0)),
                  pl.BlockSpec((1, D_MODEL), lambda b, i: (0, 0))],
        out_specs=row,
        out_shape=jax.ShapeDtypeStruct((B, S, D_MODEL), F32),
        compiler_params=_params("arbitrary", "arbitrary"),
        name="final",
    )(x1, y, mod, g_post)


def _pad_heads_cols(w, nh, scale=1.0):
    k = w.shape[0]
    w = (w * scale).reshape(k, nh, HEAD_DIM)
    return jnp.pad(w, ((0, 0), (0, 0), (0, HEAD_PAD - HEAD_DIM))).reshape(k, nh * HEAD_PAD)


def _pad_heads_rows(w, nh):
    n = w.shape[1]
    w = w.reshape(nh, HEAD_DIM, n)
    return jnp.pad(w, ((0, 0), (0, HEAD_PAD - HEAD_DIM), (0, 0))).reshape(nh * HEAD_PAD, n)


def _layer(x, c8, w_ada, b_ada, g_pre_mix, g_post_mix, g_pre_ffn, g_post_ffn,
           w_in, b_fgate, swa_sinks, w_out, w_query, sub_keys, w_u, w_v):
    B, S, D = x.shape
    T = B * S
    scale = HEAD_DIM ** -0.5
    mod = _ada_call(c8, w_ada, b_ada)[:B].reshape(B, N_MOD, D)

    o = 0
    parts = []
    for nh, sc in ((FOX_HEADS, scale), (FOX_HEADS, 1.0), (FOX_HEADS, 1.0)):
        parts.append(_pad_heads_cols(w_in[:, o:o + nh * HEAD_DIM], nh, sc))
        o += nh * HEAD_DIM
    parts.append(jnp.pad(w_in[:, o:o + FOX_HEADS], ((0, 0), (0, HEAD_PAD - FOX_HEADS))))
    o += FOX_HEADS
    for nh, sc in ((SWA_HEADS, scale), (SWA_KV_HEADS, 1.0), (SWA_KV_HEADS, 1.0)):
        parts.append(_pad_heads_cols(w_in[:, o:o + nh * HEAD_DIM], nh, sc))
        o += nh * HEAD_DIM
    w_all = jnp.concatenate(parts, axis=1).astype(BF16)
    bf_pad = jnp.pad(b_fgate, (0, HEAD_PAD - FOX_HEADS)).reshape(1, HEAD_PAD)

    qt, kp, vt, sq, sk, sv = _inproj_call(x, mod, g_pre_mix.reshape(1, D), w_all, bf_pad)
    fo = _fox_call(qt, kp, vt)
    so = _swa_call(swa_sinks, sq, sk, sv)

    nf = FOX_HEADS * HEAD_DIM
    wof = _pad_heads_rows(w_out[:nf], FOX_HEADS).astype(BF16)
    wos = _pad_heads_rows(w_out[nf:], SWA_HEADS).astype(BF16)
    keys = sub_keys.reshape(2 * PEER_HEADS, N_KEYS, PEER_HALF).astype(BF16)
    x1, h2, idx, gates = _route_call(fo, so, x, mod, g_post_mix.reshape(1, D), g_pre_ffn.reshape(1, D),
                                     wof, wos, w_query.astype(BF16), keys)

    table = jnp.concatenate([w_u.reshape(N_EXPERTS, 8, HEAD_PAD), w_v.reshape(N_EXPERTS, 8, HEAD_PAD)],
                            axis=1).astype(BF16)
    y3 = _peer_call(idx.reshape(T * PEER_PICKS), table, h2.reshape(T, 8, HEAD_PAD), gates)
    return _final_call(x1, y3.reshape(B, S, D), mod, g_post_ffn.reshape(1, D))


def kernel(x, c, w_ada, b_ada, g_pre_mix, g_post_mix, g_pre_ffn, g_post_ffn, w_in, b_fgate, swa_sinks, w_out,
           w_query, sub_keys, w_u, w_v):
    B = x.shape[0]
    c8 = jnp.pad(c, ((0, 8 - B), (0, 0)))
    for l in range(w_ada.shape[0]):
        x = _layer(x, c8, w_ada[l], b_ada[l], g_pre_mix[l], g_post_mix[l], g_pre_ffn[l], g_post_ffn[l],
                   w_in[l], b_fgate[l], swa_sinks[l], w_out[l], w_query[l], sub_keys[l], w_u[l], w_v[l])
    return x
```

```python
import numpy as np
import jax
import jax.numpy as jnp
from jax import lax
from jax.experimental import pallas as pl
from jax.experimental.pallas import tpu as pltpu
from jax.experimental.pallas import tpu_sc as plsc

F32 = jnp.float32
BF16 = jnp.bfloat16

D_MODEL = 1024
HEAD_DIM = 64
HEAD_PAD = 128
FOX_HEADS = 8
SWA_HEADS = 8
SWA_KV_HEADS = 2
SWA_GROUP = SWA_HEADS // SWA_KV_HEADS
WINDOW = 128
PEER_HEADS = 8
PEER_HALF = 128
N_KEYS = 128
N_EXPERTS = N_KEYS * N_KEYS
PEER_TOPK = 16
PEER_PICKS = PEER_HEADS * PEER_TOPK
N_MOD = 6
RMS_EPS = 1e-6
NEG_INF = -1e30

_F_LANE = HEAD_DIM
_ONE_LANE = HEAD_DIM

ROW_TILE = 1024
FOX_TILE = 1024
FOX_PAR = 1
SWA_TILE = 512
ROUTE_TILE = 256
PEER_TILE = 8
PEER_PHASES = 4
PEER_UNROLL = 8
PEER_CHAIN = 8
PEER_GROUP = 16
SC_SHARE = 0.25
TABLE_ROWS = 16
TABLE_COLS = PEER_PICKS * TABLE_ROWS

_VMEM_LIMIT = 56 * 1024 * 1024


def _dot(a, b):
    return jnp.dot(a, b, preferred_element_type=F32)


def _dot_nt(a, b):
    return lax.dot_general(a, b, (((1,), (1,)), ((), ())), preferred_element_type=F32)


def _split3(x):
    hi = x.astype(BF16)
    r = x - hi.astype(F32)
    mid = r.astype(BF16)
    lo = (r - mid.astype(F32)).astype(BF16)
    return hi, mid, lo


def _rms(x):
    return x * lax.rsqrt(jnp.mean(x * x, axis=-1, keepdims=True) + RMS_EPS)


def _params(*sem):
    return pltpu.CompilerParams(dimension_semantics=sem, vmem_limit_bytes=_VMEM_LIMIT)


def _ada_kernel(c_ref, w_ref, b_ref, o_ref):
    c = c_ref[...]
    s = (c * jax.nn.sigmoid(c)).astype(BF16)
    o_ref[...] = _dot(s, w_ref[...].astype(BF16)) + b_ref[...]


def _ada_call(c8, w_ada, b_ada):
    n = w_ada.shape[1]
    tn = 1536
    return pl.pallas_call(
        _ada_kernel,
        grid=(n // tn,),
        in_specs=[pl.BlockSpec((8, D_MODEL), lambda j: (0, 0)),
                  pl.BlockSpec((D_MODEL, tn), lambda j: (0, j)),
                  pl.BlockSpec((1, tn), lambda j: (0, j))],
        out_specs=pl.BlockSpec((8, tn), lambda j: (0, j)),
        out_shape=jax.ShapeDtypeStruct((8, n), F32),
        compiler_params=_params("arbitrary"),
        name="ada",
    )(c8, w_ada, b_ada.reshape(1, n))


_NQ = FOX_HEADS * HEAD_PAD
_NS = SWA_HEADS * HEAD_PAD
_NKV = SWA_KV_HEADS * HEAD_PAD
_IN_COLS = 3 * _NQ + HEAD_PAD + _NS + 2 * _NKV


def _inproj_kernel(x_ref, mod_ref, g_ref, w_ref, bf_ref, tri_ref, pq_ref, pk_ref, cst_ref,
                   qt_ref, kp_ref, vt_ref, sq_ref, sk_ref, sv_ref, carry_ref):
    i = pl.program_id(1)

    @pl.when(i == 0)
    def _():
        carry_ref[...] = jnp.zeros_like(carry_ref)

    x = x_ref[0]
    sh1 = mod_ref[0, 0:1, :]
    sc1 = mod_ref[0, 1:2, :]
    h = _rms(x) * g_ref[...] * (1.0 + sc1) + sh1
    proj = _dot(h.astype(BF16), w_ref[...])

    z = proj[:, 3 * _NQ:3 * _NQ + HEAD_PAD] + bf_ref[...]
    ls = jnp.minimum(z, 0.0) - jnp.log(1.0 + jnp.exp(-jnp.abs(z)))
    tri = tri_ref[...]
    hi, mid, lo = _split3(ls)
    fcum = _dot(tri, hi) + _dot(tri, mid) + _dot(tri, lo) + carry_ref[...]
    carry_ref[...] = fcum[fcum.shape[0] - 1:, :]

    fh, fm, fl = _split3(fcum)
    eq = _dot(fh, pq_ref[0]) + _dot(fm, pq_ref[1]) + _dot(fl, pq_ref[2]) + cst_ref[0:1, :]
    ek = _dot(fh, pk_ref[0]) + _dot(fm, pk_ref[1]) + _dot(fl, pk_ref[2]) + cst_ref[1:2, :]
    qt_ref[0, 0] = (proj[:, 0:_NQ] + eq).T.astype(BF16)
    kp_ref[0] = (proj[:, _NQ:2 * _NQ] + ek).astype(BF16)
    vt_ref[0, 0] = (proj[:, 2 * _NQ:3 * _NQ] + cst_ref[2:3, :]).T.astype(BF16)
    o = 3 * _NQ + HEAD_PAD
    sq_ref[0] = proj[:, o:o + _NS].astype(BF16)
    sk_ref[0] = proj[:, o + _NS:o + _NS + _NKV].astype(BF16)
    sv_ref[0] = proj[:, o + _NS + _NKV:o + _NS + 2 * _NKV].astype(BF16)


def _inproj_consts(tr):
    tri = np.tril(np.ones((tr, tr), np.float32))
    pq = np.zeros((3, HEAD_PAD, _NQ), np.float32)
    pk = np.zeros((3, HEAD_PAD, _NQ), np.float32)
    cst = np.zeros((8, _NQ), np.float32)
    for h in range(FOX_HEADS):
        b = h * HEAD_PAD + _F_LANE
        for j in range(3):
            pq[j, h, b + j] = 1.0
            pk[j, h, b + 3 + j] = -1.0
            cst[0, b + 3 + j] = 1.0
            cst[1, b + j] = 1.0
        cst[2, h * HEAD_PAD + _ONE_LANE] = 1.0
    return (jnp.asarray(tri, BF16), jnp.asarray(pq, BF16), jnp.asarray(pk, BF16), jnp.asarray(cst, F32))


def _inproj_call(x, mod, g_pre, w_all, bf_pad):
    B, S, _ = x.shape
    tr = min(ROW_TILE, S)
    n = S // tr
    tri, pq, pk, cst = _inproj_consts(tr)
    row = lambda w: pl.BlockSpec((1, tr, w), lambda b, i: (b, i, 0))
    slab = pl.BlockSpec((1, 1, _NQ, tr), lambda b, i: (b, i, 0, 0))
    full = lambda a: pl.BlockSpec(a.shape, lambda b, i: (0,) * a.ndim)
    outs = [jax.ShapeDtypeStruct((B, n, _NQ, tr), BF16), jax.ShapeDtypeStruct((B, S, _NQ), BF16),
            jax.ShapeDtypeStruct((B, n, _NQ, tr), BF16), jax.ShapeDtypeStruct((B, S, _NS), BF16),
            jax.ShapeDtypeStruct((B, S, _NKV), BF16), jax.ShapeDtypeStruct((B, S, _NKV), BF16)]
    return pl.pallas_call(
        _inproj_kernel,
        grid=(B, n),
        in_specs=[row(D_MODEL),
                  pl.BlockSpec((1, N_MOD, D_MODEL), lambda b, i: (b, 0, 0)),
                  full(g_pre), full(w_all), full(bf_pad), full(tri), full(pq), full(pk), full(cst)],
        out_specs=[slab, row(_NQ), slab, row(_NS), row(_NKV), row(_NKV)],
        out_shape=outs,
        scratch_shapes=[pltpu.VMEM((1, HEAD_PAD), F32)],
        compiler_params=_params("arbitrary", "arbitrary"),
        name="inproj",
    )(x, mod, g_pre, w_all, bf_pad, tri, pq, pk, cst)


def _fox_kernel(qt_ref, k_ref, vt_ref, o_ref, m_ref, acc_ref):
    i = pl.program_id(2)
    t = o_ref.shape[1]
    m_ref[...] = jnp.full_like(m_ref, NEG_INF)
    acc_ref[...] = jnp.zeros_like(acc_ref)

    def block(j, masked):
        off = pl.multiple_of(j * t, t)
        for hh in range(FOX_PAR):
            lanes = slice(hh * HEAD_PAD, (hh + 1) * HEAD_PAD)
            k = k_ref[0, pl.ds(off, t), lanes]
            s = _dot(k, qt_ref[0, 0, lanes, :])
            if masked:
                r = lax.broadcasted_iota(jnp.int32, s.shape, 0)
                c = lax.broadcasted_iota(jnp.int32, s.shape, 1)
                s = jnp.where(r <= c, s, NEG_INF)
            m_prev = m_ref[hh]
            m_new = jnp.maximum(m_prev, jnp.max(s, axis=0, keepdims=True))
            p = jnp.exp(s - m_new)
            acc_ref[hh] = jnp.exp(m_prev - m_new) * acc_ref[hh] + _dot(vt_ref[0, j, lanes, :], p.astype(BF16))
            m_ref[hh] = m_new

    def body(j, carry):
        block(j, False)
        return carry

    lax.fori_loop(0, i, body, 0)
    block(i, True)
    for hh in range(FOX_PAR):
        acc = acc_ref[hh]
        o_ref[0, :, hh * HEAD_PAD:(hh + 1) * HEAD_PAD] = (acc / acc[_ONE_LANE:_ONE_LANE + 1, :]).T.astype(BF16)


def _fox_call(qt, kp, vt):
    B, n, _, t = qt.shape
    S = n * t
    w = FOX_PAR * HEAD_PAD
    return pl.pallas_call(
        _fox_kernel,
        grid=(B, FOX_HEADS // FOX_PAR, n),
        in_specs=[pl.BlockSpec((1, 1, w, t), lambda b, h, i: (b, i, h, 0)),
                  pl.BlockSpec((1, S, w), lambda b, h, i: (b, 0, h)),
                  pl.BlockSpec((1, n, w, t), lambda b, h, i: (b, 0, h, 0))],
        out_specs=pl.BlockSpec((1, t, w), lambda b, h, i: (b, i, h)),
        out_shape=jax.ShapeDtypeStruct((B, S, _NQ), BF16),
        scratch_shapes=[pltpu.VMEM((FOX_PAR, 1, t), F32), pltpu.VMEM((FOX_PAR, HEAD_PAD, t), F32)],
        compiler_params=_params("arbitrary", "arbitrary", "arbitrary"),
        name="fox",
    )(qt, kp, vt)


def _swa_kernel(sink_ref, q_ref, kc_ref, kp_ref, vc_ref, vp_ref, o_ref):
    i = pl.program_id(1)
    nsub = q_ref.shape[1] // WINDOW
    r = lax.broadcasted_iota(jnp.int32, (WINDOW, 2 * WINDOW), 0)
    j = lax.broadcasted_iota(jnp.int32, (WINDOW, 2 * WINDOW), 1)
    dist = r + WINDOW - j
    valid = (dist >= 0) & (dist < WINDOW)
    distf = dist.astype(F32)
    for qb in range(nsub):
        rows = slice(qb * WINDOW, (qb + 1) * WINDOW)
        if qb == 0:
            ok = valid & ((j >= WINDOW) | (i > 0))
        else:
            ok = valid
        for g in range(SWA_KV_HEADS):
            lanes = slice(g * HEAD_PAD, (g + 1) * HEAD_PAD)
            if qb == 0:
                kprev, vprev = kp_ref[0, :, lanes], vp_ref[0, :, lanes]
            else:
                prev = slice((qb - 1) * WINDOW, qb * WINDOW)
                kprev, vprev = kc_ref[0, prev, lanes], vc_ref[0, prev, lanes]
            kk = jnp.concatenate([kprev, kc_ref[0, rows, lanes]], axis=0)
            vv = jnp.concatenate([vprev, vc_ref[0, rows, lanes]], axis=0)
            for u in range(SWA_GROUP):
                hq = g * SWA_GROUP + u
                slope = 2.0 ** (-8.0 * (hq + 1) / SWA_HEADS)
                q = q_ref[0, rows, hq * HEAD_PAD:(hq + 1) * HEAD_PAD]
                s = _dot_nt(q, kk) - slope * distf
                s = jnp.where(ok, s, NEG_INF)
                sink = sink_ref[hq]
                m = jnp.maximum(jnp.max(s, axis=1, keepdims=True), sink)
                p = jnp.exp(s - m)
                den = jnp.sum(p, axis=1, keepdims=True) + jnp.exp(sink - m)
                o = _dot(p.astype(BF16), vv) / den
                o_ref[0, rows, hq * HEAD_PAD:(hq + 1) * HEAD_PAD] = o.astype(BF16)


def _swa_call(sinks, sq, sk, sv):
    B, S, _ = sq.shape
    t = min(SWA_TILE, S)
    per = t // WINDOW
    cur = lambda w: pl.BlockSpec((1, t, w), lambda b, i: (b, i, 0))
    prv = lambda w: pl.BlockSpec((1, WINDOW, w), lambda b, i: (b, jnp.maximum(i * per - 1, 0), 0))
    return pl.pallas_call(
        _swa_kernel,
        grid=(B, S // t),
        in_specs=[pl.BlockSpec(memory_space=pltpu.SMEM),
                  cur(_NS), cur(_NKV), prv(_NKV), cur(_NKV), prv(_NKV)],
        out_specs=cur(_NS),
        out_shape=jax.ShapeDtypeStruct((B, S, _NS), BF16),
        compiler_params=_params("arbitrary", "arbitrary"),
        name="swa",
    )(sinks, sq, sk, sk, sv, sv)


_NO_ID = 1 << 20


def _topk_rows(s, k, val_ref, idx_ref, ids=None):
    if ids is None:
        ids = lax.broadcasted_iota(jnp.int32, s.shape, 0)
    for r in range(k):
        m = jnp.max(s, axis=0, keepdims=True)
        i = jnp.min(jnp.where(s == m, ids, _NO_ID), axis=0, keepdims=True)
        val_ref[pl.ds(r, 1), :] = m
        idx_ref[pl.ds(r, 1), :] = i
        s = jnp.where(ids == i, -jnp.inf, s)


def _cand_counts():
    return [PEER_TOPK // (a + 1) for a in range(PEER_TOPK)]


_CAND_ROWS = 56


def _route_kernel(fo_ref, so_ref, x_ref, mod_ref, gpost_ref, gpre_ref, wof_ref, wos_ref, wq_ref, keys_ref, cid_ref,
                  x1_ref, h2_ref, idx_ref, gate_ref, qs_ref, sv_ref, si_ref, et_ref, gt_ref, cand_ref):
    x = x_ref[0]
    gt1 = mod_ref[0, 2:3, :]
    sh2 = mod_ref[0, 3:4, :]
    sc2 = mod_ref[0, 4:5, :]
    y = _dot(fo_ref[0], wof_ref[...]) + _dot(so_ref[0], wos_ref[...])
    x1 = x + gt1 * (_rms(y) * gpost_ref[...])
    x1_ref[0] = x1
    h2 = _rms(x1) * gpre_ref[...] * (1.0 + sc2) + sh2
    h2_ref[0] = h2
    qp = _dot(h2.astype(BF16), wq_ref[...])
    nhp = 2 * PEER_HEADS
    for hp in range(nhp):
        qs_ref[hp] = qp[:, hp * PEER_HALF:(hp + 1) * PEER_HALF].astype(BF16)

    def half(hp, carry):
        sc = _dot_nt(keys_ref[hp], qs_ref[hp])
        _topk_rows(sc, PEER_TOPK, sv_ref.at[hp], si_ref.at[hp])
        return carry

    lax.fori_loop(0, nhp, half, 0)

    counts = _cand_counts()
    used = sum(counts)
    cand_ref[pl.ds(used, _CAND_ROWS - used), :] = jnp.full((_CAND_ROWS - used, cand_ref.shape[1]), -jnp.inf, F32)
    cid = cid_ref[...]

    def head(h, carry):
        v0, v1 = sv_ref[2 * h], sv_ref[2 * h + 1]
        i0, i1 = si_ref[2 * h], si_ref[2 * h + 1]
        off = 0
        for a, nb in enumerate(counts):
            cand_ref[pl.ds(off, nb), :] = v0[a:a + 1, :] + v1[0:nb, :]
            off += nb
        _topk_rows(cand_ref[...], PEER_TOPK, gt_ref.at[h], et_ref.at[h], ids=cid)
        cv, ci = gt_ref[h], et_ref[h]
        ca, cb = ci >> 4, ci & (PEER_TOPK - 1)
        e1 = jnp.zeros_like(ci)
        e2 = jnp.zeros_like(ci)
        for a in range(PEER_TOPK):
            e1 = jnp.where(ca == a, i0[a:a + 1, :], e1)
            e2 = jnp.where(cb == a, i1[a:a + 1, :], e2)
        et_ref[h] = e1 * N_KEYS + e2
        ex = jnp.exp(cv - cv[0:1, :])
        gt_ref[h] = ex / jnp.sum(ex, axis=0, keepdims=True)
        return carry

    lax.fori_loop(0, PEER_HEADS, head, 0)
    tt = et_ref.shape[2]
    et = lax.bitcast_convert_type(et_ref[...].reshape(PEER_PICKS, tt), F32)
    idx_ref[...] = lax.bitcast_convert_type(et.T, jnp.int32)
    gate_ref[...] = gt_ref[...].reshape(PEER_PICKS, tt).T


def _route_call(fo, so, x, mod, g_post, g_pre, wof, wos, wq, keys):
    B, S, _ = x.shape
    tt = min(ROUTE_TILE, S)
    per = S // tt
    row = lambda w: pl.BlockSpec((1, tt, w), lambda b, i: (b, i, 0))
    full = lambda a: pl.BlockSpec(a.shape, lambda b, i: (0,) * a.ndim)
    tok = pl.BlockSpec((tt, PEER_PICKS), lambda b, i: (b * per + i, 0))
    flat = [a * PEER_TOPK + b for a, nb in enumerate(_cand_counts()) for b in range(nb)]
    flat += [_NO_ID] * (_CAND_ROWS - len(flat))
    cid = jnp.asarray(np.broadcast_to(np.asarray(flat, np.int32)[:, None], (_CAND_ROWS, tt)))
    return pl.pallas_call(
        _route_kernel,
        grid=(B, per),
        in_specs=[row(_NQ), row(_NS), row(D_MODEL),
                  pl.BlockSpec((1, N_MOD, D_MODEL), lambda b, i: (b, 0, 0)),
                  full(g_post), full(g_pre), full(wof), full(wos), full(wq), full(keys), full(cid)],
        out_specs=[row(D_MODEL), row(D_MODEL), tok, tok],
        out_shape=[jax.ShapeDtypeStruct((B, S, D_MODEL), F32),
                   jax.ShapeDtypeStruct((B, S, D_MODEL), F32),
                   jax.ShapeDtypeStruct((B * S, PEER_PICKS), jnp.int32),
                   jax.ShapeDtypeStruct((B * S, PEER_PICKS), F32)],
        scratch_shapes=[pltpu.VMEM((2 * PEER_HEADS, tt, PEER_HALF), BF16),
                        pltpu.VMEM((2 * PEER_HEADS, PEER_TOPK, tt), F32),
                        pltpu.VMEM((2 * PEER_HEADS, PEER_TOPK, tt), jnp.int32),
                        pltpu.VMEM((PEER_HEADS, PEER_TOPK, tt), jnp.int32),
                        pltpu.VMEM((PEER_HEADS, PEER_TOPK, tt), F32),
                        pltpu.VMEM((_CAND_ROWS, tt), F32)],
        compiler_params=_params("arbitrary", "arbitrary"),
        name="route",
    )(fo, so, x, mod, g_post, g_pre, wof, wos, wq, keys, cid)


def _peer_consts():
    half = TABLE_ROWS // 2
    gsum = np.zeros((PEER_GROUP, PEER_GROUP * TABLE_ROWS), np.float32)
    ev = np.zeros((PEER_PICKS, TABLE_COLS), np.float32)
    for j in range(PEER_GROUP):
        gsum[j, j * TABLE_ROWS:j * TABLE_ROWS + half] = 1.0
    for k in range(PEER_PICKS):
        ev[k, k * TABLE_ROWS + half:(k + 1) * TABLE_ROWS] = 1.0
    return jnp.asarray(gsum, BF16), jnp.asarray(ev, BF16)


def _peer_kernel(idx_hbm, tab_hbm, hs_ref, g_ref, gsum_ref, ev_ref, y_ref,
                 buf0, buf1, buf2, buf3, ib0, ib1, wr_ref, sem_g, sem_i):
    s = pl.program_id(0)
    last = pl.num_programs(0) - 1
    tt = PEER_TILE
    npt = tt * PEER_PICKS
    half = TABLE_ROWS // 2
    ngrp = PEER_PICKS // PEER_GROUP
    grows = PEER_GROUP * TABLE_ROWS
    bufs = (buf0, buf1, buf2, buf3)
    ibs = (ib0, ib1)

    def idx_fetch(tile, j):
        return pltpu.make_async_copy(idx_hbm.at[pl.ds(tile * npt, npt)], ibs[j], sem_i.at[j])

    def rows_done(j):
        return pltpu.make_async_copy(tab_hbm.at[pl.ds(0, npt)], bufs[j], sem_g.at[j])

    def issue_rows(ib, buf, sem):
        c = ib[0] >> 31
        for i in range(npt):
            e = ib[i] + c
            pltpu.make_async_copy(tab_hbm.at[e], buf.at[i], sem).start(priority=i % 2)
            if i % PEER_CHAIN == PEER_CHAIN - 1:
                c = e >> 31

    @pl.when(s == 0)
    def _():
        for j in range(2):
            first = idx_fetch(j, j)
            first.start()
            first.wait()

            def body(i, carry):
                pltpu.make_async_copy(tab_hbm.at[ibs[j][i]], bufs[j].at[i], sem_g.at[j]).start()
                return carry

            lax.fori_loop(0, npt, body, 0)
        idx_fetch(2, 0).start()

    row8 = lax.broadcasted_iota(jnp.int32, (half, TABLE_COLS), 0)
    col8 = lax.broadcasted_iota(jnp.int32, (half, TABLE_COLS), 1) % TABLE_ROWS
    mask_v = col8 == row8 + half
    lane = lax.broadcasted_iota(jnp.int32, (PEER_PICKS, HEAD_PAD), 1)
    gsum = gsum_ref[...]

    def evaluate(cur, tok0):
        def ubody(it, at):
            for u in range(PEER_UNROLL):
                t = it * PEER_UNROLL + u
                hrow = hs_ref[tok0 + t]
                h16 = jnp.concatenate([hrow, jnp.zeros_like(hrow)], axis=0).astype(BF16)
                tw = cur[pl.ds(t * PEER_PICKS, PEER_PICKS)]
                prod = (tw * h16[None]).reshape(TABLE_COLS, HEAD_PAD)
                parts = [_dot(gsum, prod[g * grows:(g + 1) * grows]) for g in range(ngrp)]
                z = jnp.sum(jnp.concatenate(parts, axis=0), axis=1, keepdims=True)
                at = jnp.where(lane == t, z, at)
            return at

        at = jnp.zeros((PEER_PICKS, HEAD_PAD), F32)
        for it in range(tt // PEER_UNROLL):
            at = ubody(it, at)
        a = at.T[:tt]
        w = jax.nn.gelu(a) * g_ref[tok0:tok0 + tt, :]
        wr_ref[...] = _dot(w.astype(BF16), ev_ref[...])

        def vbody(it, carry):
            for u in range(PEER_UNROLL):
                t = it * PEER_UNROLL + u
                wrow = jnp.broadcast_to(wr_ref[pl.ds(t, 1), :], (half, TABLE_COLS))
                wexp = jnp.where(mask_v, wrow, 0.0).astype(BF16)
                wb = cur[pl.ds(t * PEER_PICKS, PEER_PICKS)].reshape(TABLE_COLS, HEAD_PAD)
                y_ref[tok0 + t] = _dot(wexp, wb)
            return carry

        for it in range(tt // PEER_UNROLL):
            vbody(it, 0)

    for p in range(PEER_PHASES):
        k = PEER_PHASES * s + p
        idx_fetch(k + 2, p % 2).wait()
        idx_fetch(k + 3, (p + 1) % 2).start()
        rows_done(p).wait()
        issue_rows(ibs[p % 2], bufs[(p + 2) % PEER_PHASES], sem_g.at[(p + 2) % PEER_PHASES])
        evaluate(bufs[p], p * tt)

    @pl.when(s == last)
    def _():
        rows_done(0).wait()
        rows_done(1).wait()
        idx_fetch(0, 0).wait()


def _peer_call(idx_flat, table, hs3, gates):
    T = hs3.shape[0]
    tt = PEER_TILE
    npt = tt * PEER_PICKS
    assert tt % PEER_UNROLL == 0 and PEER_PICKS % PEER_GROUP == 0 and tt <= HEAD_PAD and PEER_PHASES == 4
    gsum, ev = _peer_consts()
    idx_pad = jnp.pad(idx_flat, (0, 3 * npt))
    step = PEER_PHASES * tt
    full = lambda a: pl.BlockSpec(a.shape, lambda s: (0,) * a.ndim)
    rows = pltpu.VMEM((npt, TABLE_ROWS, HEAD_PAD), BF16)
    return pl.pallas_call(
        _peer_kernel,
        grid=(T // step,),
        in_specs=[pl.BlockSpec(memory_space=pl.ANY),
                  pl.BlockSpec(memory_space=pl.ANY),
                  pl.BlockSpec((step, 8, HEAD_PAD), lambda s: (s, 0, 0)),
                  pl.BlockSpec((step, PEER_PICKS), lambda s: (s, 0)),
                  full(gsum), full(ev)],
        out_specs=pl.BlockSpec((step, 8, HEAD_PAD), lambda s: (s, 0, 0)),
        out_shape=jax.ShapeDtypeStruct((T, 8, HEAD_PAD), F32),
        scratch_shapes=[rows, rows, rows, rows,
                        pltpu.SMEM((npt,), jnp.int32),
                        pltpu.SMEM((npt,), jnp.int32),
                        pltpu.VMEM((tt, TABLE_COLS), F32),
                        pltpu.SemaphoreType.DMA((PEER_PHASES,)),
                        pltpu.SemaphoreType.DMA((2,))],
        compiler_params=_params("arbitrary"),
        name="peer",
    )(idx_pad, table, hs3, gates, gsum, ev)


SC_LANES = 16
SC_GROUP = 16
_GELU_C = 0.7978845608028654


def _peer_sc_kernel(idx_hbm, tab_hbm, h_hbm, g_hbm, y_hbm, idx_v, g_v, h_v, o_v, rows_v, sem):
    ncores = lax.axis_size("c")
    wid = lax.axis_index("s") * ncores + lax.axis_index("c")
    per = h_hbm.shape[0] // (ncores * lax.axis_size("s"))
    nchunk = D_MODEL // SC_LANES
    ngrp = PEER_PICKS // SC_GROUP
    lane = lax.iota(jnp.int32, SC_LANES)

    def gather(g, slot):
        return pltpu.make_async_copy(tab_hbm.at[idx_v.at[pl.ds(g * SC_GROUP, SC_GROUP)]], rows_v.at[slot], sem.at[slot])

    def token(i, carry):
        t = wid * per + i
        pltpu.sync_copy(idx_hbm.at[t], idx_v)
        gather(0, 0).start()
        pltpu.sync_copy(g_hbm.at[t], g_v)
        pltpu.sync_copy(h_hbm.at[t], h_v)

        def zero(c, carry):
            o_v[pl.ds(c * SC_LANES, SC_LANES)] = jnp.zeros((SC_LANES,), F32)
            return carry

        lax.fori_loop(0, nchunk, zero, 0)

        for g in range(ngrp):
            slot = g % 2
            if g + 1 < ngrp:
                gather(g + 1, 1 - slot).start()
            gather(g, slot).wait()

            def dot_body(c, accs):
                hv = h_v[pl.ds(c * SC_LANES, SC_LANES)]
                out = []
                for p in range(SC_GROUP):
                    w = rows_v[slot, p, pl.ds(c * SC_LANES, SC_LANES)]
                    u = lax.bitcast_convert_type(w << 16, F32)
                    out.append(accs[p] + u * hv)
                return tuple(out)

            accs = lax.fori_loop(0, nchunk, dot_body,
                                 tuple(jnp.zeros((SC_LANES,), F32) for _ in range(SC_GROUP)))
            a = jnp.zeros((SC_LANES,), F32)
            for p in range(SC_GROUP):
                a = jnp.where(lane == p, jnp.sum(accs[p]), a)
            z = _GELU_C * (a + 0.044715 * a * a * a)
            th = 1.0 - 2.0 / (jnp.exp(2.0 * z) + 1.0)
            wv = 0.5 * a * (1.0 + th) * g_v[pl.ds(g * SC_GROUP, SC_GROUP)]
            ws = [jnp.sum(jnp.where(lane == p, wv, 0.0)) for p in range(SC_GROUP)]

            def ax_body(c, carry):
                o = o_v[pl.ds(c * SC_LANES, SC_LANES)]
                for p in range(SC_GROUP):
                    w = rows_v[slot, p, pl.ds(c * SC_LANES, SC_LANES)]
                    v = lax.bitcast_convert_type(w & jnp.int32(-65536), F32)
                    o = o + ws[p] * v
                o_v[pl.ds(c * SC_LANES, SC_LANES)] = o
                return carry

            lax.fori_loop(0, nchunk, ax_body, 0)

        pltpu.sync_copy(o_v, y_hbm.at[t])
        return carry

    lax.fori_loop(0, per, token, 0)


def _peer_sc_call(idx2, table_i32, h2, gates):
    ts = h2.shape[0]
    mesh = plsc.VectorSubcoreMesh(core_axis_name="c", subcore_axis_name="s")
    run = pl.kernel(
        _peer_sc_kernel,
        out_type=jax.ShapeDtypeStruct((ts, D_MODEL), F32),
        mesh=mesh,
        scratch_types=[pltpu.VMEM((PEER_PICKS,), jnp.int32),
                       pltpu.VMEM((PEER_PICKS,), F32),
                       pltpu.VMEM((D_MODEL,), F32),
                       pltpu.VMEM((D_MODEL,), F32),
                       pltpu.VMEM((2, SC_GROUP, D_MODEL), jnp.int32),
                       pltpu.SemaphoreType.DMA((2,))],
        compiler_params=pltpu.CompilerParams(needs_layout_passes=False),
        name="peer_sc",
    )
    return run(idx2, table_i32, h2, gates)


def _final_kernel(x1_ref, y_ref, mod_ref, g_ref, o_ref):
    gt2 = mod_ref[0, 5:6, :]
    o_ref[0] = x1_ref[0] + gt2 * (_rms(y_ref[0]) * g_ref[...])


def _final_call(x1, y, mod, g_post):
    B, S, _ = x1.shape
    tr = min(ROW_TILE, S)
    row = pl.BlockSpec((1, tr, D_MODEL), lambda b, i: (b, i, 0))
    return pl.pallas_call(
        _final_kernel,
        grid=(B, S // tr),
        in_specs=[row, row, pl.BlockSpec((1, N_MOD, D_MODEL), lambda b, i: (b, 0, 0)),
                  pl.BlockSpec((1, D_MODEL), lambda b, i: (0, 0))],
        out_specs=row,
        out_shape=jax.ShapeDtypeStruct((B, S, D_MODEL), F32),
        compiler_params=_params("arbitrary", "arbitrary"),
        name="final",
    )(x1, y, mod, g_post)


def _pad_heads_cols(w, nh, scale=1.0):
    k = w.shape[0]
    w = (w * scale).reshape(k, nh, HEAD_DIM)
    return jnp.pad(w, ((0, 0), (0, 0), (0, HEAD_PAD - HEAD_DIM))).reshape(k, nh * HEAD_PAD)


def _pad_heads_rows(w, nh):
    n = w.shape[1]
    w = w.reshape(nh, HEAD_DIM, n)
    return jnp.pad(w, ((0, 0), (0, HEAD_PAD - HEAD_DIM), (0, 0))).reshape(nh * HEAD_PAD, n)


def _layer(x, c8, w_ada, b_ada, g_pre_mix, g_post_mix, g_pre_ffn, g_post_ffn,
           w_in, b_fgate, swa_sinks, w_out, w_query, sub_keys, w_u, w_v):
    B, S, D = x.shape
    T = B * S
    scale = HEAD_DIM ** -0.5
    mod = _ada_call(c8, w_ada, b_ada)[:B].reshape(B, N_MOD, D)

    o = 0
    parts = []
    for nh, sc in ((FOX_HEADS, scale), (FOX_HEADS, 1.0), (FOX_HEADS, 1.0)):
        parts.append(_pad_heads_cols(w_in[:, o:o + nh * HEAD_DIM], nh, sc))
        o += nh * HEAD_DIM
    parts.append(jnp.pad(w_in[:, o:o + FOX_HEADS], ((0, 0), (0, HEAD_PAD - FOX_HEADS))))
    o += FOX_HEADS
    for nh, sc in ((SWA_HEADS, scale), (SWA_KV_HEADS, 1.0), (SWA_KV_HEADS, 1.0)):
        parts.append(_pad_heads_cols(w_in[:, o:o + nh * HEAD_DIM], nh, sc))
        o += nh * HEAD_DIM
    w_all = jnp.concatenate(parts, axis=1).astype(BF16)
    bf_pad = jnp.pad(b_fgate, (0, HEAD_PAD - FOX_HEADS)).reshape(1, HEAD_PAD)

    qt, kp, vt, sq, sk, sv = _inproj_call(x, mod, g_pre_mix.reshape(1, D), w_all, bf_pad)
    fo = _fox_call(qt, kp, vt)
    so = _swa_call(swa_sinks, sq, sk, sv)

    nf = FOX_HEADS * HEAD_DIM
    wof = _pad_heads_rows(w_out[:nf], FOX_HEADS).astype(BF16)
    wos = _pad_heads_rows(w_out[nf:], SWA_HEADS).astype(BF16)
    keys = sub_keys.reshape(2 * PEER_HEADS, N_KEYS, PEER_HALF).astype(BF16)
    x1, h2, idx, gates = _route_call(fo, so, x, mod, g_post_mix.reshape(1, D), g_pre_ffn.reshape(1, D),
                                     wof, wos, w_query.astype(BF16), keys)

    ub, vb = w_u.astype(BF16), w_v.astype(BF16)
    table = jnp.concatenate([ub.reshape(N_EXPERTS, 8, HEAD_PAD), vb.reshape(N_EXPERTS, 8, HEAD_PAD)], axis=1)
    h2f = h2.reshape(T, D)
    ts = (int(T * SC_SHARE) // 1024) * 1024
    t1 = T - ts
    y_tc = _peer_call(idx[:t1].reshape(t1 * PEER_PICKS), table, h2f[:t1].reshape(t1, 8, HEAD_PAD), gates[:t1])
    y = y_tc.reshape(t1, D)
    if ts:
        u16 = lax.bitcast_convert_type(ub, jnp.uint16).astype(jnp.uint32)
        v16 = lax.bitcast_convert_type(vb, jnp.uint16).astype(jnp.uint32)
        table_i32 = lax.bitcast_convert_type((v16 << 16) | u16, jnp.int32)
        y_sc = _peer_sc_call(idx[t1:], table_i32, h2f[t1:], gates[t1:])
        y = jnp.concatenate([y, y_sc], axis=0)
    return _final_call(x1, y.reshape(B, S, D), mod, g_post_ffn.reshape(1, D))


def kernel(x, c, w_ada, b_ada, g_pre_mix, g_post_mix, g_pre_ffn, g_post_ffn, w_in, b_fgate, swa_sinks, w_out,
           w_query, sub_keys, w_u, w_v):
    B = x.shape[0]
    c8 = jnp.pad(c, ((0, 8 - B), (0, 0)))
    for l in range(w_ada.shape[0]):
        x = _layer(x, c8, w_ada[l], b_ada[l], g_pre_mix[l], g_post_mix[l], g_pre_ffn[l], g_post_ffn[l],
                   w_in[l], b_fgate[l], swa_sinks[l], w_out[l], w_query[l], sub_keys[l], w_u[l], w_v[l])
    return x
```

```python
import numpy as np
import jax
import jax.numpy as jnp
from jax import lax
from jax.experimental import pallas as pl
from jax.experimental.pallas import tpu as pltpu
from jax.experimental.pallas import tpu_sc as plsc

F32 = jnp.float32
BF16 = jnp.bfloat16

D_MODEL = 1024
HEAD_DIM = 64
HEAD_PAD = 128
FOX_HEADS = 8
SWA_HEADS = 8
SWA_KV_HEADS = 2
SWA_GROUP = SWA_HEADS // SWA_KV_HEADS
WINDOW = 128
PEER_HEADS = 8
PEER_HALF = 128
N_KEYS = 128
N_EXPERTS = N_KEYS * N_KEYS
PEER_TOPK = 16
PEER_PICKS = PEER_HEADS * PEER_TOPK
N_MOD = 6
RMS_EPS = 1e-6
NEG_INF = -1e30

_F_LANE = HEAD_DIM
_ONE_LANE = HEAD_DIM

ROW_TILE = 1024
FOX_TILE = 1024
FOX_PAR = 1
SWA_TILE = 512
ROUTE_TILE = 256
PEER_TILE = 8
PEER_PHASES = 4
PEER_UNROLL = 8
PEER_CHAIN = 8
PEER_GROUP = 16
SC_SHARE = 0.375
TABLE_ROWS = 16
TABLE_COLS = PEER_PICKS * TABLE_ROWS

_VMEM_LIMIT = 56 * 1024 * 1024


def _dot(a, b):
    return jnp.dot(a, b, preferred_element_type=F32)


def _dot_nt(a, b):
    return lax.dot_general(a, b, (((1,), (1,)), ((), ())), preferred_element_type=F32)


def _split3(x):
    hi = x.astype(BF16)
    r = x - hi.astype(F32)
    mid = r.astype(BF16)
    lo = (r - mid.astype(F32)).astype(BF16)
    return hi, mid, lo


def _rms(x):
    return x * lax.rsqrt(jnp.mean(x * x, axis=-1, keepdims=True) + RMS_EPS)


def _params(*sem):
    return pltpu.CompilerParams(dimension_semantics=sem, vmem_limit_bytes=_VMEM_LIMIT)


def _ada_kernel(c_ref, w_ref, b_ref, o_ref):
    c = c_ref[...]
    s = (c * jax.nn.sigmoid(c)).astype(BF16)
    o_ref[...] = _dot(s, w_ref[...].astype(BF16)) + b_ref[...]


def _ada_call(c8, w_ada, b_ada):
    n = w_ada.shape[1]
    tn = 1536
    return pl.pallas_call(
        _ada_kernel,
        grid=(n // tn,),
        in_specs=[pl.BlockSpec((8, D_MODEL), lambda j: (0, 0)),
                  pl.BlockSpec((D_MODEL, tn), lambda j: (0, j)),
                  pl.BlockSpec((1, tn), lambda j: (0, j))],
        out_specs=pl.BlockSpec((8, tn), lambda j: (0, j)),
        out_shape=jax.ShapeDtypeStruct((8, n), F32),
        compiler_params=_params("arbitrary"),
        name="ada",
    )(c8, w_ada, b_ada.reshape(1, n))


_NQ = FOX_HEADS * HEAD_PAD
_NS = SWA_HEADS * HEAD_PAD
_NKV = SWA_KV_HEADS * HEAD_PAD
_IN_COLS = 3 * _NQ + HEAD_PAD + _NS + 2 * _NKV


def _inproj_kernel(x_ref, mod_ref, g_ref, w_ref, bf_ref, tri_ref, pq_ref, pk_ref, cst_ref,
                   qt_ref, kp_ref, vt_ref, sq_ref, sk_ref, sv_ref, carry_ref):
    i = pl.program_id(1)

    @pl.when(i == 0)
    def _():
        carry_ref[...] = jnp.zeros_like(carry_ref)

    x = x_ref[0]
    sh1 = mod_ref[0, 0:1, :]
    sc1 = mod_ref[0, 1:2, :]
    h = _rms(x) * g_ref[...] * (1.0 + sc1) + sh1
    proj = _dot(h.astype(BF16), w_ref[...])

    z = proj[:, 3 * _NQ:3 * _NQ + HEAD_PAD] + bf_ref[...]
    ls = jnp.minimum(z, 0.0) - jnp.log(1.0 + jnp.exp(-jnp.abs(z)))
    tri = tri_ref[...]
    hi, mid, lo = _split3(ls)
    fcum = _dot(tri, hi) + _dot(tri, mid) + _dot(tri, lo) + carry_ref[...]
    carry_ref[...] = fcum[fcum.shape[0] - 1:, :]

    fh, fm, fl = _split3(fcum)
    eq = _dot(fh, pq_ref[0]) + _dot(fm, pq_ref[1]) + _dot(fl, pq_ref[2]) + cst_ref[0:1, :]
    ek = _dot(fh, pk_ref[0]) + _dot(fm, pk_ref[1]) + _dot(fl, pk_ref[2]) + cst_ref[1:2, :]
    qt_ref[0, 0] = (proj[:, 0:_NQ] + eq).T.astype(BF16)
    kp_ref[0] = (proj[:, _NQ:2 * _NQ] + ek).astype(BF16)
    vt_ref[0, 0] = (proj[:, 2 * _NQ:3 * _NQ] + cst_ref[2:3, :]).T.astype(BF16)
    o = 3 * _NQ + HEAD_PAD
    sq_ref[0] = proj[:, o:o + _NS].astype(BF16)
    sk_ref[0] = proj[:, o + _NS:o + _NS + _NKV].astype(BF16)
    sv_ref[0] = proj[:, o + _NS + _NKV:o + _NS + 2 * _NKV].astype(BF16)


def _inproj_consts(tr):
    tri = np.tril(np.ones((tr, tr), np.float32))
    pq = np.zeros((3, HEAD_PAD, _NQ), np.float32)
    pk = np.zeros((3, HEAD_PAD, _NQ), np.float32)
    cst = np.zeros((8, _NQ), np.float32)
    for h in range(FOX_HEADS):
        b = h * HEAD_PAD + _F_LANE
        for j in range(3):
            pq[j, h, b + j] = 1.0
            pk[j, h, b + 3 + j] = -1.0
            cst[0, b + 3 + j] = 1.0
            cst[1, b + j] = 1.0
        cst[2, h * HEAD_PAD + _ONE_LANE] = 1.0
    return (jnp.asarray(tri, BF16), jnp.asarray(pq, BF16), jnp.asarray(pk, BF16), jnp.asarray(cst, F32))


def _inproj_call(x, mod, g_pre, w_all, bf_pad):
    B, S, _ = x.shape
    tr = min(ROW_TILE, S)
    n = S // tr
    tri, pq, pk, cst = _inproj_consts(tr)
    row = lambda w: pl.BlockSpec((1, tr, w), lambda b, i: (b, i, 0))
    slab = pl.BlockSpec((1, 1, _NQ, tr), lambda b, i: (b, i, 0, 0))
    full = lambda a: pl.BlockSpec(a.shape, lambda b, i: (0,) * a.ndim)
    outs = [jax.ShapeDtypeStruct((B, n, _NQ, tr), BF16), jax.ShapeDtypeStruct((B, S, _NQ), BF16),
            jax.ShapeDtypeStruct((B, n, _NQ, tr), BF16), jax.ShapeDtypeStruct((B, S, _NS), BF16),
            jax.ShapeDtypeStruct((B, S, _NKV), BF16), jax.ShapeDtypeStruct((B, S, _NKV), BF16)]
    return pl.pallas_call(
        _inproj_kernel,
        grid=(B, n),
        in_specs=[row(D_MODEL),
                  pl.BlockSpec((1, N_MOD, D_MODEL), lambda b, i: (b, 0, 0)),
                  full(g_pre), full(w_all), full(bf_pad), full(tri), full(pq), full(pk), full(cst)],
        out_specs=[slab, row(_NQ), slab, row(_NS), row(_NKV), row(_NKV)],
        out_shape=outs,
        scratch_shapes=[pltpu.VMEM((1, HEAD_PAD), F32)],
        compiler_params=_params("arbitrary", "arbitrary"),
        name="inproj",
    )(x, mod, g_pre, w_all, bf_pad, tri, pq, pk, cst)


def _fox_kernel(qt_ref, k_ref, vt_ref, o_ref, m_ref, acc_ref):
    i = pl.program_id(2)
    t = o_ref.shape[1]
    m_ref[...] = jnp.full_like(m_ref, NEG_INF)
    acc_ref[...] = jnp.zeros_like(acc_ref)

    def block(j, masked):
        off = pl.multiple_of(j * t, t)
        for hh in range(FOX_PAR):
            lanes = slice(hh * HEAD_PAD, (hh + 1) * HEAD_PAD)
            k = k_ref[0, pl.ds(off, t), lanes]
            s = _dot(k, qt_ref[0, 0, lanes, :])
            if masked:
                r = lax.broadcasted_iota(jnp.int32, s.shape, 0)
                c = lax.broadcasted_iota(jnp.int32, s.shape, 1)
                s = jnp.where(r <= c, s, NEG_INF)
            m_prev = m_ref[hh]
            m_new = jnp.maximum(m_prev, jnp.max(s, axis=0, keepdims=True))
            p = jnp.exp(s - m_new)
            acc_ref[hh] = jnp.exp(m_prev - m_new) * acc_ref[hh] + _dot(vt_ref[0, j, lanes, :], p.astype(BF16))
            m_ref[hh] = m_new

    def body(j, carry):
        block(j, False)
        return carry

    lax.fori_loop(0, i, body, 0)
    block(i, True)
    for hh in range(FOX_PAR):
        acc = acc_ref[hh]
        o_ref[0, :, hh * HEAD_PAD:(hh + 1) * HEAD_PAD] = (acc / acc[_ONE_LANE:_ONE_LANE + 1, :]).T.astype(BF16)


def _fox_call(qt, kp, vt):
    B, n, _, t = qt.shape
    S = n * t
    w = FOX_PAR * HEAD_PAD
    return pl.pallas_call(
        _fox_kernel,
        grid=(B, FOX_HEADS // FOX_PAR, n),
        in_specs=[pl.BlockSpec((1, 1, w, t), lambda b, h, i: (b, i, h, 0)),
                  pl.BlockSpec((1, S, w), lambda b, h, i: (b, 0, h)),
                  pl.BlockSpec((1, n, w, t), lambda b, h, i: (b, 0, h, 0))],
        out_specs=pl.BlockSpec((1, t, w), lambda b, h, i: (b, i, h)),
        out_shape=jax.ShapeDtypeStruct((B, S, _NQ), BF16),
        scratch_shapes=[pltpu.VMEM((FOX_PAR, 1, t), F32), pltpu.VMEM((FOX_PAR, HEAD_PAD, t), F32)],
        compiler_params=_params("arbitrary", "arbitrary", "arbitrary"),
        name="fox",
    )(qt, kp, vt)


def _swa_kernel(sink_ref, q_ref, kc_ref, kp_ref, vc_ref, vp_ref, o_ref):
    i = pl.program_id(1)
    nsub = q_ref.shape[1] // WINDOW
    r = lax.broadcasted_iota(jnp.int32, (WINDOW, 2 * WINDOW), 0)
    j = lax.broadcasted_iota(jnp.int32, (WINDOW, 2 * WINDOW), 1)
    dist = r + WINDOW - j
    valid = (dist >= 0) & (dist < WINDOW)
    distf = dist.astype(F32)
    for qb in range(nsub):
        rows = slice(qb * WINDOW, (qb + 1) * WINDOW)
        if qb == 0:
            ok = valid & ((j >= WINDOW) | (i > 0))
        else:
            ok = valid
        for g in range(SWA_KV_HEADS):
            lanes = slice(g * HEAD_PAD, (g + 1) * HEAD_PAD)
            if qb == 0:
                kprev, vprev = kp_ref[0, :, lanes], vp_ref[0, :, lanes]
            else:
                prev = slice((qb - 1) * WINDOW, qb * WINDOW)
                kprev, vprev = kc_ref[0, prev, lanes], vc_ref[0, prev, lanes]
            kk = jnp.concatenate([kprev, kc_ref[0, rows, lanes]], axis=0)
            vv = jnp.concatenate([vprev, vc_ref[0, rows, lanes]], axis=0)
            for u in range(SWA_GROUP):
                hq = g * SWA_GROUP + u
                slope = 2.0 ** (-8.0 * (hq + 1) / SWA_HEADS)
                q = q_ref[0, rows, hq * HEAD_PAD:(hq + 1) * HEAD_PAD]
                s = _dot_nt(q, kk) - slope * distf
                s = jnp.where(ok, s, NEG_INF)
                sink = sink_ref[hq]
                m = jnp.maximum(jnp.max(s, axis=1, keepdims=True), sink)
                p = jnp.exp(s - m)
                den = jnp.sum(p, axis=1, keepdims=True) + jnp.exp(sink - m)
                o = _dot(p.astype(BF16), vv) / den
                o_ref[0, rows, hq * HEAD_PAD:(hq + 1) * HEAD_PAD] = o.astype(BF16)


def _swa_call(sinks, sq, sk, sv):
    B, S, _ = sq.shape
    t = min(SWA_TILE, S)
    per = t // WINDOW
    cur = lambda w: pl.BlockSpec((1, t, w), lambda b, i: (b, i, 0))
    prv = lambda w: pl.BlockSpec((1, WINDOW, w), lambda b, i: (b, jnp.maximum(i * per - 1, 0), 0))
    return pl.pallas_call(
        _swa_kernel,
        grid=(B, S // t),
        in_specs=[pl.BlockSpec(memory_space=pltpu.SMEM),
                  cur(_NS), cur(_NKV), prv(_NKV), cur(_NKV), prv(_NKV)],
        out_specs=cur(_NS),
        out_shape=jax.ShapeDtypeStruct((B, S, _NS), BF16),
        compiler_params=_params("arbitrary", "arbitrary"),
        name="swa",
    )(sinks, sq, sk, sk, sv, sv)


_NO_ID = 1 << 20


def _topk_rows(s, k, val_ref, idx_ref, ids=None):
    if ids is None:
        ids = lax.broadcasted_iota(jnp.int32, s.shape, 0)
    for r in range(k):
        m = jnp.max(s, axis=0, keepdims=True)
        i = jnp.min(jnp.where(s == m, ids, _NO_ID), axis=0, keepdims=True)
        val_ref[pl.ds(r, 1), :] = m
        idx_ref[pl.ds(r, 1), :] = i
        s = jnp.where(ids == i, -jnp.inf, s)


def _cand_counts():
    return [PEER_TOPK // (a + 1) for a in range(PEER_TOPK)]


_CAND_ROWS = 56


def _route_kernel(fo_ref, so_ref, x_ref, mod_ref, gpost_ref, gpre_ref, wof_ref, wos_ref, wq_ref, keys_ref, cid_ref,
                  x1_ref, h2_ref, idx_ref, gate_ref, qs_ref, sv_ref, si_ref, et_ref, gt_ref, cand_ref):
    x = x_ref[0]
    gt1 = mod_ref[0, 2:3, :]
    sh2 = mod_ref[0, 3:4, :]
    sc2 = mod_ref[0, 4:5, :]
    y = _dot(fo_ref[0], wof_ref[...]) + _dot(so_ref[0], wos_ref[...])
    x1 = x + gt1 * (_rms(y) * gpost_ref[...])
    x1_ref[0] = x1
    h2 = _rms(x1) * gpre_ref[...] * (1.0 + sc2) + sh2
    h2_ref[0] = h2
    qp = _dot(h2.astype(BF16), wq_ref[...])
    nhp = 2 * PEER_HEADS
    for hp in range(nhp):
        qs_ref[hp] = qp[:, hp * PEER_HALF:(hp + 1) * PEER_HALF].astype(BF16)

    def half(hp, carry):
        sc = _dot_nt(keys_ref[hp], qs_ref[hp])
        _topk_rows(sc, PEER_TOPK, sv_ref.at[hp], si_ref.at[hp])
        return carry

    lax.fori_loop(0, nhp, half, 0)

    counts = _cand_counts()
    used = sum(counts)
    cand_ref[pl.ds(used, _CAND_ROWS - used), :] = jnp.full((_CAND_ROWS - used, cand_ref.shape[1]), -jnp.inf, F32)
    cid = cid_ref[...]

    def head(h, carry):
        v0, v1 = sv_ref[2 * h], sv_ref[2 * h + 1]
        i0, i1 = si_ref[2 * h], si_ref[2 * h + 1]
        off = 0
        for a, nb in enumerate(counts):
            cand_ref[pl.ds(off, nb), :] = v0[a:a + 1, :] + v1[0:nb, :]
            off += nb
        _topk_rows(cand_ref[...], PEER_TOPK, gt_ref.at[h], et_ref.at[h], ids=cid)
        cv, ci = gt_ref[h], et_ref[h]
        ca, cb = ci >> 4, ci & (PEER_TOPK - 1)
        e1 = jnp.zeros_like(ci)
        e2 = jnp.zeros_like(ci)
        for a in range(PEER_TOPK):
            e1 = jnp.where(ca == a, i0[a:a + 1, :], e1)
            e2 = jnp.where(cb == a, i1[a:a + 1, :], e2)
        et_ref[h] = e1 * N_KEYS + e2
        ex = jnp.exp(cv - cv[0:1, :])
        gt_ref[h] = ex / jnp.sum(ex, axis=0, keepdims=True)
        return carry

    lax.fori_loop(0, PEER_HEADS, head, 0)
    tt = et_ref.shape[2]
    et = lax.bitcast_convert_type(et_ref[...].reshape(PEER_PICKS, tt), F32)
    idx_ref[...] = lax.bitcast_convert_type(et.T, jnp.int32)
    gate_ref[...] = gt_ref[...].reshape(PEER_PICKS, tt).T


def _route_call(fo, so, x, mod, g_post, g_pre, wof, wos, wq, keys):
    B, S, _ = x.shape
    tt = min(ROUTE_TILE, S)
    per = S // tt
    row = lambda w: pl.BlockSpec((1, tt, w), lambda b, i: (b, i, 0))
    full = lambda a: pl.BlockSpec(a.shape, lambda b, i: (0,) * a.ndim)
    tok = pl.BlockSpec((tt, PEER_PICKS), lambda b, i: (b * per + i, 0))
    flat = [a * PEER_TOPK + b for a, nb in enumerate(_cand_counts()) for b in range(nb)]
    flat += [_NO_ID] * (_CAND_ROWS - len(flat))
    cid = jnp.asarray(np.broadcast_to(np.asarray(flat, np.int32)[:, None], (_CAND_ROWS, tt)))
    return pl.pallas_call(
        _route_kernel,
        grid=(B, per),
        in_specs=[row(_NQ), row(_NS), row(D_MODEL),
                  pl.BlockSpec((1, N_MOD, D_MODEL), lambda b, i: (b, 0, 0)),
                  full(g_post), full(g_pre), full(wof), full(wos), full(wq), full(keys), full(cid)],
        out_specs=[row(D_MODEL), row(D_MODEL), tok, tok],
        out_shape=[jax.ShapeDtypeStruct((B, S, D_MODEL), F32),
                   jax.ShapeDtypeStruct((B, S, D_MODEL), F32),
                   jax.ShapeDtypeStruct((B * S, PEER_PICKS), jnp.int32),
                   jax.ShapeDtypeStruct((B * S, PEER_PICKS), F32)],
        scratch_shapes=[pltpu.VMEM((2 * PEER_HEADS, tt, PEER_HALF), BF16),
                        pltpu.VMEM((2 * PEER_HEADS, PEER_TOPK, tt), F32),
                        pltpu.VMEM((2 * PEER_HEADS, PEER_TOPK, tt), jnp.int32),
                        pltpu.VMEM((PEER_HEADS, PEER_TOPK, tt), jnp.int32),
                        pltpu.VMEM((PEER_HEADS, PEER_TOPK, tt), F32),
                        pltpu.VMEM((_CAND_ROWS, tt), F32)],
        compiler_params=_params("arbitrary", "arbitrary"),
        name="route",
    )(fo, so, x, mod, g_post, g_pre, wof, wos, wq, keys, cid)


def _peer_consts():
    half = TABLE_ROWS // 2
    gsum = np.zeros((PEER_GROUP, PEER_GROUP * TABLE_ROWS), np.float32)
    ev = np.zeros((PEER_PICKS, TABLE_COLS), np.float32)
    for j in range(PEER_GROUP):
        gsum[j, j * TABLE_ROWS:j * TABLE_ROWS + half] = 1.0
    for k in range(PEER_PICKS):
        ev[k, k * TABLE_ROWS + half:(k + 1) * TABLE_ROWS] = 1.0
    return jnp.asarray(gsum, BF16), jnp.asarray(ev, BF16)


def _peer_kernel(idx_hbm, tab_hbm, hs_ref, g_ref, gsum_ref, ev_ref, y_ref,
                 buf0, buf1, buf2, buf3, ib0, ib1, wr_ref, sem_g, sem_i):
    s = pl.program_id(0)
    last = pl.num_programs(0) - 1
    tt = PEER_TILE
    npt = tt * PEER_PICKS
    half = TABLE_ROWS // 2
    ngrp = PEER_PICKS // PEER_GROUP
    grows = PEER_GROUP * TABLE_ROWS
    bufs = (buf0, buf1, buf2, buf3)
    ibs = (ib0, ib1)

    def idx_fetch(tile, j):
        return pltpu.make_async_copy(idx_hbm.at[pl.ds(tile * npt, npt)], ibs[j], sem_i.at[j])

    def rows_done(j):
        return pltpu.make_async_copy(tab_hbm.at[pl.ds(0, npt)], bufs[j], sem_g.at[j])

    def issue_rows(ib, buf, sem):
        c = ib[0] >> 31
        for i in range(npt):
            e = ib[i] + c
            pltpu.make_async_copy(tab_hbm.at[e], buf.at[i], sem).start(priority=i % 2)
            if i % PEER_CHAIN == PEER_CHAIN - 1:
                c = e >> 31

    @pl.when(s == 0)
    def _():
        for j in range(2):
            first = idx_fetch(j, j)
            first.start()
            first.wait()

            def body(i, carry):
                pltpu.make_async_copy(tab_hbm.at[ibs[j][i]], bufs[j].at[i], sem_g.at[j]).start()
                return carry

            lax.fori_loop(0, npt, body, 0)
        idx_fetch(2, 0).start()

    row8 = lax.broadcasted_iota(jnp.int32, (half, TABLE_COLS), 0)
    col8 = lax.broadcasted_iota(jnp.int32, (half, TABLE_COLS), 1) % TABLE_ROWS
    mask_v = col8 == row8 + half
    lane = lax.broadcasted_iota(jnp.int32, (PEER_PICKS, HEAD_PAD), 1)
    gsum = gsum_ref[...]

    def evaluate(cur, tok0):
        def ubody(it, at):
            for u in range(PEER_UNROLL):
                t = it * PEER_UNROLL + u
                hrow = hs_ref[tok0 + t]
                h16 = jnp.concatenate([hrow, jnp.zeros_like(hrow)], axis=0).astype(BF16)
                tw = cur[pl.ds(t * PEER_PICKS, PEER_PICKS)]
                prod = (tw * h16[None]).reshape(TABLE_COLS, HEAD_PAD)
                parts = [_dot(gsum, prod[g * grows:(g + 1) * grows]) for g in range(ngrp)]
                z = jnp.sum(jnp.concatenate(parts, axis=0), axis=1, keepdims=True)
                at = jnp.where(lane == t, z, at)
            return at

        at = jnp.zeros((PEER_PICKS, HEAD_PAD), F32)
        for it in range(tt // PEER_UNROLL):
            at = ubody(it, at)
        a = at.T[:tt]
        w = jax.nn.gelu(a) * g_ref[tok0:tok0 + tt, :]
        wr_ref[...] = _dot(w.astype(BF16), ev_ref[...])

        def vbody(it, carry):
            for u in range(PEER_UNROLL):
                t = it * PEER_UNROLL + u
                wrow = jnp.broadcast_to(wr_ref[pl.ds(t, 1), :], (half, TABLE_COLS))
                wexp = jnp.where(mask_v, wrow, 0.0).astype(BF16)
                wb = cur[pl.ds(t * PEER_PICKS, PEER_PICKS)].reshape(TABLE_COLS, HEAD_PAD)
                y_ref[tok0 + t] = _dot(wexp, wb)
            return carry

        for it in range(tt // PEER_UNROLL):
            vbody(it, 0)

    for p in range(PEER_PHASES):
        k = PEER_PHASES * s + p
        idx_fetch(k + 2, p % 2).wait()
        idx_fetch(k + 3, (p + 1) % 2).start()
        rows_done(p).wait()
        issue_rows(ibs[p % 2], bufs[(p + 2) % PEER_PHASES], sem_g.at[(p + 2) % PEER_PHASES])
        evaluate(bufs[p], p * tt)

    @pl.when(s == last)
    def _():
        rows_done(0).wait()
        rows_done(1).wait()
        idx_fetch(0, 0).wait()


def _peer_call(idx_flat, table, hs3, gates):
    T = hs3.shape[0]
    tt = PEER_TILE
    npt = tt * PEER_PICKS
    assert tt % PEER_UNROLL == 0 and PEER_PICKS % PEER_GROUP == 0 and tt <= HEAD_PAD and PEER_PHASES == 4
    gsum, ev = _peer_consts()
    idx_pad = jnp.pad(idx_flat, (0, 3 * npt))
    step = PEER_PHASES * tt
    full = lambda a: pl.BlockSpec(a.shape, lambda s: (0,) * a.ndim)
    rows = pltpu.VMEM((npt, TABLE_ROWS, HEAD_PAD), BF16)
    return pl.pallas_call(
        _peer_kernel,
        grid=(T // step,),
        in_specs=[pl.BlockSpec(memory_space=pl.ANY),
                  pl.BlockSpec(memory_space=pl.ANY),
                  pl.BlockSpec((step, 8, HEAD_PAD), lambda s: (s, 0, 0)),
                  pl.BlockSpec((step, PEER_PICKS), lambda s: (s, 0)),
                  full(gsum), full(ev)],
        out_specs=pl.BlockSpec((step, 8, HEAD_PAD), lambda s: (s, 0, 0)),
        out_shape=jax.ShapeDtypeStruct((T, 8, HEAD_PAD), F32),
        scratch_shapes=[rows, rows, rows, rows,
                        pltpu.SMEM((npt,), jnp.int32),
                        pltpu.SMEM((npt,), jnp.int32),
                        pltpu.VMEM((tt, TABLE_COLS), F32),
                        pltpu.SemaphoreType.DMA((PEER_PHASES,)),
                        pltpu.SemaphoreType.DMA((2,))],
        compiler_params=_params("arbitrary"),
        name="peer",
    )(idx_pad, table, hs3, gates, gsum, ev)


SC_LANES = 16
SC_GROUP = 16
_GELU_C = 0.7978845608028654


def _peer_sc_kernel(idx_hbm, tab_hbm, h_hbm, g_hbm, y_hbm,
                    idx_a, idx_b, g_a, g_b, h_a, h_b, o_v, rows_v, sem, sem_in):
    ncores = lax.axis_size("c")
    wid = lax.axis_index("s") * ncores + lax.axis_index("c")
    per = h_hbm.shape[0] // (ncores * lax.axis_size("s"))
    base = wid * per
    nchunk = D_MODEL // SC_LANES
    ngrp = PEER_PICKS // SC_GROUP
    lane = lax.iota(jnp.int32, SC_LANES)
    sets = ((idx_a, g_a, h_a), (idx_b, g_b, h_b))

    def fetch(t, k):
        return (pltpu.make_async_copy(idx_hbm.at[t], sets[k][0], sem_in.at[k]),
                pltpu.make_async_copy(g_hbm.at[t], sets[k][1], sem_in.at[k]),
                pltpu.make_async_copy(h_hbm.at[t], sets[k][2], sem_in.at[k]))

    def gather(k, g, slot):
        return pltpu.make_async_copy(tab_hbm.at[sets[k][0].at[pl.ds(g * SC_GROUP, SC_GROUP)]], rows_v.at[slot],
                                     sem.at[slot])

    def process(t, k, t_next):
        g_v, h_v = sets[k][1], sets[k][2]

        def zero(c, carry):
            o_v[pl.ds(c * SC_LANES, SC_LANES)] = jnp.zeros((SC_LANES,), F32)
            return carry

        lax.fori_loop(0, nchunk, zero, 0)

        for g in range(ngrp):
            slot = g % 2
            if g + 1 < ngrp:
                gather(k, g + 1, 1 - slot).start()
            else:
                for c in fetch(t_next, 1 - k):
                    c.wait()
                gather(1 - k, 0, 0).start()
            gather(k, g, slot).wait()

            def dot_body(c, accs):
                hv = h_v[pl.ds(c * SC_LANES, SC_LANES)]
                out = []
                for p in range(SC_GROUP):
                    w = rows_v[slot, p, pl.ds(c * SC_LANES, SC_LANES)]
                    u = lax.bitcast_convert_type(w << 16, F32)
                    out.append(accs[p] + u * hv)
                return tuple(out)

            accs = lax.fori_loop(0, nchunk, dot_body,
                                 tuple(jnp.zeros((SC_LANES,), F32) for _ in range(SC_GROUP)))
            a = jnp.zeros((SC_LANES,), F32)
            for p in range(SC_GROUP):
                a = jnp.where(lane == p, jnp.sum(accs[p]), a)
            z = _GELU_C * (a + 0.044715 * a * a * a)
            th = 1.0 - 2.0 / (jnp.exp(2.0 * z) + 1.0)
            wv = 0.5 * a * (1.0 + th) * g_v[pl.ds(g * SC_GROUP, SC_GROUP)]
            ws = [jnp.sum(jnp.where(lane == p, wv, 0.0)) for p in range(SC_GROUP)]

            def ax_body(c, carry):
                o = o_v[pl.ds(c * SC_LANES, SC_LANES)]
                for p in range(SC_GROUP):
                    w = rows_v[slot, p, pl.ds(c * SC_LANES, SC_LANES)]
                    v = lax.bitcast_convert_type(w & jnp.int32(-65536), F32)
                    o = o + ws[p] * v
                o_v[pl.ds(c * SC_LANES, SC_LANES)] = o
                return carry

            lax.fori_loop(0, nchunk, ax_body, 0)

        pltpu.sync_copy(o_v, y_hbm.at[t])

    for c in fetch(base, 0):
        c.start()
    for c in fetch(base, 0):
        c.wait()
    gather(0, 0, 0).start()

    def pair(j, carry):
        t0 = base + 2 * j
        t2 = jnp.minimum(t0 + 2, base + per - 1)
        for c in fetch(t0 + 1, 1):
            c.start()
        process(t0, 0, t0 + 1)
        for c in fetch(t2, 0):
            c.start()
        process(t0 + 1, 1, t2)
        return carry

    lax.fori_loop(0, per // 2, pair, 0)
    gather(0, 0, 0).wait()


def _peer_sc_call(idx2, table_i32, h2, gates):
    ts = h2.shape[0]
    mesh = plsc.VectorSubcoreMesh(core_axis_name="c", subcore_axis_name="s")
    run = pl.kernel(
        _peer_sc_kernel,
        out_type=jax.ShapeDtypeStruct((ts, D_MODEL), F32),
        mesh=mesh,
        scratch_types=[pltpu.VMEM((PEER_PICKS,), jnp.int32), pltpu.VMEM((PEER_PICKS,), jnp.int32),
                       pltpu.VMEM((PEER_PICKS,), F32), pltpu.VMEM((PEER_PICKS,), F32),
                       pltpu.VMEM((D_MODEL,), F32), pltpu.VMEM((D_MODEL,), F32),
                       pltpu.VMEM((D_MODEL,), F32),
                       pltpu.VMEM((2, SC_GROUP, D_MODEL), jnp.int32),
                       pltpu.SemaphoreType.DMA((2,)),
                       pltpu.SemaphoreType.DMA((2,))],
        compiler_params=pltpu.CompilerParams(needs_layout_passes=False),
        name="peer_sc",
    )
    return run(idx2, table_i32, h2, gates)


def _final_kernel(x1_ref, y_ref, mod_ref, g_ref, o_ref):
    gt2 = mod_ref[0, 5:6, :]
    o_ref[0] = x1_ref[0] + gt2 * (_rms(y_ref[0]) * g_ref[...])


def _final_call(x1, y, mod, g_post):
    B, S, _ = x1.shape
    tr = min(ROW_TILE, S)
    row = pl.BlockSpec((1, tr, D_MODEL), lambda b, i: (b, i, 0))
    return pl.pallas_call(
        _final_kernel,
        grid=(B, S // tr),
        in_specs=[row, row, pl.BlockSpec((1, N_MOD, D_MODEL), lambda b, i: (b, 0, 0)),
                  pl.BlockSpec((1, D_MODEL), lambda b, i: (0, 0))],
        out_specs=row,
        out_shape=jax.ShapeDtypeStruct((B, S, D_MODEL), F32),
        compiler_params=_params("arbitrary", "arbitrary"),
        name="final",
    )(x1, y, mod, g_post)


def _pad_heads_cols(w, nh, scale=1.0):
    k = w.shape[0]
    w = (w * scale).reshape(k, nh, HEAD_DIM)
    return jnp.pad(w, ((0, 0), (0, 0), (0, HEAD_PAD - HEAD_DIM))).reshape(k, nh * HEAD_PAD)


def _pad_heads_rows(w, nh):
    n = w.shape[1]
    w = w.reshape(nh, HEAD_DIM, n)
    return jnp.pad(w, ((0, 0), (0, HEAD_PAD - HEAD_DIM), (0, 0))).reshape(nh * HEAD_PAD, n)


def _layer(x, c8, w_ada, b_ada, g_pre_mix, g_post_mix, g_pre_ffn, g_post_ffn,
           w_in, b_fgate, swa_sinks, w_out, w_query, sub_keys, w_u, w_v):
    B, S, D = x.shape
    T = B * S
    scale = HEAD_DIM ** -0.5
    mod = _ada_call(c8, w_ada, b_ada)[:B].reshape(B, N_MOD, D)

    o = 0
    parts = []
    for nh, sc in ((FOX_HEADS, scale), (FOX_HEADS, 1.0), (FOX_HEADS, 1.0)):
        parts.append(_pad_heads_cols(w_in[:, o:o + nh * HEAD_DIM], nh, sc))
        o += nh * HEAD_DIM
    parts.append(jnp.pad(w_in[:, o:o + FOX_HEADS], ((0, 0), (0, HEAD_PAD - FOX_HEADS))))
    o += FOX_HEADS
    for nh, sc in ((SWA_HEADS, scale), (SWA_KV_HEADS, 1.0), (SWA_KV_HEADS, 1.0)):
        parts.append(_pad_heads_cols(w_in[:, o:o + nh * HEAD_DIM], nh, sc))
        o += nh * HEAD_DIM
    w_all = jnp.concatenate(parts, axis=1).astype(BF16)
    bf_pad = jnp.pad(b_fgate, (0, HEAD_PAD - FOX_HEADS)).reshape(1, HEAD_PAD)

    qt, kp, vt, sq, sk, sv = _inproj_call(x, mod, g_pre_mix.reshape(1, D), w_all, bf_pad)
    fo = _fox_call(qt, kp, vt)
    so = _swa_call(swa_sinks, sq, sk, sv)

    nf = FOX_HEADS * HEAD_DIM
    wof = _pad_heads_rows(w_out[:nf], FOX_HEADS).astype(BF16)
    wos = _pad_heads_rows(w_out[nf:], SWA_HEADS).astype(BF16)
    keys = sub_keys.reshape(2 * PEER_HEADS, N_KEYS, PEER_HALF).astype(BF16)
    x1, h2, idx, gates = _route_call(fo, so, x, mod, g_post_mix.reshape(1, D), g_pre_ffn.reshape(1, D),
                                     wof, wos, w_query.astype(BF16), keys)

    ub, vb = w_u.astype(BF16), w_v.astype(BF16)
    table = jnp.concatenate([ub.reshape(N_EXPERTS, 8, HEAD_PAD), vb.reshape(N_EXPERTS, 8, HEAD_PAD)], axis=1)
    h2f = h2.reshape(T, D)
    ts = (int(T * SC_SHARE) // 1024) * 1024
    t1 = T - ts
    y_tc = _peer_call(idx[:t1].reshape(t1 * PEER_PICKS), table, h2f[:t1].reshape(t1, 8, HEAD_PAD), gates[:t1])
    y = y_tc.reshape(t1, D)
    if ts:
        u16 = lax.bitcast_convert_type(ub, jnp.uint16).astype(jnp.uint32)
        v16 = lax.bitcast_convert_type(vb, jnp.uint16).astype(jnp.uint32)
        table_i32 = lax.bitcast_convert_type((v16 << 16) | u16, jnp.int32)
        y_sc = _peer_sc_call(idx[t1:], table_i32, h2f[t1:], gates[t1:])
        y = jnp.concatenate([y, y_sc], axis=0)
    return _final_call(x1, y.reshape(B, S, D), mod, g_post_ffn.reshape(1, D))


def kernel(x, c, w_ada, b_ada, g_pre_mix, g_post_mix, g_pre_ffn, g_post_ffn, w_in, b_fgate, swa_sinks, w_out,
           w_query, sub_keys, w_u, w_v):
    B = x.shape[0]
    c8 = jnp.pad(c, ((0, 8 - B), (0, 0)))
    for l in range(w_ada.shape[0]):
        x = _layer(x, c8, w_ada[l], b_ada[l], g_pre_mix[l], g_post_mix[l], g_pre_ffn[l], g_post_ffn[l],
                   w_in[l], b_fgate[l], swa_sinks[l], w_out[l], w_query[l], sub_keys[l], w_u[l], w_v[l])
    return x
```

```python
import numpy as np
import jax
import jax.numpy as jnp
from jax import lax
from jax.experimental import pallas as pl
from jax.experimental.pallas import tpu as pltpu
from jax.experimental.pallas import tpu_sc as plsc

F32 = jnp.float32
BF16 = jnp.bfloat16

D_MODEL = 1024
HEAD_DIM = 64
HEAD_PAD = 128
FOX_HEADS = 8
SWA_HEADS = 8
SWA_KV_HEADS = 2
SWA_GROUP = SWA_HEADS // SWA_KV_HEADS
WINDOW = 128
PEER_HEADS = 8
PEER_HALF = 128
N_KEYS = 128
N_EXPERTS = N_KEYS * N_KEYS
PEER_TOPK = 16
PEER_PICKS = PEER_HEADS * PEER_TOPK
N_MOD = 6
RMS_EPS = 1e-6
NEG_INF = -1e30

_F_LANE = HEAD_DIM
_ONE_LANE = HEAD_DIM

ROW_TILE = 1024
FOX_TILE = 1024
FOX_PAR = 1
SWA_TILE = 512
ROUTE_TILE = 256
PEER_TILE = 8
PEER_PHASES = 4
PEER_UNROLL = 8
PEER_CHAIN = 8
PEER_GROUP = 16
SC_SHARE = 0.375
TABLE_ROWS = 16
TABLE_COLS = PEER_PICKS * TABLE_ROWS

_VMEM_LIMIT = 56 * 1024 * 1024


def _dot(a, b):
    return jnp.dot(a, b, preferred_element_type=F32)


def _dot_nt(a, b):
    return lax.dot_general(a, b, (((1,), (1,)), ((), ())), preferred_element_type=F32)


def _split3(x):
    hi = x.astype(BF16)
    r = x - hi.astype(F32)
    mid = r.astype(BF16)
    lo = (r - mid.astype(F32)).astype(BF16)
    return hi, mid, lo


def _rms(x):
    return x * lax.rsqrt(jnp.mean(x * x, axis=-1, keepdims=True) + RMS_EPS)


def _params(*sem):
    return pltpu.CompilerParams(dimension_semantics=sem, vmem_limit_bytes=_VMEM_LIMIT)


def _ada_kernel(c_ref, w_ref, b_ref, o_ref):
    c = c_ref[...]
    s = (c * jax.nn.sigmoid(c)).astype(BF16)
    o_ref[...] = _dot(s, w_ref[...].astype(BF16)) + b_ref[...]


def _ada_call(c8, w_ada, b_ada):
    n = w_ada.shape[1]
    tn = 1536
    return pl.pallas_call(
        _ada_kernel,
        grid=(n // tn,),
        in_specs=[pl.BlockSpec((8, D_MODEL), lambda j: (0, 0)),
                  pl.BlockSpec((D_MODEL, tn), lambda j: (0, j)),
                  pl.BlockSpec((1, tn), lambda j: (0, j))],
        out_specs=pl.BlockSpec((8, tn), lambda j: (0, j)),
        out_shape=jax.ShapeDtypeStruct((8, n), F32),
        compiler_params=_params("arbitrary"),
        name="ada",
    )(c8, w_ada, b_ada.reshape(1, n))


_NQ = FOX_HEADS * HEAD_PAD
_NS = SWA_HEADS * HEAD_PAD
_NKV = SWA_KV_HEADS * HEAD_PAD
_IN_COLS = 3 * _NQ + HEAD_PAD + _NS + 2 * _NKV


def _inproj_kernel(x_ref, mod_ref, g_ref, w_ref, bf_ref, tri_ref, pq_ref, pk_ref, cst_ref,
                   qt_ref, kp_ref, vt_ref, sq_ref, sk_ref, sv_ref, carry_ref):
    i = pl.program_id(1)

    @pl.when(i == 0)
    def _():
        carry_ref[...] = jnp.zeros_like(carry_ref)

    x = x_ref[0]
    sh1 = mod_ref[0, 0:1, :]
    sc1 = mod_ref[0, 1:2, :]
    h = _rms(x) * g_ref[...] * (1.0 + sc1) + sh1
    proj = _dot(h.astype(BF16), w_ref[...])

    z = proj[:, 3 * _NQ:3 * _NQ + HEAD_PAD] + bf_ref[...]
    ls = jnp.minimum(z, 0.0) - jnp.log(1.0 + jnp.exp(-jnp.abs(z)))
    tri = tri_ref[...]
    hi, mid, lo = _split3(ls)
    fcum = _dot(tri, hi) + _dot(tri, mid) + _dot(tri, lo) + carry_ref[...]
    carry_ref[...] = fcum[fcum.shape[0] - 1:, :]

    fh, fm, fl = _split3(fcum)
    eq = _dot(fh, pq_ref[0]) + _dot(fm, pq_ref[1]) + _dot(fl, pq_ref[2]) + cst_ref[0:1, :]
    ek = _dot(fh, pk_ref[0]) + _dot(fm, pk_ref[1]) + _dot(fl, pk_ref[2]) + cst_ref[1:2, :]
    qt_ref[0, 0] = (proj[:, 0:_NQ] + eq).T.astype(BF16)
    kp_ref[0] = (proj[:, _NQ:2 * _NQ] + ek).astype(BF16)
    vt_ref[0, 0] = (proj[:, 2 * _NQ:3 * _NQ] + cst_ref[2:3, :]).T.astype(BF16)
    o = 3 * _NQ + HEAD_PAD
    sq_ref[0] = proj[:, o:o + _NS].astype(BF16)
    sk_ref[0] = proj[:, o + _NS:o + _NS + _NKV].astype(BF16)
    sv_ref[0] = proj[:, o + _NS + _NKV:o + _NS + 2 * _NKV].astype(BF16)


def _inproj_consts(tr):
    tri = np.tril(np.ones((tr, tr), np.float32))
    pq = np.zeros((3, HEAD_PAD, _NQ), np.float32)
    pk = np.zeros((3, HEAD_PAD, _NQ), np.float32)
    cst = np.zeros((8, _NQ), np.float32)
    for h in range(FOX_HEADS):
        b = h * HEAD_PAD + _F_LANE
        for j in range(3):
            pq[j, h, b + j] = 1.0
            pk[j, h, b + 3 + j] = -1.0
            cst[0, b + 3 + j] = 1.0
            cst[1, b + j] = 1.0
        cst[2, h * HEAD_PAD + _ONE_LANE] = 1.0
    return (jnp.asarray(tri, BF16), jnp.asarray(pq, BF16), jnp.asarray(pk, BF16), jnp.asarray(cst, F32))


def _inproj_call(x, mod, g_pre, w_all, bf_pad):
    B, S, _ = x.shape
    tr = min(ROW_TILE, S)
    n = S // tr
    tri, pq, pk, cst = _inproj_consts(tr)
    row = lambda w: pl.BlockSpec((1, tr, w), lambda b, i: (b, i, 0))
    slab = pl.BlockSpec((1, 1, _NQ, tr), lambda b, i: (b, i, 0, 0))
    full = lambda a: pl.BlockSpec(a.shape, lambda b, i: (0,) * a.ndim)
    outs = [jax.ShapeDtypeStruct((B, n, _NQ, tr), BF16), jax.ShapeDtypeStruct((B, S, _NQ), BF16),
            jax.ShapeDtypeStruct((B, n, _NQ, tr), BF16), jax.ShapeDtypeStruct((B, S, _NS), BF16),
            jax.ShapeDtypeStruct((B, S, _NKV), BF16), jax.ShapeDtypeStruct((B, S, _NKV), BF16)]
    return pl.pallas_call(
        _inproj_kernel,
        grid=(B, n),
        in_specs=[row(D_MODEL),
                  pl.BlockSpec((1, N_MOD, D_MODEL), lambda b, i: (b, 0, 0)),
                  full(g_pre), full(w_all), full(bf_pad), full(tri), full(pq), full(pk), full(cst)],
        out_specs=[slab, row(_NQ), slab, row(_NS), row(_NKV), row(_NKV)],
        out_shape=outs,
        scratch_shapes=[pltpu.VMEM((1, HEAD_PAD), F32)],
        compiler_params=_params("arbitrary", "arbitrary"),
        name="inproj",
    )(x, mod, g_pre, w_all, bf_pad, tri, pq, pk, cst)


def _fox_kernel(qt_ref, k_ref, vt_ref, o_ref, m_ref, acc_ref):
    i = pl.program_id(2)
    t = o_ref.shape[1]
    m_ref[...] = jnp.full_like(m_ref, NEG_INF)
    acc_ref[...] = jnp.zeros_like(acc_ref)

    def block(j, masked):
        off = pl.multiple_of(j * t, t)
        for hh in range(FOX_PAR):
            lanes = slice(hh * HEAD_PAD, (hh + 1) * HEAD_PAD)
            k = k_ref[0, pl.ds(off, t), lanes]
            s = _dot(k, qt_ref[0, 0, lanes, :])
            if masked:
                r = lax.broadcasted_iota(jnp.int32, s.shape, 0)
                c = lax.broadcasted_iota(jnp.int32, s.shape, 1)
                s = jnp.where(r <= c, s, NEG_INF)
            m_prev = m_ref[hh]
            m_new = jnp.maximum(m_prev, jnp.max(s, axis=0, keepdims=True))
            p = jnp.exp(s - m_new)
            acc_ref[hh] = jnp.exp(m_prev - m_new) * acc_ref[hh] + _dot(vt_ref[0, j, lanes, :], p.astype(BF16))
            m_ref[hh] = m_new

    def body(j, carry):
        block(j, False)
        return carry

    lax.fori_loop(0, i, body, 0)
    block(i, True)
    for hh in range(FOX_PAR):
        acc = acc_ref[hh]
        o_ref[0, :, hh * HEAD_PAD:(hh + 1) * HEAD_PAD] = (acc / acc[_ONE_LANE:_ONE_LANE + 1, :]).T.astype(BF16)


def _fox_call(qt, kp, vt):
    B, n, _, t = qt.shape
    S = n * t
    w = FOX_PAR * HEAD_PAD
    return pl.pallas_call(
        _fox_kernel,
        grid=(B, FOX_HEADS // FOX_PAR, n),
        in_specs=[pl.BlockSpec((1, 1, w, t), lambda b, h, i: (b, i, h, 0)),
                  pl.BlockSpec((1, S, w), lambda b, h, i: (b, 0, h)),
                  pl.BlockSpec((1, n, w, t), lambda b, h, i: (b, 0, h, 0))],
        out_specs=pl.BlockSpec((1, t, w), lambda b, h, i: (b, i, h)),
        out_shape=jax.ShapeDtypeStruct((B, S, _NQ), BF16),
        scratch_shapes=[pltpu.VMEM((FOX_PAR, 1, t), F32), pltpu.VMEM((FOX_PAR, HEAD_PAD, t), F32)],
        compiler_params=_params("arbitrary", "arbitrary", "arbitrary"),
        name="fox",
    )(qt, kp, vt)


def _swa_kernel(sink_ref, q_ref, kc_ref, kp_ref, vc_ref, vp_ref, o_ref):
    i = pl.program_id(1)
    nsub = q_ref.shape[1] // WINDOW
    r = lax.broadcasted_iota(jnp.int32, (WINDOW, 2 * WINDOW), 0)
    j = lax.broadcasted_iota(jnp.int32, (WINDOW, 2 * WINDOW), 1)
    dist = r + WINDOW - j
    valid = (dist >= 0) & (dist < WINDOW)
    distf = dist.astype(F32)
    for qb in range(nsub):
        rows = slice(qb * WINDOW, (qb + 1) * WINDOW)
        if qb == 0:
            ok = valid & ((j >= WINDOW) | (i > 0))
        else:
            ok = valid
        for g in range(SWA_KV_HEADS):
            lanes = slice(g * HEAD_PAD, (g + 1) * HEAD_PAD)
            if qb == 0:
                kprev, vprev = kp_ref[0, :, lanes], vp_ref[0, :, lanes]
            else:
                prev = slice((qb - 1) * WINDOW, qb * WINDOW)
                kprev, vprev = kc_ref[0, prev, lanes], vc_ref[0, prev, lanes]
            kk = jnp.concatenate([kprev, kc_ref[0, rows, lanes]], axis=0)
            vv = jnp.concatenate([vprev, vc_ref[0, rows, lanes]], axis=0)
            for u in range(SWA_GROUP):
                hq = g * SWA_GROUP + u
                slope = 2.0 ** (-8.0 * (hq + 1) / SWA_HEADS)
                q = q_ref[0, rows, hq * HEAD_PAD:(hq + 1) * HEAD_PAD]
                s = _dot_nt(q, kk) - slope * distf
                s = jnp.where(ok, s, NEG_INF)
                sink = sink_ref[hq]
                m = jnp.maximum(jnp.max(s, axis=1, keepdims=True), sink)
                p = jnp.exp(s - m)
                den = jnp.sum(p, axis=1, keepdims=True) + jnp.exp(sink - m)
                o = _dot(p.astype(BF16), vv) / den
                o_ref[0, rows, hq * HEAD_PAD:(hq + 1) * HEAD_PAD] = o.astype(BF16)


def _swa_call(sinks, sq, sk, sv):
    B, S, _ = sq.shape
    t = min(SWA_TILE, S)
    per = t // WINDOW
    cur = lambda w: pl.BlockSpec((1, t, w), lambda b, i: (b, i, 0))
    prv = lambda w: pl.BlockSpec((1, WINDOW, w), lambda b, i: (b, jnp.maximum(i * per - 1, 0), 0))
    return pl.pallas_call(
        _swa_kernel,
        grid=(B, S // t),
        in_specs=[pl.BlockSpec(memory_space=pltpu.SMEM),
                  cur(_NS), cur(_NKV), prv(_NKV), cur(_NKV), prv(_NKV)],
        out_specs=cur(_NS),
        out_shape=jax.ShapeDtypeStruct((B, S, _NS), BF16),
        compiler_params=_params("arbitrary", "arbitrary"),
        name="swa",
    )(sinks, sq, sk, sk, sv, sv)


_NO_ID = 1 << 20


def _topk_rows(s, k, val_ref, idx_ref, ids=None):
    if ids is None:
        ids = lax.broadcasted_iota(jnp.int32, s.shape, 0)
    for r in range(k):
        m = jnp.max(s, axis=0, keepdims=True)
        i = jnp.min(jnp.where(s == m, ids, _NO_ID), axis=0, keepdims=True)
        val_ref[pl.ds(r, 1), :] = m
        idx_ref[pl.ds(r, 1), :] = i
        s = jnp.where(ids == i, -jnp.inf, s)


def _cand_counts():
    return [PEER_TOPK // (a + 1) for a in range(PEER_TOPK)]


_CAND_ROWS = 56


def _route_kernel(fo_ref, so_ref, x_ref, mod_ref, gpost_ref, gpre_ref, wof_ref, wos_ref, wq_ref, keys_ref, cid_ref,
                  x1_ref, h2_ref, idx_ref, gate_ref, qs_ref, sv_ref, si_ref, et_ref, gt_ref, cand_ref):
    x = x_ref[0]
    gt1 = mod_ref[0, 2:3, :]
    sh2 = mod_ref[0, 3:4, :]
    sc2 = mod_ref[0, 4:5, :]
    y = _dot(fo_ref[0], wof_ref[...]) + _dot(so_ref[0], wos_ref[...])
    x1 = x + gt1 * (_rms(y) * gpost_ref[...])
    x1_ref[0] = x1
    h2 = _rms(x1) * gpre_ref[...] * (1.0 + sc2) + sh2
    h2_ref[0] = h2
    qp = _dot(h2.astype(BF16), wq_ref[...])
    nhp = 2 * PEER_HEADS
    for hp in range(nhp):
        qs_ref[hp] = qp[:, hp * PEER_HALF:(hp + 1) * PEER_HALF].astype(BF16)

    def half(hp, carry):
        sc = _dot_nt(keys_ref[hp], qs_ref[hp])
        _topk_rows(sc, PEER_TOPK, sv_ref.at[hp], si_ref.at[hp])
        return carry

    lax.fori_loop(0, nhp, half, 0)

    counts = _cand_counts()
    used = sum(counts)
    cand_ref[pl.ds(used, _CAND_ROWS - used), :] = jnp.full((_CAND_ROWS - used, cand_ref.shape[1]), -jnp.inf, F32)
    cid = cid_ref[...]

    def head(h, carry):
        v0, v1 = sv_ref[2 * h], sv_ref[2 * h + 1]
        i0, i1 = si_ref[2 * h], si_ref[2 * h + 1]
        off = 0
        for a, nb in enumerate(counts):
            cand_ref[pl.ds(off, nb), :] = v0[a:a + 1, :] + v1[0:nb, :]
            off += nb
        _topk_rows(cand_ref[...], PEER_TOPK, gt_ref.at[h], et_ref.at[h], ids=cid)
        cv, ci = gt_ref[h], et_ref[h]
        ca, cb = ci >> 4, ci & (PEER_TOPK - 1)
        e1 = jnp.zeros_like(ci)
        e2 = jnp.zeros_like(ci)
        for a in range(PEER_TOPK):
            e1 = jnp.where(ca == a, i0[a:a + 1, :], e1)
            e2 = jnp.where(cb == a, i1[a:a + 1, :], e2)
        et_ref[h] = e1 * N_KEYS + e2
        ex = jnp.exp(cv - cv[0:1, :])
        gt_ref[h] = ex / jnp.sum(ex, axis=0, keepdims=True)
        return carry

    lax.fori_loop(0, PEER_HEADS, head, 0)
    tt = et_ref.shape[2]
    et = lax.bitcast_convert_type(et_ref[...].reshape(PEER_PICKS, tt), F32)
    idx_ref[...] = lax.bitcast_convert_type(et.T, jnp.int32)
    gate_ref[...] = gt_ref[...].reshape(PEER_PICKS, tt).T


def _route_call(fo, so, x, mod, g_post, g_pre, wof, wos, wq, keys):
    B, S, _ = x.shape
    tt = min(ROUTE_TILE, S)
    per = S // tt
    row = lambda w: pl.BlockSpec((1, tt, w), lambda b, i: (b, i, 0))
    full = lambda a: pl.BlockSpec(a.shape, lambda b, i: (0,) * a.ndim)
    tok = pl.BlockSpec((tt, PEER_PICKS), lambda b, i: (b * per + i, 0))
    flat = [a * PEER_TOPK + b for a, nb in enumerate(_cand_counts()) for b in range(nb)]
    flat += [_NO_ID] * (_CAND_ROWS - len(flat))
    cid = jnp.asarray(np.broadcast_to(np.asarray(flat, np.int32)[:, None], (_CAND_ROWS, tt)))
    return pl.pallas_call(
        _route_kernel,
        grid=(B, per),
        in_specs=[row(_NQ), row(_NS), row(D_MODEL),
                  pl.BlockSpec((1, N_MOD, D_MODEL), lambda b, i: (b, 0, 0)),
                  full(g_post), full(g_pre), full(wof), full(wos), full(wq), full(keys), full(cid)],
        out_specs=[row(D_MODEL), row(D_MODEL), tok, tok],
        out_shape=[jax.ShapeDtypeStruct((B, S, D_MODEL), F32),
                   jax.ShapeDtypeStruct((B, S, D_MODEL), F32),
                   jax.ShapeDtypeStruct((B * S, PEER_PICKS), jnp.int32),
                   jax.ShapeDtypeStruct((B * S, PEER_PICKS), F32)],
        scratch_shapes=[pltpu.VMEM((2 * PEER_HEADS, tt, PEER_HALF), BF16),
                        pltpu.VMEM((2 * PEER_HEADS, PEER_TOPK, tt), F32),
                        pltpu.VMEM((2 * PEER_HEADS, PEER_TOPK, tt), jnp.int32),
                        pltpu.VMEM((PEER_HEADS, PEER_TOPK, tt), jnp.int32),
                        pltpu.VMEM((PEER_HEADS, PEER_TOPK, tt), F32),
                        pltpu.VMEM((_CAND_ROWS, tt), F32)],
        compiler_params=_params("arbitrary", "arbitrary"),
        name="route",
    )(fo, so, x, mod, g_post, g_pre, wof, wos, wq, keys, cid)


def _peer_consts():
    half = TABLE_ROWS // 2
    gsum = np.zeros((PEER_GROUP, PEER_GROUP * TABLE_ROWS), np.float32)
    ev = np.zeros((PEER_PICKS, TABLE_COLS), np.float32)
    for j in range(PEER_GROUP):
        gsum[j, j * TABLE_ROWS:j * TABLE_ROWS + half] = 1.0
    for k in range(PEER_PICKS):
        ev[k, k * TABLE_ROWS + half:(k + 1) * TABLE_ROWS] = 1.0
    return jnp.asarray(gsum, BF16), jnp.asarray(ev, BF16)


def _peer_kernel(idx_hbm, tab_hbm, hs_ref, g_ref, gsum_ref, ev_ref, y_ref,
                 buf0, buf1, buf2, buf3, ib0, ib1, wr_ref, sem_g, sem_i):
    s = pl.program_id(0)
    last = pl.num_programs(0) - 1
    tt = PEER_TILE
    npt = tt * PEER_PICKS
    half = TABLE_ROWS // 2
    ngrp = PEER_PICKS // PEER_GROUP
    grows = PEER_GROUP * TABLE_ROWS
    bufs = (buf0, buf1, buf2, buf3)
    ibs = (ib0, ib1)

    def idx_fetch(tile, j):
        return pltpu.make_async_copy(idx_hbm.at[pl.ds(tile * npt, npt)], ibs[j], sem_i.at[j])

    def rows_done(j):
        return pltpu.make_async_copy(tab_hbm.at[pl.ds(0, npt)], bufs[j], sem_g.at[j])

    def issue_rows(ib, buf, sem):
        c = ib[0] >> 31
        for i in range(npt):
            e = ib[i] + c
            pltpu.make_async_copy(tab_hbm.at[e], buf.at[i], sem).start(priority=i % 2)
            if i % PEER_CHAIN == PEER_CHAIN - 1:
                c = e >> 31

    @pl.when(s == 0)
    def _():
        for j in range(2):
            first = idx_fetch(j, j)
            first.start()
            first.wait()

            def body(i, carry):
                pltpu.make_async_copy(tab_hbm.at[ibs[j][i]], bufs[j].at[i], sem_g.at[j]).start()
                return carry

            lax.fori_loop(0, npt, body, 0)
        idx_fetch(2, 0).start()

    row8 = lax.broadcasted_iota(jnp.int32, (half, TABLE_COLS), 0)
    col8 = lax.broadcasted_iota(jnp.int32, (half, TABLE_COLS), 1) % TABLE_ROWS
    mask_v = col8 == row8 + half
    lane = lax.broadcasted_iota(jnp.int32, (PEER_PICKS, HEAD_PAD), 1)
    gsum = gsum_ref[...]

    def evaluate(cur, tok0):
        def ubody(it, at):
            for u in range(PEER_UNROLL):
                t = it * PEER_UNROLL + u
                hrow = hs_ref[tok0 + t]
                h16 = jnp.concatenate([hrow, jnp.zeros_like(hrow)], axis=0).astype(BF16)
                tw = cur[pl.ds(t * PEER_PICKS, PEER_PICKS)]
                prod = (tw * h16[None]).reshape(TABLE_COLS, HEAD_PAD)
                parts = [_dot(gsum, prod[g * grows:(g + 1) * grows]) for g in range(ngrp)]
                z = jnp.sum(jnp.concatenate(parts, axis=0), axis=1, keepdims=True)
                at = jnp.where(lane == t, z, at)
            return at

        at = jnp.zeros((PEER_PICKS, HEAD_PAD), F32)
        for it in range(tt // PEER_UNROLL):
            at = ubody(it, at)
        a = at.T[:tt]
        w = jax.nn.gelu(a) * g_ref[tok0:tok0 + tt, :]
        wr_ref[...] = _dot(w.astype(BF16), ev_ref[...])

        def vbody(it, carry):
            for u in range(PEER_UNROLL):
                t = it * PEER_UNROLL + u
                wrow = jnp.broadcast_to(wr_ref[pl.ds(t, 1), :], (half, TABLE_COLS))
                wexp = jnp.where(mask_v, wrow, 0.0).astype(BF16)
                wb = cur[pl.ds(t * PEER_PICKS, PEER_PICKS)].reshape(TABLE_COLS, HEAD_PAD)
                y_ref[tok0 + t] = _dot(wexp, wb)
            return carry

        for it in range(tt // PEER_UNROLL):
            vbody(it, 0)

    for p in range(PEER_PHASES):
        k = PEER_PHASES * s + p
        idx_fetch(k + 2, p % 2).wait()
        idx_fetch(k + 3, (p + 1) % 2).start()
        rows_done(p).wait()
        issue_rows(ibs[p % 2], bufs[(p + 2) % PEER_PHASES], sem_g.at[(p + 2) % PEER_PHASES])
        evaluate(bufs[p], p * tt)

    @pl.when(s == last)
    def _():
        rows_done(0).wait()
        rows_done(1).wait()
        idx_fetch(0, 0).wait()


def _peer_call(idx_flat, table, hs3, gates):
    T = hs3.shape[0]
    tt = PEER_TILE
    npt = tt * PEER_PICKS
    assert tt % PEER_UNROLL == 0 and PEER_PICKS % PEER_GROUP == 0 and tt <= HEAD_PAD and PEER_PHASES == 4
    gsum, ev = _peer_consts()
    idx_pad = jnp.pad(idx_flat, (0, 3 * npt))
    step = PEER_PHASES * tt
    full = lambda a: pl.BlockSpec(a.shape, lambda s: (0,) * a.ndim)
    rows = pltpu.VMEM((npt, TABLE_ROWS, HEAD_PAD), BF16)
    return pl.pallas_call(
        _peer_kernel,
        grid=(T // step,),
        in_specs=[pl.BlockSpec(memory_space=pl.ANY),
                  pl.BlockSpec(memory_space=pl.ANY),
                  pl.BlockSpec((step, 8, HEAD_PAD), lambda s: (s, 0, 0)),
                  pl.BlockSpec((step, PEER_PICKS), lambda s: (s, 0)),
                  full(gsum), full(ev)],
        out_specs=pl.BlockSpec((step, 8, HEAD_PAD), lambda s: (s, 0, 0)),
        out_shape=jax.ShapeDtypeStruct((T, 8, HEAD_PAD), F32),
        scratch_shapes=[rows, rows, rows, rows,
                        pltpu.SMEM((npt,), jnp.int32),
                        pltpu.SMEM((npt,), jnp.int32),
                        pltpu.VMEM((tt, TABLE_COLS), F32),
                        pltpu.SemaphoreType.DMA((PEER_PHASES,)),
                        pltpu.SemaphoreType.DMA((2,))],
        compiler_params=_params("arbitrary"),
        name="peer",
    )(idx_pad, table, hs3, gates, gsum, ev)


SC_LANES = 16
SC_GROUP = 16
SC_UNROLL = 2
_GELU_C = 0.7978845608028654


def _peer_sc_kernel(idx_hbm, tab_hbm, h_hbm, g_hbm, y_hbm,
                    idx_a, idx_b, g_a, g_b, h_a, h_b, o_v, rows_v, sem, sem_in):
    ncores = lax.axis_size("c")
    wid = lax.axis_index("s") * ncores + lax.axis_index("c")
    per = h_hbm.shape[0] // (ncores * lax.axis_size("s"))
    base = wid * per
    nchunk = D_MODEL // SC_LANES
    ngrp = PEER_PICKS // SC_GROUP
    lane = lax.iota(jnp.int32, SC_LANES)
    sets = ((idx_a, g_a, h_a), (idx_b, g_b, h_b))

    def fetch(t, k):
        return (pltpu.make_async_copy(idx_hbm.at[t], sets[k][0], sem_in.at[k]),
                pltpu.make_async_copy(g_hbm.at[t], sets[k][1], sem_in.at[k]),
                pltpu.make_async_copy(h_hbm.at[t], sets[k][2], sem_in.at[k]))

    def gather(k, g, slot):
        return pltpu.make_async_copy(tab_hbm.at[sets[k][0].at[pl.ds(g * SC_GROUP, SC_GROUP)]], rows_v.at[slot],
                                     sem.at[slot])

    def process(t, k, t_next):
        g_v, h_v = sets[k][1], sets[k][2]

        def zero(c, carry):
            o_v[pl.ds(c * SC_LANES, SC_LANES)] = jnp.zeros((SC_LANES,), F32)
            return carry

        lax.fori_loop(0, nchunk, zero, 0)

        for g in range(ngrp):
            slot = g % 2
            if g + 1 < ngrp:
                gather(k, g + 1, 1 - slot).start()
            else:
                for c in fetch(t_next, 1 - k):
                    c.wait()
                gather(1 - k, 0, 0).start()
            gather(k, g, slot).wait()

            def dot_body(c, accs):
                hv = h_v[pl.ds(c * SC_LANES, SC_LANES)]
                out = []
                for p in range(SC_GROUP):
                    w = rows_v[slot, p, pl.ds(c * SC_LANES, SC_LANES)]
                    u = lax.bitcast_convert_type(w << 16, F32)
                    out.append(accs[p] + u * hv)
                return tuple(out)

            accs = plsc.parallel_loop(0, nchunk, unroll=SC_UNROLL,
                                      carry=tuple(jnp.zeros((SC_LANES,), F32) for _ in range(SC_GROUP)))(dot_body)
            a = jnp.zeros((SC_LANES,), F32)
            for p in range(SC_GROUP):
                a = jnp.where(lane == p, jnp.sum(accs[p]), a)
            z = _GELU_C * (a + 0.044715 * a * a * a)
            th = 1.0 - 2.0 / (jnp.exp(2.0 * z) + 1.0)
            wv = 0.5 * a * (1.0 + th) * g_v[pl.ds(g * SC_GROUP, SC_GROUP)]
            ws = [jnp.sum(jnp.where(lane == p, wv, 0.0)) for p in range(SC_GROUP)]

            def ax_body(c):
                o = o_v[pl.ds(c * SC_LANES, SC_LANES)]
                for p in range(SC_GROUP):
                    w = rows_v[slot, p, pl.ds(c * SC_LANES, SC_LANES)]
                    v = lax.bitcast_convert_type(w & jnp.int32(-65536), F32)
                    o = o + ws[p] * v
                o_v[pl.ds(c * SC_LANES, SC_LANES)] = o

            plsc.parallel_loop(0, nchunk, unroll=SC_UNROLL)(ax_body)

        pltpu.sync_copy(o_v, y_hbm.at[t])

    for c in fetch(base, 0):
        c.start()
    for c in fetch(base, 0):
        c.wait()
    gather(0, 0, 0).start()

    def pair(j, carry):
        t0 = base + 2 * j
        t2 = jnp.minimum(t0 + 2, base + per - 1)
        for c in fetch(t0 + 1, 1):
            c.start()
        process(t0, 0, t0 + 1)
        for c in fetch(t2, 0):
            c.start()
        process(t0 + 1, 1, t2)
        return carry

    lax.fori_loop(0, per // 2, pair, 0)
    gather(0, 0, 0).wait()


def _peer_sc_call(idx2, table_i32, h2, gates):
    ts = h2.shape[0]
    mesh = plsc.VectorSubcoreMesh(core_axis_name="c", subcore_axis_name="s")
    run = pl.kernel(
        _peer_sc_kernel,
        out_type=jax.ShapeDtypeStruct((ts, D_MODEL), F32),
        mesh=mesh,
        scratch_types=[pltpu.VMEM((PEER_PICKS,), jnp.int32), pltpu.VMEM((PEER_PICKS,), jnp.int32),
                       pltpu.VMEM((PEER_PICKS,), F32), pltpu.VMEM((PEER_PICKS,), F32),
                       pltpu.VMEM((D_MODEL,), F32), pltpu.VMEM((D_MODEL,), F32),
                       pltpu.VMEM((D_MODEL,), F32),
                       pltpu.VMEM((2, SC_GROUP, D_MODEL), jnp.int32),
                       pltpu.SemaphoreType.DMA((2,)),
                       pltpu.SemaphoreType.DMA((2,))],
        compiler_params=pltpu.CompilerParams(needs_layout_passes=False),
        name="peer_sc",
    )
    return run(idx2, table_i32, h2, gates)


def _final_kernel(x1_ref, y_ref, mod_ref, g_ref, o_ref):
    gt2 = mod_ref[0, 5:6, :]
    o_ref[0] = x1_ref[0] + gt2 * (_rms(y_ref[0]) * g_ref[...])


def _final_call(x1, y, mod, g_post):
    B, S, _ = x1.shape
    tr = min(ROW_TILE, S)
    row = pl.BlockSpec((1, tr, D_MODEL), lambda b, i: (b, i, 0))
    return pl.pallas_call(
        _final_kernel,
        grid=(B, S // tr),
        in_specs=[row, row, pl.BlockSpec((1, N_MOD, D_MODEL), lambda b, i: (b, 0, 0)),
                  pl.BlockSpec((1, D_MODEL), lambda b, i: (0, 0))],
        out_specs=row,
        out_shape=jax.ShapeDtypeStruct((B, S, D_MODEL), F32),
        compiler_params=_params("arbitrary", "arbitrary"),
        name="final",
    )(x1, y, mod, g_post)


def _pad_heads_cols(w, nh, scale=1.0):
    k = w.shape[0]
    w = (w * scale).reshape(k, nh, HEAD_DIM)
    return jnp.pad(w, ((0, 0), (0, 0), (0, HEAD_PAD - HEAD_DIM))).reshape(k, nh * HEAD_PAD)


def _pad_heads_rows(w, nh):
    n = w.shape[1]
    w = w.reshape(nh, HEAD_DIM, n)
    return jnp.pad(w, ((0, 0), (0, HEAD_PAD - HEAD_DIM), (0, 0))).reshape(nh * HEAD_PAD, n)


def _layer(x, c8, w_ada, b_ada, g_pre_mix, g_post_mix, g_pre_ffn, g_post_ffn,
           w_in, b_fgate, swa_sinks, w_out, w_query, sub_keys, w_u, w_v):
    B, S, D = x.shape
    T = B * S
    scale = HEAD_DIM ** -0.5
    mod = _ada_call(c8, w_ada, b_ada)[:B].reshape(B, N_MOD, D)

    o = 0
    parts = []
    for nh, sc in ((FOX_HEADS, scale), (FOX_HEADS, 1.0), (FOX_HEADS, 1.0)):
        parts.append(_pad_heads_cols(w_in[:, o:o + nh * HEAD_DIM], nh, sc))
        o += nh * HEAD_DIM
    parts.append(jnp.pad(w_in[:, o:o + FOX_HEADS], ((0, 0), (0, HEAD_PAD - FOX_HEADS))))
    o += FOX_HEADS
    for nh, sc in ((SWA_HEADS, scale), (SWA_KV_HEADS, 1.0), (SWA_KV_HEADS, 1.0)):
        parts.append(_pad_heads_cols(w_in[:, o:o + nh * HEAD_DIM], nh, sc))
        o += nh * HEAD_DIM
    w_all = jnp.concatenate(parts, axis=1).astype(BF16)
    bf_pad = jnp.pad(b_fgate, (0, HEAD_PAD - FOX_HEADS)).reshape(1, HEAD_PAD)

    qt, kp, vt, sq, sk, sv = _inproj_call(x, mod, g_pre_mix.reshape(1, D), w_all, bf_pad)
    fo = _fox_call(qt, kp, vt)
    so = _swa_call(swa_sinks, sq, sk, sv)

    nf = FOX_HEADS * HEAD_DIM
    wof = _pad_heads_rows(w_out[:nf], FOX_HEADS).astype(BF16)
    wos = _pad_heads_rows(w_out[nf:], SWA_HEADS).astype(BF16)
    keys = sub_keys.reshape(2 * PEER_HEADS, N_KEYS, PEER_HALF).astype(BF16)
    x1, h2, idx, gates = _route_call(fo, so, x, mod, g_post_mix.reshape(1, D), g_pre_ffn.reshape(1, D),
                                     wof, wos, w_query.astype(BF16), keys)

    ub, vb = w_u.astype(BF16), w_v.astype(BF16)
    table = jnp.concatenate([ub.reshape(N_EXPERTS, 8, HEAD_PAD), vb.reshape(N_EXPERTS, 8, HEAD_PAD)], axis=1)
    h2f = h2.reshape(T, D)
    ts = (int(T * SC_SHARE) // 1024) * 1024
    t1 = T - ts
    y_tc = _peer_call(idx[:t1].reshape(t1 * PEER_PICKS), table, h2f[:t1].reshape(t1, 8, HEAD_PAD), gates[:t1])
    y = y_tc.reshape(t1, D)
    if ts:
        u16 = lax.bitcast_convert_type(ub, jnp.uint16).astype(jnp.uint32)
        v16 = lax.bitcast_convert_type(vb, jnp.uint16).astype(jnp.uint32)
        table_i32 = lax.bitcast_convert_type((v16 << 16) | u16, jnp.int32)
        y_sc = _peer_sc_call(idx[t1:], table_i32, h2f[t1:], gates[t1:])
        y = jnp.concatenate([y, y_sc], axis=0)
    return _final_call(x1, y.reshape(B, S, D), mod, g_post_ffn.reshape(1, D))


def kernel(x, c, w_ada, b_ada, g_pre_mix, g_post_mix, g_pre_ffn, g_post_ffn, w_in, b_fgate, swa_sinks, w_out,
           w_query, sub_keys, w_u, w_v):
    B = x.shape[0]
    c8 = jnp.pad(c, ((0, 8 - B), (0, 0)))
    for l in range(w_ada.shape[0]):
        x = _layer(x, c8, w_ada[l], b_ada[l], g_pre_mix[l], g_post_mix[l], g_pre_ffn[l], g_post_ffn[l],
                   w_in[l], b_fgate[l], swa_sinks[l], w_out[l], w_query[l], sub_keys[l], w_u[l], w_v[l])
    return x
```

```python
import numpy as np
import jax
import jax.numpy as jnp
from jax import lax
from jax.experimental import pallas as pl
from jax.experimental.pallas import tpu as pltpu
from jax.experimental.pallas import tpu_sc as plsc

F32 = jnp.float32
BF16 = jnp.bfloat16

D_MODEL = 1024
HEAD_DIM = 64
HEAD_PAD = 128
FOX_HEADS = 8
SWA_HEADS = 8
SWA_KV_HEADS = 2
SWA_GROUP = SWA_HEADS // SWA_KV_HEADS
WINDOW = 128
PEER_HEADS = 8
PEER_HALF = 128
N_KEYS = 128
N_EXPERTS = N_KEYS * N_KEYS
PEER_TOPK = 16
PEER_PICKS = PEER_HEADS * PEER_TOPK
N_MOD = 6
RMS_EPS = 1e-6
NEG_INF = -1e30

_F_LANE = HEAD_DIM
_ONE_LANE = HEAD_DIM

ROW_TILE = 1024
FOX_TILE = 1024
FOX_PAR = 1
SWA_TILE = 512
ROUTE_TILE = 256
PEER_TILE = 8
PEER_PHASES = 4
PEER_UNROLL = 8
PEER_CHAIN = 8
PEER_GROUP = 16
SC_SHARE = 0.47
TABLE_ROWS = 16
TABLE_COLS = PEER_PICKS * TABLE_ROWS

_VMEM_LIMIT = 56 * 1024 * 1024


def _dot(a, b):
    return jnp.dot(a, b, preferred_element_type=F32)


def _dot_nt(a, b):
    return lax.dot_general(a, b, (((1,), (1,)), ((), ())), preferred_element_type=F32)


def _split3(x):
    hi = x.astype(BF16)
    r = x - hi.astype(F32)
    mid = r.astype(BF16)
    lo = (r - mid.astype(F32)).astype(BF16)
    return hi, mid, lo


def _rms(x):
    return x * lax.rsqrt(jnp.mean(x * x, axis=-1, keepdims=True) + RMS_EPS)


def _params(*sem):
    return pltpu.CompilerParams(dimension_semantics=sem, vmem_limit_bytes=_VMEM_LIMIT)


def _ada_kernel(c_ref, w_ref, b_ref, o_ref):
    c = c_ref[...]
    s = (c * jax.nn.sigmoid(c)).astype(BF16)
    o_ref[...] = _dot(s, w_ref[...].astype(BF16)) + b_ref[...]


def _ada_call(c8, w_ada, b_ada):
    n = w_ada.shape[1]
    tn = 1536
    return pl.pallas_call(
        _ada_kernel,
        grid=(n // tn,),
        in_specs=[pl.BlockSpec((8, D_MODEL), lambda j: (0, 0)),
                  pl.BlockSpec((D_MODEL, tn), lambda j: (0, j)),
                  pl.BlockSpec((1, tn), lambda j: (0, j))],
        out_specs=pl.BlockSpec((8, tn), lambda j: (0, j)),
        out_shape=jax.ShapeDtypeStruct((8, n), F32),
        compiler_params=_params("arbitrary"),
        name="ada",
    )(c8, w_ada, b_ada.reshape(1, n))


_NQ = FOX_HEADS * HEAD_PAD
_NS = SWA_HEADS * HEAD_PAD
_NKV = SWA_KV_HEADS * HEAD_PAD
_IN_COLS = 3 * _NQ + HEAD_PAD + _NS + 2 * _NKV


def _inproj_kernel(x_ref, mod_ref, g_ref, w_ref, bf_ref, tri_ref, pq_ref, pk_ref, cst_ref,
                   qt_ref, kp_ref, vt_ref, sq_ref, sk_ref, sv_ref, carry_ref):
    i = pl.program_id(1)

    @pl.when(i == 0)
    def _():
        carry_ref[...] = jnp.zeros_like(carry_ref)

    x = x_ref[0]
    sh1 = mod_ref[0, 0:1, :]
    sc1 = mod_ref[0, 1:2, :]
    h = _rms(x) * g_ref[...] * (1.0 + sc1) + sh1
    proj = _dot(h.astype(BF16), w_ref[...])

    z = proj[:, 3 * _NQ:3 * _NQ + HEAD_PAD] + bf_ref[...]
    ls = jnp.minimum(z, 0.0) - jnp.log(1.0 + jnp.exp(-jnp.abs(z)))
    tri = tri_ref[...]
    hi, mid, lo = _split3(ls)
    fcum = _dot(tri, hi) + _dot(tri, mid) + _dot(tri, lo) + carry_ref[...]
    carry_ref[...] = fcum[fcum.shape[0] - 1:, :]

    fh, fm, fl = _split3(fcum)
    eq = _dot(fh, pq_ref[0]) + _dot(fm, pq_ref[1]) + _dot(fl, pq_ref[2]) + cst_ref[0:1, :]
    ek = _dot(fh, pk_ref[0]) + _dot(fm, pk_ref[1]) + _dot(fl, pk_ref[2]) + cst_ref[1:2, :]
    qt_ref[0, 0] = (proj[:, 0:_NQ] + eq).T.astype(BF16)
    kp_ref[0] = (proj[:, _NQ:2 * _NQ] + ek).astype(BF16)
    vt_ref[0, 0] = (proj[:, 2 * _NQ:3 * _NQ] + cst_ref[2:3, :]).T.astype(BF16)
    o = 3 * _NQ + HEAD_PAD
    sq_ref[0] = proj[:, o:o + _NS].astype(BF16)
    sk_ref[0] = proj[:, o + _NS:o + _NS + _NKV].astype(BF16)
    sv_ref[0] = proj[:, o + _NS + _NKV:o + _NS + 2 * _NKV].astype(BF16)


def _inproj_consts(tr):
    tri = np.tril(np.ones((tr, tr), np.float32))
    pq = np.zeros((3, HEAD_PAD, _NQ), np.float32)
    pk = np.zeros((3, HEAD_PAD, _NQ), np.float32)
    cst = np.zeros((8, _NQ), np.float32)
    for h in range(FOX_HEADS):
        b = h * HEAD_PAD + _F_LANE
        for j in range(3):
            pq[j, h, b + j] = 1.0
            pk[j, h, b + 3 + j] = -1.0
            cst[0, b + 3 + j] = 1.0
            cst[1, b + j] = 1.0
        cst[2, h * HEAD_PAD + _ONE_LANE] = 1.0
    return (jnp.asarray(tri, BF16), jnp.asarray(pq, BF16), jnp.asarray(pk, BF16), jnp.asarray(cst, F32))


def _inproj_call(x, mod, g_pre, w_all, bf_pad):
    B, S, _ = x.shape
    tr = min(ROW_TILE, S)
    n = S // tr
    tri, pq, pk, cst = _inproj_consts(tr)
    row = lambda w: pl.BlockSpec((1, tr, w), lambda b, i: (b, i, 0))
    slab = pl.BlockSpec((1, 1, _NQ, tr), lambda b, i: (b, i, 0, 0))
    full = lambda a: pl.BlockSpec(a.shape, lambda b, i: (0,) * a.ndim)
    outs = [jax.ShapeDtypeStruct((B, n, _NQ, tr), BF16), jax.ShapeDtypeStruct((B, S, _NQ), BF16),
            jax.ShapeDtypeStruct((B, n, _NQ, tr), BF16), jax.ShapeDtypeStruct((B, S, _NS), BF16),
            jax.ShapeDtypeStruct((B, S, _NKV), BF16), jax.ShapeDtypeStruct((B, S, _NKV), BF16)]
    return pl.pallas_call(
        _inproj_kernel,
        grid=(B, n),
        in_specs=[row(D_MODEL),
                  pl.BlockSpec((1, N_MOD, D_MODEL), lambda b, i: (b, 0, 0)),
                  full(g_pre), full(w_all), full(bf_pad), full(tri), full(pq), full(pk), full(cst)],
        out_specs=[slab, row(_NQ), slab, row(_NS), row(_NKV), row(_NKV)],
        out_shape=outs,
        scratch_shapes=[pltpu.VMEM((1, HEAD_PAD), F32)],
        compiler_params=_params("arbitrary", "arbitrary"),
        name="inproj",
    )(x, mod, g_pre, w_all, bf_pad, tri, pq, pk, cst)


def _fox_kernel(qt_ref, k_ref, vt_ref, o_ref, m_ref, acc_ref):
    i = pl.program_id(2)
    t = o_ref.shape[1]
    m_ref[...] = jnp.full_like(m_ref, NEG_INF)
    acc_ref[...] = jnp.zeros_like(acc_ref)

    def block(j, masked):
        off = pl.multiple_of(j * t, t)
        for hh in range(FOX_PAR):
            lanes = slice(hh * HEAD_PAD, (hh + 1) * HEAD_PAD)
            k = k_ref[0, pl.ds(off, t), lanes]
            s = _dot(k, qt_ref[0, 0, lanes, :])
            if masked:
                r = lax.broadcasted_iota(jnp.int32, s.shape, 0)
                c = lax.broadcasted_iota(jnp.int32, s.shape, 1)
                s = jnp.where(r <= c, s, NEG_INF)
            m_prev = m_ref[hh]
            m_new = jnp.maximum(m_prev, jnp.max(s, axis=0, keepdims=True))
            p = jnp.exp(s - m_new)
            acc_ref[hh] = jnp.exp(m_prev - m_new) * acc_ref[hh] + _dot(vt_ref[0, j, lanes, :], p.astype(BF16))
            m_ref[hh] = m_new

    def body(j, carry):
        block(j, False)
        return carry

    lax.fori_loop(0, i, body, 0)
    block(i, True)
    for hh in range(FOX_PAR):
        acc = acc_ref[hh]
        o_ref[0, :, hh * HEAD_PAD:(hh + 1) * HEAD_PAD] = (acc / acc[_ONE_LANE:_ONE_LANE + 1, :]).T.astype(BF16)


def _fox_call(qt, kp, vt):
    B, n, _, t = qt.shape
    S = n * t
    w = FOX_PAR * HEAD_PAD
    return pl.pallas_call(
        _fox_kernel,
        grid=(B, FOX_HEADS // FOX_PAR, n),
        in_specs=[pl.BlockSpec((1, 1, w, t), lambda b, h, i: (b, i, h, 0)),
                  pl.BlockSpec((1, S, w), lambda b, h, i: (b, 0, h)),
                  pl.BlockSpec((1, n, w, t), lambda b, h, i: (b, 0, h, 0))],
        out_specs=pl.BlockSpec((1, t, w), lambda b, h, i: (b, i, h)),
        out_shape=jax.ShapeDtypeStruct((B, S, _NQ), BF16),
        scratch_shapes=[pltpu.VMEM((FOX_PAR, 1, t), F32), pltpu.VMEM((FOX_PAR, HEAD_PAD, t), F32)],
        compiler_params=_params("arbitrary", "arbitrary", "arbitrary"),
        name="fox",
    )(qt, kp, vt)


def _swa_kernel(sink_ref, q_ref, kc_ref, kp_ref, vc_ref, vp_ref, o_ref):
    i = pl.program_id(1)
    nsub = q_ref.shape[1] // WINDOW
    r = lax.broadcasted_iota(jnp.int32, (WINDOW, 2 * WINDOW), 0)
    j = lax.broadcasted_iota(jnp.int32, (WINDOW, 2 * WINDOW), 1)
    dist = r + WINDOW - j
    valid = (dist >= 0) & (dist < WINDOW)
    distf = dist.astype(F32)
    for qb in range(nsub):
        rows = slice(qb * WINDOW, (qb + 1) * WINDOW)
        if qb == 0:
            ok = valid & ((j >= WINDOW) | (i > 0))
        else:
            ok = valid
        for g in range(SWA_KV_HEADS):
            lanes = slice(g * HEAD_PAD, (g + 1) * HEAD_PAD)
            if qb == 0:
                kprev, vprev = kp_ref[0, :, lanes], vp_ref[0, :, lanes]
            else:
                prev = slice((qb - 1) * WINDOW, qb * WINDOW)
                kprev, vprev = kc_ref[0, prev, lanes], vc_ref[0, prev, lanes]
            kk = jnp.concatenate([kprev, kc_ref[0, rows, lanes]], axis=0)
            vv = jnp.concatenate([vprev, vc_ref[0, rows, lanes]], axis=0)
            for u in range(SWA_GROUP):
                hq = g * SWA_GROUP + u
                slope = 2.0 ** (-8.0 * (hq + 1) / SWA_HEADS)
                q = q_ref[0, rows, hq * HEAD_PAD:(hq + 1) * HEAD_PAD]
                s = _dot_nt(q, kk) - slope * distf
                s = jnp.where(ok, s, NEG_INF)
                sink = sink_ref[hq]
                m = jnp.maximum(jnp.max(s, axis=1, keepdims=True), sink)
                p = jnp.exp(s - m)
                den = jnp.sum(p, axis=1, keepdims=True) + jnp.exp(sink - m)
                o = _dot(p.astype(BF16), vv) / den
                o_ref[0, rows, hq * HEAD_PAD:(hq + 1) * HEAD_PAD] = o.astype(BF16)


def _swa_call(sinks, sq, sk, sv):
    B, S, _ = sq.shape
    t = min(SWA_TILE, S)
    per = t // WINDOW
    cur = lambda w: pl.BlockSpec((1, t, w), lambda b, i: (b, i, 0))
    prv = lambda w: pl.BlockSpec((1, WINDOW, w), lambda b, i: (b, jnp.maximum(i * per - 1, 0), 0))
    return pl.pallas_call(
        _swa_kernel,
        grid=(B, S // t),
        in_specs=[pl.BlockSpec(memory_space=pltpu.SMEM),
                  cur(_NS), cur(_NKV), prv(_NKV), cur(_NKV), prv(_NKV)],
        out_specs=cur(_NS),
        out_shape=jax.ShapeDtypeStruct((B, S, _NS), BF16),
        compiler_params=_params("arbitrary", "arbitrary"),
        name="swa",
    )(sinks, sq, sk, sk, sv, sv)


_NO_ID = 1 << 20


def _topk_rows(s, k, val_ref, idx_ref, ids=None):
    if ids is None:
        ids = lax.broadcasted_iota(jnp.int32, s.shape, 0)
    for r in range(k):
        m = jnp.max(s, axis=0, keepdims=True)
        i = jnp.min(jnp.where(s == m, ids, _NO_ID), axis=0, keepdims=True)
        val_ref[pl.ds(r, 1), :] = m
        idx_ref[pl.ds(r, 1), :] = i
        s = jnp.where(ids == i, -jnp.inf, s)


def _cand_counts():
    return [PEER_TOPK // (a + 1) for a in range(PEER_TOPK)]


_CAND_ROWS = 56


def _route_kernel(fo_ref, so_ref, x_ref, mod_ref, gpost_ref, gpre_ref, wof_ref, wos_ref, wq_ref, keys_ref, cid_ref,
                  x1_ref, h2_ref, idx_ref, gate_ref, qs_ref, sv_ref, si_ref, et_ref, gt_ref, cand_ref):
    x = x_ref[0]
    gt1 = mod_ref[0, 2:3, :]
    sh2 = mod_ref[0, 3:4, :]
    sc2 = mod_ref[0, 4:5, :]
    y = _dot(fo_ref[0], wof_ref[...]) + _dot(so_ref[0], wos_ref[...])
    x1 = x + gt1 * (_rms(y) * gpost_ref[...])
    x1_ref[0] = x1
    h2 = _rms(x1) * gpre_ref[...] * (1.0 + sc2) + sh2
    h2_ref[0] = h2
    qp = _dot(h2.astype(BF16), wq_ref[...])
    nhp = 2 * PEER_HEADS
    for hp in range(nhp):
        qs_ref[hp] = qp[:, hp * PEER_HALF:(hp + 1) * PEER_HALF].astype(BF16)

    def half(hp, carry):
        sc = _dot_nt(keys_ref[hp], qs_ref[hp])
        _topk_rows(sc, PEER_TOPK, sv_ref.at[hp], si_ref.at[hp])
        return carry

    lax.fori_loop(0, nhp, half, 0)

    counts = _cand_counts()
    used = sum(counts)
    cand_ref[pl.ds(used, _CAND_ROWS - used), :] = jnp.full((_CAND_ROWS - used, cand_ref.shape[1]), -jnp.inf, F32)
    cid = cid_ref[...]

    def head(h, carry):
        v0, v1 = sv_ref[2 * h], sv_ref[2 * h + 1]
        i0, i1 = si_ref[2 * h], si_ref[2 * h + 1]
        off = 0
        for a, nb in enumerate(counts):
            cand_ref[pl.ds(off, nb), :] = v0[a:a + 1, :] + v1[0:nb, :]
            off += nb
        _topk_rows(cand_ref[...], PEER_TOPK, gt_ref.at[h], et_ref.at[h], ids=cid)
        cv, ci = gt_ref[h], et_ref[h]
        ca, cb = ci >> 4, ci & (PEER_TOPK - 1)
        e1 = jnp.zeros_like(ci)
        e2 = jnp.zeros_like(ci)
        for a in range(PEER_TOPK):
            e1 = jnp.where(ca == a, i0[a:a + 1, :], e1)
            e2 = jnp.where(cb == a, i1[a:a + 1, :], e2)
        et_ref[h] = e1 * N_KEYS + e2
        ex = jnp.exp(cv - cv[0:1, :])
        gt_ref[h] = ex / jnp.sum(ex, axis=0, keepdims=True)
        return carry

    lax.fori_loop(0, PEER_HEADS, head, 0)
    tt = et_ref.shape[2]
    et = lax.bitcast_convert_type(et_ref[...].reshape(PEER_PICKS, tt), F32)
    idx_ref[...] = lax.bitcast_convert_type(et.T, jnp.int32)
    gate_ref[...] = gt_ref[...].reshape(PEER_PICKS, tt).T


def _route_call(fo, so, x, mod, g_post, g_pre, wof, wos, wq, keys):
    B, S, _ = x.shape
    tt = min(ROUTE_TILE, S)
    per = S // tt
    row = lambda w: pl.BlockSpec((1, tt, w), lambda b, i: (b, i, 0))
    full = lambda a: pl.BlockSpec(a.shape, lambda b, i: (0,) * a.ndim)
    tok = pl.BlockSpec((tt, PEER_PICKS), lambda b, i: (b * per + i, 0))
    flat = [a * PEER_TOPK + b for a, nb in enumerate(_cand_counts()) for b in range(nb)]
    flat += [_NO_ID] * (_CAND_ROWS - len(flat))
    cid = jnp.asarray(np.broadcast_to(np.asarray(flat, np.int32)[:, None], (_CAND_ROWS, tt)))
    return pl.pallas_call(
        _route_kernel,
        grid=(B, per),
        in_specs=[row(_NQ), row(_NS), row(D_MODEL),
                  pl.BlockSpec((1, N_MOD, D_MODEL), lambda b, i: (b, 0, 0)),
                  full(g_post), full(g_pre), full(wof), full(wos), full(wq), full(keys), full(cid)],
        out_specs=[row(D_MODEL), row(D_MODEL), tok, tok],
        out_shape=[jax.ShapeDtypeStruct((B, S, D_MODEL), F32),
                   jax.ShapeDtypeStruct((B, S, D_MODEL), F32),
                   jax.ShapeDtypeStruct((B * S, PEER_PICKS), jnp.int32),
                   jax.ShapeDtypeStruct((B * S, PEER_PICKS), F32)],
        scratch_shapes=[pltpu.VMEM((2 * PEER_HEADS, tt, PEER_HALF), BF16),
                        pltpu.VMEM((2 * PEER_HEADS, PEER_TOPK, tt), F32),
                        pltpu.VMEM((2 * PEER_HEADS, PEER_TOPK, tt), jnp.int32),
                        pltpu.VMEM((PEER_HEADS, PEER_TOPK, tt), jnp.int32),
                        pltpu.VMEM((PEER_HEADS, PEER_TOPK, tt), F32),
                        pltpu.VMEM((_CAND_ROWS, tt), F32)],
        compiler_params=_params("arbitrary", "arbitrary"),
        name="route",
    )(fo, so, x, mod, g_post, g_pre, wof, wos, wq, keys, cid)


def _peer_consts():
    half = TABLE_ROWS // 2
    gsum = np.zeros((PEER_GROUP, PEER_GROUP * TABLE_ROWS), np.float32)
    ev = np.zeros((PEER_PICKS, TABLE_COLS), np.float32)
    for j in range(PEER_GROUP):
        gsum[j, j * TABLE_ROWS:j * TABLE_ROWS + half] = 1.0
    for k in range(PEER_PICKS):
        ev[k, k * TABLE_ROWS + half:(k + 1) * TABLE_ROWS] = 1.0
    return jnp.asarray(gsum, BF16), jnp.asarray(ev, BF16)


def _peer_kernel(idx_hbm, tab_hbm, hs_ref, g_ref, gsum_ref, ev_ref, y_ref,
                 buf0, buf1, buf2, buf3, ib0, ib1, wr_ref, sem_g, sem_i):
    s = pl.program_id(0)
    last = pl.num_programs(0) - 1
    tt = PEER_TILE
    npt = tt * PEER_PICKS
    half = TABLE_ROWS // 2
    ngrp = PEER_PICKS // PEER_GROUP
    grows = PEER_GROUP * TABLE_ROWS
    bufs = (buf0, buf1, buf2, buf3)
    ibs = (ib0, ib1)

    def idx_fetch(tile, j):
        return pltpu.make_async_copy(idx_hbm.at[pl.ds(tile * npt, npt)], ibs[j], sem_i.at[j])

    def rows_done(j):
        return pltpu.make_async_copy(tab_hbm.at[pl.ds(0, npt)], bufs[j], sem_g.at[j])

    def issue_rows(ib, buf, sem):
        c = ib[0] >> 31
        for i in range(npt):
            e = ib[i] + c
            pltpu.make_async_copy(tab_hbm.at[e], buf.at[i], sem).start(priority=i % 2)
            if i % PEER_CHAIN == PEER_CHAIN - 1:
                c = e >> 31

    @pl.when(s == 0)
    def _():
        for j in range(2):
            first = idx_fetch(j, j)
            first.start()
            first.wait()

            def body(i, carry):
                pltpu.make_async_copy(tab_hbm.at[ibs[j][i]], bufs[j].at[i], sem_g.at[j]).start()
                return carry

            lax.fori_loop(0, npt, body, 0)
        idx_fetch(2, 0).start()

    row8 = lax.broadcasted_iota(jnp.int32, (half, TABLE_COLS), 0)
    col8 = lax.broadcasted_iota(jnp.int32, (half, TABLE_COLS), 1) % TABLE_ROWS
    mask_v = col8 == row8 + half
    lane = lax.broadcasted_iota(jnp.int32, (PEER_PICKS, HEAD_PAD), 1)
    gsum = gsum_ref[...]

    def evaluate(cur, tok0):
        def ubody(it, at):
            for u in range(PEER_UNROLL):
                t = it * PEER_UNROLL + u
                hrow = hs_ref[tok0 + t]
                h16 = jnp.concatenate([hrow, jnp.zeros_like(hrow)], axis=0).astype(BF16)
                tw = cur[pl.ds(t * PEER_PICKS, PEER_PICKS)]
                prod = (tw * h16[None]).reshape(TABLE_COLS, HEAD_PAD)
                parts = [_dot(gsum, prod[g * grows:(g + 1) * grows]) for g in range(ngrp)]
                z = jnp.sum(jnp.concatenate(parts, axis=0), axis=1, keepdims=True)
                at = jnp.where(lane == t, z, at)
            return at

        at = jnp.zeros((PEER_PICKS, HEAD_PAD), F32)
        for it in range(tt // PEER_UNROLL):
            at = ubody(it, at)
        a = at.T[:tt]
        w = jax.nn.gelu(a) * g_ref[tok0:tok0 + tt, :]
        wr_ref[...] = _dot(w.astype(BF16), ev_ref[...])

        def vbody(it, carry):
            for u in range(PEER_UNROLL):
                t = it * PEER_UNROLL + u
                wrow = jnp.broadcast_to(wr_ref[pl.ds(t, 1), :], (half, TABLE_COLS))
                wexp = jnp.where(mask_v, wrow, 0.0).astype(BF16)
                wb = cur[pl.ds(t * PEER_PICKS, PEER_PICKS)].reshape(TABLE_COLS, HEAD_PAD)
                y_ref[tok0 + t] = _dot(wexp, wb)
            return carry

        for it in range(tt // PEER_UNROLL):
            vbody(it, 0)

    for p in range(PEER_PHASES):
        k = PEER_PHASES * s + p
        idx_fetch(k + 2, p % 2).wait()
        idx_fetch(k + 3, (p + 1) % 2).start()
        rows_done(p).wait()
        issue_rows(ibs[p % 2], bufs[(p + 2) % PEER_PHASES], sem_g.at[(p + 2) % PEER_PHASES])
        evaluate(bufs[p], p * tt)

    @pl.when(s == last)
    def _():
        rows_done(0).wait()
        rows_done(1).wait()
        idx_fetch(0, 0).wait()


def _peer_call(idx_flat, table, hs3, gates):
    T = hs3.shape[0]
    tt = PEER_TILE
    npt = tt * PEER_PICKS
    assert tt % PEER_UNROLL == 0 and PEER_PICKS % PEER_GROUP == 0 and tt <= HEAD_PAD and PEER_PHASES == 4
    gsum, ev = _peer_consts()
    idx_pad = jnp.pad(idx_flat, (0, 3 * npt))
    step = PEER_PHASES * tt
    full = lambda a: pl.BlockSpec(a.shape, lambda s: (0,) * a.ndim)
    rows = pltpu.VMEM((npt, TABLE_ROWS, HEAD_PAD), BF16)
    return pl.pallas_call(
        _peer_kernel,
        grid=(T // step,),
        in_specs=[pl.BlockSpec(memory_space=pl.ANY),
                  pl.BlockSpec(memory_space=pl.ANY),
                  pl.BlockSpec((step, 8, HEAD_PAD), lambda s: (s, 0, 0)),
                  pl.BlockSpec((step, PEER_PICKS), lambda s: (s, 0)),
                  full(gsum), full(ev)],
        out_specs=pl.BlockSpec((step, 8, HEAD_PAD), lambda s: (s, 0, 0)),
        out_shape=jax.ShapeDtypeStruct((T, 8, HEAD_PAD), F32),
        scratch_shapes=[rows, rows, rows, rows,
                        pltpu.SMEM((npt,), jnp.int32),
                        pltpu.SMEM((npt,), jnp.int32),
                        pltpu.VMEM((tt, TABLE_COLS), F32),
                        pltpu.SemaphoreType.DMA((PEER_PHASES,)),
                        pltpu.SemaphoreType.DMA((2,))],
        compiler_params=_params("arbitrary"),
        name="peer",
    )(idx_pad, table, hs3, gates, gsum, ev)


SC_LANES = 16
SC_GROUP = 16
SC_UNROLL = 4
_GELU_C = 0.7978845608028654


def _peer_sc_kernel(idx_hbm, tab_hbm, h_hbm, g_hbm, y_hbm,
                    idx_a, idx_b, g_a, g_b, h_a, h_b, o_v, rows_v, sem, sem_in):
    ncores = lax.axis_size("c")
    wid = lax.axis_index("s") * ncores + lax.axis_index("c")
    per = h_hbm.shape[0] // (ncores * lax.axis_size("s"))
    base = wid * per
    nchunk = D_MODEL // SC_LANES
    ngrp = PEER_PICKS // SC_GROUP
    lane = lax.iota(jnp.int32, SC_LANES)
    sets = ((idx_a, g_a, h_a), (idx_b, g_b, h_b))

    def fetch(t, k):
        return (pltpu.make_async_copy(idx_hbm.at[t], sets[k][0], sem_in.at[k]),
                pltpu.make_async_copy(g_hbm.at[t], sets[k][1], sem_in.at[k]),
                pltpu.make_async_copy(h_hbm.at[t], sets[k][2], sem_in.at[k]))

    def gather(k, g, slot):
        return pltpu.make_async_copy(tab_hbm.at[sets[k][0].at[pl.ds(g * SC_GROUP, SC_GROUP)]], rows_v.at[slot],
                                     sem.at[slot])

    def process(t, k, t_next):
        g_v, h_v = sets[k][1], sets[k][2]

        def zero(c, carry):
            o_v[pl.ds(c * SC_LANES, SC_LANES)] = jnp.zeros((SC_LANES,), F32)
            return carry

        lax.fori_loop(0, nchunk, zero, 0)

        for g in range(ngrp):
            slot = g % 2
            if g + 1 < ngrp:
                gather(k, g + 1, 1 - slot).start()
            else:
                for c in fetch(t_next, 1 - k):
                    c.wait()
                gather(1 - k, 0, 0).start()
            gather(k, g, slot).wait()

            def dot_body(c, accs):
                hv = h_v[pl.ds(c * SC_LANES, SC_LANES)]
                out = []
                for p in range(SC_GROUP):
                    w = rows_v[slot, p, pl.ds(c * SC_LANES, SC_LANES)]
                    u = lax.bitcast_convert_type(w << 16, F32)
                    out.append(accs[p] + u * hv)
                return tuple(out)

            accs = plsc.parallel_loop(0, nchunk, unroll=SC_UNROLL,
                                      carry=tuple(jnp.zeros((SC_LANES,), F32) for _ in range(SC_GROUP)))(dot_body)
            a = jnp.zeros((SC_LANES,), F32)
            for p in range(SC_GROUP):
                a = jnp.where(lane == p, jnp.sum(accs[p]), a)
            z = _GELU_C * (a + 0.044715 * a * a * a)
            th = 1.0 - 2.0 / (jnp.exp(2.0 * z) + 1.0)
            wv = 0.5 * a * (1.0 + th) * g_v[pl.ds(g * SC_GROUP, SC_GROUP)]
            ws = [jnp.sum(jnp.where(lane == p, wv, 0.0)) for p in range(SC_GROUP)]

            def ax_body(c):
                o = o_v[pl.ds(c * SC_LANES, SC_LANES)]
                for p in range(SC_GROUP):
                    w = rows_v[slot, p, pl.ds(c * SC_LANES, SC_LANES)]
                    v = lax.bitcast_convert_type(w & jnp.int32(-65536), F32)
                    o = o + ws[p] * v
                o_v[pl.ds(c * SC_LANES, SC_LANES)] = o

            plsc.parallel_loop(0, nchunk, unroll=SC_UNROLL)(ax_body)

        pltpu.sync_copy(o_v, y_hbm.at[t])

    for c in fetch(base, 0):
        c.start()
    for c in fetch(base, 0):
        c.wait()
    gather(0, 0, 0).start()

    def pair(j, carry):
        t0 = base + 2 * j
        t2 = jnp.minimum(t0 + 2, base + per - 1)
        for c in fetch(t0 + 1, 1):
            c.start()
        process(t0, 0, t0 + 1)
        for c in fetch(t2, 0):
            c.start()
        process(t0 + 1, 1, t2)
        return carry

    lax.fori_loop(0, per // 2, pair, 0)
    gather(0, 0, 0).wait()


def _peer_sc_call(idx2, table_i32, h2, gates):
    ts = h2.shape[0]
    mesh = plsc.VectorSubcoreMesh(core_axis_name="c", subcore_axis_name="s")
    run = pl.kernel(
        _peer_sc_kernel,
        out_type=jax.ShapeDtypeStruct((ts, D_MODEL), F32),
        mesh=mesh,
        scratch_types=[pltpu.VMEM((PEER_PICKS,), jnp.int32), pltpu.VMEM((PEER_PICKS,), jnp.int32),
                       pltpu.VMEM((PEER_PICKS,), F32), pltpu.VMEM((PEER_PICKS,), F32),
                       pltpu.VMEM((D_MODEL,), F32), pltpu.VMEM((D_MODEL,), F32),
                       pltpu.VMEM((D_MODEL,), F32),
                       pltpu.VMEM((2, SC_GROUP, D_MODEL), jnp.int32),
                       pltpu.SemaphoreType.DMA((2,)),
                       pltpu.SemaphoreType.DMA((2,))],
        compiler_params=pltpu.CompilerParams(needs_layout_passes=False),
        name="peer_sc",
    )
    return run(idx2, table_i32, h2, gates)


def _final_kernel(x1_ref, y_ref, mod_ref, g_ref, o_ref):
    gt2 = mod_ref[0, 5:6, :]
    o_ref[0] = x1_ref[0] + gt2 * (_rms(y_ref[0]) * g_ref[...])


def _final_call(x1, y, mod, g_post):
    B, S, _ = x1.shape
    tr = min(ROW_TILE, S)
    row = pl.BlockSpec((1, tr, D_MODEL), lambda b, i: (b, i, 0))
    return pl.pallas_call(
        _final_kernel,
        grid=(B, S // tr),
        in_specs=[row, row, pl.BlockSpec((1, N_MOD, D_MODEL), lambda b, i: (b, 0, 0)),
                  pl.BlockSpec((1, D_MODEL), lambda b, i: (0, 0))],
        out_specs=row,
        out_shape=jax.ShapeDtypeStruct((B, S, D_MODEL), F32),
        compiler_params=_params("arbitrary", "arbitrary"),
        name="final",
    )(x1, y, mod, g_post)


def _pad_heads_cols(w, nh, scale=1.0):
    k = w.shape[0]
    w = (w * scale).reshape(k, nh, HEAD_DIM)
    return jnp.pad(w, ((0, 0), (0, 0), (0, HEAD_PAD - HEAD_DIM))).reshape(k, nh * HEAD_PAD)


def _pad_heads_rows(w, nh):
    n = w.shape[1]
    w = w.reshape(nh, HEAD_DIM, n)
    return jnp.pad(w, ((0, 0), (0, HEAD_PAD - HEAD_DIM), (0, 0))).reshape(nh * HEAD_PAD, n)


def _layer(x, c8, w_ada, b_ada, g_pre_mix, g_post_mix, g_pre_ffn, g_post_ffn,
           w_in, b_fgate, swa_sinks, w_out, w_query, sub_keys, w_u, w_v):
    B, S, D = x.shape
    T = B * S
    scale = HEAD_DIM ** -0.5
    mod = _ada_call(c8, w_ada, b_ada)[:B].reshape(B, N_MOD, D)

    o = 0
    parts = []
    for nh, sc in ((FOX_HEADS, scale), (FOX_HEADS, 1.0), (FOX_HEADS, 1.0)):
        parts.append(_pad_heads_cols(w_in[:, o:o + nh * HEAD_DIM], nh, sc))
        o += nh * HEAD_DIM
    parts.append(jnp.pad(w_in[:, o:o + FOX_HEADS], ((0, 0), (0, HEAD_PAD - FOX_HEADS))))
    o += FOX_HEADS
    for nh, sc in ((SWA_HEADS, scale), (SWA_KV_HEADS, 1.0), (SWA_KV_HEADS, 1.0)):
        parts.append(_pad_heads_cols(w_in[:, o:o + nh * HEAD_DIM], nh, sc))
        o += nh * HEAD_DIM
    w_all = jnp.concatenate(parts, axis=1).astype(BF16)
    bf_pad = jnp.pad(b_fgate, (0, HEAD_PAD - FOX_HEADS)).reshape(1, HEAD_PAD)

    qt, kp, vt, sq, sk, sv = _inproj_call(x, mod, g_pre_mix.reshape(1, D), w_all, bf_pad)
    fo = _fox_call(qt, kp, vt)
    so = _swa_call(swa_sinks, sq, sk, sv)

    nf = FOX_HEADS * HEAD_DIM
    wof = _pad_heads_rows(w_out[:nf], FOX_HEADS).astype(BF16)
    wos = _pad_heads_rows(w_out[nf:], SWA_HEADS).astype(BF16)
    keys = sub_keys.reshape(2 * PEER_HEADS, N_KEYS, PEER_HALF).astype(BF16)
    x1, h2, idx, gates = _route_call(fo, so, x, mod, g_post_mix.reshape(1, D), g_pre_ffn.reshape(1, D),
                                     wof, wos, w_query.astype(BF16), keys)

    ub, vb = w_u.astype(BF16), w_v.astype(BF16)
    table = jnp.concatenate([ub.reshape(N_EXPERTS, 8, HEAD_PAD), vb.reshape(N_EXPERTS, 8, HEAD_PAD)], axis=1)
    h2f = h2.reshape(T, D)
    ts = (int(T * SC_SHARE) // 1024) * 1024
    t1 = T - ts
    y_tc = _peer_call(idx[:t1].reshape(t1 * PEER_PICKS), table, h2f[:t1].reshape(t1, 8, HEAD_PAD), gates[:t1])
    y = y_tc.reshape(t1, D)
    if ts:
        u16 = lax.bitcast_convert_type(ub, jnp.uint16).astype(jnp.uint32)
        v16 = lax.bitcast_convert_type(vb, jnp.uint16).astype(jnp.uint32)
        table_i32 = lax.bitcast_convert_type((v16 << 16) | u16, jnp.int32)
        y_sc = _peer_sc_call(idx[t1:], table_i32, h2f[t1:], gates[t1:])
        y = jnp.concatenate([y, y_sc], axis=0)
    return _final_call(x1, y.reshape(B, S, D), mod, g_post_ffn.reshape(1, D))


def kernel(x, c, w_ada, b_ada, g_pre_mix, g_post_mix, g_pre_ffn, g_post_ffn, w_in, b_fgate, swa_sinks, w_out,
           w_query, sub_keys, w_u, w_v):
    B = x.shape[0]
    c8 = jnp.pad(c, ((0, 8 - B), (0, 0)))
    for l in range(w_ada.shape[0]):
        x = _layer(x, c8, w_ada[l], b_ada[l], g_pre_mix[l], g_post_mix[l], g_pre_ffn[l], g_post_ffn[l],
                   w_in[l], b_fgate[l], swa_sinks[l], w_out[l], w_query[l], sub_keys[l], w_u[l], w_v[l])
    return x
```

```python
import functools

import numpy as np
import jax
import jax.numpy as jnp
from jax import lax
from jax.experimental import pallas as pl
from jax.experimental.pallas import tpu as pltpu
from jax.experimental.pallas import tpu_sc as plsc

F32 = jnp.float32
BF16 = jnp.bfloat16

D_MODEL = 1024
HEAD_DIM = 64
HEAD_PAD = 128
FOX_HEADS = 8
SWA_HEADS = 8
SWA_KV_HEADS = 2
SWA_GROUP = SWA_HEADS // SWA_KV_HEADS
WINDOW = 128
PEER_HEADS = 8
PEER_HALF = 128
N_KEYS = 128
N_EXPERTS = N_KEYS * N_KEYS
PEER_TOPK = 16
PEER_PICKS = PEER_HEADS * PEER_TOPK
N_MOD = 6
RMS_EPS = 1e-6
NEG_INF = -1e30

_F_LANE = HEAD_DIM
_ONE_LANE = HEAD_DIM

ROW_TILE = 1024
FOX_TILE = 1024
FOX_PAR = 1
SWA_TILE = 512
ROUTE_TILE = 256
PEER_TILE = 8
PEER_PHASES = 4
PEER_UNROLL = 8
PEER_CHAIN = 8
PEER_GROUP = 16
SC_SHARE = 0.47
TABLE_ROWS = 16
TABLE_COLS = PEER_PICKS * TABLE_ROWS

_VMEM_LIMIT = 56 * 1024 * 1024


def _dot(a, b):
    return jnp.dot(a, b, preferred_element_type=F32)


def _dot_nt(a, b):
    return lax.dot_general(a, b, (((1,), (1,)), ((), ())), preferred_element_type=F32)


def _split3(x):
    hi = x.astype(BF16)
    r = x - hi.astype(F32)
    mid = r.astype(BF16)
    lo = (r - mid.astype(F32)).astype(BF16)
    return hi, mid, lo


def _rms(x):
    return x * lax.rsqrt(jnp.mean(x * x, axis=-1, keepdims=True) + RMS_EPS)


def _params(*sem):
    return pltpu.CompilerParams(dimension_semantics=sem, vmem_limit_bytes=_VMEM_LIMIT)


def _ada_kernel(c_ref, w_ref, b_ref, o_ref):
    c = c_ref[...]
    s = (c * jax.nn.sigmoid(c)).astype(BF16)
    o_ref[...] = _dot(s, w_ref[...].astype(BF16)) + b_ref[...]


def _ada_call(c8, w_ada, b_ada):
    n = w_ada.shape[1]
    tn = 1536
    return pl.pallas_call(
        _ada_kernel,
        grid=(n // tn,),
        in_specs=[pl.BlockSpec((8, D_MODEL), lambda j: (0, 0)),
                  pl.BlockSpec((D_MODEL, tn), lambda j: (0, j)),
                  pl.BlockSpec((1, tn), lambda j: (0, j))],
        out_specs=pl.BlockSpec((8, tn), lambda j: (0, j)),
        out_shape=jax.ShapeDtypeStruct((8, n), F32),
        compiler_params=_params("arbitrary"),
        name="ada",
    )(c8, w_ada, b_ada.reshape(1, n))


_NQ = FOX_HEADS * HEAD_PAD
_NS = SWA_HEADS * HEAD_PAD
_NKV = SWA_KV_HEADS * HEAD_PAD
_IN_COLS = 3 * _NQ + HEAD_PAD + _NS + 2 * _NKV


def _inproj_kernel(x_ref, mod_ref, g_ref, w_ref, bf_ref, tri_ref, pq_ref, pk_ref, cst_ref,
                   qt_ref, kp_ref, vt_ref, sq_ref, sk_ref, sv_ref, carry_ref):
    i = pl.program_id(1)

    @pl.when(i == 0)
    def _():
        carry_ref[...] = jnp.zeros_like(carry_ref)

    x = x_ref[0]
    sh1 = mod_ref[0, 0:1, :]
    sc1 = mod_ref[0, 1:2, :]
    h = _rms(x) * g_ref[...] * (1.0 + sc1) + sh1
    proj = _dot(h.astype(BF16), w_ref[...])

    z = proj[:, 3 * _NQ:3 * _NQ + HEAD_PAD] + bf_ref[...]
    ls = jnp.minimum(z, 0.0) - jnp.log(1.0 + jnp.exp(-jnp.abs(z)))
    tri = tri_ref[...]
    hi, mid, lo = _split3(ls)
    fcum = _dot(tri, hi) + _dot(tri, mid) + _dot(tri, lo) + carry_ref[...]
    carry_ref[...] = fcum[fcum.shape[0] - 1:, :]

    fh, fm, fl = _split3(fcum)
    eq = _dot(fh, pq_ref[0]) + _dot(fm, pq_ref[1]) + _dot(fl, pq_ref[2]) + cst_ref[0:1, :]
    ek = _dot(fh, pk_ref[0]) + _dot(fm, pk_ref[1]) + _dot(fl, pk_ref[2]) + cst_ref[1:2, :]
    qt_ref[0, 0] = (proj[:, 0:_NQ] + eq).T.astype(BF16)
    kp_ref[0] = (proj[:, _NQ:2 * _NQ] + ek).astype(BF16)
    vt_ref[0, 0] = (proj[:, 2 * _NQ:3 * _NQ] + cst_ref[2:3, :]).T.astype(BF16)
    o = 3 * _NQ + HEAD_PAD
    sq_ref[0] = proj[:, o:o + _NS].astype(BF16)
    sk_ref[0] = proj[:, o + _NS:o + _NS + _NKV].astype(BF16)
    sv_ref[0] = proj[:, o + _NS + _NKV:o + _NS + 2 * _NKV].astype(BF16)


def _inproj_consts(tr):
    tri = np.tril(np.ones((tr, tr), np.float32))
    pq = np.zeros((3, HEAD_PAD, _NQ), np.float32)
    pk = np.zeros((3, HEAD_PAD, _NQ), np.float32)
    cst = np.zeros((8, _NQ), np.float32)
    for h in range(FOX_HEADS):
        b = h * HEAD_PAD + _F_LANE
        for j in range(3):
            pq[j, h, b + j] = 1.0
            pk[j, h, b + 3 + j] = -1.0
            cst[0, b + 3 + j] = 1.0
            cst[1, b + j] = 1.0
        cst[2, h * HEAD_PAD + _ONE_LANE] = 1.0
    return (jnp.asarray(tri, BF16), jnp.asarray(pq, BF16), jnp.asarray(pk, BF16), jnp.asarray(cst, F32))


def _inproj_call(x, mod, g_pre, w_all, bf_pad):
    B, S, _ = x.shape
    tr = min(ROW_TILE, S)
    n = S // tr
    tri, pq, pk, cst = _inproj_consts(tr)
    row = lambda w: pl.BlockSpec((1, tr, w), lambda b, i: (b, i, 0))
    slab = pl.BlockSpec((1, 1, _NQ, tr), lambda b, i: (b, i, 0, 0))
    full = lambda a: pl.BlockSpec(a.shape, lambda b, i: (0,) * a.ndim)
    outs = [jax.ShapeDtypeStruct((B, n, _NQ, tr), BF16), jax.ShapeDtypeStruct((B, S, _NQ), BF16),
            jax.ShapeDtypeStruct((B, n, _NQ, tr), BF16), jax.ShapeDtypeStruct((B, S, _NS), BF16),
            jax.ShapeDtypeStruct((B, S, _NKV), BF16), jax.ShapeDtypeStruct((B, S, _NKV), BF16)]
    return pl.pallas_call(
        _inproj_kernel,
        grid=(B, n),
        in_specs=[row(D_MODEL),
                  pl.BlockSpec((1, N_MOD, D_MODEL), lambda b, i: (b, 0, 0)),
                  full(g_pre), full(w_all), full(bf_pad), full(tri), full(pq), full(pk), full(cst)],
        out_specs=[slab, row(_NQ), slab, row(_NS), row(_NKV), row(_NKV)],
        out_shape=outs,
        scratch_shapes=[pltpu.VMEM((1, HEAD_PAD), F32)],
        compiler_params=_params("arbitrary", "arbitrary"),
        name="inproj",
    )(x, mod, g_pre, w_all, bf_pad, tri, pq, pk, cst)


def _fox_kernel(qt_ref, k_ref, vt_ref, o_ref, m_ref, acc_ref):
    i = pl.program_id(2)
    t = o_ref.shape[1]
    m_ref[...] = jnp.full_like(m_ref, NEG_INF)
    acc_ref[...] = jnp.zeros_like(acc_ref)

    def block(j, masked):
        off = pl.multiple_of(j * t, t)
        for hh in range(FOX_PAR):
            lanes = slice(hh * HEAD_PAD, (hh + 1) * HEAD_PAD)
            k = k_ref[0, pl.ds(off, t), lanes]
            s = _dot(k, qt_ref[0, 0, lanes, :])
            if masked:
                r = lax.broadcasted_iota(jnp.int32, s.shape, 0)
                c = lax.broadcasted_iota(jnp.int32, s.shape, 1)
                s = jnp.where(r <= c, s, NEG_INF)
            m_prev = m_ref[hh]
            m_new = jnp.maximum(m_prev, jnp.max(s, axis=0, keepdims=True))
            p = jnp.exp(s - m_new)
            acc_ref[hh] = jnp.exp(m_prev - m_new) * acc_ref[hh] + _dot(vt_ref[0, j, lanes, :], p.astype(BF16))
            m_ref[hh] = m_new

    def body(j, carry):
        block(j, False)
        return carry

    lax.fori_loop(0, i, body, 0)
    block(i, True)
    for hh in range(FOX_PAR):
        acc = acc_ref[hh]
        o_ref[0, :, hh * HEAD_PAD:(hh + 1) * HEAD_PAD] = (acc / acc[_ONE_LANE:_ONE_LANE + 1, :]).T.astype(BF16)


def _fox_call(qt, kp, vt):
    B, n, _, t = qt.shape
    S = n * t
    w = FOX_PAR * HEAD_PAD
    return pl.pallas_call(
        _fox_kernel,
        grid=(B, FOX_HEADS // FOX_PAR, n),
        in_specs=[pl.BlockSpec((1, 1, w, t), lambda b, h, i: (b, i, h, 0)),
                  pl.BlockSpec((1, S, w), lambda b, h, i: (b, 0, h)),
                  pl.BlockSpec((1, n, w, t), lambda b, h, i: (b, 0, h, 0))],
        out_specs=pl.BlockSpec((1, t, w), lambda b, h, i: (b, i, h)),
        out_shape=jax.ShapeDtypeStruct((B, S, _NQ), BF16),
        scratch_shapes=[pltpu.VMEM((FOX_PAR, 1, t), F32), pltpu.VMEM((FOX_PAR, HEAD_PAD, t), F32)],
        compiler_params=_params("arbitrary", "arbitrary", "arbitrary"),
        name="fox",
    )(qt, kp, vt)


def _swa_kernel(sink_ref, q_ref, kc_ref, kp_ref, vc_ref, vp_ref, o_ref):
    i = pl.program_id(1)
    nsub = q_ref.shape[1] // WINDOW
    r = lax.broadcasted_iota(jnp.int32, (WINDOW, 2 * WINDOW), 0)
    j = lax.broadcasted_iota(jnp.int32, (WINDOW, 2 * WINDOW), 1)
    dist = r + WINDOW - j
    valid = (dist >= 0) & (dist < WINDOW)
    distf = dist.astype(F32)
    for qb in range(nsub):
        rows = slice(qb * WINDOW, (qb + 1) * WINDOW)
        if qb == 0:
            ok = valid & ((j >= WINDOW) | (i > 0))
        else:
            ok = valid
        for g in range(SWA_KV_HEADS):
            lanes = slice(g * HEAD_PAD, (g + 1) * HEAD_PAD)
            if qb == 0:
                kprev, vprev = kp_ref[0, :, lanes], vp_ref[0, :, lanes]
            else:
                prev = slice((qb - 1) * WINDOW, qb * WINDOW)
                kprev, vprev = kc_ref[0, prev, lanes], vc_ref[0, prev, lanes]
            kk = jnp.concatenate([kprev, kc_ref[0, rows, lanes]], axis=0)
            vv = jnp.concatenate([vprev, vc_ref[0, rows, lanes]], axis=0)
            for u in range(SWA_GROUP):
                hq = g * SWA_GROUP + u
                slope = 2.0 ** (-8.0 * (hq + 1) / SWA_HEADS)
                q = q_ref[0, rows, hq * HEAD_PAD:(hq + 1) * HEAD_PAD]
                s = _dot_nt(q, kk) - slope * distf
                s = jnp.where(ok, s, NEG_INF)
                sink = sink_ref[hq]
                m = jnp.maximum(jnp.max(s, axis=1, keepdims=True), sink)
                p = jnp.exp(s - m)
                den = jnp.sum(p, axis=1, keepdims=True) + jnp.exp(sink - m)
                o = _dot(p.astype(BF16), vv) / den
                o_ref[0, rows, hq * HEAD_PAD:(hq + 1) * HEAD_PAD] = o.astype(BF16)


def _swa_call(sinks, sq, sk, sv):
    B, S, _ = sq.shape
    t = min(SWA_TILE, S)
    per = t // WINDOW
    cur = lambda w: pl.BlockSpec((1, t, w), lambda b, i: (b, i, 0))
    prv = lambda w: pl.BlockSpec((1, WINDOW, w), lambda b, i: (b, jnp.maximum(i * per - 1, 0), 0))
    return pl.pallas_call(
        _swa_kernel,
        grid=(B, S // t),
        in_specs=[pl.BlockSpec(memory_space=pltpu.SMEM),
                  cur(_NS), cur(_NKV), prv(_NKV), cur(_NKV), prv(_NKV)],
        out_specs=cur(_NS),
        out_shape=jax.ShapeDtypeStruct((B, S, _NS), BF16),
        compiler_params=_params("arbitrary", "arbitrary"),
        name="swa",
    )(sinks, sq, sk, sk, sv, sv)


_NO_ID = 1 << 20


def _topk_rows(s, k, val_ref, idx_ref, ids=None):
    if ids is None:
        ids = lax.broadcasted_iota(jnp.int32, s.shape, 0)
    for r in range(k):
        m = jnp.max(s, axis=0, keepdims=True)
        i = jnp.min(jnp.where(s == m, ids, _NO_ID), axis=0, keepdims=True)
        val_ref[pl.ds(r, 1), :] = m
        idx_ref[pl.ds(r, 1), :] = i
        s = jnp.where(ids == i, -jnp.inf, s)


def _cand_counts():
    return [PEER_TOPK // (a + 1) for a in range(PEER_TOPK)]


_CAND_ROWS = 56


def _route_kernel(fo_ref, so_ref, x_ref, mod_ref, gpost_ref, gpre_ref, wof_ref, wos_ref, wq_ref, keys_ref, cid_ref,
                  x1_ref, h2_ref, idx_ref, gate_ref, qs_ref, sv_ref, si_ref, et_ref, gt_ref, cand_ref):
    x = x_ref[0]
    gt1 = mod_ref[0, 2:3, :]
    sh2 = mod_ref[0, 3:4, :]
    sc2 = mod_ref[0, 4:5, :]
    y = _dot(fo_ref[0], wof_ref[...]) + _dot(so_ref[0], wos_ref[...])
    x1 = x + gt1 * (_rms(y) * gpost_ref[...])
    x1_ref[0] = x1
    h2 = _rms(x1) * gpre_ref[...] * (1.0 + sc2) + sh2
    h2_ref[0] = h2
    qp = _dot(h2.astype(BF16), wq_ref[...])
    nhp = 2 * PEER_HEADS
    for hp in range(nhp):
        qs_ref[hp] = qp[:, hp * PEER_HALF:(hp + 1) * PEER_HALF].astype(BF16)

    def half(hp, carry):
        sc = _dot_nt(keys_ref[hp], qs_ref[hp])
        _topk_rows(sc, PEER_TOPK, sv_ref.at[hp], si_ref.at[hp])
        return carry

    lax.fori_loop(0, nhp, half, 0)

    counts = _cand_counts()
    used = sum(counts)
    cand_ref[pl.ds(used, _CAND_ROWS - used), :] = jnp.full((_CAND_ROWS - used, cand_ref.shape[1]), -jnp.inf, F32)
    cid = cid_ref[...]

    def head(h, carry):
        v0, v1 = sv_ref[2 * h], sv_ref[2 * h + 1]
        i0, i1 = si_ref[2 * h], si_ref[2 * h + 1]
        off = 0
        for a, nb in enumerate(counts):
            cand_ref[pl.ds(off, nb), :] = v0[a:a + 1, :] + v1[0:nb, :]
            off += nb
        _topk_rows(cand_ref[...], PEER_TOPK, gt_ref.at[h], et_ref.at[h], ids=cid)
        cv, ci = gt_ref[h], et_ref[h]
        ca, cb = ci >> 4, ci & (PEER_TOPK - 1)
        e1 = jnp.zeros_like(ci)
        e2 = jnp.zeros_like(ci)
        for a in range(PEER_TOPK):
            e1 = jnp.where(ca == a, i0[a:a + 1, :], e1)
            e2 = jnp.where(cb == a, i1[a:a + 1, :], e2)
        et_ref[h] = e1 * N_KEYS + e2
        ex = jnp.exp(cv - cv[0:1, :])
        gt_ref[h] = ex / jnp.sum(ex, axis=0, keepdims=True)
        return carry

    lax.fori_loop(0, PEER_HEADS, head, 0)
    tt = et_ref.shape[2]
    et = lax.bitcast_convert_type(et_ref[...].reshape(PEER_PICKS, tt), F32)
    idx_ref[...] = lax.bitcast_convert_type(et.T, jnp.int32)
    gate_ref[...] = gt_ref[...].reshape(PEER_PICKS, tt).T


def _route_call(fo, so, x, mod, g_post, g_pre, wof, wos, wq, keys):
    B, S, _ = x.shape
    tt = min(ROUTE_TILE, S)
    per = S // tt
    row = lambda w: pl.BlockSpec((1, tt, w), lambda b, i: (b, i, 0))
    full = lambda a: pl.BlockSpec(a.shape, lambda b, i: (0,) * a.ndim)
    tok = pl.BlockSpec((tt, PEER_PICKS), lambda b, i: (b * per + i, 0))
    flat = [a * PEER_TOPK + b for a, nb in enumerate(_cand_counts()) for b in range(nb)]
    flat += [_NO_ID] * (_CAND_ROWS - len(flat))
    cid = jnp.asarray(np.broadcast_to(np.asarray(flat, np.int32)[:, None], (_CAND_ROWS, tt)))
    return pl.pallas_call(
        _route_kernel,
        grid=(B, per),
        in_specs=[row(_NQ), row(_NS), row(D_MODEL),
                  pl.BlockSpec((1, N_MOD, D_MODEL), lambda b, i: (b, 0, 0)),
                  full(g_post), full(g_pre), full(wof), full(wos), full(wq), full(keys), full(cid)],
        out_specs=[row(D_MODEL), row(D_MODEL), tok, tok],
        out_shape=[jax.ShapeDtypeStruct((B, S, D_MODEL), F32),
                   jax.ShapeDtypeStruct((B, S, D_MODEL), F32),
                   jax.ShapeDtypeStruct((B * S, PEER_PICKS), jnp.int32),
                   jax.ShapeDtypeStruct((B * S, PEER_PICKS), F32)],
        scratch_shapes=[pltpu.VMEM((2 * PEER_HEADS, tt, PEER_HALF), BF16),
                        pltpu.VMEM((2 * PEER_HEADS, PEER_TOPK, tt), F32),
                        pltpu.VMEM((2 * PEER_HEADS, PEER_TOPK, tt), jnp.int32),
                        pltpu.VMEM((PEER_HEADS, PEER_TOPK, tt), jnp.int32),
                        pltpu.VMEM((PEER_HEADS, PEER_TOPK, tt), F32),
                        pltpu.VMEM((_CAND_ROWS, tt), F32)],
        compiler_params=_params("arbitrary", "arbitrary"),
        name="route",
    )(fo, so, x, mod, g_post, g_pre, wof, wos, wq, keys, cid)


def _peer_consts():
    half = TABLE_ROWS // 2
    gsum = np.zeros((PEER_GROUP, PEER_GROUP * TABLE_ROWS), np.float32)
    ev = np.zeros((PEER_PICKS, TABLE_COLS), np.float32)
    for j in range(PEER_GROUP):
        gsum[j, j * TABLE_ROWS:j * TABLE_ROWS + half] = 1.0
    for k in range(PEER_PICKS):
        ev[k, k * TABLE_ROWS + half:(k + 1) * TABLE_ROWS] = 1.0
    return jnp.asarray(gsum, BF16), jnp.asarray(ev, BF16)


def _peer_kernel(idx_hbm, tab_hbm, hs_ref, g_ref, gsum_ref, ev_ref, y_ref,
                 buf0, buf1, buf2, buf3, ib0, ib1, wr_ref, sem_g, sem_i):
    s = pl.program_id(0)
    last = pl.num_programs(0) - 1
    tt = PEER_TILE
    npt = tt * PEER_PICKS
    half = TABLE_ROWS // 2
    ngrp = PEER_PICKS // PEER_GROUP
    grows = PEER_GROUP * TABLE_ROWS
    bufs = (buf0, buf1, buf2, buf3)
    ibs = (ib0, ib1)

    def idx_fetch(tile, j):
        return pltpu.make_async_copy(idx_hbm.at[pl.ds(tile * npt, npt)], ibs[j], sem_i.at[j])

    def rows_done(j):
        return pltpu.make_async_copy(tab_hbm.at[pl.ds(0, npt)], bufs[j], sem_g.at[j])

    def issue_rows(ib, buf, sem):
        c = ib[0] >> 31
        for i in range(npt):
            e = ib[i] + c
            pltpu.make_async_copy(tab_hbm.at[e], buf.at[i], sem).start(priority=i % 2)
            if i % PEER_CHAIN == PEER_CHAIN - 1:
                c = e >> 31

    @pl.when(s == 0)
    def _():
        for j in range(2):
            first = idx_fetch(j, j)
            first.start()
            first.wait()

            def body(i, carry):
                pltpu.make_async_copy(tab_hbm.at[ibs[j][i]], bufs[j].at[i], sem_g.at[j]).start()
                return carry

            lax.fori_loop(0, npt, body, 0)
        idx_fetch(2, 0).start()

    row8 = lax.broadcasted_iota(jnp.int32, (half, TABLE_COLS), 0)
    col8 = lax.broadcasted_iota(jnp.int32, (half, TABLE_COLS), 1) % TABLE_ROWS
    mask_v = col8 == row8 + half
    lane = lax.broadcasted_iota(jnp.int32, (PEER_PICKS, HEAD_PAD), 1)
    gsum = gsum_ref[...]

    def evaluate(cur, tok0):
        def ubody(it, at):
            for u in range(PEER_UNROLL):
                t = it * PEER_UNROLL + u
                hrow = hs_ref[tok0 + t]
                h16 = jnp.concatenate([hrow, jnp.zeros_like(hrow)], axis=0).astype(BF16)
                tw = cur[pl.ds(t * PEER_PICKS, PEER_PICKS)]
                prod = (tw * h16[None]).reshape(TABLE_COLS, HEAD_PAD)
                parts = [_dot(gsum, prod[g * grows:(g + 1) * grows]) for g in range(ngrp)]
                z = jnp.sum(jnp.concatenate(parts, axis=0), axis=1, keepdims=True)
                at = jnp.where(lane == t, z, at)
            return at

        at = jnp.zeros((PEER_PICKS, HEAD_PAD), F32)
        for it in range(tt // PEER_UNROLL):
            at = ubody(it, at)
        a = at.T[:tt]
        w = jax.nn.gelu(a) * g_ref[tok0:tok0 + tt, :]
        wr_ref[...] = _dot(w.astype(BF16), ev_ref[...])

        def vbody(it, carry):
            for u in range(PEER_UNROLL):
                t = it * PEER_UNROLL + u
                wrow = jnp.broadcast_to(wr_ref[pl.ds(t, 1), :], (half, TABLE_COLS))
                wexp = jnp.where(mask_v, wrow, 0.0).astype(BF16)
                wb = cur[pl.ds(t * PEER_PICKS, PEER_PICKS)].reshape(TABLE_COLS, HEAD_PAD)
                y_ref[tok0 + t] = _dot(wexp, wb)
            return carry

        for it in range(tt // PEER_UNROLL):
            vbody(it, 0)

    for p in range(PEER_PHASES):
        k = PEER_PHASES * s + p
        idx_fetch(k + 2, p % 2).wait()
        idx_fetch(k + 3, (p + 1) % 2).start()
        rows_done(p).wait()
        issue_rows(ibs[p % 2], bufs[(p + 2) % PEER_PHASES], sem_g.at[(p + 2) % PEER_PHASES])
        evaluate(bufs[p], p * tt)

    @pl.when(s == last)
    def _():
        rows_done(0).wait()
        rows_done(1).wait()
        idx_fetch(0, 0).wait()


def _peer_call(idx_flat, table, hs3, gates, T):
    tt = PEER_TILE
    npt = tt * PEER_PICKS
    assert tt % PEER_UNROLL == 0 and PEER_PICKS % PEER_GROUP == 0 and tt <= HEAD_PAD and PEER_PHASES == 4
    gsum, ev = _peer_consts()
    short = max(0, (T + 3 * tt) * PEER_PICKS - idx_flat.shape[0])
    idx_pad = jnp.pad(idx_flat, (0, short)) if short else idx_flat
    step = PEER_PHASES * tt
    full = lambda a: pl.BlockSpec(a.shape, lambda s: (0,) * a.ndim)
    rows = pltpu.VMEM((npt, TABLE_ROWS, HEAD_PAD), BF16)
    return pl.pallas_call(
        _peer_kernel,
        grid=(T // step,),
        in_specs=[pl.BlockSpec(memory_space=pl.ANY),
                  pl.BlockSpec(memory_space=pl.ANY),
                  pl.BlockSpec((step, 8, HEAD_PAD), lambda s: (s, 0, 0)),
                  pl.BlockSpec((step, PEER_PICKS), lambda s: (s, 0)),
                  full(gsum), full(ev)],
        out_specs=pl.BlockSpec((step, 8, HEAD_PAD), lambda s: (s, 0, 0)),
        out_shape=jax.ShapeDtypeStruct((T, 8, HEAD_PAD), F32),
        scratch_shapes=[rows, rows, rows, rows,
                        pltpu.SMEM((npt,), jnp.int32),
                        pltpu.SMEM((npt,), jnp.int32),
                        pltpu.VMEM((tt, TABLE_COLS), F32),
                        pltpu.SemaphoreType.DMA((PEER_PHASES,)),
                        pltpu.SemaphoreType.DMA((2,))],
        compiler_params=_params("arbitrary"),
        name="peer",
    )(idx_pad, table, hs3, gates, gsum, ev)


SC_LANES = 16
SC_GROUP = 16
SC_UNROLL = 2
_GELU_C = 0.7978845608028654


def _peer_sc_kernel(t1, idx_hbm, tab_hbm, h_hbm, g_hbm, y_hbm,
                    idx_a, idx_b, g_a, g_b, h_a, h_b, o_v, rows_v, sem, sem_in):
    ncores = lax.axis_size("c")
    wid = lax.axis_index("s") * ncores + lax.axis_index("c")
    per = y_hbm.shape[0] // (ncores * lax.axis_size("s"))
    base = t1 + wid * per
    nchunk = D_MODEL // SC_LANES
    ngrp = PEER_PICKS // SC_GROUP
    lane = lax.iota(jnp.int32, SC_LANES)
    sets = ((idx_a, g_a, h_a), (idx_b, g_b, h_b))

    def fetch(t, k):
        return (pltpu.make_async_copy(idx_hbm.at[t], sets[k][0], sem_in.at[k]),
                pltpu.make_async_copy(g_hbm.at[t], sets[k][1], sem_in.at[k]),
                pltpu.make_async_copy(h_hbm.at[t], sets[k][2], sem_in.at[k]))

    def gather(k, g, slot):
        return pltpu.make_async_copy(tab_hbm.at[sets[k][0].at[pl.ds(g * SC_GROUP, SC_GROUP)]], rows_v.at[slot],
                                     sem.at[slot])

    def process(t, k, t_next):
        g_v, h_v = sets[k][1], sets[k][2]

        def zero(c, carry):
            o_v[pl.ds(c * SC_LANES, SC_LANES)] = jnp.zeros((SC_LANES,), F32)
            return carry

        lax.fori_loop(0, nchunk, zero, 0)

        for g in range(ngrp):
            slot = g % 2
            if g + 1 < ngrp:
                gather(k, g + 1, 1 - slot).start()
            else:
                for c in fetch(t_next, 1 - k):
                    c.wait()
                gather(1 - k, 0, 0).start()
            gather(k, g, slot).wait()

            def dot_body(c, accs):
                hv = h_v[pl.ds(c * SC_LANES, SC_LANES)]
                out = []
                for p in range(SC_GROUP):
                    w = rows_v[slot, p, pl.ds(c * SC_LANES, SC_LANES)]
                    u = lax.bitcast_convert_type(w << 16, F32)
                    out.append(accs[p] + u * hv)
                return tuple(out)

            accs = plsc.parallel_loop(0, nchunk, unroll=SC_UNROLL,
                                      carry=tuple(jnp.zeros((SC_LANES,), F32) for _ in range(SC_GROUP)))(dot_body)
            a = jnp.zeros((SC_LANES,), F32)
            for p in range(SC_GROUP):
                a = jnp.where(lane == p, jnp.sum(accs[p]), a)
            z = _GELU_C * (a + 0.044715 * a * a * a)
            th = 1.0 - 2.0 / (jnp.exp(2.0 * z) + 1.0)
            wv = 0.5 * a * (1.0 + th) * g_v[pl.ds(g * SC_GROUP, SC_GROUP)]
            ws = [jnp.sum(jnp.where(lane == p, wv, 0.0)) for p in range(SC_GROUP)]

            def ax_body(c):
                o = o_v[pl.ds(c * SC_LANES, SC_LANES)]
                for p in range(SC_GROUP):
                    w = rows_v[slot, p, pl.ds(c * SC_LANES, SC_LANES)]
                    v = lax.bitcast_convert_type(w & jnp.int32(-65536), F32)
                    o = o + ws[p] * v
                o_v[pl.ds(c * SC_LANES, SC_LANES)] = o

            plsc.parallel_loop(0, nchunk, unroll=SC_UNROLL)(ax_body)

        pltpu.sync_copy(o_v, y_hbm.at[t - t1])

    for c in fetch(base, 0):
        c.start()
    for c in fetch(base, 0):
        c.wait()
    gather(0, 0, 0).start()

    def pair(j, carry):
        t0 = base + 2 * j
        t2 = jnp.minimum(t0 + 2, base + per - 1)
        for c in fetch(t0 + 1, 1):
            c.start()
        process(t0, 0, t0 + 1)
        for c in fetch(t2, 0):
            c.start()
        process(t0 + 1, 1, t2)
        return carry

    lax.fori_loop(0, per // 2, pair, 0)
    gather(0, 0, 0).wait()


def _peer_sc_call(idx2, table_i32, h2, gates, t1):
    ts = h2.shape[0] - t1
    mesh = plsc.VectorSubcoreMesh(core_axis_name="c", subcore_axis_name="s")
    run = pl.kernel(
        functools.partial(_peer_sc_kernel, t1),
        out_type=jax.ShapeDtypeStruct((ts, D_MODEL), F32),
        mesh=mesh,
        scratch_types=[pltpu.VMEM((PEER_PICKS,), jnp.int32), pltpu.VMEM((PEER_PICKS,), jnp.int32),
                       pltpu.VMEM((PEER_PICKS,), F32), pltpu.VMEM((PEER_PICKS,), F32),
                       pltpu.VMEM((D_MODEL,), F32), pltpu.VMEM((D_MODEL,), F32),
                       pltpu.VMEM((D_MODEL,), F32),
                       pltpu.VMEM((2, SC_GROUP, D_MODEL), jnp.int32),
                       pltpu.SemaphoreType.DMA((2,)),
                       pltpu.SemaphoreType.DMA((2,))],
        compiler_params=pltpu.CompilerParams(needs_layout_passes=False),
        name="peer_sc",
    )
    return run(idx2, table_i32, h2, gates)


def _final_kernel(n1, x1_ref, ya_ref, yb_ref, mod_ref, g_ref, o_ref):
    gi = pl.program_id(0) * pl.num_programs(1) + pl.program_id(1)
    y = jnp.where(gi < n1, ya_ref[...], yb_ref[...])
    gt2 = mod_ref[0, 5:6, :]
    o_ref[0] = x1_ref[0] + gt2 * (_rms(y) * g_ref[...])


def _final_call(x1, ya, yb, mod, g_post):
    B, S, _ = x1.shape
    tr = min(ROW_TILE, S)
    per = S // tr
    assert ya.shape[0] % tr == 0 and yb.shape[0] % tr == 0
    n1 = ya.shape[0] // tr
    row = pl.BlockSpec((1, tr, D_MODEL), lambda b, i: (b, i, 0))
    first = pl.BlockSpec((tr, D_MODEL), lambda b, i: (jnp.minimum(b * per + i, n1 - 1), 0))
    rest = pl.BlockSpec((tr, D_MODEL), lambda b, i: (jnp.maximum(b * per + i - n1, 0), 0))
    return pl.pallas_call(
        functools.partial(_final_kernel, n1),
        grid=(B, per),
        in_specs=[row, first, rest, pl.BlockSpec((1, N_MOD, D_MODEL), lambda b, i: (b, 0, 0)),
                  pl.BlockSpec((1, D_MODEL), lambda b, i: (0, 0))],
        out_specs=row,
        out_shape=jax.ShapeDtypeStruct((B, S, D_MODEL), F32),
        compiler_params=_params("arbitrary", "arbitrary"),
        name="final",
    )(x1, ya, yb, mod, g_post)


def _pad_heads_cols(w, nh, scale=1.0):
    k = w.shape[0]
    w = (w * scale).reshape(k, nh, HEAD_DIM)
    return jnp.pad(w, ((0, 0), (0, 0), (0, HEAD_PAD - HEAD_DIM))).reshape(k, nh * HEAD_PAD)


def _pad_heads_rows(w, nh):
    n = w.shape[1]
    w = w.reshape(nh, HEAD_DIM, n)
    return jnp.pad(w, ((0, 0), (0, HEAD_PAD - HEAD_DIM), (0, 0))).reshape(nh * HEAD_PAD, n)


def _layer(x, c8, w_ada, b_ada, g_pre_mix, g_post_mix, g_pre_ffn, g_post_ffn,
           w_in, b_fgate, swa_sinks, w_out, w_query, sub_keys, w_u, w_v):
    B, S, D = x.shape
    T = B * S
    scale = HEAD_DIM ** -0.5
    mod = _ada_call(c8, w_ada, b_ada)[:B].reshape(B, N_MOD, D)

    o = 0
    parts = []
    for nh, sc in ((FOX_HEADS, scale), (FOX_HEADS, 1.0), (FOX_HEADS, 1.0)):
        parts.append(_pad_heads_cols(w_in[:, o:o + nh * HEAD_DIM], nh, sc))
        o += nh * HEAD_DIM
    parts.append(jnp.pad(w_in[:, o:o + FOX_HEADS], ((0, 0), (0, HEAD_PAD - FOX_HEADS))))
    o += FOX_HEADS
    for nh, sc in ((SWA_HEADS, scale), (SWA_KV_HEADS, 1.0), (SWA_KV_HEADS, 1.0)):
        parts.append(_pad_heads_cols(w_in[:, o:o + nh * HEAD_DIM], nh, sc))
        o += nh * HEAD_DIM
    w_all = jnp.concatenate(parts, axis=1).astype(BF16)
    bf_pad = jnp.pad(b_fgate, (0, HEAD_PAD - FOX_HEADS)).reshape(1, HEAD_PAD)

    qt, kp, vt, sq, sk, sv = _inproj_call(x, mod, g_pre_mix.reshape(1, D), w_all, bf_pad)
    fo = _fox_call(qt, kp, vt)
    so = _swa_call(swa_sinks, sq, sk, sv)

    nf = FOX_HEADS * HEAD_DIM
    wof = _pad_heads_rows(w_out[:nf], FOX_HEADS).astype(BF16)
    wos = _pad_heads_rows(w_out[nf:], SWA_HEADS).astype(BF16)
    keys = sub_keys.reshape(2 * PEER_HEADS, N_KEYS, PEER_HALF).astype(BF16)
    x1, h2, idx, gates = _route_call(fo, so, x, mod, g_post_mix.reshape(1, D), g_pre_ffn.reshape(1, D),
                                     wof, wos, w_query.astype(BF16), keys)

    ub, vb = w_u.astype(BF16), w_v.astype(BF16)
    table = jnp.concatenate([ub.reshape(N_EXPERTS, 8, HEAD_PAD), vb.reshape(N_EXPERTS, 8, HEAD_PAD)], axis=1)
    h2f = h2.reshape(T, D)
    tile = min(ROW_TILE, S)
    ts = (int(T * SC_SHARE) // tile) * tile
    t1 = T - ts
    y_tc = _peer_call(idx.reshape(T * PEER_PICKS), table, h2f.reshape(T, 8, HEAD_PAD), gates, t1).reshape(t1, D)
    if not ts:
        return _final_call(x1, y_tc, y_tc, mod, g_post_ffn.reshape(1, D))
    u16 = lax.bitcast_convert_type(ub, jnp.uint16).astype(jnp.uint32)
    v16 = lax.bitcast_convert_type(vb, jnp.uint16).astype(jnp.uint32)
    table_i32 = lax.bitcast_convert_type((v16 << 16) | u16, jnp.int32)
    y_sc = _peer_sc_call(idx, table_i32, h2f, gates, t1)
    return _final_call(x1, y_tc, y_sc, mod, g_post_ffn.reshape(1, D))


def kernel(x, c, w_ada, b_ada, g_pre_mix, g_post_mix, g_pre_ffn, g_post_ffn, w_in, b_fgate, swa_sinks, w_out,
           w_query, sub_keys, w_u, w_v):
    B = x.shape[0]
    c8 = jnp.pad(c, ((0, 8 - B), (0, 0)))
    for l in range(w_ada.shape[0]):
        x = _layer(x, c8, w_ada[l], b_ada[l], g_pre_mix[l], g_post_mix[l], g_pre_ffn[l], g_post_ffn[l],
                   w_in[l], b_fgate[l], swa_sinks[l], w_out[l], w_query[l], sub_keys[l], w_u[l], w_v[l])
    return x
```

```python
import functools

import numpy as np
import jax
import jax.numpy as jnp
from jax import lax
from jax.experimental import pallas as pl
from jax.experimental.pallas import tpu as pltpu
from jax.experimental.pallas import tpu_sc as plsc

F32 = jnp.float32
BF16 = jnp.bfloat16

D_MODEL = 1024
HEAD_DIM = 64
HEAD_PAD = 128
FOX_HEADS = 8
SWA_HEADS = 8
SWA_KV_HEADS = 2
SWA_GROUP = SWA_HEADS // SWA_KV_HEADS
WINDOW = 128
PEER_HEADS = 8
PEER_HALF = 128
N_KEYS = 128
N_EXPERTS = N_KEYS * N_KEYS
PEER_TOPK = 16
PEER_PICKS = PEER_HEADS * PEER_TOPK
N_MOD = 6
RMS_EPS = 1e-6
NEG_INF = -1e30

_F_LANE = HEAD_DIM
_ONE_LANE = HEAD_DIM

ROW_TILE = 1024
FOX_TILE = 1024
FOX_PAR = 1
SWA_TILE = 512
ROUTE_TILE = 256
PEER_TILE = 8
PEER_PHASES = 4
PEER_UNROLL = 8
PEER_CHAIN = 8
PEER_GROUP = 16
SC_SHARE = 0.47
TABLE_ROWS = 16
TABLE_COLS = PEER_PICKS * TABLE_ROWS

_VMEM_LIMIT = 56 * 1024 * 1024


def _dot(a, b):
    return jnp.dot(a, b, preferred_element_type=F32)


def _dot_nt(a, b):
    return lax.dot_general(a, b, (((1,), (1,)), ((), ())), preferred_element_type=F32)


def _split3(x):
    hi = x.astype(BF16)
    r = x - hi.astype(F32)
    mid = r.astype(BF16)
    lo = (r - mid.astype(F32)).astype(BF16)
    return hi, mid, lo


def _rms(x):
    return x * lax.rsqrt(jnp.mean(x * x, axis=-1, keepdims=True) + RMS_EPS)


def _params(*sem):
    return pltpu.CompilerParams(dimension_semantics=sem, vmem_limit_bytes=_VMEM_LIMIT)


def _ada_kernel(c_ref, w_ref, b_ref, o_ref):
    c = c_ref[...]
    s = (c * jax.nn.sigmoid(c)).astype(BF16)
    o_ref[...] = _dot(s, w_ref[...].astype(BF16)) + b_ref[...]


def _ada_call(c8, w_ada, b_ada):
    n = w_ada.shape[1]
    tn = 1536
    return pl.pallas_call(
        _ada_kernel,
        grid=(n // tn,),
        in_specs=[pl.BlockSpec((8, D_MODEL), lambda j: (0, 0)),
                  pl.BlockSpec((D_MODEL, tn), lambda j: (0, j)),
                  pl.BlockSpec((1, tn), lambda j: (0, j))],
        out_specs=pl.BlockSpec((8, tn), lambda j: (0, j)),
        out_shape=jax.ShapeDtypeStruct((8, n), F32),
        compiler_params=_params("arbitrary"),
        name="ada",
    )(c8, w_ada, b_ada.reshape(1, n))


_NQ = FOX_HEADS * HEAD_PAD
_NS = SWA_HEADS * HEAD_PAD
_NKV = SWA_KV_HEADS * HEAD_PAD
_IN_COLS = 3 * _NQ + HEAD_PAD + _NS + 2 * _NKV


def _inproj_kernel(x_ref, mod_ref, g_ref, w_ref, bf_ref, tri_ref, pq_ref, pk_ref, cst_ref,
                   qt_ref, kp_ref, vt_ref, sq_ref, sk_ref, sv_ref, carry_ref):
    i = pl.program_id(1)

    @pl.when(i == 0)
    def _():
        carry_ref[...] = jnp.zeros_like(carry_ref)

    x = x_ref[0]
    sh1 = mod_ref[0, 0:1, :]
    sc1 = mod_ref[0, 1:2, :]
    h = _rms(x) * g_ref[...] * (1.0 + sc1) + sh1
    proj = _dot(h.astype(BF16), w_ref[...])

    z = proj[:, 3 * _NQ:3 * _NQ + HEAD_PAD] + bf_ref[...]
    ls = jnp.minimum(z, 0.0) - jnp.log(1.0 + jnp.exp(-jnp.abs(z)))
    tri = tri_ref[...]
    hi, mid, lo = _split3(ls)
    fcum = _dot(tri, hi) + _dot(tri, mid) + _dot(tri, lo) + carry_ref[...]
    carry_ref[...] = fcum[fcum.shape[0] - 1:, :]

    fh, fm, fl = _split3(fcum)
    eq = _dot(fh, pq_ref[0]) + _dot(fm, pq_ref[1]) + _dot(fl, pq_ref[2]) + cst_ref[0:1, :]
    ek = _dot(fh, pk_ref[0]) + _dot(fm, pk_ref[1]) + _dot(fl, pk_ref[2]) + cst_ref[1:2, :]
    qt_ref[0, 0] = (proj[:, 0:_NQ] + eq).T.astype(BF16)
    kp_ref[0] = (proj[:, _NQ:2 * _NQ] + ek).astype(BF16)
    vt_ref[0, 0] = (proj[:, 2 * _NQ:3 * _NQ] + cst_ref[2:3, :]).T.astype(BF16)
    o = 3 * _NQ + HEAD_PAD
    sq_ref[0] = proj[:, o:o + _NS].astype(BF16)
    sk_ref[0] = proj[:, o + _NS:o + _NS + _NKV].astype(BF16)
    sv_ref[0] = proj[:, o + _NS + _NKV:o + _NS + 2 * _NKV].astype(BF16)


def _inproj_consts(tr):
    tri = np.tril(np.ones((tr, tr), np.float32))
    pq = np.zeros((3, HEAD_PAD, _NQ), np.float32)
    pk = np.zeros((3, HEAD_PAD, _NQ), np.float32)
    cst = np.zeros((8, _NQ), np.float32)
    for h in range(FOX_HEADS):
        b = h * HEAD_PAD + _F_LANE
        for j in range(3):
            pq[j, h, b + j] = 1.0
            pk[j, h, b + 3 + j] = -1.0
            cst[0, b + 3 + j] = 1.0
            cst[1, b + j] = 1.0
        cst[2, h * HEAD_PAD + _ONE_LANE] = 1.0
    return (jnp.asarray(tri, BF16), jnp.asarray(pq, BF16), jnp.asarray(pk, BF16), jnp.asarray(cst, F32))


def _inproj_call(x, mod, g_pre, w_all, bf_pad):
    B, S, _ = x.shape
    tr = min(ROW_TILE, S)
    n = S // tr
    tri, pq, pk, cst = _inproj_consts(tr)
    row = lambda w: pl.BlockSpec((1, tr, w), lambda b, i: (b, i, 0))
    slab = pl.BlockSpec((1, 1, _NQ, tr), lambda b, i: (b, i, 0, 0))
    full = lambda a: pl.BlockSpec(a.shape, lambda b, i: (0,) * a.ndim)
    outs = [jax.ShapeDtypeStruct((B, n, _NQ, tr), BF16), jax.ShapeDtypeStruct((B, S, _NQ), BF16),
            jax.ShapeDtypeStruct((B, n, _NQ, tr), BF16), jax.ShapeDtypeStruct((B, S, _NS), BF16),
            jax.ShapeDtypeStruct((B, S, _NKV), BF16), jax.ShapeDtypeStruct((B, S, _NKV), BF16)]
    return pl.pallas_call(
        _inproj_kernel,
        grid=(B, n),
        in_specs=[row(D_MODEL),
                  pl.BlockSpec((1, N_MOD, D_MODEL), lambda b, i: (b, 0, 0)),
                  full(g_pre), full(w_all), full(bf_pad), full(tri), full(pq), full(pk), full(cst)],
        out_specs=[slab, row(_NQ), slab, row(_NS), row(_NKV), row(_NKV)],
        out_shape=outs,
        scratch_shapes=[pltpu.VMEM((1, HEAD_PAD), F32)],
        compiler_params=_params("arbitrary", "arbitrary"),
        name="inproj",
    )(x, mod, g_pre, w_all, bf_pad, tri, pq, pk, cst)


def _fox_kernel(qt_ref, k_ref, vt_ref, o_ref, m_ref, acc_ref):
    i = pl.program_id(2)
    t = o_ref.shape[1]
    m_ref[...] = jnp.full_like(m_ref, NEG_INF)
    acc_ref[...] = jnp.zeros_like(acc_ref)

    def block(j, masked):
        off = pl.multiple_of(j * t, t)
        for hh in range(FOX_PAR):
            lanes = slice(hh * HEAD_PAD, (hh + 1) * HEAD_PAD)
            k = k_ref[0, pl.ds(off, t), lanes]
            s = _dot(k, qt_ref[0, 0, lanes, :])
            if masked:
                r = lax.broadcasted_iota(jnp.int32, s.shape, 0)
                c = lax.broadcasted_iota(jnp.int32, s.shape, 1)
                s = jnp.where(r <= c, s, NEG_INF)
            m_prev = m_ref[hh]
            m_new = jnp.maximum(m_prev, jnp.max(s, axis=0, keepdims=True))
            p = jnp.exp(s - m_new)
            acc_ref[hh] = jnp.exp(m_prev - m_new) * acc_ref[hh] + _dot(vt_ref[0, j, lanes, :], p.astype(BF16))
            m_ref[hh] = m_new

    def body(j, carry):
        block(j, False)
        return carry

    lax.fori_loop(0, i, body, 0)
    block(i, True)
    for hh in range(FOX_PAR):
        acc = acc_ref[hh]
        o_ref[0, :, hh * HEAD_PAD:(hh + 1) * HEAD_PAD] = (acc / acc[_ONE_LANE:_ONE_LANE + 1, :]).T.astype(BF16)


def _fox_call(qt, kp, vt):
    B, n, _, t = qt.shape
    S = n * t
    w = FOX_PAR * HEAD_PAD
    return pl.pallas_call(
        _fox_kernel,
        grid=(B, FOX_HEADS // FOX_PAR, n),
        in_specs=[pl.BlockSpec((1, 1, w, t), lambda b, h, i: (b, i, h, 0)),
                  pl.BlockSpec((1, S, w), lambda b, h, i: (b, 0, h)),
                  pl.BlockSpec((1, n, w, t), lambda b, h, i: (b, 0, h, 0))],
        out_specs=pl.BlockSpec((1, t, w), lambda b, h, i: (b, i, h)),
        out_shape=jax.ShapeDtypeStruct((B, S, _NQ), BF16),
        scratch_shapes=[pltpu.VMEM((FOX_PAR, 1, t), F32), pltpu.VMEM((FOX_PAR, HEAD_PAD, t), F32)],
        compiler_params=_params("arbitrary", "arbitrary", "arbitrary"),
        name="fox",
    )(qt, kp, vt)


def _swa_kernel(sink_ref, q_ref, kc_ref, kp_ref, vc_ref, vp_ref, o_ref):
    i = pl.program_id(1)
    nsub = q_ref.shape[1] // WINDOW
    r = lax.broadcasted_iota(jnp.int32, (WINDOW, 2 * WINDOW), 0)
    j = lax.broadcasted_iota(jnp.int32, (WINDOW, 2 * WINDOW), 1)
    dist = r + WINDOW - j
    valid = (dist >= 0) & (dist < WINDOW)
    distf = dist.astype(F32)
    for qb in range(nsub):
        rows = slice(qb * WINDOW, (qb + 1) * WINDOW)
        if qb == 0:
            ok = valid & ((j >= WINDOW) | (i > 0))
        else:
            ok = valid
        for g in range(SWA_KV_HEADS):
            lanes = slice(g * HEAD_PAD, (g + 1) * HEAD_PAD)
            if qb == 0:
                kprev, vprev = kp_ref[0, :, lanes], vp_ref[0, :, lanes]
            else:
                prev = slice((qb - 1) * WINDOW, qb * WINDOW)
                kprev, vprev = kc_ref[0, prev, lanes], vc_ref[0, prev, lanes]
            kk = jnp.concatenate([kprev, kc_ref[0, rows, lanes]], axis=0)
            vv = jnp.concatenate([vprev, vc_ref[0, rows, lanes]], axis=0)
            for u in range(SWA_GROUP):
                hq = g * SWA_GROUP + u
                slope = 2.0 ** (-8.0 * (hq + 1) / SWA_HEADS)
                q = q_ref[0, rows, hq * HEAD_PAD:(hq + 1) * HEAD_PAD]
                s = _dot_nt(q, kk) - slope * distf
                s = jnp.where(ok, s, NEG_INF)
                sink = sink_ref[hq]
                m = jnp.maximum(jnp.max(s, axis=1, keepdims=True), sink)
                p = jnp.exp(s - m)
                den = jnp.sum(p, axis=1, keepdims=True) + jnp.exp(sink - m)
                o = _dot(p.astype(BF16), vv) / den
                o_ref[0, rows, hq * HEAD_PAD:(hq + 1) * HEAD_PAD] = o.astype(BF16)


def _swa_call(sinks, sq, sk, sv):
    B, S, _ = sq.shape
    t = min(SWA_TILE, S)
    per = t // WINDOW
    cur = lambda w: pl.BlockSpec((1, t, w), lambda b, i: (b, i, 0))
    prv = lambda w: pl.BlockSpec((1, WINDOW, w), lambda b, i: (b, jnp.maximum(i * per - 1, 0), 0))
    return pl.pallas_call(
        _swa_kernel,
        grid=(B, S // t),
        in_specs=[pl.BlockSpec(memory_space=pltpu.SMEM),
                  cur(_NS), cur(_NKV), prv(_NKV), cur(_NKV), prv(_NKV)],
        out_specs=cur(_NS),
        out_shape=jax.ShapeDtypeStruct((B, S, _NS), BF16),
        compiler_params=_params("arbitrary", "arbitrary"),
        name="swa",
    )(sinks, sq, sk, sk, sv, sv)


_NO_ID = 1 << 20


def _topk_rows(s, k, val_ref, idx_ref, ids=None):
    if ids is None:
        ids = lax.broadcasted_iota(jnp.int32, s.shape, 0)
    for r in range(k):
        m = jnp.max(s, axis=0, keepdims=True)
        i = jnp.min(jnp.where(s == m, ids, _NO_ID), axis=0, keepdims=True)
        val_ref[pl.ds(r, 1), :] = m
        idx_ref[pl.ds(r, 1), :] = i
        s = jnp.where(ids == i, -jnp.inf, s)


def _cand_counts():
    return [PEER_TOPK // (a + 1) for a in range(PEER_TOPK)]


_CAND_ROWS = 56


def _route_kernel(fo_ref, so_ref, x_ref, mod_ref, gpost_ref, gpre_ref, wof_ref, wos_ref, wq_ref, keys_ref, cid_ref,
                  x1_ref, h2_ref, idx_ref, gate_ref, qs_ref, sv_ref, si_ref, et_ref, gt_ref, cand_ref):
    x = x_ref[0]
    gt1 = mod_ref[0, 2:3, :]
    sh2 = mod_ref[0, 3:4, :]
    sc2 = mod_ref[0, 4:5, :]
    y = _dot(fo_ref[0], wof_ref[...]) + _dot(so_ref[0], wos_ref[...])
    x1 = x + gt1 * (_rms(y) * gpost_ref[...])
    x1_ref[0] = x1
    h2 = _rms(x1) * gpre_ref[...] * (1.0 + sc2) + sh2
    h2_ref[0] = h2
    qp = _dot(h2.astype(BF16), wq_ref[...])
    nhp = 2 * PEER_HEADS
    for hp in range(nhp):
        qs_ref[hp] = qp[:, hp * PEER_HALF:(hp + 1) * PEER_HALF].astype(BF16)

    def half(q, carry):
        for u in range(4):
            hp = 4 * q + u
            sc = _dot_nt(keys_ref[hp], qs_ref[hp])
            _topk_rows(sc, PEER_TOPK, sv_ref.at[hp], si_ref.at[hp])
        return carry

    lax.fori_loop(0, nhp // 4, half, 0)

    counts = _cand_counts()
    used = sum(counts)
    par = cand_ref.shape[0]
    for u in range(par):
        cand_ref[u, pl.ds(used, _CAND_ROWS - used), :] = jnp.full((_CAND_ROWS - used, cand_ref.shape[2]), -jnp.inf, F32)
    cid = cid_ref[...]

    def head(hh, carry):
        for u in range(par):
            h = par * hh + u
            v0, v1 = sv_ref[2 * h], sv_ref[2 * h + 1]
            i0, i1 = si_ref[2 * h], si_ref[2 * h + 1]
            off = 0
            for a, nb in enumerate(counts):
                cand_ref[u, pl.ds(off, nb), :] = v0[a:a + 1, :] + v1[0:nb, :]
                off += nb
            _topk_rows(cand_ref[u], PEER_TOPK, gt_ref.at[h], et_ref.at[h], ids=cid)
            cv, ci = gt_ref[h], et_ref[h]
            ca, cb = ci >> 4, ci & (PEER_TOPK - 1)
            e1 = jnp.zeros_like(ci)
            e2 = jnp.zeros_like(ci)
            for a in range(PEER_TOPK):
                e1 = jnp.where(ca == a, i0[a:a + 1, :], e1)
                e2 = jnp.where(cb == a, i1[a:a + 1, :], e2)
            et_ref[h] = e1 * N_KEYS + e2
            ex = jnp.exp(cv - cv[0:1, :])
            gt_ref[h] = ex / jnp.sum(ex, axis=0, keepdims=True)
        return carry

    lax.fori_loop(0, PEER_HEADS // par, head, 0)
    tt = et_ref.shape[2]
    et = lax.bitcast_convert_type(et_ref[...].reshape(PEER_PICKS, tt), F32)
    idx_ref[...] = lax.bitcast_convert_type(et.T, jnp.int32)
    gate_ref[...] = gt_ref[...].reshape(PEER_PICKS, tt).T


def _route_call(fo, so, x, mod, g_post, g_pre, wof, wos, wq, keys):
    B, S, _ = x.shape
    tt = min(ROUTE_TILE, S)
    per = S // tt
    row = lambda w: pl.BlockSpec((1, tt, w), lambda b, i: (b, i, 0))
    full = lambda a: pl.BlockSpec(a.shape, lambda b, i: (0,) * a.ndim)
    tok = pl.BlockSpec((tt, PEER_PICKS), lambda b, i: (b * per + i, 0))
    flat = [a * PEER_TOPK + b for a, nb in enumerate(_cand_counts()) for b in range(nb)]
    flat += [_NO_ID] * (_CAND_ROWS - len(flat))
    cid = jnp.asarray(np.broadcast_to(np.asarray(flat, np.int32)[:, None], (_CAND_ROWS, tt)))
    return pl.pallas_call(
        _route_kernel,
        grid=(B, per),
        in_specs=[row(_NQ), row(_NS), row(D_MODEL),
                  pl.BlockSpec((1, N_MOD, D_MODEL), lambda b, i: (b, 0, 0)),
                  full(g_post), full(g_pre), full(wof), full(wos), full(wq), full(keys), full(cid)],
        out_specs=[row(D_MODEL), row(D_MODEL), tok, tok],
        out_shape=[jax.ShapeDtypeStruct((B, S, D_MODEL), F32),
                   jax.ShapeDtypeStruct((B, S, D_MODEL), F32),
                   jax.ShapeDtypeStruct((B * S, PEER_PICKS), jnp.int32),
                   jax.ShapeDtypeStruct((B * S, PEER_PICKS), F32)],
        scratch_shapes=[pltpu.VMEM((2 * PEER_HEADS, tt, PEER_HALF), BF16),
                        pltpu.VMEM((2 * PEER_HEADS, PEER_TOPK, tt), F32),
                        pltpu.VMEM((2 * PEER_HEADS, PEER_TOPK, tt), jnp.int32),
                        pltpu.VMEM((PEER_HEADS, PEER_TOPK, tt), jnp.int32),
                        pltpu.VMEM((PEER_HEADS, PEER_TOPK, tt), F32),
                        pltpu.VMEM((2, _CAND_ROWS, tt), F32)],
        compiler_params=_params("arbitrary", "arbitrary"),
        name="route",
    )(fo, so, x, mod, g_post, g_pre, wof, wos, wq, keys, cid)


def _peer_consts():
    half = TABLE_ROWS // 2
    gsum = np.zeros((PEER_GROUP, PEER_GROUP * TABLE_ROWS), np.float32)
    ev = np.zeros((PEER_PICKS, TABLE_COLS), np.float32)
    for j in range(PEER_GROUP):
        gsum[j, j * TABLE_ROWS:j * TABLE_ROWS + half] = 1.0
    for k in range(PEER_PICKS):
        ev[k, k * TABLE_ROWS + half:(k + 1) * TABLE_ROWS] = 1.0
    return jnp.asarray(gsum, BF16), jnp.asarray(ev, BF16)


def _peer_kernel(idx_hbm, tab_hbm, hs_ref, g_ref, gsum_ref, ev_ref, y_ref,
                 buf0, buf1, buf2, buf3, ib0, ib1, wr_ref, sem_g, sem_i):
    s = pl.program_id(0)
    last = pl.num_programs(0) - 1
    tt = PEER_TILE
    npt = tt * PEER_PICKS
    half = TABLE_ROWS // 2
    ngrp = PEER_PICKS // PEER_GROUP
    grows = PEER_GROUP * TABLE_ROWS
    bufs = (buf0, buf1, buf2, buf3)
    ibs = (ib0, ib1)

    def idx_fetch(tile, j):
        return pltpu.make_async_copy(idx_hbm.at[pl.ds(tile * npt, npt)], ibs[j], sem_i.at[j])

    def rows_done(j):
        return pltpu.make_async_copy(tab_hbm.at[pl.ds(0, npt)], bufs[j], sem_g.at[j])

    def issue_rows(ib, buf, sem):
        c = ib[0] >> 31
        for i in range(npt):
            e = ib[i] + c
            pltpu.make_async_copy(tab_hbm.at[e], buf.at[i], sem).start(priority=i % 2)
            if i % PEER_CHAIN == PEER_CHAIN - 1:
                c = e >> 31

    @pl.when(s == 0)
    def _():
        for j in range(2):
            first = idx_fetch(j, j)
            first.start()
            first.wait()

            def body(i, carry):
                pltpu.make_async_copy(tab_hbm.at[ibs[j][i]], bufs[j].at[i], sem_g.at[j]).start()
                return carry

            lax.fori_loop(0, npt, body, 0)
        idx_fetch(2, 0).start()

    row8 = lax.broadcasted_iota(jnp.int32, (half, TABLE_COLS), 0)
    col8 = lax.broadcasted_iota(jnp.int32, (half, TABLE_COLS), 1) % TABLE_ROWS
    mask_v = col8 == row8 + half
    lane = lax.broadcasted_iota(jnp.int32, (PEER_PICKS, HEAD_PAD), 1)
    gsum = gsum_ref[...]

    def evaluate(cur, tok0):
        def ubody(it, at):
            for u in range(PEER_UNROLL):
                t = it * PEER_UNROLL + u
                hrow = hs_ref[tok0 + t]
                h16 = jnp.concatenate([hrow, jnp.zeros_like(hrow)], axis=0).astype(BF16)
                tw = cur[pl.ds(t * PEER_PICKS, PEER_PICKS)]
                prod = (tw * h16[None]).reshape(TABLE_COLS, HEAD_PAD)
                parts = [_dot(gsum, prod[g * grows:(g + 1) * grows]) for g in range(ngrp)]
                z = jnp.sum(jnp.concatenate(parts, axis=0), axis=1, keepdims=True)
                at = jnp.where(lane == t, z, at)
            return at

        at = jnp.zeros((PEER_PICKS, HEAD_PAD), F32)
        for it in range(tt // PEER_UNROLL):
            at = ubody(it, at)
        a = at.T[:tt]
        w = jax.nn.gelu(a) * g_ref[tok0:tok0 + tt, :]
        wr_ref[...] = _dot(w.astype(BF16), ev_ref[...])

        def vbody(it, carry):
            for u in range(PEER_UNROLL):
                t = it * PEER_UNROLL + u
                wrow = jnp.broadcast_to(wr_ref[pl.ds(t, 1), :], (half, TABLE_COLS))
                wexp = jnp.where(mask_v, wrow, 0.0).astype(BF16)
                wb = cur[pl.ds(t * PEER_PICKS, PEER_PICKS)].reshape(TABLE_COLS, HEAD_PAD)
                y_ref[tok0 + t] = _dot(wexp, wb)
            return carry

        for it in range(tt // PEER_UNROLL):
            vbody(it, 0)

    for p in range(PEER_PHASES):
        k = PEER_PHASES * s + p
        idx_fetch(k + 2, p % 2).wait()
        idx_fetch(k + 3, (p + 1) % 2).start()
        rows_done(p).wait()
        issue_rows(ibs[p % 2], bufs[(p + 2) % PEER_PHASES], sem_g.at[(p + 2) % PEER_PHASES])
        evaluate(bufs[p], p * tt)

    @pl.when(s == last)
    def _():
        rows_done(0).wait()
        rows_done(1).wait()
        idx_fetch(0, 0).wait()


def _peer_call(idx_flat, table, hs3, gates, T):
    tt = PEER_TILE
    npt = tt * PEER_PICKS
    assert tt % PEER_UNROLL == 0 and PEER_PICKS % PEER_GROUP == 0 and tt <= HEAD_PAD and PEER_PHASES == 4
    gsum, ev = _peer_consts()
    short = max(0, (T + 3 * tt) * PEER_PICKS - idx_flat.shape[0])
    idx_pad = jnp.pad(idx_flat, (0, short)) if short else idx_flat
    step = PEER_PHASES * tt
    full = lambda a: pl.BlockSpec(a.shape, lambda s: (0,) * a.ndim)
    rows = pltpu.VMEM((npt, TABLE_ROWS, HEAD_PAD), BF16)
    return pl.pallas_call(
        _peer_kernel,
        grid=(T // step,),
        in_specs=[pl.BlockSpec(memory_space=pl.ANY),
                  pl.BlockSpec(memory_space=pl.ANY),
                  pl.BlockSpec((step, 8, HEAD_PAD), lambda s: (s, 0, 0)),
                  pl.BlockSpec((step, PEER_PICKS), lambda s: (s, 0)),
                  full(gsum), full(ev)],
        out_specs=pl.BlockSpec((step, 8, HEAD_PAD), lambda s: (s, 0, 0)),
        out_shape=jax.ShapeDtypeStruct((T, 8, HEAD_PAD), F32),
        scratch_shapes=[rows, rows, rows, rows,
                        pltpu.SMEM((npt,), jnp.int32),
                        pltpu.SMEM((npt,), jnp.int32),
                        pltpu.VMEM((tt, TABLE_COLS), F32),
                        pltpu.SemaphoreType.DMA((PEER_PHASES,)),
                        pltpu.SemaphoreType.DMA((2,))],
        compiler_params=_params("arbitrary"),
        name="peer",
    )(idx_pad, table, hs3, gates, gsum, ev)


SC_LANES = 16
SC_GROUP = 16
SC_UNROLL = 2
_GELU_C = 0.7978845608028654


def _peer_sc_kernel(t1, idx_hbm, tab_hbm, h_hbm, g_hbm, y_hbm,
                    idx_a, idx_b, g_a, g_b, h_a, h_b, o_v, rows_v, sem, sem_in):
    ncores = lax.axis_size("c")
    wid = lax.axis_index("s") * ncores + lax.axis_index("c")
    per = y_hbm.shape[0] // (ncores * lax.axis_size("s"))
    base = t1 + wid * per
    nchunk = D_MODEL // SC_LANES
    ngrp = PEER_PICKS // SC_GROUP
    lane = lax.iota(jnp.int32, SC_LANES)
    sets = ((idx_a, g_a, h_a), (idx_b, g_b, h_b))

    def fetch(t, k):
        return (pltpu.make_async_copy(idx_hbm.at[t], sets[k][0], sem_in.at[k]),
                pltpu.make_async_copy(g_hbm.at[t], sets[k][1], sem_in.at[k]),
                pltpu.make_async_copy(h_hbm.at[t], sets[k][2], sem_in.at[k]))

    def gather(k, g, slot):
        return pltpu.make_async_copy(tab_hbm.at[sets[k][0].at[pl.ds(g * SC_GROUP, SC_GROUP)]], rows_v.at[slot],
                                     sem.at[slot])

    def process(t, k, t_next):
        g_v, h_v = sets[k][1], sets[k][2]

        def zero(c, carry):
            o_v[pl.ds(c * SC_LANES, SC_LANES)] = jnp.zeros((SC_LANES,), F32)
            return carry

        lax.fori_loop(0, nchunk, zero, 0)

        for g in range(ngrp):
            slot = g % 2
            if g + 1 < ngrp:
                gather(k, g + 1, 1 - slot).start()
            else:
                for c in fetch(t_next, 1 - k):
                    c.wait()
                gather(1 - k, 0, 0).start()
            gather(k, g, slot).wait()

            def dot_body(c, accs):
                hv = h_v[pl.ds(c * SC_LANES, SC_LANES)]
                out = []
                for p in range(SC_GROUP):
                    w = rows_v[slot, p, pl.ds(c * SC_LANES, SC_LANES)]
                    u = lax.bitcast_convert_type(w << 16, F32)
                    out.append(accs[p] + u * hv)
                return tuple(out)

            accs = plsc.parallel_loop(0, nchunk, unroll=SC_UNROLL,
                                      carry=tuple(jnp.zeros((SC_LANES,), F32) for _ in range(SC_GROUP)))(dot_body)
            a = jnp.zeros((SC_LANES,), F32)
            for p in range(SC_GROUP):
                a = jnp.where(lane == p, jnp.sum(accs[p]), a)
            z = _GELU_C * (a + 0.044715 * a * a * a)
            th = 1.0 - 2.0 / (jnp.exp(2.0 * z) + 1.0)
            wv = 0.5 * a * (1.0 + th) * g_v[pl.ds(g * SC_GROUP, SC_GROUP)]
            ws = [jnp.sum(jnp.where(lane == p, wv, 0.0)) for p in range(SC_GROUP)]

            def ax_body(c):
                o = o_v[pl.ds(c * SC_LANES, SC_LANES)]
                for p in range(SC_GROUP):
                    w = rows_v[slot, p, pl.ds(c * SC_LANES, SC_LANES)]
                    v = lax.bitcast_convert_type(w & jnp.int32(-65536), F32)
                    o = o + ws[p] * v
                o_v[pl.ds(c * SC_LANES, SC_LANES)] = o

            plsc.parallel_loop(0, nchunk, unroll=SC_UNROLL)(ax_body)

        pltpu.sync_copy(o_v, y_hbm.at[t - t1])

    for c in fetch(base, 0):
        c.start()
    for c in fetch(base, 0):
        c.wait()
    gather(0, 0, 0).start()

    def pair(j, carry):
        t0 = base + 2 * j
        t2 = jnp.minimum(t0 + 2, base + per - 1)
        for c in fetch(t0 + 1, 1):
            c.start()
        process(t0, 0, t0 + 1)
        for c in fetch(t2, 0):
            c.start()
        process(t0 + 1, 1, t2)
        return carry

    lax.fori_loop(0, per // 2, pair, 0)
    gather(0, 0, 0).wait()


def _peer_sc_call(idx2, table_i32, h2, gates, t1):
    ts = h2.shape[0] - t1
    mesh = plsc.VectorSubcoreMesh(core_axis_name="c", subcore_axis_name="s")
    run = pl.kernel(
        functools.partial(_peer_sc_kernel, t1),
        out_type=jax.ShapeDtypeStruct((ts, D_MODEL), F32),
        mesh=mesh,
        scratch_types=[pltpu.VMEM((PEER_PICKS,), jnp.int32), pltpu.VMEM((PEER_PICKS,), jnp.int32),
                       pltpu.VMEM((PEER_PICKS,), F32), pltpu.VMEM((PEER_PICKS,), F32),
                       pltpu.VMEM((D_MODEL,), F32), pltpu.VMEM((D_MODEL,), F32),
                       pltpu.VMEM((D_MODEL,), F32),
                       pltpu.VMEM((2, SC_GROUP, D_MODEL), jnp.int32),
                       pltpu.SemaphoreType.DMA((2,)),
                       pltpu.SemaphoreType.DMA((2,))],
        compiler_params=pltpu.CompilerParams(needs_layout_passes=False),
        name="peer_sc",
    )
    return run(idx2, table_i32, h2, gates)


def _final_kernel(n1, x1_ref, ya_ref, yb_ref, mod_ref, g_ref, o_ref):
    gi = pl.program_id(0) * pl.num_programs(1) + pl.program_id(1)
    y = jnp.where(gi < n1, ya_ref[...], yb_ref[...])
    gt2 = mod_ref[0, 5:6, :]
    o_ref[0] = x1_ref[0] + gt2 * (_rms(y) * g_ref[...])


def _final_call(x1, ya, yb, mod, g_post):
    B, S, _ = x1.shape
    tr = min(ROW_TILE, S)
    per = S // tr
    assert ya.shape[0] % tr == 0 and yb.shape[0] % tr == 0
    n1 = ya.shape[0] // tr
    row = pl.BlockSpec((1, tr, D_MODEL), lambda b, i: (b, i, 0))
    first = pl.BlockSpec((tr, D_MODEL), lambda b, i: (jnp.minimum(b * per + i, n1 - 1), 0))
    rest = pl.BlockSpec((tr, D_MODEL), lambda b, i: (jnp.maximum(b * per + i - n1, 0), 0))
    return pl.pallas_call(
        functools.partial(_final_kernel, n1),
        grid=(B, per),
        in_specs=[row, first, rest, pl.BlockSpec((1, N_MOD, D_MODEL), lambda b, i: (b, 0, 0)),
                  pl.BlockSpec((1, D_MODEL), lambda b, i: (0, 0))],
        out_specs=row,
        out_shape=jax.ShapeDtypeStruct((B, S, D_MODEL), F32),
        compiler_params=_params("arbitrary", "arbitrary"),
        name="final",
    )(x1, ya, yb, mod, g_post)


def _pad_heads_cols(w, nh, scale=1.0):
    k = w.shape[0]
    w = (w * scale).reshape(k, nh, HEAD_DIM)
    return jnp.pad(w, ((0, 0), (0, 0), (0, HEAD_PAD - HEAD_DIM))).reshape(k, nh * HEAD_PAD)


def _pad_heads_rows(w, nh):
    n = w.shape[1]
    w = w.reshape(nh, HEAD_DIM, n)
    return jnp.pad(w, ((0, 0), (0, HEAD_PAD - HEAD_DIM), (0, 0))).reshape(nh * HEAD_PAD, n)


def _layer(x, c8, w_ada, b_ada, g_pre_mix, g_post_mix, g_pre_ffn, g_post_ffn,
           w_in, b_fgate, swa_sinks, w_out, w_query, sub_keys, w_u, w_v):
    B, S, D = x.shape
    T = B * S
    scale = HEAD_DIM ** -0.5
    mod = _ada_call(c8, w_ada, b_ada)[:B].reshape(B, N_MOD, D)

    o = 0
    parts = []
    for nh, sc in ((FOX_HEADS, scale), (FOX_HEADS, 1.0), (FOX_HEADS, 1.0)):
        parts.append(_pad_heads_cols(w_in[:, o:o + nh * HEAD_DIM], nh, sc))
        o += nh * HEAD_DIM
    parts.append(jnp.pad(w_in[:, o:o + FOX_HEADS], ((0, 0), (0, HEAD_PAD - FOX_HEADS))))
    o += FOX_HEADS
    for nh, sc in ((SWA_HEADS, scale), (SWA_KV_HEADS, 1.0), (SWA_KV_HEADS, 1.0)):
        parts.append(_pad_heads_cols(w_in[:, o:o + nh * HEAD_DIM], nh, sc))
        o += nh * HEAD_DIM
    w_all = jnp.concatenate(parts, axis=1).astype(BF16)
    bf_pad = jnp.pad(b_fgate, (0, HEAD_PAD - FOX_HEADS)).reshape(1, HEAD_PAD)

    qt, kp, vt, sq, sk, sv = _inproj_call(x, mod, g_pre_mix.reshape(1, D), w_all, bf_pad)
    fo = _fox_call(qt, kp, vt)
    so = _swa_call(swa_sinks, sq, sk, sv)

    nf = FOX_HEADS * HEAD_DIM
    wof = _pad_heads_rows(w_out[:nf], FOX_HEADS).astype(BF16)
    wos = _pad_heads_rows(w_out[nf:], SWA_HEADS).astype(BF16)
    keys = sub_keys.reshape(2 * PEER_HEADS, N_KEYS, PEER_HALF).astype(BF16)
    x1, h2, idx, gates = _route_call(fo, so, x, mod, g_post_mix.reshape(1, D), g_pre_ffn.reshape(1, D),
                                     wof, wos, w_query.astype(BF16), keys)

    ub, vb = w_u.astype(BF16), w_v.astype(BF16)
    table = jnp.concatenate([ub.reshape(N_EXPERTS, 8, HEAD_PAD), vb.reshape(N_EXPERTS, 8, HEAD_PAD)], axis=1)
    h2f = h2.reshape(T, D)
    tile = min(ROW_TILE, S)
    ts = (int(T * SC_SHARE) // tile) * tile
    t1 = T - ts
    y_tc = _peer_call(idx.reshape(T * PEER_PICKS), table, h2f.reshape(T, 8, HEAD_PAD), gates, t1).reshape(t1, D)
    if not ts:
        return _final_call(x1, y_tc, y_tc, mod, g_post_ffn.reshape(1, D))
    u16 = lax.bitcast_convert_type(ub, jnp.uint16).astype(jnp.uint32)
    v16 = lax.bitcast_convert_type(vb, jnp.uint16).astype(jnp.uint32)
    table_i32 = lax.bitcast_convert_type((v16 << 16) | u16, jnp.int32)
    y_sc = _peer_sc_call(idx, table_i32, h2f, gates, t1)
    return _final_call(x1, y_tc, y_sc, mod, g_post_ffn.reshape(1, D))


def kernel(x, c, w_ada, b_ada, g_pre_mix, g_post_mix, g_pre_ffn, g_post_ffn, w_in, b_fgate, swa_sinks, w_out,
           w_query, sub_keys, w_u, w_v):
    B = x.shape[0]
    c8 = jnp.pad(c, ((0, 8 - B), (0, 0)))
    for l in range(w_ada.shape[0]):
        x = _layer(x, c8, w_ada[l], b_ada[l], g_pre_mix[l], g_post_mix[l], g_pre_ffn[l], g_post_ffn[l],
                   w_in[l], b_fgate[l], swa_sinks[l], w_out[l], w_query[l], sub_keys[l], w_u[l], w_v[l])
    return x
```

```python
import functools

import numpy as np
import jax
import jax.numpy as jnp
from jax import lax
from jax.experimental import pallas as pl
from jax.experimental.pallas import tpu as pltpu
from jax.experimental.pallas import tpu_sc as plsc

F32 = jnp.float32
BF16 = jnp.bfloat16

D_MODEL = 1024
HEAD_DIM = 64
HEAD_PAD = 128
FOX_HEADS = 8
SWA_HEADS = 8
SWA_KV_HEADS = 2
SWA_GROUP = SWA_HEADS // SWA_KV_HEADS
WINDOW = 128
PEER_HEADS = 8
PEER_HALF = 128
N_KEYS = 128
N_EXPERTS = N_KEYS * N_KEYS
PEER_TOPK = 16
PEER_PICKS = PEER_HEADS * PEER_TOPK
N_MOD = 6
RMS_EPS = 1e-6
NEG_INF = -1e30

_F_LANE = HEAD_DIM
_ONE_LANE = HEAD_DIM

ROW_TILE = 1024
FOX_TILE = 1024
SWA_TILE = 512
ROUTE_TILE = 256
PEER_TILE = 8
PEER_PHASES = 4
PEER_UNROLL = 8
PEER_CHAIN = 8
PEER_GROUP = 16
SC_SHARE = 0.47
TABLE_ROWS = 16
TABLE_COLS = PEER_PICKS * TABLE_ROWS

_VMEM_LIMIT = 56 * 1024 * 1024


def _dot(a, b):
    return jnp.dot(a, b, preferred_element_type=F32)


def _dot_nt(a, b):
    return lax.dot_general(a, b, (((1,), (1,)), ((), ())), preferred_element_type=F32)


def _split3(x):
    hi = x.astype(BF16)
    r = x - hi.astype(F32)
    mid = r.astype(BF16)
    lo = (r - mid.astype(F32)).astype(BF16)
    return hi, mid, lo


def _rms(x):
    return x * lax.rsqrt(jnp.mean(x * x, axis=-1, keepdims=True) + RMS_EPS)


def _params(*sem):
    return pltpu.CompilerParams(dimension_semantics=sem, vmem_limit_bytes=_VMEM_LIMIT)


def _ada_kernel(c_ref, w_ref, b_ref, o_ref):
    c = c_ref[...]
    s = (c * jax.nn.sigmoid(c)).astype(BF16)
    o_ref[...] = _dot(s, w_ref[...].astype(BF16)) + b_ref[...]


def _ada_call(c8, w_ada, b_ada):
    n = w_ada.shape[1]
    tn = 1536
    return pl.pallas_call(
        _ada_kernel,
        grid=(n // tn,),
        in_specs=[pl.BlockSpec((8, D_MODEL), lambda j: (0, 0)),
                  pl.BlockSpec((D_MODEL, tn), lambda j: (0, j)),
                  pl.BlockSpec((1, tn), lambda j: (0, j))],
        out_specs=pl.BlockSpec((8, tn), lambda j: (0, j)),
        out_shape=jax.ShapeDtypeStruct((8, n), F32),
        compiler_params=_params("arbitrary"),
        name="ada",
    )(c8, w_ada, b_ada.reshape(1, n))


_NQ = FOX_HEADS * HEAD_PAD
_NS = SWA_HEADS * HEAD_PAD
_NKV = SWA_KV_HEADS * HEAD_PAD
_IN_COLS = 3 * _NQ + HEAD_PAD + _NS + 2 * _NKV


def _inproj_kernel(x_ref, mod_ref, g_ref, w_ref, bf_ref, tri_ref, pq_ref, pk_ref, cst_ref,
                   qt_ref, kp_ref, vt_ref, sq_ref, sk_ref, sv_ref, carry_ref):
    i = pl.program_id(1)

    @pl.when(i == 0)
    def _():
        carry_ref[...] = jnp.zeros_like(carry_ref)

    x = x_ref[0]
    sh1 = mod_ref[0, 0:1, :]
    sc1 = mod_ref[0, 1:2, :]
    h = _rms(x) * g_ref[...] * (1.0 + sc1) + sh1
    proj = _dot(h.astype(BF16), w_ref[...])

    z = proj[:, 3 * _NQ:3 * _NQ + HEAD_PAD] + bf_ref[...]
    ls = jnp.minimum(z, 0.0) - jnp.log(1.0 + jnp.exp(-jnp.abs(z)))
    tri = tri_ref[...]
    hi, mid, lo = _split3(ls)
    fcum = _dot(tri, hi) + _dot(tri, mid) + _dot(tri, lo) + carry_ref[...]
    carry_ref[...] = fcum[fcum.shape[0] - 1:, :]

    fh, fm, fl = _split3(fcum)
    eq = _dot(fh, pq_ref[0]) + _dot(fm, pq_ref[1]) + _dot(fl, pq_ref[2]) + cst_ref[0:1, :]
    ek = _dot(fh, pk_ref[0]) + _dot(fm, pk_ref[1]) + _dot(fl, pk_ref[2]) + cst_ref[1:2, :]
    qt_ref[0, 0] = (proj[:, 0:_NQ] + eq).T.astype(BF16)
    kp_ref[0] = (proj[:, _NQ:2 * _NQ] + ek).astype(BF16)
    vt_ref[0, 0] = (proj[:, 2 * _NQ:3 * _NQ] + cst_ref[2:3, :]).T.astype(BF16)
    o = 3 * _NQ + HEAD_PAD
    sq_ref[0] = proj[:, o:o + _NS].astype(BF16)
    sk_ref[0] = proj[:, o + _NS:o + _NS + _NKV].astype(BF16)
    sv_ref[0] = proj[:, o + _NS + _NKV:o + _NS + 2 * _NKV].astype(BF16)


def _inproj_consts(tr):
    tri = np.tril(np.ones((tr, tr), np.float32))
    pq = np.zeros((3, HEAD_PAD, _NQ), np.float32)
    pk = np.zeros((3, HEAD_PAD, _NQ), np.float32)
    cst = np.zeros((8, _NQ), np.float32)
    for h in range(FOX_HEADS):
        b = h * HEAD_PAD + _F_LANE
        for j in range(3):
            pq[j, h, b + j] = 1.0
            pk[j, h, b + 3 + j] = -1.0
            cst[0, b + 3 + j] = 1.0
            cst[1, b + j] = 1.0
        cst[2, h * HEAD_PAD + _ONE_LANE] = 1.0
    return (jnp.asarray(tri, BF16), jnp.asarray(pq, BF16), jnp.asarray(pk, BF16), jnp.asarray(cst, F32))


def _inproj_call(x, mod, g_pre, w_all, bf_pad):
    B, S, _ = x.shape
    tr = min(ROW_TILE, S)
    n = S // tr
    tri, pq, pk, cst = _inproj_consts(tr)
    row = lambda w: pl.BlockSpec((1, tr, w), lambda b, i: (b, i, 0))
    slab = pl.BlockSpec((1, 1, _NQ, tr), lambda b, i: (b, i, 0, 0))
    full = lambda a: pl.BlockSpec(a.shape, lambda b, i: (0,) * a.ndim)
    outs = [jax.ShapeDtypeStruct((B, n, _NQ, tr), BF16), jax.ShapeDtypeStruct((B, S, _NQ), BF16),
            jax.ShapeDtypeStruct((B, n, _NQ, tr), BF16), jax.ShapeDtypeStruct((B, S, _NS), BF16),
            jax.ShapeDtypeStruct((B, S, _NKV), BF16), jax.ShapeDtypeStruct((B, S, _NKV), BF16)]
    return pl.pallas_call(
        _inproj_kernel,
        grid=(B, n),
        in_specs=[row(D_MODEL),
                  pl.BlockSpec((1, N_MOD, D_MODEL), lambda b, i: (b, 0, 0)),
                  full(g_pre), full(w_all), full(bf_pad), full(tri), full(pq), full(pk), full(cst)],
        out_specs=[slab, row(_NQ), slab, row(_NS), row(_NKV), row(_NKV)],
        out_shape=outs,
        scratch_shapes=[pltpu.VMEM((1, HEAD_PAD), F32)],
        compiler_params=_params("arbitrary", "arbitrary"),
        name="inproj",
    )(x, mod, g_pre, w_all, bf_pad, tri, pq, pk, cst)


def _fox_kernel(qt_ref, k_ref, vt_ref, o_ref, m_ref, acc_ref, sa_ref, sb_ref):
    i = pl.program_id(2)
    t = o_ref.shape[1]
    m_ref[...] = jnp.full_like(m_ref, NEG_INF)
    acc_ref[...] = jnp.zeros_like(acc_ref)
    bufs = (sa_ref, sb_ref)

    def scores(j, dst):
        off = pl.multiple_of(j * t, t)
        dst[...] = _dot(k_ref[0, pl.ds(off, t), :], qt_ref[0, 0])

    def absorb(j, src, masked):
        s = src[...]
        if masked:
            r = lax.broadcasted_iota(jnp.int32, s.shape, 0)
            c = lax.broadcasted_iota(jnp.int32, s.shape, 1)
            s = jnp.where(r <= c, s, NEG_INF)
        m_prev = m_ref[...]
        m_new = jnp.maximum(m_prev, jnp.max(s, axis=0, keepdims=True))
        p = jnp.exp(s - m_new)
        acc_ref[...] = jnp.exp(m_prev - m_new) * acc_ref[...] + _dot(vt_ref[0, j], p.astype(BF16))
        m_ref[...] = m_new

    scores(0, sa_ref)

    def pair(jj, carry):
        j = 2 * jj
        scores(j + 1, sb_ref)
        absorb(j, sa_ref, False)
        scores(j + 2, sa_ref)
        absorb(j + 1, sb_ref, False)
        return carry

    lax.fori_loop(0, i // 2, pair, 0)

    @pl.when(i % 2 == 1)
    def _():
        scores(i, sb_ref)
        absorb(i - 1, sa_ref, False)
        absorb(i, sb_ref, True)

    @pl.when(i % 2 == 0)
    def _():
        absorb(i, sa_ref, True)

    acc = acc_ref[...]
    o_ref[0] = (acc / acc[_ONE_LANE:_ONE_LANE + 1, :]).T.astype(BF16)


def _fox_call(qt, kp, vt):
    B, n, _, t = qt.shape
    S = n * t
    w = HEAD_PAD
    return pl.pallas_call(
        _fox_kernel,
        grid=(B, FOX_HEADS, n),
        in_specs=[pl.BlockSpec((1, 1, w, t), lambda b, h, i: (b, i, h, 0)),
                  pl.BlockSpec((1, S, w), lambda b, h, i: (b, 0, h)),
                  pl.BlockSpec((1, n, w, t), lambda b, h, i: (b, 0, h, 0))],
        out_specs=pl.BlockSpec((1, t, w), lambda b, h, i: (b, i, h)),
        out_shape=jax.ShapeDtypeStruct((B, S, _NQ), BF16),
        scratch_shapes=[pltpu.VMEM((1, t), F32), pltpu.VMEM((HEAD_PAD, t), F32),
                        pltpu.VMEM((t, t), F32), pltpu.VMEM((t, t), F32)],
        compiler_params=_params("arbitrary", "arbitrary", "arbitrary"),
        name="fox",
    )(qt, kp, vt)


def _swa_kernel(sink_ref, q_ref, kc_ref, kp_ref, vc_ref, vp_ref, o_ref):
    i = pl.program_id(1)
    nsub = q_ref.shape[1] // WINDOW
    r = lax.broadcasted_iota(jnp.int32, (WINDOW, 2 * WINDOW), 0)
    j = lax.broadcasted_iota(jnp.int32, (WINDOW, 2 * WINDOW), 1)
    dist = r + WINDOW - j
    valid = (dist >= 0) & (dist < WINDOW)
    distf = dist.astype(F32)
    for qb in range(nsub):
        rows = slice(qb * WINDOW, (qb + 1) * WINDOW)
        if qb == 0:
            ok = valid & ((j >= WINDOW) | (i > 0))
        else:
            ok = valid
        for g in range(SWA_KV_HEADS):
            lanes = slice(g * HEAD_PAD, (g + 1) * HEAD_PAD)
            if qb == 0:
                kprev, vprev = kp_ref[0, :, lanes], vp_ref[0, :, lanes]
            else:
                prev = slice((qb - 1) * WINDOW, qb * WINDOW)
                kprev, vprev = kc_ref[0, prev, lanes], vc_ref[0, prev, lanes]
            kk = jnp.concatenate([kprev, kc_ref[0, rows, lanes]], axis=0)
            vv = jnp.concatenate([vprev, vc_ref[0, rows, lanes]], axis=0)
            for u in range(SWA_GROUP):
                hq = g * SWA_GROUP + u
                slope = 2.0 ** (-8.0 * (hq + 1) / SWA_HEADS)
                q = q_ref[0, rows, hq * HEAD_PAD:(hq + 1) * HEAD_PAD]
                s = _dot_nt(q, kk) - slope * distf
                s = jnp.where(ok, s, NEG_INF)
                sink = sink_ref[hq]
                m = jnp.maximum(jnp.max(s, axis=1, keepdims=True), sink)
                p = jnp.exp(s - m)
                den = jnp.sum(p, axis=1, keepdims=True) + jnp.exp(sink - m)
                o = _dot(p.astype(BF16), vv) / den
                o_ref[0, rows, hq * HEAD_PAD:(hq + 1) * HEAD_PAD] = o.astype(BF16)


def _swa_call(sinks, sq, sk, sv):
    B, S, _ = sq.shape
    t = min(SWA_TILE, S)
    per = t // WINDOW
    cur = lambda w: pl.BlockSpec((1, t, w), lambda b, i: (b, i, 0))
    prv = lambda w: pl.BlockSpec((1, WINDOW, w), lambda b, i: (b, jnp.maximum(i * per - 1, 0), 0))
    return pl.pallas_call(
        _swa_kernel,
        grid=(B, S // t),
        in_specs=[pl.BlockSpec(memory_space=pltpu.SMEM),
                  cur(_NS), cur(_NKV), prv(_NKV), cur(_NKV), prv(_NKV)],
        out_specs=cur(_NS),
        out_shape=jax.ShapeDtypeStruct((B, S, _NS), BF16),
        compiler_params=_params("arbitrary", "arbitrary"),
        name="swa",
    )(sinks, sq, sk, sk, sv, sv)


_NO_ID = 1 << 20


def _topk_rows(s, k, val_ref, idx_ref, ids=None):
    if ids is None:
        ids = lax.broadcasted_iota(jnp.int32, s.shape, 0)
    for r in range(k):
        m = jnp.max(s, axis=0, keepdims=True)
        i = jnp.min(jnp.where(s == m, ids, _NO_ID), axis=0, keepdims=True)
        val_ref[pl.ds(r, 1), :] = m
        idx_ref[pl.ds(r, 1), :] = i
        s = jnp.where(ids == i, -jnp.inf, s)


def _cand_counts():
    return [PEER_TOPK // (a + 1) for a in range(PEER_TOPK)]


_CAND_ROWS = 56


def _route_kernel(fo_ref, so_ref, x_ref, mod_ref, gpost_ref, gpre_ref, wof_ref, wos_ref, wq_ref, keys_ref, cid_ref,
                  x1_ref, h2_ref, idx_ref, gate_ref, qs_ref, sv_ref, si_ref, et_ref, gt_ref, cand_ref):
    x = x_ref[0]
    gt1 = mod_ref[0, 2:3, :]
    sh2 = mod_ref[0, 3:4, :]
    sc2 = mod_ref[0, 4:5, :]
    y = _dot(fo_ref[0], wof_ref[...]) + _dot(so_ref[0], wos_ref[...])
    x1 = x + gt1 * (_rms(y) * gpost_ref[...])
    x1_ref[0] = x1
    h2 = _rms(x1) * gpre_ref[...] * (1.0 + sc2) + sh2
    h2_ref[0] = h2
    qp = _dot(h2.astype(BF16), wq_ref[...])
    nhp = 2 * PEER_HEADS
    for hp in range(nhp):
        qs_ref[hp] = qp[:, hp * PEER_HALF:(hp + 1) * PEER_HALF].astype(BF16)

    def half(q, carry):
        for u in range(4):
            hp = 4 * q + u
            sc = _dot_nt(keys_ref[hp], qs_ref[hp])
            _topk_rows(sc, PEER_TOPK, sv_ref.at[hp], si_ref.at[hp])
        return carry

    lax.fori_loop(0, nhp // 4, half, 0)

    counts = _cand_counts()
    used = sum(counts)
    par = cand_ref.shape[0]
    for u in range(par):
        cand_ref[u, pl.ds(used, _CAND_ROWS - used), :] = jnp.full((_CAND_ROWS - used, cand_ref.shape[2]), -jnp.inf, F32)
    cid = cid_ref[...]

    def head(hh, carry):
        for u in range(par):
            h = par * hh + u
            v0, v1 = sv_ref[2 * h], sv_ref[2 * h + 1]
            i0, i1 = si_ref[2 * h], si_ref[2 * h + 1]
            off = 0
            for a, nb in enumerate(counts):
                cand_ref[u, pl.ds(off, nb), :] = v0[a:a + 1, :] + v1[0:nb, :]
                off += nb
            _topk_rows(cand_ref[u], PEER_TOPK, gt_ref.at[h], et_ref.at[h], ids=cid)
            cv, ci = gt_ref[h], et_ref[h]
            ca, cb = ci >> 4, ci & (PEER_TOPK - 1)
            e1 = jnp.zeros_like(ci)
            e2 = jnp.zeros_like(ci)
            for a in range(PEER_TOPK):
                e1 = jnp.where(ca == a, i0[a:a + 1, :], e1)
                e2 = jnp.where(cb == a, i1[a:a + 1, :], e2)
            et_ref[h] = e1 * N_KEYS + e2
            ex = jnp.exp(cv - cv[0:1, :])
            gt_ref[h] = ex / jnp.sum(ex, axis=0, keepdims=True)
        return carry

    lax.fori_loop(0, PEER_HEADS // par, head, 0)
    tt = et_ref.shape[2]
    et = lax.bitcast_convert_type(et_ref[...].reshape(PEER_PICKS, tt), F32)
    idx_ref[...] = lax.bitcast_convert_type(et.T, jnp.int32)
    gate_ref[...] = gt_ref[...].reshape(PEER_PICKS, tt).T


def _route_call(fo, so, x, mod, g_post, g_pre, wof, wos, wq, keys):
    B, S, _ = x.shape
    tt = min(ROUTE_TILE, S)
    per = S // tt
    row = lambda w: pl.BlockSpec((1, tt, w), lambda b, i: (b, i, 0))
    full = lambda a: pl.BlockSpec(a.shape, lambda b, i: (0,) * a.ndim)
    tok = pl.BlockSpec((tt, PEER_PICKS), lambda b, i: (b * per + i, 0))
    flat = [a * PEER_TOPK + b for a, nb in enumerate(_cand_counts()) for b in range(nb)]
    flat += [_NO_ID] * (_CAND_ROWS - len(flat))
    cid = jnp.asarray(np.broadcast_to(np.asarray(flat, np.int32)[:, None], (_CAND_ROWS, tt)))
    return pl.pallas_call(
        _route_kernel,
        grid=(B, per),
        in_specs=[row(_NQ), row(_NS), row(D_MODEL),
                  pl.BlockSpec((1, N_MOD, D_MODEL), lambda b, i: (b, 0, 0)),
                  full(g_post), full(g_pre), full(wof), full(wos), full(wq), full(keys), full(cid)],
        out_specs=[row(D_MODEL), row(D_MODEL), tok, tok],
        out_shape=[jax.ShapeDtypeStruct((B, S, D_MODEL), F32),
                   jax.ShapeDtypeStruct((B, S, D_MODEL), F32),
                   jax.ShapeDtypeStruct((B * S, PEER_PICKS), jnp.int32),
                   jax.ShapeDtypeStruct((B * S, PEER_PICKS), F32)],
        scratch_shapes=[pltpu.VMEM((2 * PEER_HEADS, tt, PEER_HALF), BF16),
                        pltpu.VMEM((2 * PEER_HEADS, PEER_TOPK, tt), F32),
                        pltpu.VMEM((2 * PEER_HEADS, PEER_TOPK, tt), jnp.int32),
                        pltpu.VMEM((PEER_HEADS, PEER_TOPK, tt), jnp.int32),
                        pltpu.VMEM((PEER_HEADS, PEER_TOPK, tt), F32),
                        pltpu.VMEM((2, _CAND_ROWS, tt), F32)],
        compiler_params=_params("arbitrary", "arbitrary"),
        name="route",
    )(fo, so, x, mod, g_post, g_pre, wof, wos, wq, keys, cid)


def _peer_consts():
    half = TABLE_ROWS // 2
    gsum = np.zeros((PEER_GROUP, PEER_GROUP * TABLE_ROWS), np.float32)
    ev = np.zeros((PEER_PICKS, TABLE_COLS), np.float32)
    for j in range(PEER_GROUP):
        gsum[j, j * TABLE_ROWS:j * TABLE_ROWS + half] = 1.0
    for k in range(PEER_PICKS):
        ev[k, k * TABLE_ROWS + half:(k + 1) * TABLE_ROWS] = 1.0
    return jnp.asarray(gsum, BF16), jnp.asarray(ev, BF16)


def _peer_kernel(idx_hbm, tab_hbm, hs_ref, g_ref, gsum_ref, ev_ref, y_ref,
                 buf0, buf1, buf2, buf3, ib0, ib1, wr_ref, sem_g, sem_i):
    s = pl.program_id(0)
    last = pl.num_programs(0) - 1
    tt = PEER_TILE
    npt = tt * PEER_PICKS
    half = TABLE_ROWS // 2
    ngrp = PEER_PICKS // PEER_GROUP
    grows = PEER_GROUP * TABLE_ROWS
    bufs = (buf0, buf1, buf2, buf3)
    ibs = (ib0, ib1)

    def idx_fetch(tile, j):
        return pltpu.make_async_copy(idx_hbm.at[pl.ds(tile * npt, npt)], ibs[j], sem_i.at[j])

    def rows_done(j):
        return pltpu.make_async_copy(tab_hbm.at[pl.ds(0, npt)], bufs[j], sem_g.at[j])

    def issue_rows(ib, buf, sem):
        c = ib[0] >> 31
        for i in range(npt):
            e = ib[i] + c
            pltpu.make_async_copy(tab_hbm.at[e], buf.at[i], sem).start(priority=i % 2)
            if i % PEER_CHAIN == PEER_CHAIN - 1:
                c = e >> 31

    @pl.when(s == 0)
    def _():
        for j in range(2):
            first = idx_fetch(j, j)
            first.start()
            first.wait()

            def body(i, carry):
                pltpu.make_async_copy(tab_hbm.at[ibs[j][i]], bufs[j].at[i], sem_g.at[j]).start()
                return carry

            lax.fori_loop(0, npt, body, 0)
        idx_fetch(2, 0).start()

    row8 = lax.broadcasted_iota(jnp.int32, (half, TABLE_COLS), 0)
    col8 = lax.broadcasted_iota(jnp.int32, (half, TABLE_COLS), 1) % TABLE_ROWS
    mask_v = col8 == row8 + half
    lane = lax.broadcasted_iota(jnp.int32, (PEER_PICKS, HEAD_PAD), 1)
    gsum = gsum_ref[...]

    def evaluate(cur, tok0):
        def ubody(it, at):
            for u in range(PEER_UNROLL):
                t = it * PEER_UNROLL + u
                hrow = hs_ref[tok0 + t]
                h16 = jnp.concatenate([hrow, jnp.zeros_like(hrow)], axis=0).astype(BF16)
                tw = cur[pl.ds(t * PEER_PICKS, PEER_PICKS)]
                prod = (tw * h16[None]).reshape(TABLE_COLS, HEAD_PAD)
                parts = [_dot(gsum, prod[g * grows:(g + 1) * grows]) for g in range(ngrp)]
                z = jnp.sum(jnp.concatenate(parts, axis=0), axis=1, keepdims=True)
                at = jnp.where(lane == t, z, at)
            return at

        at = jnp.zeros((PEER_PICKS, HEAD_PAD), F32)
        for it in range(tt // PEER_UNROLL):
            at = ubody(it, at)
        a = at.T[:tt]
        w = jax.nn.gelu(a) * g_ref[tok0:tok0 + tt, :]
        wr_ref[...] = _dot(w.astype(BF16), ev_ref[...])

        def vbody(it, carry):
            for u in range(PEER_UNROLL):
                t = it * PEER_UNROLL + u
                wrow = jnp.broadcast_to(wr_ref[pl.ds(t, 1), :], (half, TABLE_COLS))
                wexp = jnp.where(mask_v, wrow, 0.0).astype(BF16)
                wb = cur[pl.ds(t * PEER_PICKS, PEER_PICKS)].reshape(TABLE_COLS, HEAD_PAD)
                y_ref[tok0 + t] = _dot(wexp, wb)
            return carry

        for it in range(tt // PEER_UNROLL):
            vbody(it, 0)

    for p in range(PEER_PHASES):
        k = PEER_PHASES * s + p
        idx_fetch(k + 2, p % 2).wait()
        idx_fetch(k + 3, (p + 1) % 2).start()
        rows_done(p).wait()
        issue_rows(ibs[p % 2], bufs[(p + 2) % PEER_PHASES], sem_g.at[(p + 2) % PEER_PHASES])
        evaluate(bufs[p], p * tt)

    @pl.when(s == last)
    def _():
        rows_done(0).wait()
        rows_done(1).wait()
        idx_fetch(0, 0).wait()


def _peer_call(idx_flat, table, hs3, gates, T):
    tt = PEER_TILE
    npt = tt * PEER_PICKS
    assert tt % PEER_UNROLL == 0 and PEER_PICKS % PEER_GROUP == 0 and tt <= HEAD_PAD and PEER_PHASES == 4
    gsum, ev = _peer_consts()
    short = max(0, (T + 3 * tt) * PEER_PICKS - idx_flat.shape[0])
    idx_pad = jnp.pad(idx_flat, (0, short)) if short else idx_flat
    step = PEER_PHASES * tt
    full = lambda a: pl.BlockSpec(a.shape, lambda s: (0,) * a.ndim)
    rows = pltpu.VMEM((npt, TABLE_ROWS, HEAD_PAD), BF16)
    return pl.pallas_call(
        _peer_kernel,
        grid=(T // step,),
        in_specs=[pl.BlockSpec(memory_space=pl.ANY),
                  pl.BlockSpec(memory_space=pl.ANY),
                  pl.BlockSpec((step, 8, HEAD_PAD), lambda s: (s, 0, 0)),
                  pl.BlockSpec((step, PEER_PICKS), lambda s: (s, 0)),
                  full(gsum), full(ev)],
        out_specs=pl.BlockSpec((step, 8, HEAD_PAD), lambda s: (s, 0, 0)),
        out_shape=jax.ShapeDtypeStruct((T, 8, HEAD_PAD), F32),
        scratch_shapes=[rows, rows, rows, rows,
                        pltpu.SMEM((npt,), jnp.int32),
                        pltpu.SMEM((npt,), jnp.int32),
                        pltpu.VMEM((tt, TABLE_COLS), F32),
                        pltpu.SemaphoreType.DMA((PEER_PHASES,)),
                        pltpu.SemaphoreType.DMA((2,))],
        compiler_params=_params("arbitrary"),
        name="peer",
    )(idx_pad, table, hs3, gates, gsum, ev)


SC_LANES = 16
SC_GROUP = 16
SC_UNROLL = 2
_GELU_C = 0.7978845608028654


def _peer_sc_kernel(t1, idx_hbm, tab_hbm, h_hbm, g_hbm, y_hbm,
                    idx_a, idx_b, g_a, g_b, h_a, h_b, o_v, rows_v, sem, sem_in):
    ncores = lax.axis_size("c")
    wid = lax.axis_index("s") * ncores + lax.axis_index("c")
    per = y_hbm.shape[0] // (ncores * lax.axis_size("s"))
    base = t1 + wid * per
    nchunk = D_MODEL // SC_LANES
    ngrp = PEER_PICKS // SC_GROUP
    lane = lax.iota(jnp.int32, SC_LANES)
    sets = ((idx_a, g_a, h_a), (idx_b, g_b, h_b))

    def fetch(t, k):
        return (pltpu.make_async_copy(idx_hbm.at[t], sets[k][0], sem_in.at[k]),
                pltpu.make_async_copy(g_hbm.at[t], sets[k][1], sem_in.at[k]),
                pltpu.make_async_copy(h_hbm.at[t], sets[k][2], sem_in.at[k]))

    def gather(k, g, slot):
        return pltpu.make_async_copy(tab_hbm.at[sets[k][0].at[pl.ds(g * SC_GROUP, SC_GROUP)]], rows_v.at[slot],
                                     sem.at[slot])

    def process(t, k, t_next):
        g_v, h_v = sets[k][1], sets[k][2]

        def zero(c, carry):
            o_v[pl.ds(c * SC_LANES, SC_LANES)] = jnp.zeros((SC_LANES,), F32)
            return carry

        lax.fori_loop(0, nchunk, zero, 0)

        for g in range(ngrp):
            slot = g % 2
            if g + 1 < ngrp:
                gather(k, g + 1, 1 - slot).start()
            else:
                for c in fetch(t_next, 1 - k):
                    c.wait()
                gather(1 - k, 0, 0).start()
            gather(k, g, slot).wait()

            def dot_body(c, accs):
                hv = h_v[pl.ds(c * SC_LANES, SC_LANES)]
                out = []
                for p in range(SC_GROUP):
                    w = rows_v[slot, p, pl.ds(c * SC_LANES, SC_LANES)]
                    u = lax.bitcast_convert_type(w << 16, F32)
                    out.append(accs[p] + u * hv)
                return tuple(out)

            accs = plsc.parallel_loop(0, nchunk, unroll=SC_UNROLL,
                                      carry=tuple(jnp.zeros((SC_LANES,), F32) for _ in range(SC_GROUP)))(dot_body)
            a = jnp.zeros((SC_LANES,), F32)
            for p in range(SC_GROUP):
                a = jnp.where(lane == p, jnp.sum(accs[p]), a)
            z = _GELU_C * (a + 0.044715 * a * a * a)
            th = 1.0 - 2.0 / (jnp.exp(2.0 * z) + 1.0)
            wv = 0.5 * a * (1.0 + th) * g_v[pl.ds(g * SC_GROUP, SC_GROUP)]
            ws = [jnp.sum(jnp.where(lane == p, wv, 0.0)) for p in range(SC_GROUP)]

            def ax_body(c):
                o = o_v[pl.ds(c * SC_LANES, SC_LANES)]
                for p in range(SC_GROUP):
                    w = rows_v[slot, p, pl.ds(c * SC_LANES, SC_LANES)]
                    v = lax.bitcast_convert_type(w & jnp.int32(-65536), F32)
                    o = o + ws[p] * v
                o_v[pl.ds(c * SC_LANES, SC_LANES)] = o

            plsc.parallel_loop(0, nchunk, unroll=SC_UNROLL)(ax_body)

        pltpu.sync_copy(o_v, y_hbm.at[t - t1])

    for c in fetch(base, 0):
        c.start()
    for c in fetch(base, 0):
        c.wait()
    gather(0, 0, 0).start()

    def pair(j, carry):
        t0 = base + 2 * j
        t2 = jnp.minimum(t0 + 2, base + per - 1)
        for c in fetch(t0 + 1, 1):
            c.start()
        process(t0, 0, t0 + 1)
        for c in fetch(t2, 0):
            c.start()
        process(t0 + 1, 1, t2)
        return carry

    lax.fori_loop(0, per // 2, pair, 0)
    gather(0, 0, 0).wait()


def _peer_sc_call(idx2, table_i32, h2, gates, t1):
    ts = h2.shape[0] - t1
    mesh = plsc.VectorSubcoreMesh(core_axis_name="c", subcore_axis_name="s")
    run = pl.kernel(
        functools.partial(_peer_sc_kernel, t1),
        out_type=jax.ShapeDtypeStruct((ts, D_MODEL), F32),
        mesh=mesh,
        scratch_types=[pltpu.VMEM((PEER_PICKS,), jnp.int32), pltpu.VMEM((PEER_PICKS,), jnp.int32),
                       pltpu.VMEM((PEER_PICKS,), F32), pltpu.VMEM((PEER_PICKS,), F32),
                       pltpu.VMEM((D_MODEL,), F32), pltpu.VMEM((D_MODEL,), F32),
                       pltpu.VMEM((D_MODEL,), F32),
                       pltpu.VMEM((2, SC_GROUP, D_MODEL), jnp.int32),
                       pltpu.SemaphoreType.DMA((2,)),
                       pltpu.SemaphoreType.DMA((2,))],
        compiler_params=pltpu.CompilerParams(needs_layout_passes=False),
        name="peer_sc",
    )
    return run(idx2, table_i32, h2, gates)


def _final_kernel(n1, x1_ref, ya_ref, yb_ref, mod_ref, g_ref, o_ref):
    gi = pl.program_id(0) * pl.num_programs(1) + pl.program_id(1)
    y = jnp.where(gi < n1, ya_ref[...], yb_ref[...])
    gt2 = mod_ref[0, 5:6, :]
    o_ref[0] = x1_ref[0] + gt2 * (_rms(y) * g_ref[...])


def _final_call(x1, ya, yb, mod, g_post):
    B, S, _ = x1.shape
    tr = min(ROW_TILE, S)
    per = S // tr
    assert ya.shape[0] % tr == 0 and yb.shape[0] % tr == 0
    n1 = ya.shape[0] // tr
    row = pl.BlockSpec((1, tr, D_MODEL), lambda b, i: (b, i, 0))
    first = pl.BlockSpec((tr, D_MODEL), lambda b, i: (jnp.minimum(b * per + i, n1 - 1), 0))
    rest = pl.BlockSpec((tr, D_MODEL), lambda b, i: (jnp.maximum(b * per + i - n1, 0), 0))
    return pl.pallas_call(
        functools.partial(_final_kernel, n1),
        grid=(B, per),
        in_specs=[row, first, rest, pl.BlockSpec((1, N_MOD, D_MODEL), lambda b, i: (b, 0, 0)),
                  pl.BlockSpec((1, D_MODEL), lambda b, i: (0, 0))],
        out_specs=row,
        out_shape=jax.ShapeDtypeStruct((B, S, D_MODEL), F32),
        compiler_params=_params("arbitrary", "arbitrary"),
        name="final",
    )(x1, ya, yb, mod, g_post)


def _pad_heads_cols(w, nh, scale=1.0):
    k = w.shape[0]
    w = (w * scale).reshape(k, nh, HEAD_DIM)
    return jnp.pad(w, ((0, 0), (0, 0), (0, HEAD_PAD - HEAD_DIM))).reshape(k, nh * HEAD_PAD)


def _pad_heads_rows(w, nh):
    n = w.shape[1]
    w = w.reshape(nh, HEAD_DIM, n)
    return jnp.pad(w, ((0, 0), (0, HEAD_PAD - HEAD_DIM), (0, 0))).reshape(nh * HEAD_PAD, n)


def _layer(x, c8, w_ada, b_ada, g_pre_mix, g_post_mix, g_pre_ffn, g_post_ffn,
           w_in, b_fgate, swa_sinks, w_out, w_query, sub_keys, w_u, w_v):
    B, S, D = x.shape
    T = B * S
    scale = HEAD_DIM ** -0.5
    mod = _ada_call(c8, w_ada, b_ada)[:B].reshape(B, N_MOD, D)

    o = 0
    parts = []
    for nh, sc in ((FOX_HEADS, scale), (FOX_HEADS, 1.0), (FOX_HEADS, 1.0)):
        parts.append(_pad_heads_cols(w_in[:, o:o + nh * HEAD_DIM], nh, sc))
        o += nh * HEAD_DIM
    parts.append(jnp.pad(w_in[:, o:o + FOX_HEADS], ((0, 0), (0, HEAD_PAD - FOX_HEADS))))
    o += FOX_HEADS
    for nh, sc in ((SWA_HEADS, scale), (SWA_KV_HEADS, 1.0), (SWA_KV_HEADS, 1.0)):
        parts.append(_pad_heads_cols(w_in[:, o:o + nh * HEAD_DIM], nh, sc))
        o += nh * HEAD_DIM
    w_all = jnp.concatenate(parts, axis=1).astype(BF16)
    bf_pad = jnp.pad(b_fgate, (0, HEAD_PAD - FOX_HEADS)).reshape(1, HEAD_PAD)

    qt, kp, vt, sq, sk, sv = _inproj_call(x, mod, g_pre_mix.reshape(1, D), w_all, bf_pad)
    fo = _fox_call(qt, kp, vt)
    so = _swa_call(swa_sinks, sq, sk, sv)

    nf = FOX_HEADS * HEAD_DIM
    wof = _pad_heads_rows(w_out[:nf], FOX_HEADS).astype(BF16)
    wos = _pad_heads_rows(w_out[nf:], SWA_HEADS).astype(BF16)
    keys = sub_keys.reshape(2 * PEER_HEADS, N_KEYS, PEER_HALF).astype(BF16)
    x1, h2, idx, gates = _route_call(fo, so, x, mod, g_post_mix.reshape(1, D), g_pre_ffn.reshape(1, D),
                                     wof, wos, w_query.astype(BF16), keys)

    ub, vb = w_u.astype(BF16), w_v.astype(BF16)
    table = jnp.concatenate([ub.reshape(N_EXPERTS, 8, HEAD_PAD), vb.reshape(N_EXPERTS, 8, HEAD_PAD)], axis=1)
    h2f = h2.reshape(T, D)
    tile = min(ROW_TILE, S)
    ts = (int(T * SC_SHARE) // tile) * tile
    t1 = T - ts
    y_tc = _peer_call(idx.reshape(T * PEER_PICKS), table, h2f.reshape(T, 8, HEAD_PAD), gates, t1).reshape(t1, D)
    if not ts:
        return _final_call(x1, y_tc, y_tc, mod, g_post_ffn.reshape(1, D))
    u16 = lax.bitcast_convert_type(ub, jnp.uint16).astype(jnp.uint32)
    v16 = lax.bitcast_convert_type(vb, jnp.uint16).astype(jnp.uint32)
    table_i32 = lax.bitcast_convert_type((v16 << 16) | u16, jnp.int32)
    y_sc = _peer_sc_call(idx, table_i32, h2f, gates, t1)
    return _final_call(x1, y_tc, y_sc, mod, g_post_ffn.reshape(1, D))


def kernel(x, c, w_ada, b_ada, g_pre_mix, g_post_mix, g_pre_ffn, g_post_ffn, w_in, b_fgate, swa_sinks, w_out,
           w_query, sub_keys, w_u, w_v):
    B = x.shape[0]
    c8 = jnp.pad(c, ((0, 8 - B), (0, 0)))
    for l in range(w_ada.shape[0]):
        x = _layer(x, c8, w_ada[l], b_ada[l], g_pre_mix[l], g_post_mix[l], g_pre_ffn[l], g_post_ffn[l],
                   w_in[l], b_fgate[l], swa_sinks[l], w_out[l], w_query[l], sub_keys[l], w_u[l], w_v[l])
    return x
```

```python
import functools

import numpy as np
import jax
import jax.numpy as jnp
from jax import lax
from jax.experimental import pallas as pl
from jax.experimental.pallas import tpu as pltpu
from jax.experimental.pallas import tpu_sc as plsc

F32 = jnp.float32
BF16 = jnp.bfloat16

D_MODEL = 1024
HEAD_DIM = 64
HEAD_PAD = 128
FOX_HEADS = 8
SWA_HEADS = 8
SWA_KV_HEADS = 2
SWA_GROUP = SWA_HEADS // SWA_KV_HEADS
WINDOW = 128
PEER_HEADS = 8
PEER_HALF = 128
N_KEYS = 128
N_EXPERTS = N_KEYS * N_KEYS
PEER_TOPK = 16
PEER_PICKS = PEER_HEADS * PEER_TOPK
N_MOD = 6
RMS_EPS = 1e-6
NEG_INF = -1e30

_F_LANE = HEAD_DIM
_ONE_LANE = HEAD_DIM

ROW_TILE = 1024
FOX_TILE = 1024
SWA_TILE = 512
ROUTE_TILE = 256
PEER_TILE = 8
PEER_PHASES = 4
PEER_UNROLL = 8
PEER_CHAIN = 8
PEER_GROUP = 16
SC_SHARE = 0.47
TABLE_ROWS = 16
TABLE_COLS = PEER_PICKS * TABLE_ROWS

_VMEM_LIMIT = 56 * 1024 * 1024


def _dot(a, b):
    return jnp.dot(a, b, preferred_element_type=F32)


def _dot_nt(a, b):
    return lax.dot_general(a, b, (((1,), (1,)), ((), ())), preferred_element_type=F32)


def _split3(x):
    hi = x.astype(BF16)
    r = x - hi.astype(F32)
    mid = r.astype(BF16)
    lo = (r - mid.astype(F32)).astype(BF16)
    return hi, mid, lo


def _rms(x):
    return x * lax.rsqrt(jnp.mean(x * x, axis=-1, keepdims=True) + RMS_EPS)


def _params(*sem):
    return pltpu.CompilerParams(dimension_semantics=sem, vmem_limit_bytes=_VMEM_LIMIT)


def _ada_kernel(c_ref, w_ref, b_ref, o_ref):
    c = c_ref[...]
    s = (c * jax.nn.sigmoid(c)).astype(BF16)
    o_ref[...] = _dot(s, w_ref[...].astype(BF16)) + b_ref[...]


def _ada_call(c8, w_ada, b_ada):
    n = w_ada.shape[1]
    tn = 1536
    return pl.pallas_call(
        _ada_kernel,
        grid=(n // tn,),
        in_specs=[pl.BlockSpec((8, D_MODEL), lambda j: (0, 0)),
                  pl.BlockSpec((D_MODEL, tn), lambda j: (0, j)),
                  pl.BlockSpec((1, tn), lambda j: (0, j))],
        out_specs=pl.BlockSpec((8, tn), lambda j: (0, j)),
        out_shape=jax.ShapeDtypeStruct((8, n), F32),
        compiler_params=_params("arbitrary"),
        name="ada",
    )(c8, w_ada, b_ada.reshape(1, n))


_NQ = FOX_HEADS * HEAD_PAD
_NS = SWA_HEADS * HEAD_PAD
_NKV = SWA_KV_HEADS * HEAD_PAD
_IN_COLS = 3 * _NQ + HEAD_PAD + _NS + 2 * _NKV


def _inproj_kernel(x_ref, mod_ref, g_ref, w_ref, bf_ref, tri_ref, pq_ref, pk_ref, cst_ref,
                   qt_ref, kp_ref, vt_ref, sq_ref, sk_ref, sv_ref, carry_ref):
    i = pl.program_id(1)

    @pl.when(i == 0)
    def _():
        carry_ref[...] = jnp.zeros_like(carry_ref)

    x = x_ref[0]
    sh1 = mod_ref[0, 0:1, :]
    sc1 = mod_ref[0, 1:2, :]
    h = _rms(x) * g_ref[...] * (1.0 + sc1) + sh1
    proj = _dot(h.astype(BF16), w_ref[...])

    z = proj[:, 3 * _NQ:3 * _NQ + HEAD_PAD] + bf_ref[...]
    ls = jnp.minimum(z, 0.0) - jnp.log(1.0 + jnp.exp(-jnp.abs(z)))
    tri = tri_ref[...]
    hi, mid, lo = _split3(ls)
    fcum = _dot(tri, hi) + _dot(tri, mid) + _dot(tri, lo) + carry_ref[...]
    carry_ref[...] = fcum[fcum.shape[0] - 1:, :]

    fh, fm, fl = _split3(fcum)
    eq = _dot(fh, pq_ref[0]) + _dot(fm, pq_ref[1]) + _dot(fl, pq_ref[2]) + cst_ref[0:1, :]
    ek = _dot(fh, pk_ref[0]) + _dot(fm, pk_ref[1]) + _dot(fl, pk_ref[2]) + cst_ref[1:2, :]
    qt_ref[0, 0] = (proj[:, 0:_NQ] + eq).T.astype(BF16)
    kp_ref[0] = (proj[:, _NQ:2 * _NQ] + ek).astype(BF16)
    vt_ref[0, 0] = (proj[:, 2 * _NQ:3 * _NQ] + cst_ref[2:3, :]).T.astype(BF16)
    o = 3 * _NQ + HEAD_PAD
    sq_ref[0] = proj[:, o:o + _NS].astype(BF16)
    sk_ref[0] = proj[:, o + _NS:o + _NS + _NKV].astype(BF16)
    sv_ref[0] = proj[:, o + _NS + _NKV:o + _NS + 2 * _NKV].astype(BF16)


def _inproj_consts(tr):
    tri = np.tril(np.ones((tr, tr), np.float32))
    pq = np.zeros((3, HEAD_PAD, _NQ), np.float32)
    pk = np.zeros((3, HEAD_PAD, _NQ), np.float32)
    cst = np.zeros((8, _NQ), np.float32)
    for h in range(FOX_HEADS):
        b = h * HEAD_PAD + _F_LANE
        for j in range(3):
            pq[j, h, b + j] = 1.0
            pk[j, h, b + 3 + j] = -1.0
            cst[0, b + 3 + j] = 1.0
            cst[1, b + j] = 1.0
        cst[2, h * HEAD_PAD + _ONE_LANE] = 1.0
    return (jnp.asarray(tri, BF16), jnp.asarray(pq, BF16), jnp.asarray(pk, BF16), jnp.asarray(cst, F32))


def _inproj_call(x, mod, g_pre, w_all, bf_pad):
    B, S, _ = x.shape
    tr = min(ROW_TILE, S)
    n = S // tr
    tri, pq, pk, cst = _inproj_consts(tr)
    row = lambda w: pl.BlockSpec((1, tr, w), lambda b, i: (b, i, 0))
    slab = pl.BlockSpec((1, 1, _NQ, tr), lambda b, i: (b, i, 0, 0))
    full = lambda a: pl.BlockSpec(a.shape, lambda b, i: (0,) * a.ndim)
    outs = [jax.ShapeDtypeStruct((B, n, _NQ, tr), BF16), jax.ShapeDtypeStruct((B, S, _NQ), BF16),
            jax.ShapeDtypeStruct((B, n, _NQ, tr), BF16), jax.ShapeDtypeStruct((B, S, _NS), BF16),
            jax.ShapeDtypeStruct((B, S, _NKV), BF16), jax.ShapeDtypeStruct((B, S, _NKV), BF16)]
    return pl.pallas_call(
        _inproj_kernel,
        grid=(B, n),
        in_specs=[row(D_MODEL),
                  pl.BlockSpec((1, N_MOD, D_MODEL), lambda b, i: (b, 0, 0)),
                  full(g_pre), full(w_all), full(bf_pad), full(tri), full(pq), full(pk), full(cst)],
        out_specs=[slab, row(_NQ), slab, row(_NS), row(_NKV), row(_NKV)],
        out_shape=outs,
        scratch_shapes=[pltpu.VMEM((1, HEAD_PAD), F32)],
        compiler_params=_params("arbitrary", "arbitrary"),
        name="inproj",
    )(x, mod, g_pre, w_all, bf_pad, tri, pq, pk, cst)


def _fox_kernel(qt_ref, k_ref, vt_ref, o_ref, m_ref, acc_ref, sa_ref, sb_ref):
    i = pl.program_id(2)
    t = o_ref.shape[1]
    m_ref[...] = jnp.full_like(m_ref, NEG_INF)
    acc_ref[...] = jnp.zeros_like(acc_ref)
    bufs = (sa_ref, sb_ref)

    def scores(j, dst):
        off = pl.multiple_of(j * t, t)
        dst[...] = _dot(k_ref[0, pl.ds(off, t), :], qt_ref[0, 0])

    def absorb(j, src, masked):
        s = src[...]
        if masked:
            r = lax.broadcasted_iota(jnp.int32, s.shape, 0)
            c = lax.broadcasted_iota(jnp.int32, s.shape, 1)
            s = jnp.where(r <= c, s, NEG_INF)
        m_prev = m_ref[...]
        m_new = jnp.maximum(m_prev, jnp.max(s, axis=0, keepdims=True))
        p = jnp.exp(s - m_new)
        acc_ref[...] = jnp.exp(m_prev - m_new) * acc_ref[...] + _dot(vt_ref[0, j], p.astype(BF16))
        m_ref[...] = m_new

    scores(0, sa_ref)

    def pair(jj, carry):
        j = 2 * jj
        scores(j + 1, sb_ref)
        absorb(j, sa_ref, False)
        scores(j + 2, sa_ref)
        absorb(j + 1, sb_ref, False)
        return carry

    lax.fori_loop(0, i // 2, pair, 0)

    @pl.when(i % 2 == 1)
    def _():
        scores(i, sb_ref)
        absorb(i - 1, sa_ref, False)
        absorb(i, sb_ref, True)

    @pl.when(i % 2 == 0)
    def _():
        absorb(i, sa_ref, True)

    acc = acc_ref[...]
    o_ref[0] = (acc / acc[_ONE_LANE:_ONE_LANE + 1, :]).T.astype(BF16)


def _fox_call(qt, kp, vt):
    B, n, _, t = qt.shape
    S = n * t
    w = HEAD_PAD
    return pl.pallas_call(
        _fox_kernel,
        grid=(B, FOX_HEADS, n),
        in_specs=[pl.BlockSpec((1, 1, w, t), lambda b, h, i: (b, i, h, 0)),
                  pl.BlockSpec((1, S, w), lambda b, h, i: (b, 0, h)),
                  pl.BlockSpec((1, n, w, t), lambda b, h, i: (b, 0, h, 0))],
        out_specs=pl.BlockSpec((1, t, w), lambda b, h, i: (b, i, h)),
        out_shape=jax.ShapeDtypeStruct((B, S, _NQ), BF16),
        scratch_shapes=[pltpu.VMEM((1, t), F32), pltpu.VMEM((HEAD_PAD, t), F32),
                        pltpu.VMEM((t, t), F32), pltpu.VMEM((t, t), F32)],
        compiler_params=_params("arbitrary", "arbitrary", "arbitrary"),
        name="fox",
    )(qt, kp, vt)


def _swa_kernel(sink_ref, q_ref, kc_ref, kp_ref, vc_ref, vp_ref, o_ref):
    i = pl.program_id(1)
    nsub = q_ref.shape[1] // WINDOW
    r = lax.broadcasted_iota(jnp.int32, (WINDOW, 2 * WINDOW), 0)
    j = lax.broadcasted_iota(jnp.int32, (WINDOW, 2 * WINDOW), 1)
    dist = r + WINDOW - j
    valid = (dist >= 0) & (dist < WINDOW)
    distf = dist.astype(F32)
    for qb in range(nsub):
        rows = slice(qb * WINDOW, (qb + 1) * WINDOW)
        if qb == 0:
            ok = valid & ((j >= WINDOW) | (i > 0))
        else:
            ok = valid
        for g in range(SWA_KV_HEADS):
            lanes = slice(g * HEAD_PAD, (g + 1) * HEAD_PAD)
            if qb == 0:
                kprev, vprev = kp_ref[0, :, lanes], vp_ref[0, :, lanes]
            else:
                prev = slice((qb - 1) * WINDOW, qb * WINDOW)
                kprev, vprev = kc_ref[0, prev, lanes], vc_ref[0, prev, lanes]
            kk = jnp.concatenate([kprev, kc_ref[0, rows, lanes]], axis=0)
            vv = jnp.concatenate([vprev, vc_ref[0, rows, lanes]], axis=0)
            for u in range(SWA_GROUP):
                hq = g * SWA_GROUP + u
                slope = 2.0 ** (-8.0 * (hq + 1) / SWA_HEADS)
                q = q_ref[0, rows, hq * HEAD_PAD:(hq + 1) * HEAD_PAD]
                s = _dot_nt(q, kk) - slope * distf
                s = jnp.where(ok, s, NEG_INF)
                sink = sink_ref[hq]
                m = jnp.maximum(jnp.max(s, axis=1, keepdims=True), sink)
                p = jnp.exp(s - m)
                den = jnp.sum(p, axis=1, keepdims=True) + jnp.exp(sink - m)
                o = _dot(p.astype(BF16), vv) / den
                o_ref[0, rows, hq * HEAD_PAD:(hq + 1) * HEAD_PAD] = o.astype(BF16)


def _swa_call(sinks, sq, sk, sv):
    B, S, _ = sq.shape
    t = min(SWA_TILE, S)
    per = t // WINDOW
    cur = lambda w: pl.BlockSpec((1, t, w), lambda b, i: (b, i, 0))
    prv = lambda w: pl.BlockSpec((1, WINDOW, w), lambda b, i: (b, jnp.maximum(i * per - 1, 0), 0))
    return pl.pallas_call(
        _swa_kernel,
        grid=(B, S // t),
        in_specs=[pl.BlockSpec(memory_space=pltpu.SMEM),
                  cur(_NS), cur(_NKV), prv(_NKV), cur(_NKV), prv(_NKV)],
        out_specs=cur(_NS),
        out_shape=jax.ShapeDtypeStruct((B, S, _NS), BF16),
        compiler_params=_params("arbitrary", "arbitrary"),
        name="swa",
    )(sinks, sq, sk, sk, sv, sv)


_NO_ID = 1 << 20


def _topk_rows(s, k, val_ref, idx_ref, ids=None):
    if ids is None:
        ids = lax.broadcasted_iota(jnp.int32, s.shape, 0)
    for r in range(k):
        m = jnp.max(s, axis=0, keepdims=True)
        i = jnp.min(jnp.where(s == m, ids, _NO_ID), axis=0, keepdims=True)
        val_ref[pl.ds(r, 1), :] = m
        idx_ref[pl.ds(r, 1), :] = i
        if r + 1 < k:
            s = jnp.where(ids == i, -jnp.inf, s)


def _cand_counts():
    return [PEER_TOPK // (a + 1) for a in range(PEER_TOPK)]


_CAND_ROWS = 56


def _route_kernel(fo_ref, so_ref, x_ref, mod_ref, gpost_ref, gpre_ref, wof_ref, wos_ref, wq_ref, keys_ref, cid_ref,
                  x1_ref, h2_ref, idx_ref, gate_ref, qs_ref, sv_ref, si_ref, et_ref, gt_ref, cand_ref):
    x = x_ref[0]
    gt1 = mod_ref[0, 2:3, :]
    sh2 = mod_ref[0, 3:4, :]
    sc2 = mod_ref[0, 4:5, :]
    y = _dot(fo_ref[0], wof_ref[...]) + _dot(so_ref[0], wos_ref[...])
    x1 = x + gt1 * (_rms(y) * gpost_ref[...])
    x1_ref[0] = x1
    h2 = _rms(x1) * gpre_ref[...] * (1.0 + sc2) + sh2
    h2_ref[0] = h2
    qp = _dot(h2.astype(BF16), wq_ref[...])
    nhp = 2 * PEER_HEADS
    for hp in range(nhp):
        qs_ref[hp] = qp[:, hp * PEER_HALF:(hp + 1) * PEER_HALF].astype(BF16)

    def half(q, carry):
        for u in range(4):
            hp = 4 * q + u
            sc = _dot_nt(keys_ref[hp], qs_ref[hp])
            _topk_rows(sc, PEER_TOPK, sv_ref.at[hp], si_ref.at[hp])
        return carry

    lax.fori_loop(0, nhp // 4, half, 0)

    counts = _cand_counts()
    used = sum(counts)
    par = cand_ref.shape[0]
    for u in range(par):
        cand_ref[u, pl.ds(used, _CAND_ROWS - used), :] = jnp.full((_CAND_ROWS - used, cand_ref.shape[2]), -jnp.inf, F32)
    cid = cid_ref[...]

    def head(hh, carry):
        for u in range(par):
            h = par * hh + u
            v0, v1 = sv_ref[2 * h], sv_ref[2 * h + 1]
            i0, i1 = si_ref[2 * h], si_ref[2 * h + 1]
            off = 0
            for a, nb in enumerate(counts):
                cand_ref[u, pl.ds(off, nb), :] = v0[a:a + 1, :] + v1[0:nb, :]
                off += nb
            _topk_rows(cand_ref[u], PEER_TOPK, gt_ref.at[h], et_ref.at[h], ids=cid)
            cv, ci = gt_ref[h], et_ref[h]
            ca, cb = ci >> 4, ci & (PEER_TOPK - 1)
            e1 = jnp.zeros_like(ci)
            e2 = jnp.zeros_like(ci)
            for a in range(PEER_TOPK):
                e1 = jnp.where(ca == a, i0[a:a + 1, :], e1)
                e2 = jnp.where(cb == a, i1[a:a + 1, :], e2)
            et_ref[h] = e1 * N_KEYS + e2
            ex = jnp.exp(cv - cv[0:1, :])
            gt_ref[h] = ex / jnp.sum(ex, axis=0, keepdims=True)
        return carry

    lax.fori_loop(0, PEER_HEADS // par, head, 0)
    tt = et_ref.shape[2]
    et = lax.bitcast_convert_type(et_ref[...].reshape(PEER_PICKS, tt), F32)
    idx_ref[...] = lax.bitcast_convert_type(et.T, jnp.int32)
    gate_ref[...] = gt_ref[...].reshape(PEER_PICKS, tt).T


def _route_call(fo, so, x, mod, g_post, g_pre, wof, wos, wq, keys):
    B, S, _ = x.shape
    tt = min(ROUTE_TILE, S)
    per = S // tt
    row = lambda w: pl.BlockSpec((1, tt, w), lambda b, i: (b, i, 0))
    full = lambda a: pl.BlockSpec(a.shape, lambda b, i: (0,) * a.ndim)
    tok = pl.BlockSpec((tt, PEER_PICKS), lambda b, i: (b * per + i, 0))
    flat = [a * PEER_TOPK + b for a, nb in enumerate(_cand_counts()) for b in range(nb)]
    flat += [_NO_ID] * (_CAND_ROWS - len(flat))
    cid = jnp.asarray(np.broadcast_to(np.asarray(flat, np.int32)[:, None], (_CAND_ROWS, tt)))
    return pl.pallas_call(
        _route_kernel,
        grid=(B, per),
        in_specs=[row(_NQ), row(_NS), row(D_MODEL),
                  pl.BlockSpec((1, N_MOD, D_MODEL), lambda b, i: (b, 0, 0)),
                  full(g_post), full(g_pre), full(wof), full(wos), full(wq), full(keys), full(cid)],
        out_specs=[row(D_MODEL), row(D_MODEL), tok, tok],
        out_shape=[jax.ShapeDtypeStruct((B, S, D_MODEL), F32),
                   jax.ShapeDtypeStruct((B, S, D_MODEL), F32),
                   jax.ShapeDtypeStruct((B * S, PEER_PICKS), jnp.int32),
                   jax.ShapeDtypeStruct((B * S, PEER_PICKS), F32)],
        scratch_shapes=[pltpu.VMEM((2 * PEER_HEADS, tt, PEER_HALF), BF16),
                        pltpu.VMEM((2 * PEER_HEADS, PEER_TOPK, tt), F32),
                        pltpu.VMEM((2 * PEER_HEADS, PEER_TOPK, tt), jnp.int32),
                        pltpu.VMEM((PEER_HEADS, PEER_TOPK, tt), jnp.int32),
                        pltpu.VMEM((PEER_HEADS, PEER_TOPK, tt), F32),
                        pltpu.VMEM((2, _CAND_ROWS, tt), F32)],
        compiler_params=_params("arbitrary", "arbitrary"),
        name="route",
    )(fo, so, x, mod, g_post, g_pre, wof, wos, wq, keys, cid)


def _peer_consts():
    half = TABLE_ROWS // 2
    gsum = np.zeros((PEER_GROUP, PEER_GROUP * TABLE_ROWS), np.float32)
    ev = np.zeros((PEER_PICKS, TABLE_COLS), np.float32)
    for j in range(PEER_GROUP):
        gsum[j, j * TABLE_ROWS:j * TABLE_ROWS + half] = 1.0
    for k in range(PEER_PICKS):
        ev[k, k * TABLE_ROWS + half:(k + 1) * TABLE_ROWS] = 1.0
    return jnp.asarray(gsum, BF16), jnp.asarray(ev, BF16)


def _peer_kernel(idx_hbm, tab_hbm, hs_ref, g_ref, gsum_ref, ev_ref, y_ref,
                 buf0, buf1, buf2, buf3, ib0, ib1, wr_ref, sem_g, sem_i):
    s = pl.program_id(0)
    last = pl.num_programs(0) - 1
    tt = PEER_TILE
    npt = tt * PEER_PICKS
    half = TABLE_ROWS // 2
    ngrp = PEER_PICKS // PEER_GROUP
    grows = PEER_GROUP * TABLE_ROWS
    bufs = (buf0, buf1, buf2, buf3)
    ibs = (ib0, ib1)

    def idx_fetch(tile, j):
        return pltpu.make_async_copy(idx_hbm.at[pl.ds(tile * npt, npt)], ibs[j], sem_i.at[j])

    def rows_done(j):
        return pltpu.make_async_copy(tab_hbm.at[pl.ds(0, npt)], bufs[j], sem_g.at[j])

    def issue_rows(ib, buf, sem):
        c = ib[0] >> 31
        for i in range(npt):
            e = ib[i] + c
            pltpu.make_async_copy(tab_hbm.at[e], buf.at[i], sem).start(priority=i % 2)
            if i % PEER_CHAIN == PEER_CHAIN - 1:
                c = e >> 31

    @pl.when(s == 0)
    def _():
        for j in range(2):
            first = idx_fetch(j, j)
            first.start()
            first.wait()

            def body(i, carry):
                pltpu.make_async_copy(tab_hbm.at[ibs[j][i]], bufs[j].at[i], sem_g.at[j]).start()
                return carry

            lax.fori_loop(0, npt, body, 0)
        idx_fetch(2, 0).start()

    row8 = lax.broadcasted_iota(jnp.int32, (half, TABLE_COLS), 0)
    col8 = lax.broadcasted_iota(jnp.int32, (half, TABLE_COLS), 1) % TABLE_ROWS
    mask_v = col8 == row8 + half
    lane = lax.broadcasted_iota(jnp.int32, (PEER_PICKS, HEAD_PAD), 1)
    gsum = gsum_ref[...]

    def evaluate(cur, tok0):
        def ubody(it, at):
            for u in range(PEER_UNROLL):
                t = it * PEER_UNROLL + u
                hrow = hs_ref[tok0 + t]
                h16 = jnp.concatenate([hrow, jnp.zeros_like(hrow)], axis=0).astype(BF16)
                tw = cur[pl.ds(t * PEER_PICKS, PEER_PICKS)]
                prod = (tw * h16[None]).reshape(TABLE_COLS, HEAD_PAD)
                parts = [_dot(gsum, prod[g * grows:(g + 1) * grows]) for g in range(ngrp)]
                z = jnp.sum(jnp.concatenate(parts, axis=0), axis=1, keepdims=True)
                at = jnp.where(lane == t, z, at)
            return at

        at = jnp.zeros((PEER_PICKS, HEAD_PAD), F32)
        for it in range(tt // PEER_UNROLL):
            at = ubody(it, at)
        a = at.T[:tt]
        w = jax.nn.gelu(a) * g_ref[tok0:tok0 + tt, :]
        wr_ref[...] = _dot(w.astype(BF16), ev_ref[...])

        def vbody(it, carry):
            for u in range(PEER_UNROLL):
                t = it * PEER_UNROLL + u
                wrow = jnp.broadcast_to(wr_ref[pl.ds(t, 1), :], (half, TABLE_COLS))
                wexp = jnp.where(mask_v, wrow, 0.0).astype(BF16)
                wb = cur[pl.ds(t * PEER_PICKS, PEER_PICKS)].reshape(TABLE_COLS, HEAD_PAD)
                y_ref[tok0 + t] = _dot(wexp, wb)
            return carry

        for it in range(tt // PEER_UNROLL):
            vbody(it, 0)

    for p in range(PEER_PHASES):
        k = PEER_PHASES * s + p
        idx_fetch(k + 2, p % 2).wait()
        idx_fetch(k + 3, (p + 1) % 2).start()
        rows_done(p).wait()
        issue_rows(ibs[p % 2], bufs[(p + 2) % PEER_PHASES], sem_g.at[(p + 2) % PEER_PHASES])
        evaluate(bufs[p], p * tt)

    @pl.when(s == last)
    def _():
        rows_done(0).wait()
        rows_done(1).wait()
        idx_fetch(0, 0).wait()


def _peer_call(idx_flat, table, hs3, gates, T):
    tt = PEER_TILE
    npt = tt * PEER_PICKS
    assert tt % PEER_UNROLL == 0 and PEER_PICKS % PEER_GROUP == 0 and tt <= HEAD_PAD and PEER_PHASES == 4
    gsum, ev = _peer_consts()
    short = max(0, (T + 3 * tt) * PEER_PICKS - idx_flat.shape[0])
    idx_pad = jnp.pad(idx_flat, (0, short)) if short else idx_flat
    step = PEER_PHASES * tt
    full = lambda a: pl.BlockSpec(a.shape, lambda s: (0,) * a.ndim)
    rows = pltpu.VMEM((npt, TABLE_ROWS, HEAD_PAD), BF16)
    return pl.pallas_call(
        _peer_kernel,
        grid=(T // step,),
        in_specs=[pl.BlockSpec(memory_space=pl.ANY),
                  pl.BlockSpec(memory_space=pl.ANY),
                  pl.BlockSpec((step, 8, HEAD_PAD), lambda s: (s, 0, 0)),
                  pl.BlockSpec((step, PEER_PICKS), lambda s: (s, 0)),
                  full(gsum), full(ev)],
        out_specs=pl.BlockSpec((step, 8, HEAD_PAD), lambda s: (s, 0, 0)),
        out_shape=jax.ShapeDtypeStruct((T, 8, HEAD_PAD), F32),
        scratch_shapes=[rows, rows, rows, rows,
                        pltpu.SMEM((npt,), jnp.int32),
                        pltpu.SMEM((npt,), jnp.int32),
                        pltpu.VMEM((tt, TABLE_COLS), F32),
                        pltpu.SemaphoreType.DMA((PEER_PHASES,)),
                        pltpu.SemaphoreType.DMA((2,))],
        compiler_params=_params("arbitrary"),
        name="peer",
    )(idx_pad, table, hs3, gates, gsum, ev)


SC_LANES = 16
SC_GROUP = 16
SC_UNROLL = 2
_GELU_C = 0.7978845608028654


def _peer_sc_kernel(t1, idx_hbm, tab_hbm, h_hbm, g_hbm, y_hbm,
                    idx_a, idx_b, g_a, g_b, h_a, h_b, o_v, rows_v, sem, sem_in):
    ncores = lax.axis_size("c")
    wid = lax.axis_index("s") * ncores + lax.axis_index("c")
    per = y_hbm.shape[0] // (ncores * lax.axis_size("s"))
    base = t1 + wid * per
    nchunk = D_MODEL // SC_LANES
    ngrp = PEER_PICKS // SC_GROUP
    lane = lax.iota(jnp.int32, SC_LANES)
    sets = ((idx_a, g_a, h_a), (idx_b, g_b, h_b))

    def fetch(t, k):
        return (pltpu.make_async_copy(idx_hbm.at[t], sets[k][0], sem_in.at[k]),
                pltpu.make_async_copy(g_hbm.at[t], sets[k][1], sem_in.at[k]),
                pltpu.make_async_copy(h_hbm.at[t], sets[k][2], sem_in.at[k]))

    def gather(k, g, slot):
        return pltpu.make_async_copy(tab_hbm.at[sets[k][0].at[pl.ds(g * SC_GROUP, SC_GROUP)]], rows_v.at[slot],
                                     sem.at[slot])

    def process(t, k, t_next):
        g_v, h_v = sets[k][1], sets[k][2]

        def zero(c, carry):
            o_v[pl.ds(c * SC_LANES, SC_LANES)] = jnp.zeros((SC_LANES,), F32)
            return carry

        lax.fori_loop(0, nchunk, zero, 0)

        for g in range(ngrp):
            slot = g % 2
            if g + 1 < ngrp:
                gather(k, g + 1, 1 - slot).start()
            else:
                for c in fetch(t_next, 1 - k):
                    c.wait()
                gather(1 - k, 0, 0).start()
            gather(k, g, slot).wait()

            def dot_body(c, accs):
                hv = h_v[pl.ds(c * SC_LANES, SC_LANES)]
                out = []
                for p in range(SC_GROUP):
                    w = rows_v[slot, p, pl.ds(c * SC_LANES, SC_LANES)]
                    u = lax.bitcast_convert_type(w << 16, F32)
                    out.append(accs[p] + u * hv)
                return tuple(out)

            accs = plsc.parallel_loop(0, nchunk, unroll=SC_UNROLL,
                                      carry=tuple(jnp.zeros((SC_LANES,), F32) for _ in range(SC_GROUP)))(dot_body)
            a = jnp.zeros((SC_LANES,), F32)
            for p in range(SC_GROUP):
                a = jnp.where(lane == p, jnp.sum(accs[p]), a)
            z = _GELU_C * (a + 0.044715 * a * a * a)
            th = 1.0 - 2.0 / (jnp.exp(2.0 * z) + 1.0)
            wv = 0.5 * a * (1.0 + th) * g_v[pl.ds(g * SC_GROUP, SC_GROUP)]
            ws = [jnp.sum(jnp.where(lane == p, wv, 0.0)) for p in range(SC_GROUP)]

            def ax_body(c):
                o = o_v[pl.ds(c * SC_LANES, SC_LANES)]
                for p in range(SC_GROUP):
                    w = rows_v[slot, p, pl.ds(c * SC_LANES, SC_LANES)]
                    v = lax.bitcast_convert_type(w & jnp.int32(-65536), F32)
                    o = o + ws[p] * v
                o_v[pl.ds(c * SC_LANES, SC_LANES)] = o

            plsc.parallel_loop(0, nchunk, unroll=SC_UNROLL)(ax_body)

        pltpu.sync_copy(o_v, y_hbm.at[t - t1])

    for c in fetch(base, 0):
        c.start()
    for c in fetch(base, 0):
        c.wait()
    gather(0, 0, 0).start()

    def pair(j, carry):
        t0 = base + 2 * j
        t2 = jnp.minimum(t0 + 2, base + per - 1)
        for c in fetch(t0 + 1, 1):
            c.start()
        process(t0, 0, t0 + 1)
        for c in fetch(t2, 0):
            c.start()
        process(t0 + 1, 1, t2)
        return carry

    lax.fori_loop(0, per // 2, pair, 0)
    gather(0, 0, 0).wait()


def _peer_sc_call(idx2, table_i32, h2, gates, t1):
    ts = h2.shape[0] - t1
    mesh = plsc.VectorSubcoreMesh(core_axis_name="c", subcore_axis_name="s")
    run = pl.kernel(
        functools.partial(_peer_sc_kernel, t1),
        out_type=jax.ShapeDtypeStruct((ts, D_MODEL), F32),
        mesh=mesh,
        scratch_types=[pltpu.VMEM((PEER_PICKS,), jnp.int32), pltpu.VMEM((PEER_PICKS,), jnp.int32),
                       pltpu.VMEM((PEER_PICKS,), F32), pltpu.VMEM((PEER_PICKS,), F32),
                       pltpu.VMEM((D_MODEL,), F32), pltpu.VMEM((D_MODEL,), F32),
                       pltpu.VMEM((D_MODEL,), F32),
                       pltpu.VMEM((2, SC_GROUP, D_MODEL), jnp.int32),
                       pltpu.SemaphoreType.DMA((2,)),
                       pltpu.SemaphoreType.DMA((2,))],
        compiler_params=pltpu.CompilerParams(needs_layout_passes=False),
        name="peer_sc",
    )
    return run(idx2, table_i32, h2, gates)


def _final_kernel(n1, x1_ref, ya_ref, yb_ref, mod_ref, g_ref, o_ref):
    gi = pl.program_id(0) * pl.num_programs(1) + pl.program_id(1)
    y = jnp.where(gi < n1, ya_ref[...], yb_ref[...])
    gt2 = mod_ref[0, 5:6, :]
    o_ref[0] = x1_ref[0] + gt2 * (_rms(y) * g_ref[...])


def _final_call(x1, ya, yb, mod, g_post):
    B, S, _ = x1.shape
    tr = min(ROW_TILE, S)
    per = S // tr
    assert ya.shape[0] % tr == 0 and yb.shape[0] % tr == 0
    n1 = ya.shape[0] // tr
    row = pl.BlockSpec((1, tr, D_MODEL), lambda b, i: (b, i, 0))
    first = pl.BlockSpec((tr, D_MODEL), lambda b, i: (jnp.minimum(b * per + i, n1 - 1), 0))
    rest = pl.BlockSpec((tr, D_MODEL), lambda b, i: (jnp.maximum(b * per + i - n1, 0), 0))
    return pl.pallas_call(
        functools.partial(_final_kernel, n1),
        grid=(B, per),
        in_specs=[row, first, rest, pl.BlockSpec((1, N_MOD, D_MODEL), lambda b, i: (b, 0, 0)),
                  pl.BlockSpec((1, D_MODEL), lambda b, i: (0, 0))],
        out_specs=row,
        out_shape=jax.ShapeDtypeStruct((B, S, D_MODEL), F32),
        compiler_params=_params("arbitrary", "arbitrary"),
        name="final",
    )(x1, ya, yb, mod, g_post)


def _pad_heads_cols(w, nh, scale=1.0):
    k = w.shape[0]
    w = (w * scale).reshape(k, nh, HEAD_DIM)
    return jnp.pad(w, ((0, 0), (0, 0), (0, HEAD_PAD - HEAD_DIM))).reshape(k, nh * HEAD_PAD)


def _pad_heads_rows(w, nh):
    n = w.shape[1]
    w = w.reshape(nh, HEAD_DIM, n)
    return jnp.pad(w, ((0, 0), (0, HEAD_PAD - HEAD_DIM), (0, 0))).reshape(nh * HEAD_PAD, n)


def _layer(x, c8, w_ada, b_ada, g_pre_mix, g_post_mix, g_pre_ffn, g_post_ffn,
           w_in, b_fgate, swa_sinks, w_out, w_query, sub_keys, w_u, w_v):
    B, S, D = x.shape
    T = B * S
    scale = HEAD_DIM ** -0.5
    mod = _ada_call(c8, w_ada, b_ada)[:B].reshape(B, N_MOD, D)

    o = 0
    parts = []
    for nh, sc in ((FOX_HEADS, scale), (FOX_HEADS, 1.0), (FOX_HEADS, 1.0)):
        parts.append(_pad_heads_cols(w_in[:, o:o + nh * HEAD_DIM], nh, sc))
        o += nh * HEAD_DIM
    parts.append(jnp.pad(w_in[:, o:o + FOX_HEADS], ((0, 0), (0, HEAD_PAD - FOX_HEADS))))
    o += FOX_HEADS
    for nh, sc in ((SWA_HEADS, scale), (SWA_KV_HEADS, 1.0), (SWA_KV_HEADS, 1.0)):
        parts.append(_pad_heads_cols(w_in[:, o:o + nh * HEAD_DIM], nh, sc))
        o += nh * HEAD_DIM
    w_all = jnp.concatenate(parts, axis=1).astype(BF16)
    bf_pad = jnp.pad(b_fgate, (0, HEAD_PAD - FOX_HEADS)).reshape(1, HEAD_PAD)

    qt, kp, vt, sq, sk, sv = _inproj_call(x, mod, g_pre_mix.reshape(1, D), w_all, bf_pad)
    fo = _fox_call(qt, kp, vt)
    so = _swa_call(swa_sinks, sq, sk, sv)

    nf = FOX_HEADS * HEAD_DIM
    wof = _pad_heads_rows(w_out[:nf], FOX_HEADS).astype(BF16)
    wos = _pad_heads_rows(w_out[nf:], SWA_HEADS).astype(BF16)
    keys = sub_keys.reshape(2 * PEER_HEADS, N_KEYS, PEER_HALF).astype(BF16)
    x1, h2, idx, gates = _route_call(fo, so, x, mod, g_post_mix.reshape(1, D), g_pre_ffn.reshape(1, D),
                                     wof, wos, w_query.astype(BF16), keys)

    ub, vb = w_u.astype(BF16), w_v.astype(BF16)
    table = jnp.concatenate([ub.reshape(N_EXPERTS, 8, HEAD_PAD), vb.reshape(N_EXPERTS, 8, HEAD_PAD)], axis=1)
    h2f = h2.reshape(T, D)
    tile = min(ROW_TILE, S)
    ts = (int(T * SC_SHARE) // tile) * tile
    t1 = T - ts
    y_tc = _peer_call(idx.reshape(T * PEER_PICKS), table, h2f.reshape(T, 8, HEAD_PAD), gates, t1).reshape(t1, D)
    if not ts:
        return _final_call(x1, y_tc, y_tc, mod, g_post_ffn.reshape(1, D))
    def bf16_bits(w):
        b = lax.bitcast_convert_type(w, jnp.uint32)
        return (b + jnp.uint32(0x7FFF) + ((b >> 16) & jnp.uint32(1))) >> 16

    table_i32 = lax.bitcast_convert_type((bf16_bits(w_v) << 16) | bf16_bits(w_u), jnp.int32)
    y_sc = _peer_sc_call(idx, table_i32, h2f, gates, t1)
    return _final_call(x1, y_tc, y_sc, mod, g_post_ffn.reshape(1, D))


def kernel(x, c, w_ada, b_ada, g_pre_mix, g_post_mix, g_pre_ffn, g_post_ffn, w_in, b_fgate, swa_sinks, w_out,
           w_query, sub_keys, w_u, w_v):
    B = x.shape[0]
    c8 = jnp.pad(c, ((0, 8 - B), (0, 0)))
    for l in range(w_ada.shape[0]):
        x = _layer(x, c8, w_ada[l], b_ada[l], g_pre_mix[l], g_post_mix[l], g_pre_ffn[l], g_post_ffn[l],
                   w_in[l], b_fgate[l], swa_sinks[l], w_out[l], w_query[l], sub_keys[l], w_u[l], w_v[l])
    return x
```

```python
import functools

import numpy as np
import jax
import jax.numpy as jnp
from jax import lax
from jax.experimental import pallas as pl
from jax.experimental.pallas import tpu as pltpu
from jax.experimental.pallas import tpu_sc as plsc

F32 = jnp.float32
BF16 = jnp.bfloat16

D_MODEL = 1024
HEAD_DIM = 64
HEAD_PAD = 128
FOX_HEADS = 8
SWA_HEADS = 8
SWA_KV_HEADS = 2
SWA_GROUP = SWA_HEADS // SWA_KV_HEADS
WINDOW = 128
PEER_HEADS = 8
PEER_HALF = 128
N_KEYS = 128
N_EXPERTS = N_KEYS * N_KEYS
PEER_TOPK = 16
PEER_PICKS = PEER_HEADS * PEER_TOPK
N_MOD = 6
RMS_EPS = 1e-6
NEG_INF = -1e30

_F_LANE = HEAD_DIM
_ONE_LANE = HEAD_DIM

ROW_TILE = 1024
FOX_TILE = 1024
SWA_TILE = 512
ROUTE_TILE = 256
PEER_TILE = 8
PEER_PHASES = 4
PEER_UNROLL = 8
PEER_CHAIN = 8
PEER_GROUP = 16
SC_SHARE = 0.5
TABLE_ROWS = 16
TABLE_COLS = PEER_PICKS * TABLE_ROWS

_VMEM_LIMIT = 56 * 1024 * 1024


def _dot(a, b):
    return jnp.dot(a, b, preferred_element_type=F32)


def _dot_nt(a, b):
    return lax.dot_general(a, b, (((1,), (1,)), ((), ())), preferred_element_type=F32)


def _split3(x):
    hi = x.astype(BF16)
    r = x - hi.astype(F32)
    mid = r.astype(BF16)
    lo = (r - mid.astype(F32)).astype(BF16)
    return hi, mid, lo


def _rms(x):
    return x * lax.rsqrt(jnp.mean(x * x, axis=-1, keepdims=True) + RMS_EPS)


def _params(*sem):
    return pltpu.CompilerParams(dimension_semantics=sem, vmem_limit_bytes=_VMEM_LIMIT)


def _ada_kernel(c_ref, w_ref, b_ref, o_ref):
    c = c_ref[...]
    s = (c * jax.nn.sigmoid(c)).astype(BF16)
    o_ref[...] = _dot(s, w_ref[...].astype(BF16)) + b_ref[...]


def _ada_call(c8, w_ada, b_ada):
    n = w_ada.shape[1]
    tn = 1536
    return pl.pallas_call(
        _ada_kernel,
        grid=(n // tn,),
        in_specs=[pl.BlockSpec((8, D_MODEL), lambda j: (0, 0)),
                  pl.BlockSpec((D_MODEL, tn), lambda j: (0, j)),
                  pl.BlockSpec((1, tn), lambda j: (0, j))],
        out_specs=pl.BlockSpec((8, tn), lambda j: (0, j)),
        out_shape=jax.ShapeDtypeStruct((8, n), F32),
        compiler_params=_params("arbitrary"),
        name="ada",
    )(c8, w_ada, b_ada.reshape(1, n))


_NQ = FOX_HEADS * HEAD_PAD
_NS = SWA_HEADS * HEAD_PAD
_NKV = SWA_KV_HEADS * HEAD_PAD
_IN_COLS = 3 * _NQ + HEAD_PAD + _NS + 2 * _NKV


def _inproj_kernel(x_ref, mod_ref, g_ref, w_ref, bf_ref, tri_ref, pq_ref, pk_ref, cst_ref,
                   qt_ref, kp_ref, vt_ref, sq_ref, sk_ref, sv_ref, carry_ref):
    i = pl.program_id(1)

    @pl.when(i == 0)
    def _():
        carry_ref[...] = jnp.zeros_like(carry_ref)

    x = x_ref[0]
    sh1 = mod_ref[0, 0:1, :]
    sc1 = mod_ref[0, 1:2, :]
    h = _rms(x) * g_ref[...] * (1.0 + sc1) + sh1
    proj = _dot(h.astype(BF16), w_ref[...])

    z = proj[:, 3 * _NQ:3 * _NQ + HEAD_PAD] + bf_ref[...]
    ls = jnp.minimum(z, 0.0) - jnp.log(1.0 + jnp.exp(-jnp.abs(z)))
    tri = tri_ref[...]
    hi, mid, lo = _split3(ls)
    fcum = _dot(tri, hi) + _dot(tri, mid) + _dot(tri, lo) + carry_ref[...]
    carry_ref[...] = fcum[fcum.shape[0] - 1:, :]

    fh, fm, fl = _split3(fcum)
    eq = _dot(fh, pq_ref[0]) + _dot(fm, pq_ref[1]) + _dot(fl, pq_ref[2]) + cst_ref[0:1, :]
    ek = _dot(fh, pk_ref[0]) + _dot(fm, pk_ref[1]) + _dot(fl, pk_ref[2]) + cst_ref[1:2, :]
    qt_ref[0, 0] = (proj[:, 0:_NQ] + eq).T.astype(BF16)
    kp_ref[0] = (proj[:, _NQ:2 * _NQ] + ek).astype(BF16)
    vt_ref[0, 0] = (proj[:, 2 * _NQ:3 * _NQ] + cst_ref[2:3, :]).T.astype(BF16)
    o = 3 * _NQ + HEAD_PAD
    sq_ref[0] = proj[:, o:o + _NS].astype(BF16)
    sk_ref[0] = proj[:, o + _NS:o + _NS + _NKV].astype(BF16)
    sv_ref[0] = proj[:, o + _NS + _NKV:o + _NS + 2 * _NKV].astype(BF16)


def _inproj_consts(tr):
    tri = np.tril(np.ones((tr, tr), np.float32))
    pq = np.zeros((3, HEAD_PAD, _NQ), np.float32)
    pk = np.zeros((3, HEAD_PAD, _NQ), np.float32)
    cst = np.zeros((8, _NQ), np.float32)
    for h in range(FOX_HEADS):
        b = h * HEAD_PAD + _F_LANE
        for j in range(3):
            pq[j, h, b + j] = 1.0
            pk[j, h, b + 3 + j] = -1.0
            cst[0, b + 3 + j] = 1.0
            cst[1, b + j] = 1.0
        cst[2, h * HEAD_PAD + _ONE_LANE] = 1.0
    return (jnp.asarray(tri, BF16), jnp.asarray(pq, BF16), jnp.asarray(pk, BF16), jnp.asarray(cst, F32))


def _inproj_call(x, mod, g_pre, w_all, bf_pad):
    B, S, _ = x.shape
    tr = min(ROW_TILE, S)
    n = S // tr
    tri, pq, pk, cst = _inproj_consts(tr)
    row = lambda w: pl.BlockSpec((1, tr, w), lambda b, i: (b, i, 0))
    slab = pl.BlockSpec((1, 1, _NQ, tr), lambda b, i: (b, i, 0, 0))
    full = lambda a: pl.BlockSpec(a.shape, lambda b, i: (0,) * a.ndim)
    outs = [jax.ShapeDtypeStruct((B, n, _NQ, tr), BF16), jax.ShapeDtypeStruct((B, S, _NQ), BF16),
            jax.ShapeDtypeStruct((B, n, _NQ, tr), BF16), jax.ShapeDtypeStruct((B, S, _NS), BF16),
            jax.ShapeDtypeStruct((B, S, _NKV), BF16), jax.ShapeDtypeStruct((B, S, _NKV), BF16)]
    return pl.pallas_call(
        _inproj_kernel,
        grid=(B, n),
        in_specs=[row(D_MODEL),
                  pl.BlockSpec((1, N_MOD, D_MODEL), lambda b, i: (b, 0, 0)),
                  full(g_pre), full(w_all), full(bf_pad), full(tri), full(pq), full(pk), full(cst)],
        out_specs=[slab, row(_NQ), slab, row(_NS), row(_NKV), row(_NKV)],
        out_shape=outs,
        scratch_shapes=[pltpu.VMEM((1, HEAD_PAD), F32)],
        compiler_params=_params("arbitrary", "arbitrary"),
        name="inproj",
    )(x, mod, g_pre, w_all, bf_pad, tri, pq, pk, cst)


def _fox_kernel(qt_ref, k_ref, vt_ref, o_ref, m_ref, acc_ref, sa_ref, sb_ref):
    i = pl.program_id(2)
    t = o_ref.shape[1]
    m_ref[...] = jnp.full_like(m_ref, NEG_INF)
    acc_ref[...] = jnp.zeros_like(acc_ref)
    bufs = (sa_ref, sb_ref)

    def scores(j, dst):
        off = pl.multiple_of(j * t, t)
        dst[...] = _dot(k_ref[0, pl.ds(off, t), :], qt_ref[0, 0])

    def absorb(j, src, masked):
        s = src[...]
        if masked:
            r = lax.broadcasted_iota(jnp.int32, s.shape, 0)
            c = lax.broadcasted_iota(jnp.int32, s.shape, 1)
            s = jnp.where(r <= c, s, NEG_INF)
        m_prev = m_ref[...]
        m_new = jnp.maximum(m_prev, jnp.max(s, axis=0, keepdims=True))
        p = jnp.exp(s - m_new)
        acc_ref[...] = jnp.exp(m_prev - m_new) * acc_ref[...] + _dot(vt_ref[0, j], p.astype(BF16))
        m_ref[...] = m_new

    scores(0, sa_ref)

    def pair(jj, carry):
        j = 2 * jj
        scores(j + 1, sb_ref)
        absorb(j, sa_ref, False)
        scores(j + 2, sa_ref)
        absorb(j + 1, sb_ref, False)
        return carry

    lax.fori_loop(0, i // 2, pair, 0)

    @pl.when(i % 2 == 1)
    def _():
        scores(i, sb_ref)
        absorb(i - 1, sa_ref, False)
        absorb(i, sb_ref, True)

    @pl.when(i % 2 == 0)
    def _():
        absorb(i, sa_ref, True)

    acc = acc_ref[...]
    o_ref[0] = (acc / acc[_ONE_LANE:_ONE_LANE + 1, :]).T.astype(BF16)


def _fox_call(qt, kp, vt):
    B, n, _, t = qt.shape
    S = n * t
    w = HEAD_PAD
    return pl.pallas_call(
        _fox_kernel,
        grid=(B, FOX_HEADS, n),
        in_specs=[pl.BlockSpec((1, 1, w, t), lambda b, h, i: (b, i, h, 0)),
                  pl.BlockSpec((1, S, w), lambda b, h, i: (b, 0, h)),
                  pl.BlockSpec((1, n, w, t), lambda b, h, i: (b, 0, h, 0))],
        out_specs=pl.BlockSpec((1, t, w), lambda b, h, i: (b, i, h)),
        out_shape=jax.ShapeDtypeStruct((B, S, _NQ), BF16),
        scratch_shapes=[pltpu.VMEM((1, t), F32), pltpu.VMEM((HEAD_PAD, t), F32),
                        pltpu.VMEM((t, t), F32), pltpu.VMEM((t, t), F32)],
        compiler_params=_params("arbitrary", "arbitrary", "arbitrary"),
        name="fox",
    )(qt, kp, vt)


def _swa_kernel(sink_ref, q_ref, kc_ref, kp_ref, vc_ref, vp_ref, o_ref):
    i = pl.program_id(1)
    nsub = q_ref.shape[1] // WINDOW
    r = lax.broadcasted_iota(jnp.int32, (WINDOW, 2 * WINDOW), 0)
    j = lax.broadcasted_iota(jnp.int32, (WINDOW, 2 * WINDOW), 1)
    dist = r + WINDOW - j
    valid = (dist >= 0) & (dist < WINDOW)
    distf = dist.astype(F32)
    for qb in range(nsub):
        rows = slice(qb * WINDOW, (qb + 1) * WINDOW)
        if qb == 0:
            ok = valid & ((j >= WINDOW) | (i > 0))
        else:
            ok = valid
        for g in range(SWA_KV_HEADS):
            lanes = slice(g * HEAD_PAD, (g + 1) * HEAD_PAD)
            if qb == 0:
                kprev, vprev = kp_ref[0, :, lanes], vp_ref[0, :, lanes]
            else:
                prev = slice((qb - 1) * WINDOW, qb * WINDOW)
                kprev, vprev = kc_ref[0, prev, lanes], vc_ref[0, prev, lanes]
            kk = jnp.concatenate([kprev, kc_ref[0, rows, lanes]], axis=0)
            vv = jnp.concatenate([vprev, vc_ref[0, rows, lanes]], axis=0)
            for u in range(SWA_GROUP):
                hq = g * SWA_GROUP + u
                slope = 2.0 ** (-8.0 * (hq + 1) / SWA_HEADS)
                q = q_ref[0, rows, hq * HEAD_PAD:(hq + 1) * HEAD_PAD]
                s = _dot_nt(q, kk) - slope * distf
                s = jnp.where(ok, s, NEG_INF)
                sink = sink_ref[hq]
                m = jnp.maximum(jnp.max(s, axis=1, keepdims=True), sink)
                p = jnp.exp(s - m)
                den = jnp.sum(p, axis=1, keepdims=True) + jnp.exp(sink - m)
                o = _dot(p.astype(BF16), vv) / den
                o_ref[0, rows, hq * HEAD_PAD:(hq + 1) * HEAD_PAD] = o.astype(BF16)


def _swa_call(sinks, sq, sk, sv):
    B, S, _ = sq.shape
    t = min(SWA_TILE, S)
    per = t // WINDOW
    cur = lambda w: pl.BlockSpec((1, t, w), lambda b, i: (b, i, 0))
    prv = lambda w: pl.BlockSpec((1, WINDOW, w), lambda b, i: (b, jnp.maximum(i * per - 1, 0), 0))
    return pl.pallas_call(
        _swa_kernel,
        grid=(B, S // t),
        in_specs=[pl.BlockSpec(memory_space=pltpu.SMEM),
                  cur(_NS), cur(_NKV), prv(_NKV), cur(_NKV), prv(_NKV)],
        out_specs=cur(_NS),
        out_shape=jax.ShapeDtypeStruct((B, S, _NS), BF16),
        compiler_params=_params("arbitrary", "arbitrary"),
        name="swa",
    )(sinks, sq, sk, sk, sv, sv)


_NO_ID = 1 << 20


def _topk_rows(s, k, val_ref, idx_ref, ids=None):
    if ids is None:
        ids = lax.broadcasted_iota(jnp.int32, s.shape, 0)
    for r in range(k):
        m = jnp.max(s, axis=0, keepdims=True)
        i = jnp.min(jnp.where(s == m, ids, _NO_ID), axis=0, keepdims=True)
        val_ref[pl.ds(r, 1), :] = m
        idx_ref[pl.ds(r, 1), :] = i
        if r + 1 < k:
            s = jnp.where(ids == i, -jnp.inf, s)


def _cand_counts():
    return [PEER_TOPK // (a + 1) for a in range(PEER_TOPK)]


_CAND_ROWS = 56


def _route_kernel(fo_ref, so_ref, x_ref, mod_ref, gpost_ref, gpre_ref, wof_ref, wos_ref, wq_ref, keys_ref, cid_ref,
                  x1_ref, h2_ref, idx_ref, gate_ref, qs_ref, sv_ref, si_ref, et_ref, gt_ref, cand_ref):
    x = x_ref[0]
    gt1 = mod_ref[0, 2:3, :]
    sh2 = mod_ref[0, 3:4, :]
    sc2 = mod_ref[0, 4:5, :]
    y = _dot(fo_ref[0], wof_ref[...]) + _dot(so_ref[0], wos_ref[...])
    x1 = x + gt1 * (_rms(y) * gpost_ref[...])
    x1_ref[0] = x1
    h2 = _rms(x1) * gpre_ref[...] * (1.0 + sc2) + sh2
    h2_ref[0] = h2
    qp = _dot(h2.astype(BF16), wq_ref[...])
    nhp = 2 * PEER_HEADS
    for hp in range(nhp):
        qs_ref[hp] = qp[:, hp * PEER_HALF:(hp + 1) * PEER_HALF].astype(BF16)

    def half(q, carry):
        for u in range(4):
            hp = 4 * q + u
            sc = _dot_nt(keys_ref[hp], qs_ref[hp])
            _topk_rows(sc, PEER_TOPK, sv_ref.at[hp], si_ref.at[hp])
        return carry

    lax.fori_loop(0, nhp // 4, half, 0)

    counts = _cand_counts()
    used = sum(counts)
    par = cand_ref.shape[0]
    for u in range(par):
        cand_ref[u, pl.ds(used, _CAND_ROWS - used), :] = jnp.full((_CAND_ROWS - used, cand_ref.shape[2]), -jnp.inf, F32)
    cid = cid_ref[...]

    def head(hh, carry):
        for u in range(par):
            h = par * hh + u
            v0, v1 = sv_ref[2 * h], sv_ref[2 * h + 1]
            i0, i1 = si_ref[2 * h], si_ref[2 * h + 1]
            off = 0
            for a, nb in enumerate(counts):
                cand_ref[u, pl.ds(off, nb), :] = v0[a:a + 1, :] + v1[0:nb, :]
                off += nb
            _topk_rows(cand_ref[u], PEER_TOPK, gt_ref.at[h], et_ref.at[h], ids=cid)
            cv, ci = gt_ref[h], et_ref[h]
            ca, cb = ci >> 4, ci & (PEER_TOPK - 1)
            e1 = jnp.zeros_like(ci)
            e2 = jnp.zeros_like(ci)
            for a in range(PEER_TOPK):
                e1 = jnp.where(ca == a, i0[a:a + 1, :], e1)
                e2 = jnp.where(cb == a, i1[a:a + 1, :], e2)
            et_ref[h] = e1 * N_KEYS + e2
            ex = jnp.exp(cv - cv[0:1, :])
            gt_ref[h] = ex / jnp.sum(ex, axis=0, keepdims=True)
        return carry

    lax.fori_loop(0, PEER_HEADS // par, head, 0)
    tt = et_ref.shape[2]
    et = lax.bitcast_convert_type(et_ref[...].reshape(PEER_PICKS, tt), F32)
    idx_ref[...] = lax.bitcast_convert_type(et.T, jnp.int32)
    gate_ref[...] = gt_ref[...].reshape(PEER_PICKS, tt).T


def _route_call(fo, so, x, mod, g_post, g_pre, wof, wos, wq, keys, b0=0, nb=None):
    _, S, _ = x.shape
    B = x.shape[0] if nb is None else nb
    tt = min(ROUTE_TILE, S)
    per = S // tt
    row_in = lambda w: pl.BlockSpec((1, tt, w), lambda b, i: (b + b0, i, 0))
    row = lambda w: pl.BlockSpec((1, tt, w), lambda b, i: (b, i, 0))
    full = lambda a: pl.BlockSpec(a.shape, lambda b, i: (0,) * a.ndim)
    tok = pl.BlockSpec((tt, PEER_PICKS), lambda b, i: (b * per + i, 0))
    flat = [a * PEER_TOPK + b for a, nb in enumerate(_cand_counts()) for b in range(nb)]
    flat += [_NO_ID] * (_CAND_ROWS - len(flat))
    cid = jnp.asarray(np.broadcast_to(np.asarray(flat, np.int32)[:, None], (_CAND_ROWS, tt)))
    return pl.pallas_call(
        _route_kernel,
        grid=(B, per),
        in_specs=[row_in(_NQ), row_in(_NS), row_in(D_MODEL),
                  pl.BlockSpec((1, N_MOD, D_MODEL), lambda b, i: (b + b0, 0, 0)),
                  full(g_post), full(g_pre), full(wof), full(wos), full(wq), full(keys), full(cid)],
        out_specs=[row(D_MODEL), row(D_MODEL), tok, tok],
        out_shape=[jax.ShapeDtypeStruct((B, S, D_MODEL), F32),
                   jax.ShapeDtypeStruct((B, S, D_MODEL), F32),
                   jax.ShapeDtypeStruct((B * S, PEER_PICKS), jnp.int32),
                   jax.ShapeDtypeStruct((B * S, PEER_PICKS), F32)],
        scratch_shapes=[pltpu.VMEM((2 * PEER_HEADS, tt, PEER_HALF), BF16),
                        pltpu.VMEM((2 * PEER_HEADS, PEER_TOPK, tt), F32),
                        pltpu.VMEM((2 * PEER_HEADS, PEER_TOPK, tt), jnp.int32),
                        pltpu.VMEM((PEER_HEADS, PEER_TOPK, tt), jnp.int32),
                        pltpu.VMEM((PEER_HEADS, PEER_TOPK, tt), F32),
                        pltpu.VMEM((2, _CAND_ROWS, tt), F32)],
        compiler_params=_params("arbitrary", "arbitrary"),
        name="route",
    )(fo, so, x, mod, g_post, g_pre, wof, wos, wq, keys, cid)


def _peer_consts():
    half = TABLE_ROWS // 2
    gsum = np.zeros((PEER_GROUP, PEER_GROUP * TABLE_ROWS), np.float32)
    ev = np.zeros((PEER_PICKS, TABLE_COLS), np.float32)
    for j in range(PEER_GROUP):
        gsum[j, j * TABLE_ROWS:j * TABLE_ROWS + half] = 1.0
    for k in range(PEER_PICKS):
        ev[k, k * TABLE_ROWS + half:(k + 1) * TABLE_ROWS] = 1.0
    return jnp.asarray(gsum, BF16), jnp.asarray(ev, BF16)


def _peer_kernel(idx_hbm, tab_hbm, hs_ref, g_ref, gsum_ref, ev_ref, y_ref,
                 buf0, buf1, buf2, buf3, ib0, ib1, wr_ref, sem_g, sem_i):
    s = pl.program_id(0)
    last = pl.num_programs(0) - 1
    tt = PEER_TILE
    npt = tt * PEER_PICKS
    half = TABLE_ROWS // 2
    ngrp = PEER_PICKS // PEER_GROUP
    grows = PEER_GROUP * TABLE_ROWS
    bufs = (buf0, buf1, buf2, buf3)
    ibs = (ib0, ib1)

    def idx_fetch(tile, j):
        return pltpu.make_async_copy(idx_hbm.at[pl.ds(tile * npt, npt)], ibs[j], sem_i.at[j])

    def rows_done(j):
        return pltpu.make_async_copy(tab_hbm.at[pl.ds(0, npt)], bufs[j], sem_g.at[j])

    def issue_rows(ib, buf, sem):
        c = ib[0] >> 31
        for i in range(npt):
            e = ib[i] + c
            pltpu.make_async_copy(tab_hbm.at[e], buf.at[i], sem).start(priority=i % 2)
            if i % PEER_CHAIN == PEER_CHAIN - 1:
                c = e >> 31

    @pl.when(s == 0)
    def _():
        for j in range(2):
            first = idx_fetch(j, j)
            first.start()
            first.wait()

            def body(i, carry):
                pltpu.make_async_copy(tab_hbm.at[ibs[j][i]], bufs[j].at[i], sem_g.at[j]).start()
                return carry

            lax.fori_loop(0, npt, body, 0)
        idx_fetch(2, 0).start()

    row8 = lax.broadcasted_iota(jnp.int32, (half, TABLE_COLS), 0)
    col8 = lax.broadcasted_iota(jnp.int32, (half, TABLE_COLS), 1) % TABLE_ROWS
    mask_v = col8 == row8 + half
    lane = lax.broadcasted_iota(jnp.int32, (PEER_PICKS, HEAD_PAD), 1)
    gsum = gsum_ref[...]

    def evaluate(cur, tok0):
        def ubody(it, at):
            for u in range(PEER_UNROLL):
                t = it * PEER_UNROLL + u
                hrow = hs_ref[tok0 + t]
                h16 = jnp.concatenate([hrow, jnp.zeros_like(hrow)], axis=0).astype(BF16)
                tw = cur[pl.ds(t * PEER_PICKS, PEER_PICKS)]
                prod = (tw * h16[None]).reshape(TABLE_COLS, HEAD_PAD)
                parts = [_dot(gsum, prod[g * grows:(g + 1) * grows]) for g in range(ngrp)]
                z = jnp.sum(jnp.concatenate(parts, axis=0), axis=1, keepdims=True)
                at = jnp.where(lane == t, z, at)
            return at

        at = jnp.zeros((PEER_PICKS, HEAD_PAD), F32)
        for it in range(tt // PEER_UNROLL):
            at = ubody(it, at)
        a = at.T[:tt]
        w = jax.nn.gelu(a) * g_ref[tok0:tok0 + tt, :]
        wr_ref[...] = _dot(w.astype(BF16), ev_ref[...])

        def vbody(it, carry):
            for u in range(PEER_UNROLL):
                t = it * PEER_UNROLL + u
                wrow = jnp.broadcast_to(wr_ref[pl.ds(t, 1), :], (half, TABLE_COLS))
                wexp = jnp.where(mask_v, wrow, 0.0).astype(BF16)
                wb = cur[pl.ds(t * PEER_PICKS, PEER_PICKS)].reshape(TABLE_COLS, HEAD_PAD)
                y_ref[tok0 + t] = _dot(wexp, wb)
            return carry

        for it in range(tt // PEER_UNROLL):
            vbody(it, 0)

    for p in range(PEER_PHASES):
        k = PEER_PHASES * s + p
        idx_fetch(k + 2, p % 2).wait()
        idx_fetch(k + 3, (p + 1) % 2).start()
        rows_done(p).wait()
        issue_rows(ibs[p % 2], bufs[(p + 2) % PEER_PHASES], sem_g.at[(p + 2) % PEER_PHASES])
        evaluate(bufs[p], p * tt)

    @pl.when(s == last)
    def _():
        rows_done(0).wait()
        rows_done(1).wait()
        idx_fetch(0, 0).wait()


def _peer_call(idx_flat, table, hs3, gates, T):
    tt = PEER_TILE
    npt = tt * PEER_PICKS
    assert tt % PEER_UNROLL == 0 and PEER_PICKS % PEER_GROUP == 0 and tt <= HEAD_PAD and PEER_PHASES == 4
    gsum, ev = _peer_consts()
    short = max(0, (T + 3 * tt) * PEER_PICKS - idx_flat.shape[0])
    idx_pad = jnp.pad(idx_flat, (0, short)) if short else idx_flat
    step = PEER_PHASES * tt
    full = lambda a: pl.BlockSpec(a.shape, lambda s: (0,) * a.ndim)
    rows = pltpu.VMEM((npt, TABLE_ROWS, HEAD_PAD), BF16)
    return pl.pallas_call(
        _peer_kernel,
        grid=(T // step,),
        in_specs=[pl.BlockSpec(memory_space=pl.ANY),
                  pl.BlockSpec(memory_space=pl.ANY),
                  pl.BlockSpec((step, 8, HEAD_PAD), lambda s: (s, 0, 0)),
                  pl.BlockSpec((step, PEER_PICKS), lambda s: (s, 0)),
                  full(gsum), full(ev)],
        out_specs=pl.BlockSpec((step, 8, HEAD_PAD), lambda s: (s, 0, 0)),
        out_shape=jax.ShapeDtypeStruct((T, 8, HEAD_PAD), F32),
        scratch_shapes=[rows, rows, rows, rows,
                        pltpu.SMEM((npt,), jnp.int32),
                        pltpu.SMEM((npt,), jnp.int32),
                        pltpu.VMEM((tt, TABLE_COLS), F32),
                        pltpu.SemaphoreType.DMA((PEER_PHASES,)),
                        pltpu.SemaphoreType.DMA((2,))],
        compiler_params=_params("arbitrary"),
        name="peer",
    )(idx_pad, table, hs3, gates, gsum, ev)


SC_LANES = 16
SC_GROUP = 16
SC_UNROLL = 2
_GELU_C = 0.7978845608028654


def _peer_sc_kernel(t1, idx_hbm, tab_hbm, h_hbm, g_hbm, y_hbm,
                    idx_a, idx_b, g_a, g_b, h_a, h_b, o_v, rows_v, sem, sem_in):
    ncores = lax.axis_size("c")
    wid = lax.axis_index("s") * ncores + lax.axis_index("c")
    per = y_hbm.shape[0] // (ncores * lax.axis_size("s"))
    base = t1 + wid * per
    nchunk = D_MODEL // SC_LANES
    ngrp = PEER_PICKS // SC_GROUP
    lane = lax.iota(jnp.int32, SC_LANES)
    sets = ((idx_a, g_a, h_a), (idx_b, g_b, h_b))

    def fetch(t, k):
        return (pltpu.make_async_copy(idx_hbm.at[t], sets[k][0], sem_in.at[k]),
                pltpu.make_async_copy(g_hbm.at[t], sets[k][1], sem_in.at[k]),
                pltpu.make_async_copy(h_hbm.at[t], sets[k][2], sem_in.at[k]))

    def gather(k, g, slot):
        return pltpu.make_async_copy(tab_hbm.at[sets[k][0].at[pl.ds(g * SC_GROUP, SC_GROUP)]], rows_v.at[slot],
                                     sem.at[slot])

    def process(t, k, t_next):
        g_v, h_v = sets[k][1], sets[k][2]

        def zero(c, carry):
            o_v[pl.ds(c * SC_LANES, SC_LANES)] = jnp.zeros((SC_LANES,), F32)
            return carry

        lax.fori_loop(0, nchunk, zero, 0)

        for g in range(ngrp):
            slot = g % 2
            if g + 1 < ngrp:
                gather(k, g + 1, 1 - slot).start()
            else:
                for c in fetch(t_next, 1 - k):
                    c.wait()
                gather(1 - k, 0, 0).start()
            gather(k, g, slot).wait()

            def dot_body(c, accs):
                hv = h_v[pl.ds(c * SC_LANES, SC_LANES)]
                out = []
                for p in range(SC_GROUP):
                    w = rows_v[slot, p, pl.ds(c * SC_LANES, SC_LANES)]
                    u = lax.bitcast_convert_type(w << 16, F32)
                    out.append(accs[p] + u * hv)
                return tuple(out)

            accs = plsc.parallel_loop(0, nchunk, unroll=SC_UNROLL,
                                      carry=tuple(jnp.zeros((SC_LANES,), F32) for _ in range(SC_GROUP)))(dot_body)
            a = jnp.zeros((SC_LANES,), F32)
            for p in range(SC_GROUP):
                a = jnp.where(lane == p, jnp.sum(accs[p]), a)
            z = _GELU_C * (a + 0.044715 * a * a * a)
            th = 1.0 - 2.0 / (jnp.exp(2.0 * z) + 1.0)
            wv = 0.5 * a * (1.0 + th) * g_v[pl.ds(g * SC_GROUP, SC_GROUP)]
            ws = [jnp.sum(jnp.where(lane == p, wv, 0.0)) for p in range(SC_GROUP)]

            def ax_body(c):
                o = o_v[pl.ds(c * SC_LANES, SC_LANES)]
                for p in range(SC_GROUP):
                    w = rows_v[slot, p, pl.ds(c * SC_LANES, SC_LANES)]
                    v = lax.bitcast_convert_type(w & jnp.int32(-65536), F32)
                    o = o + ws[p] * v
                o_v[pl.ds(c * SC_LANES, SC_LANES)] = o

            plsc.parallel_loop(0, nchunk, unroll=SC_UNROLL)(ax_body)

        pltpu.sync_copy(o_v, y_hbm.at[t - t1])

    for c in fetch(base, 0):
        c.start()
    for c in fetch(base, 0):
        c.wait()
    gather(0, 0, 0).start()

    def pair(j, carry):
        t0 = base + 2 * j
        t2 = jnp.minimum(t0 + 2, base + per - 1)
        for c in fetch(t0 + 1, 1):
            c.start()
        process(t0, 0, t0 + 1)
        for c in fetch(t2, 0):
            c.start()
        process(t0 + 1, 1, t2)
        return carry

    lax.fori_loop(0, per // 2, pair, 0)
    gather(0, 0, 0).wait()


def _peer_sc_call(idx2, table_i32, h2, gates, t1):
    ts = h2.shape[0] - t1
    mesh = plsc.VectorSubcoreMesh(core_axis_name="c", subcore_axis_name="s")
    run = pl.kernel(
        functools.partial(_peer_sc_kernel, t1),
        out_type=jax.ShapeDtypeStruct((ts, D_MODEL), F32),
        mesh=mesh,
        scratch_types=[pltpu.VMEM((PEER_PICKS,), jnp.int32), pltpu.VMEM((PEER_PICKS,), jnp.int32),
                       pltpu.VMEM((PEER_PICKS,), F32), pltpu.VMEM((PEER_PICKS,), F32),
                       pltpu.VMEM((D_MODEL,), F32), pltpu.VMEM((D_MODEL,), F32),
                       pltpu.VMEM((D_MODEL,), F32),
                       pltpu.VMEM((2, SC_GROUP, D_MODEL), jnp.int32),
                       pltpu.SemaphoreType.DMA((2,)),
                       pltpu.SemaphoreType.DMA((2,))],
        compiler_params=pltpu.CompilerParams(needs_layout_passes=False),
        name="peer_sc",
    )
    return run(idx2, table_i32, h2, gates)


def _final_kernel(nba, n1, xa_ref, xb_ref, ya_ref, yb_ref, mod_ref, g_ref, o_ref):
    b = pl.program_id(0)
    gi = b * pl.num_programs(1) + pl.program_id(1)
    x1 = jnp.where(b < nba, xa_ref[0], xb_ref[0])
    y = jnp.where(gi < n1, ya_ref[...], yb_ref[...])
    gt2 = mod_ref[0, 5:6, :]
    o_ref[0] = x1 + gt2 * (_rms(y) * g_ref[...])


def _final_call(xa, xb, ya, yb, mod, g_post):
    nba, S, _ = xa.shape
    B = mod.shape[0]
    tr = min(ROW_TILE, S)
    per = S // tr
    assert ya.shape[0] % tr == 0 and yb.shape[0] % tr == 0
    n1 = ya.shape[0] // tr
    row = pl.BlockSpec((1, tr, D_MODEL), lambda b, i: (b, i, 0))
    xfirst = pl.BlockSpec((1, tr, D_MODEL), lambda b, i: (jnp.minimum(b, nba - 1), i, 0))
    xrest = pl.BlockSpec((1, tr, D_MODEL), lambda b, i: (jnp.maximum(b - nba, 0), i, 0))
    first = pl.BlockSpec((tr, D_MODEL), lambda b, i: (jnp.minimum(b * per + i, n1 - 1), 0))
    rest = pl.BlockSpec((tr, D_MODEL), lambda b, i: (jnp.maximum(b * per + i - n1, 0), 0))
    return pl.pallas_call(
        functools.partial(_final_kernel, nba, n1),
        grid=(B, per),
        in_specs=[xfirst, xrest, first, rest, pl.BlockSpec((1, N_MOD, D_MODEL), lambda b, i: (b, 0, 0)),
                  pl.BlockSpec((1, D_MODEL), lambda b, i: (0, 0))],
        out_specs=row,
        out_shape=jax.ShapeDtypeStruct((B, S, D_MODEL), F32),
        compiler_params=_params("arbitrary", "arbitrary"),
        name="final",
    )(xa, xb, ya, yb, mod, g_post)


def _pad_heads_cols(w, nh, scale=1.0):
    k = w.shape[0]
    w = (w * scale).reshape(k, nh, HEAD_DIM)
    return jnp.pad(w, ((0, 0), (0, 0), (0, HEAD_PAD - HEAD_DIM))).reshape(k, nh * HEAD_PAD)


def _pad_heads_rows(w, nh):
    n = w.shape[1]
    w = w.reshape(nh, HEAD_DIM, n)
    return jnp.pad(w, ((0, 0), (0, HEAD_PAD - HEAD_DIM), (0, 0))).reshape(nh * HEAD_PAD, n)


def _layer(x, c8, w_ada, b_ada, g_pre_mix, g_post_mix, g_pre_ffn, g_post_ffn,
           w_in, b_fgate, swa_sinks, w_out, w_query, sub_keys, w_u, w_v):
    B, S, D = x.shape
    T = B * S
    scale = HEAD_DIM ** -0.5
    mod = _ada_call(c8, w_ada, b_ada)[:B].reshape(B, N_MOD, D)

    o = 0
    parts = []
    for nh, sc in ((FOX_HEADS, scale), (FOX_HEADS, 1.0), (FOX_HEADS, 1.0)):
        parts.append(_pad_heads_cols(w_in[:, o:o + nh * HEAD_DIM], nh, sc))
        o += nh * HEAD_DIM
    parts.append(jnp.pad(w_in[:, o:o + FOX_HEADS], ((0, 0), (0, HEAD_PAD - FOX_HEADS))))
    o += FOX_HEADS
    for nh, sc in ((SWA_HEADS, scale), (SWA_KV_HEADS, 1.0), (SWA_KV_HEADS, 1.0)):
        parts.append(_pad_heads_cols(w_in[:, o:o + nh * HEAD_DIM], nh, sc))
        o += nh * HEAD_DIM
    w_all = jnp.concatenate(parts, axis=1).astype(BF16)
    bf_pad = jnp.pad(b_fgate, (0, HEAD_PAD - FOX_HEADS)).reshape(1, HEAD_PAD)

    qt, kp, vt, sq, sk, sv = _inproj_call(x, mod, g_pre_mix.reshape(1, D), w_all, bf_pad)
    fo = _fox_call(qt, kp, vt)
    so = _swa_call(swa_sinks, sq, sk, sv)

    nf = FOX_HEADS * HEAD_DIM
    wof = _pad_heads_rows(w_out[:nf], FOX_HEADS).astype(BF16)
    wos = _pad_heads_rows(w_out[nf:], SWA_HEADS).astype(BF16)
    keys = sub_keys.reshape(2 * PEER_HEADS, N_KEYS, PEER_HALF).astype(BF16)
    route = functools.partial(_route_call, fo, so, x, mod, g_post_mix.reshape(1, D), g_pre_ffn.reshape(1, D),
                              wof, wos, w_query.astype(BF16), keys)
    gpf = g_post_ffn.reshape(1, D)

    ub, vb = w_u.astype(BF16), w_v.astype(BF16)
    table = jnp.concatenate([ub.reshape(N_EXPERTS, 8, HEAD_PAD), vb.reshape(N_EXPERTS, 8, HEAD_PAD)], axis=1)

    def tc_experts(idx, h2, gates):
        n = idx.shape[0]
        return _peer_call(idx.reshape(n * PEER_PICKS), table, h2.reshape(n, 8, HEAD_PAD), gates, n).reshape(n, D)

    nba = int(B * SC_SHARE)
    if not nba:
        x1, h2, idx, gates = route()
        y = tc_experts(idx, h2, gates)
        return _final_call(x1, x1, y, y, mod, gpf)

    def bf16_bits(w):
        b = lax.bitcast_convert_type(w, jnp.uint32)
        return (b + jnp.uint32(0x7FFF) + ((b >> 16) & jnp.uint32(1))) >> 16

    table_i32 = lax.bitcast_convert_type((bf16_bits(w_v) << 16) | bf16_bits(w_u), jnp.int32)
    xa, ha, ia, ga = route(b0=0, nb=nba)
    y_sc = _peer_sc_call(ia, table_i32, ha.reshape(nba * S, D), ga, 0)
    xb, hb, ib, gb = route(b0=nba, nb=B - nba)
    y_tc = tc_experts(ib, hb, gb)
    return _final_call(xa, xb, y_sc, y_tc, mod, gpf)


def kernel(x, c, w_ada, b_ada, g_pre_mix, g_post_mix, g_pre_ffn, g_post_ffn, w_in, b_fgate, swa_sinks, w_out,
           w_query, sub_keys, w_u, w_v):
    B = x.shape[0]
    c8 = jnp.pad(c, ((0, 8 - B), (0, 0)))
    for l in range(w_ada.shape[0]):
        x = _layer(x, c8, w_ada[l], b_ada[l], g_pre_mix[l], g_post_mix[l], g_pre_ffn[l], g_post_ffn[l],
                   w_in[l], b_fgate[l], swa_sinks[l], w_out[l], w_query[l], sub_keys[l], w_u[l], w_v[l])
    return x
```

```python
import functools

import numpy as np
import jax
import jax.numpy as jnp
from jax import lax
from jax.experimental import pallas as pl
from jax.experimental.pallas import tpu as pltpu
from jax.experimental.pallas import tpu_sc as plsc

F32 = jnp.float32
BF16 = jnp.bfloat16

D_MODEL = 1024
HEAD_DIM = 64
HEAD_PAD = 128
FOX_HEADS = 8
SWA_HEADS = 8
SWA_KV_HEADS = 2
SWA_GROUP = SWA_HEADS // SWA_KV_HEADS
WINDOW = 128
PEER_HEADS = 8
PEER_HALF = 128
N_KEYS = 128
N_EXPERTS = N_KEYS * N_KEYS
PEER_TOPK = 16
PEER_PICKS = PEER_HEADS * PEER_TOPK
N_MOD = 6
RMS_EPS = 1e-6
NEG_INF = -1e30

_F_LANE = HEAD_DIM
_ONE_LANE = HEAD_DIM

ROW_TILE = 1024
FOX_TILE = 1024
SWA_TILE = 512
ROUTE_TILE = 256
PEER_TILE = 8
PEER_PHASES = 4
PEER_UNROLL = 8
PEER_CHAIN = 8
PEER_GROUP = 16
SC_SHARE = 0.5
TABLE_ROWS = 16
TABLE_COLS = PEER_PICKS * TABLE_ROWS

_VMEM_LIMIT = 56 * 1024 * 1024


def _dot(a, b):
    return jnp.dot(a, b, preferred_element_type=F32)


def _dot_nt(a, b):
    return lax.dot_general(a, b, (((1,), (1,)), ((), ())), preferred_element_type=F32)


def _split3(x):
    hi = x.astype(BF16)
    r = x - hi.astype(F32)
    mid = r.astype(BF16)
    lo = (r - mid.astype(F32)).astype(BF16)
    return hi, mid, lo


def _rms(x):
    return x * lax.rsqrt(jnp.mean(x * x, axis=-1, keepdims=True) + RMS_EPS)


def _params(*sem):
    return pltpu.CompilerParams(dimension_semantics=sem, vmem_limit_bytes=_VMEM_LIMIT)


def _ada_kernel(c_ref, w_ref, b_ref, o_ref):
    c = c_ref[...]
    s = (c * jax.nn.sigmoid(c)).astype(BF16)
    o_ref[...] = _dot(s, w_ref[...].astype(BF16)) + b_ref[...]


def _ada_call(c8, w_ada, b_ada):
    n = w_ada.shape[1]
    tn = 1536
    return pl.pallas_call(
        _ada_kernel,
        grid=(n // tn,),
        in_specs=[pl.BlockSpec((8, D_MODEL), lambda j: (0, 0)),
                  pl.BlockSpec((D_MODEL, tn), lambda j: (0, j)),
                  pl.BlockSpec((1, tn), lambda j: (0, j))],
        out_specs=pl.BlockSpec((8, tn), lambda j: (0, j)),
        out_shape=jax.ShapeDtypeStruct((8, n), F32),
        compiler_params=_params("arbitrary"),
        name="ada",
    )(c8, w_ada, b_ada.reshape(1, n))


_NQ = FOX_HEADS * HEAD_PAD
_NS = SWA_HEADS * HEAD_PAD
_NKV = SWA_KV_HEADS * HEAD_PAD
_IN_COLS = 3 * _NQ + HEAD_PAD + _NS + 2 * _NKV


def _inproj_kernel(x_ref, mod_ref, g_ref, w_ref, bf_ref, tri_ref, pq_ref, pk_ref, cst_ref,
                   qt_ref, kp_ref, vt_ref, sq_ref, sk_ref, sv_ref, carry_ref):
    i = pl.program_id(1)

    @pl.when(i == 0)
    def _():
        carry_ref[...] = jnp.zeros_like(carry_ref)

    x = x_ref[0]
    sh1 = mod_ref[0, 0:1, :]
    sc1 = mod_ref[0, 1:2, :]
    h = _rms(x) * g_ref[...] * (1.0 + sc1) + sh1
    proj = _dot(h.astype(BF16), w_ref[...])

    z = proj[:, 3 * _NQ:3 * _NQ + HEAD_PAD] + bf_ref[...]
    ls = jnp.minimum(z, 0.0) - jnp.log(1.0 + jnp.exp(-jnp.abs(z)))
    tri = tri_ref[...]
    hi, mid, lo = _split3(ls)
    fcum = _dot(tri, hi) + _dot(tri, mid) + _dot(tri, lo) + carry_ref[...]
    carry_ref[...] = fcum[fcum.shape[0] - 1:, :]

    fh, fm, fl = _split3(fcum)
    eq = _dot(fh, pq_ref[0]) + _dot(fm, pq_ref[1]) + _dot(fl, pq_ref[2]) + cst_ref[0:1, :]
    ek = _dot(fh, pk_ref[0]) + _dot(fm, pk_ref[1]) + _dot(fl, pk_ref[2]) + cst_ref[1:2, :]
    qt_ref[0, 0] = (proj[:, 0:_NQ] + eq).T.astype(BF16)
    kp_ref[0] = (proj[:, _NQ:2 * _NQ] + ek).astype(BF16)
    vt_ref[0, 0] = (proj[:, 2 * _NQ:3 * _NQ] + cst_ref[2:3, :]).T.astype(BF16)
    o = 3 * _NQ + HEAD_PAD
    sq_ref[0] = proj[:, o:o + _NS].astype(BF16)
    sk_ref[0] = proj[:, o + _NS:o + _NS + _NKV].astype(BF16)
    sv_ref[0] = proj[:, o + _NS + _NKV:o + _NS + 2 * _NKV].astype(BF16)


def _inproj_consts(tr):
    tri = np.tril(np.ones((tr, tr), np.float32))
    pq = np.zeros((3, HEAD_PAD, _NQ), np.float32)
    pk = np.zeros((3, HEAD_PAD, _NQ), np.float32)
    cst = np.zeros((8, _NQ), np.float32)
    for h in range(FOX_HEADS):
        b = h * HEAD_PAD + _F_LANE
        for j in range(3):
            pq[j, h, b + j] = 1.0
            pk[j, h, b + 3 + j] = -1.0
            cst[0, b + 3 + j] = 1.0
            cst[1, b + j] = 1.0
        cst[2, h * HEAD_PAD + _ONE_LANE] = 1.0
    return (jnp.asarray(tri, BF16), jnp.asarray(pq, BF16), jnp.asarray(pk, BF16), jnp.asarray(cst, F32))


def _inproj_call(x, mod, g_pre, w_all, bf_pad):
    B, S, _ = x.shape
    tr = min(ROW_TILE, S)
    n = S // tr
    tri, pq, pk, cst = _inproj_consts(tr)
    row = lambda w: pl.BlockSpec((1, tr, w), lambda b, i: (b, i, 0))
    slab = pl.BlockSpec((1, 1, _NQ, tr), lambda b, i: (b, i, 0, 0))
    full = lambda a: pl.BlockSpec(a.shape, lambda b, i: (0,) * a.ndim)
    outs = [jax.ShapeDtypeStruct((B, n, _NQ, tr), BF16), jax.ShapeDtypeStruct((B, S, _NQ), BF16),
            jax.ShapeDtypeStruct((B, n, _NQ, tr), BF16), jax.ShapeDtypeStruct((B, S, _NS), BF16),
            jax.ShapeDtypeStruct((B, S, _NKV), BF16), jax.ShapeDtypeStruct((B, S, _NKV), BF16)]
    return pl.pallas_call(
        _inproj_kernel,
        grid=(B, n),
        in_specs=[row(D_MODEL),
                  pl.BlockSpec((1, N_MOD, D_MODEL), lambda b, i: (b, 0, 0)),
                  full(g_pre), full(w_all), full(bf_pad), full(tri), full(pq), full(pk), full(cst)],
        out_specs=[slab, row(_NQ), slab, row(_NS), row(_NKV), row(_NKV)],
        out_shape=outs,
        scratch_shapes=[pltpu.VMEM((1, HEAD_PAD), F32)],
        compiler_params=_params("arbitrary", "arbitrary"),
        name="inproj",
    )(x, mod, g_pre, w_all, bf_pad, tri, pq, pk, cst)


def _fox_kernel(qt_ref, k_ref, vt_ref, o_ref, m_ref, acc_ref, sa_ref, sb_ref):
    i = pl.program_id(2)
    t = o_ref.shape[1]
    m_ref[...] = jnp.full_like(m_ref, NEG_INF)
    acc_ref[...] = jnp.zeros_like(acc_ref)
    bufs = (sa_ref, sb_ref)

    def scores(j, dst):
        off = pl.multiple_of(j * t, t)
        dst[...] = _dot(k_ref[0, pl.ds(off, t), :], qt_ref[0, 0])

    def absorb(j, src, masked):
        s = src[...]
        if masked:
            r = lax.broadcasted_iota(jnp.int32, s.shape, 0)
            c = lax.broadcasted_iota(jnp.int32, s.shape, 1)
            s = jnp.where(r <= c, s, NEG_INF)
        m_prev = m_ref[...]
        m_new = jnp.maximum(m_prev, jnp.max(s, axis=0, keepdims=True))
        p = jnp.exp(s - m_new)
        acc_ref[...] = jnp.exp(m_prev - m_new) * acc_ref[...] + _dot(vt_ref[0, j], p.astype(BF16))
        m_ref[...] = m_new

    scores(0, sa_ref)

    def pair(jj, carry):
        j = 2 * jj
        scores(j + 1, sb_ref)
        absorb(j, sa_ref, False)
        scores(j + 2, sa_ref)
        absorb(j + 1, sb_ref, False)
        return carry

    lax.fori_loop(0, i // 2, pair, 0)

    @pl.when(i % 2 == 1)
    def _():
        scores(i, sb_ref)
        absorb(i - 1, sa_ref, False)
        absorb(i, sb_ref, True)

    @pl.when(i % 2 == 0)
    def _():
        absorb(i, sa_ref, True)

    acc = acc_ref[...]
    o_ref[0] = (acc / acc[_ONE_LANE:_ONE_LANE + 1, :]).T.astype(BF16)


def _fox_call(qt, kp, vt):
    B, n, _, t = qt.shape
    S = n * t
    w = HEAD_PAD
    return pl.pallas_call(
        _fox_kernel,
        grid=(B, FOX_HEADS, n),
        in_specs=[pl.BlockSpec((1, 1, w, t), lambda b, h, i: (b, i, h, 0)),
                  pl.BlockSpec((1, S, w), lambda b, h, i: (b, 0, h)),
                  pl.BlockSpec((1, n, w, t), lambda b, h, i: (b, 0, h, 0))],
        out_specs=pl.BlockSpec((1, t, w), lambda b, h, i: (b, i, h)),
        out_shape=jax.ShapeDtypeStruct((B, S, _NQ), BF16),
        scratch_shapes=[pltpu.VMEM((1, t), F32), pltpu.VMEM((HEAD_PAD, t), F32),
                        pltpu.VMEM((t, t), F32), pltpu.VMEM((t, t), F32)],
        compiler_params=_params("arbitrary", "arbitrary", "arbitrary"),
        name="fox",
    )(qt, kp, vt)


def _swa_kernel(sink_ref, q_ref, kc_ref, kp_ref, vc_ref, vp_ref, o_ref):
    i = pl.program_id(1)
    nsub = q_ref.shape[1] // WINDOW
    r = lax.broadcasted_iota(jnp.int32, (WINDOW, 2 * WINDOW), 0)
    j = lax.broadcasted_iota(jnp.int32, (WINDOW, 2 * WINDOW), 1)
    dist = r + WINDOW - j
    valid = (dist >= 0) & (dist < WINDOW)
    distf = dist.astype(F32)
    for qb in range(nsub):
        rows = slice(qb * WINDOW, (qb + 1) * WINDOW)
        if qb == 0:
            ok = valid & ((j >= WINDOW) | (i > 0))
        else:
            ok = valid
        for g in range(SWA_KV_HEADS):
            lanes = slice(g * HEAD_PAD, (g + 1) * HEAD_PAD)
            if qb == 0:
                kprev, vprev = kp_ref[0, :, lanes], vp_ref[0, :, lanes]
            else:
                prev = slice((qb - 1) * WINDOW, qb * WINDOW)
                kprev, vprev = kc_ref[0, prev, lanes], vc_ref[0, prev, lanes]
            kk = jnp.concatenate([kprev, kc_ref[0, rows, lanes]], axis=0)
            vv = jnp.concatenate([vprev, vc_ref[0, rows, lanes]], axis=0)
            for u in range(SWA_GROUP):
                hq = g * SWA_GROUP + u
                slope = 2.0 ** (-8.0 * (hq + 1) / SWA_HEADS)
                q = q_ref[0, rows, hq * HEAD_PAD:(hq + 1) * HEAD_PAD]
                s = _dot_nt(q, kk) - slope * distf
                s = jnp.where(ok, s, NEG_INF)
                sink = sink_ref[hq]
                m = jnp.maximum(jnp.max(s, axis=1, keepdims=True), sink)
                p = jnp.exp(s - m)
                den = jnp.sum(p, axis=1, keepdims=True) + jnp.exp(sink - m)
                o = _dot(p.astype(BF16), vv) / den
                o_ref[0, rows, hq * HEAD_PAD:(hq + 1) * HEAD_PAD] = o.astype(BF16)


def _swa_call(sinks, sq, sk, sv):
    B, S, _ = sq.shape
    t = min(SWA_TILE, S)
    per = t // WINDOW
    cur = lambda w: pl.BlockSpec((1, t, w), lambda b, i: (b, i, 0))
    prv = lambda w: pl.BlockSpec((1, WINDOW, w), lambda b, i: (b, jnp.maximum(i * per - 1, 0), 0))
    return pl.pallas_call(
        _swa_kernel,
        grid=(B, S // t),
        in_specs=[pl.BlockSpec(memory_space=pltpu.SMEM),
                  cur(_NS), cur(_NKV), prv(_NKV), cur(_NKV), prv(_NKV)],
        out_specs=cur(_NS),
        out_shape=jax.ShapeDtypeStruct((B, S, _NS), BF16),
        compiler_params=_params("arbitrary", "arbitrary"),
        name="swa",
    )(sinks, sq, sk, sk, sv, sv)


_NO_ID = 1 << 20


def _topk_rows(s, k, val_ref, idx_ref, ids=None):
    if ids is None:
        ids = lax.broadcasted_iota(jnp.int32, s.shape, 0)
    for r in range(k):
        m = jnp.max(s, axis=0, keepdims=True)
        i = jnp.min(jnp.where(s == m, ids, _NO_ID), axis=0, keepdims=True)
        val_ref[pl.ds(r, 1), :] = m
        idx_ref[pl.ds(r, 1), :] = i
        if r + 1 < k:
            s = jnp.where(ids == i, -jnp.inf, s)


def _cand_counts():
    return [PEER_TOPK // (a + 1) for a in range(PEER_TOPK)]


_CAND_ROWS = 56


def _route_kernel(fo_ref, so_ref, x_ref, mod_ref, gpost_ref, gpre_ref, wof_ref, wos_ref, wq_ref, keys_ref, cid_ref,
                  x1_ref, h2_ref, idx_ref, gate_ref, qs_ref, sv_ref, si_ref, et_ref, gt_ref, cand_ref):
    x = x_ref[0]
    gt1 = mod_ref[0, 2:3, :]
    sh2 = mod_ref[0, 3:4, :]
    sc2 = mod_ref[0, 4:5, :]
    y = _dot(fo_ref[0], wof_ref[...]) + _dot(so_ref[0], wos_ref[...])
    x1 = x + gt1 * (_rms(y) * gpost_ref[...])
    x1_ref[0] = x1
    h2 = _rms(x1) * gpre_ref[...] * (1.0 + sc2) + sh2
    h2_ref[0] = h2
    qp = _dot(h2.astype(BF16), wq_ref[...])
    nhp = 2 * PEER_HEADS
    for hp in range(nhp):
        qs_ref[hp] = qp[:, hp * PEER_HALF:(hp + 1) * PEER_HALF].astype(BF16)

    def half(q, carry):
        for u in range(4):
            hp = 4 * q + u
            sc = _dot_nt(keys_ref[hp], qs_ref[hp])
            _topk_rows(sc, PEER_TOPK, sv_ref.at[hp], si_ref.at[hp])
        return carry

    lax.fori_loop(0, nhp // 4, half, 0)

    counts = _cand_counts()
    used = sum(counts)
    par = cand_ref.shape[0]
    for u in range(par):
        cand_ref[u, pl.ds(used, _CAND_ROWS - used), :] = jnp.full((_CAND_ROWS - used, cand_ref.shape[2]), -jnp.inf, F32)
    cid = cid_ref[...]

    def head(hh, carry):
        for u in range(par):
            h = par * hh + u
            v0, v1 = sv_ref[2 * h], sv_ref[2 * h + 1]
            i0, i1 = si_ref[2 * h], si_ref[2 * h + 1]
            off = 0
            for a, nb in enumerate(counts):
                cand_ref[u, pl.ds(off, nb), :] = v0[a:a + 1, :] + v1[0:nb, :]
                off += nb
            _topk_rows(cand_ref[u], PEER_TOPK, gt_ref.at[h], et_ref.at[h], ids=cid)
            cv, ci = gt_ref[h], et_ref[h]
            ca, cb = ci >> 4, ci & (PEER_TOPK - 1)
            e1 = jnp.zeros_like(ci)
            e2 = jnp.zeros_like(ci)
            for a in range(PEER_TOPK):
                e1 = jnp.where(ca == a, i0[a:a + 1, :], e1)
                e2 = jnp.where(cb == a, i1[a:a + 1, :], e2)
            et_ref[h] = e1 * N_KEYS + e2
            ex = jnp.exp(cv - cv[0:1, :])
            gt_ref[h] = ex / jnp.sum(ex, axis=0, keepdims=True)
        return carry

    lax.fori_loop(0, PEER_HEADS // par, head, 0)
    tt = et_ref.shape[2]
    et = lax.bitcast_convert_type(et_ref[...].reshape(PEER_PICKS, tt), F32)
    idx_ref[...] = lax.bitcast_convert_type(et.T, jnp.int32)
    gate_ref[...] = gt_ref[...].reshape(PEER_PICKS, tt).T


def _route_call(fo, so, x, mod, g_post, g_pre, wof, wos, wq, keys, b0=0, nb=None):
    _, S, _ = x.shape
    B = x.shape[0] if nb is None else nb
    tt = min(ROUTE_TILE, S)
    per = S // tt
    row_in = lambda w: pl.BlockSpec((1, tt, w), lambda b, i: (b + b0, i, 0))
    row = lambda w: pl.BlockSpec((1, tt, w), lambda b, i: (b, i, 0))
    full = lambda a: pl.BlockSpec(a.shape, lambda b, i: (0,) * a.ndim)
    tok = pl.BlockSpec((tt, PEER_PICKS), lambda b, i: (b * per + i, 0))
    flat = [a * PEER_TOPK + b for a, nb in enumerate(_cand_counts()) for b in range(nb)]
    flat += [_NO_ID] * (_CAND_ROWS - len(flat))
    cid = jnp.asarray(np.broadcast_to(np.asarray(flat, np.int32)[:, None], (_CAND_ROWS, tt)))
    return pl.pallas_call(
        _route_kernel,
        grid=(B, per),
        in_specs=[row_in(_NQ), row_in(_NS), row_in(D_MODEL),
                  pl.BlockSpec((1, N_MOD, D_MODEL), lambda b, i: (b + b0, 0, 0)),
                  full(g_post), full(g_pre), full(wof), full(wos), full(wq), full(keys), full(cid)],
        out_specs=[row(D_MODEL), row(D_MODEL), tok, tok],
        out_shape=[jax.ShapeDtypeStruct((B, S, D_MODEL), F32),
                   jax.ShapeDtypeStruct((B, S, D_MODEL), F32),
                   jax.ShapeDtypeStruct((B * S, PEER_PICKS), jnp.int32),
                   jax.ShapeDtypeStruct((B * S, PEER_PICKS), F32)],
        scratch_shapes=[pltpu.VMEM((2 * PEER_HEADS, tt, PEER_HALF), BF16),
                        pltpu.VMEM((2 * PEER_HEADS, PEER_TOPK, tt), F32),
                        pltpu.VMEM((2 * PEER_HEADS, PEER_TOPK, tt), jnp.int32),
                        pltpu.VMEM((PEER_HEADS, PEER_TOPK, tt), jnp.int32),
                        pltpu.VMEM((PEER_HEADS, PEER_TOPK, tt), F32),
                        pltpu.VMEM((2, _CAND_ROWS, tt), F32)],
        compiler_params=_params("arbitrary", "arbitrary"),
        name="route",
    )(fo, so, x, mod, g_post, g_pre, wof, wos, wq, keys, cid)


def _peer_consts():
    half = TABLE_ROWS // 2
    gsum = np.zeros((PEER_GROUP, PEER_GROUP * TABLE_ROWS), np.float32)
    ev = np.zeros((PEER_PICKS, TABLE_COLS), np.float32)
    for j in range(PEER_GROUP):
        gsum[j, j * TABLE_ROWS:j * TABLE_ROWS + half] = 1.0
    for k in range(PEER_PICKS):
        ev[k, k * TABLE_ROWS + half:(k + 1) * TABLE_ROWS] = 1.0
    return jnp.asarray(gsum, BF16), jnp.asarray(ev, BF16)


def _peer_kernel(idx_hbm, tab_hbm, hs_ref, g_ref, gsum_ref, ev_ref, y_ref,
                 buf0, buf1, buf2, buf3, ib0, ib1, wr_ref, sem_g, sem_i):
    s = pl.program_id(0)
    last = pl.num_programs(0) - 1
    tt = PEER_TILE
    npt = tt * PEER_PICKS
    half = TABLE_ROWS // 2
    ngrp = PEER_PICKS // PEER_GROUP
    grows = PEER_GROUP * TABLE_ROWS
    bufs = (buf0, buf1, buf2, buf3)
    ibs = (ib0, ib1)

    def idx_fetch(tile, j):
        return pltpu.make_async_copy(idx_hbm.at[pl.ds(tile * npt, npt)], ibs[j], sem_i.at[j])

    def rows_done(j):
        return pltpu.make_async_copy(tab_hbm.at[pl.ds(0, npt)], bufs[j], sem_g.at[j])

    def issue_rows(ib, buf, sem):
        c = ib[0] >> 31
        for i in range(npt):
            e = ib[i] + c
            pltpu.make_async_copy(tab_hbm.at[e], buf.at[i], sem).start(priority=i % 2)
            if i % PEER_CHAIN == PEER_CHAIN - 1:
                c = e >> 31

    @pl.when(s == 0)
    def _():
        for j in range(2):
            first = idx_fetch(j, j)
            first.start()
            first.wait()

            def body(i, carry):
                pltpu.make_async_copy(tab_hbm.at[ibs[j][i]], bufs[j].at[i], sem_g.at[j]).start()
                return carry

            lax.fori_loop(0, npt, body, 0)
        idx_fetch(2, 0).start()

    row8 = lax.broadcasted_iota(jnp.int32, (half, TABLE_COLS), 0)
    col8 = lax.broadcasted_iota(jnp.int32, (half, TABLE_COLS), 1) % TABLE_ROWS
    mask_v = col8 == row8 + half
    lane = lax.broadcasted_iota(jnp.int32, (PEER_PICKS, HEAD_PAD), 1)
    gsum = gsum_ref[...]

    def evaluate(cur, tok0):
        def ubody(it, at):
            for u in range(PEER_UNROLL):
                t = it * PEER_UNROLL + u
                hrow = hs_ref[tok0 + t]
                h16 = jnp.concatenate([hrow, jnp.zeros_like(hrow)], axis=0).astype(BF16)
                tw = cur[pl.ds(t * PEER_PICKS, PEER_PICKS)]
                prod = (tw * h16[None]).reshape(TABLE_COLS, HEAD_PAD)
                parts = [_dot(gsum, prod[g * grows:(g + 1) * grows]) for g in range(ngrp)]
                z = jnp.sum(jnp.concatenate(parts, axis=0), axis=1, keepdims=True)
                at = jnp.where(lane == t, z, at)
            return at

        at = jnp.zeros((PEER_PICKS, HEAD_PAD), F32)
        for it in range(tt // PEER_UNROLL):
            at = ubody(it, at)
        a = at.T[:tt]
        w = jax.nn.gelu(a) * g_ref[tok0:tok0 + tt, :]
        wr_ref[...] = _dot(w.astype(BF16), ev_ref[...])

        def vbody(it, carry):
            for u in range(PEER_UNROLL):
                t = it * PEER_UNROLL + u
                wrow = jnp.broadcast_to(wr_ref[pl.ds(t, 1), :], (half, TABLE_COLS))
                wexp = jnp.where(mask_v, wrow, 0.0).astype(BF16)
                wb = cur[pl.ds(t * PEER_PICKS, PEER_PICKS)].reshape(TABLE_COLS, HEAD_PAD)
                y_ref[tok0 + t] = _dot(wexp, wb)
            return carry

        for it in range(tt // PEER_UNROLL):
            vbody(it, 0)

    for p in range(PEER_PHASES):
        k = PEER_PHASES * s + p
        idx_fetch(k + 2, p % 2).wait()
        idx_fetch(k + 3, (p + 1) % 2).start()
        rows_done(p).wait()
        issue_rows(ibs[p % 2], bufs[(p + 2) % PEER_PHASES], sem_g.at[(p + 2) % PEER_PHASES])
        evaluate(bufs[p], p * tt)

    @pl.when(s == last)
    def _():
        rows_done(0).wait()
        rows_done(1).wait()
        idx_fetch(0, 0).wait()


def _peer_call(idx_flat, table, hs3, gates, T):
    tt = PEER_TILE
    npt = tt * PEER_PICKS
    assert tt % PEER_UNROLL == 0 and PEER_PICKS % PEER_GROUP == 0 and tt <= HEAD_PAD and PEER_PHASES == 4
    gsum, ev = _peer_consts()
    short = max(0, (T + 3 * tt) * PEER_PICKS - idx_flat.shape[0])
    idx_pad = jnp.pad(idx_flat, (0, short)) if short else idx_flat
    step = PEER_PHASES * tt
    full = lambda a: pl.BlockSpec(a.shape, lambda s: (0,) * a.ndim)
    rows = pltpu.VMEM((npt, TABLE_ROWS, HEAD_PAD), BF16)
    return pl.pallas_call(
        _peer_kernel,
        grid=(T // step,),
        in_specs=[pl.BlockSpec(memory_space=pl.ANY),
                  pl.BlockSpec(memory_space=pl.ANY),
                  pl.BlockSpec((step, 8, HEAD_PAD), lambda s: (s, 0, 0)),
                  pl.BlockSpec((step, PEER_PICKS), lambda s: (s, 0)),
                  full(gsum), full(ev)],
        out_specs=pl.BlockSpec((step, 8, HEAD_PAD), lambda s: (s, 0, 0)),
        out_shape=jax.ShapeDtypeStruct((T, 8, HEAD_PAD), F32),
        scratch_shapes=[rows, rows, rows, rows,
                        pltpu.SMEM((npt,), jnp.int32),
                        pltpu.SMEM((npt,), jnp.int32),
                        pltpu.VMEM((tt, TABLE_COLS), F32),
                        pltpu.SemaphoreType.DMA((PEER_PHASES,)),
                        pltpu.SemaphoreType.DMA((2,))],
        compiler_params=_params("arbitrary"),
        name="peer",
    )(idx_pad, table, hs3, gates, gsum, ev)


SC_LANES = 16
SC_GROUP = 16
SC_UNROLL = 2
_GELU_C = 0.7978845608028654


def _peer_sc_kernel(t1, idx_hbm, tab_hbm, h_hbm, g_hbm, y_hbm,
                    idx_a, idx_b, g_a, g_b, h_a, h_b, o_v, rows_v, sem, sem_in):
    ncores = lax.axis_size("c")
    wid = lax.axis_index("s") * ncores + lax.axis_index("c")
    per = y_hbm.shape[0] // (ncores * lax.axis_size("s"))
    base = t1 + wid * per
    nchunk = D_MODEL // SC_LANES
    ngrp = PEER_PICKS // SC_GROUP
    lane = lax.iota(jnp.int32, SC_LANES)
    sets = ((idx_a, g_a, h_a), (idx_b, g_b, h_b))

    def fetch(t, k):
        return (pltpu.make_async_copy(idx_hbm.at[t], sets[k][0], sem_in.at[k]),
                pltpu.make_async_copy(g_hbm.at[t], sets[k][1], sem_in.at[k]),
                pltpu.make_async_copy(h_hbm.at[t], sets[k][2], sem_in.at[k]))

    def gather(k, g, slot):
        return pltpu.make_async_copy(tab_hbm.at[sets[k][0].at[pl.ds(g * SC_GROUP, SC_GROUP)]], rows_v.at[slot],
                                     sem.at[slot])

    def process(t, k, t_next):
        g_v, h_v = sets[k][1], sets[k][2]

        def zero(c, carry):
            o_v[pl.ds(c * SC_LANES, SC_LANES)] = jnp.zeros((SC_LANES,), F32)
            return carry

        lax.fori_loop(0, nchunk, zero, 0)

        for g in range(ngrp):
            slot = g % 2
            if g + 1 < ngrp:
                gather(k, g + 1, 1 - slot).start()
            else:
                for c in fetch(t_next, 1 - k):
                    c.wait()
                gather(1 - k, 0, 0).start()
            gather(k, g, slot).wait()

            def dot_body(c, accs):
                hv = h_v[pl.ds(c * SC_LANES, SC_LANES)]
                out = []
                for p in range(SC_GROUP):
                    w = rows_v[slot, p, pl.ds(c * SC_LANES, SC_LANES)]
                    u = lax.bitcast_convert_type(w << 16, F32)
                    out.append(accs[p] + u * hv)
                return tuple(out)

            accs = plsc.parallel_loop(0, nchunk, unroll=SC_UNROLL,
                                      carry=tuple(jnp.zeros((SC_LANES,), F32) for _ in range(SC_GROUP)))(dot_body)
            a = jnp.zeros((SC_LANES,), F32)
            for p in range(SC_GROUP):
                a = jnp.where(lane == p, jnp.sum(accs[p]), a)
            z = _GELU_C * (a + 0.044715 * a * a * a)
            th = 1.0 - 2.0 / (jnp.exp(2.0 * z) + 1.0)
            wv = 0.5 * a * (1.0 + th) * g_v[pl.ds(g * SC_GROUP, SC_GROUP)]
            ws = [jnp.sum(jnp.where(lane == p, wv, 0.0)) for p in range(SC_GROUP)]

            def ax_body(c):
                o = o_v[pl.ds(c * SC_LANES, SC_LANES)]
                for p in range(SC_GROUP):
                    w = rows_v[slot, p, pl.ds(c * SC_LANES, SC_LANES)]
                    v = lax.bitcast_convert_type(w & jnp.int32(-65536), F32)
                    o = o + ws[p] * v
                o_v[pl.ds(c * SC_LANES, SC_LANES)] = o

            plsc.parallel_loop(0, nchunk, unroll=SC_UNROLL)(ax_body)

        pltpu.sync_copy(o_v, y_hbm.at[t - t1])

    for c in fetch(base, 0):
        c.start()
    for c in fetch(base, 0):
        c.wait()
    gather(0, 0, 0).start()

    def pair(j, carry):
        t0 = base + 2 * j
        t2 = jnp.minimum(t0 + 2, base + per - 1)
        for c in fetch(t0 + 1, 1):
            c.start()
        process(t0, 0, t0 + 1)
        for c in fetch(t2, 0):
            c.start()
        process(t0 + 1, 1, t2)
        return carry

    lax.fori_loop(0, per // 2, pair, 0)
    gather(0, 0, 0).wait()


def _peer_sc_call(idx2, table_i32, h2, gates, t1):
    ts = h2.shape[0] - t1
    mesh = plsc.VectorSubcoreMesh(core_axis_name="c", subcore_axis_name="s")
    run = pl.kernel(
        functools.partial(_peer_sc_kernel, t1),
        out_type=jax.ShapeDtypeStruct((ts, D_MODEL), F32),
        mesh=mesh,
        scratch_types=[pltpu.VMEM((PEER_PICKS,), jnp.int32), pltpu.VMEM((PEER_PICKS,), jnp.int32),
                       pltpu.VMEM((PEER_PICKS,), F32), pltpu.VMEM((PEER_PICKS,), F32),
                       pltpu.VMEM((D_MODEL,), F32), pltpu.VMEM((D_MODEL,), F32),
                       pltpu.VMEM((D_MODEL,), F32),
                       pltpu.VMEM((2, SC_GROUP, D_MODEL), jnp.int32),
                       pltpu.SemaphoreType.DMA((2,)),
                       pltpu.SemaphoreType.DMA((2,))],
        compiler_params=pltpu.CompilerParams(needs_layout_passes=False),
        name="peer_sc",
    )
    return run(idx2, table_i32, h2, gates)


def _final_kernel(nba, n1, xa_ref, xb_ref, ya_ref, yb_ref, mod_ref, g_ref, o_ref):
    b = pl.program_id(0)
    gi = b * pl.num_programs(1) + pl.program_id(1)
    x1 = jnp.where(b < nba, xa_ref[0], xb_ref[0])
    y = jnp.where(gi < n1, ya_ref[...], yb_ref[...])
    gt2 = mod_ref[0, 5:6, :]
    o_ref[0] = x1 + gt2 * (_rms(y) * g_ref[...])


def _final_call(xa, xb, ya, yb, mod, g_post):
    nba, S, _ = xa.shape
    B = mod.shape[0]
    tr = min(ROW_TILE, S)
    per = S // tr
    assert ya.shape[0] % tr == 0 and yb.shape[0] % tr == 0
    n1 = ya.shape[0] // tr
    row = pl.BlockSpec((1, tr, D_MODEL), lambda b, i: (b, i, 0))
    xfirst = pl.BlockSpec((1, tr, D_MODEL), lambda b, i: (jnp.minimum(b, nba - 1), i, 0))
    xrest = pl.BlockSpec((1, tr, D_MODEL), lambda b, i: (jnp.maximum(b - nba, 0), i, 0))
    first = pl.BlockSpec((tr, D_MODEL), lambda b, i: (jnp.minimum(b * per + i, n1 - 1), 0))
    rest = pl.BlockSpec((tr, D_MODEL), lambda b, i: (jnp.maximum(b * per + i - n1, 0), 0))
    return pl.pallas_call(
        functools.partial(_final_kernel, nba, n1),
        grid=(B, per),
        in_specs=[xfirst, xrest, first, rest, pl.BlockSpec((1, N_MOD, D_MODEL), lambda b, i: (b, 0, 0)),
                  pl.BlockSpec((1, D_MODEL), lambda b, i: (0, 0))],
        out_specs=row,
        out_shape=jax.ShapeDtypeStruct((B, S, D_MODEL), F32),
        compiler_params=_params("arbitrary", "arbitrary"),
        name="final",
    )(xa, xb, ya, yb, mod, g_post)


def _pad_heads_cols(w, nh, scale=1.0):
    k = w.shape[0]
    w = (w * scale).reshape(k, nh, HEAD_DIM)
    return jnp.pad(w, ((0, 0), (0, 0), (0, HEAD_PAD - HEAD_DIM))).reshape(k, nh * HEAD_PAD)


def _pad_heads_rows(w, nh):
    n = w.shape[1]
    w = w.reshape(nh, HEAD_DIM, n)
    return jnp.pad(w, ((0, 0), (0, HEAD_PAD - HEAD_DIM), (0, 0))).reshape(nh * HEAD_PAD, n)


def _layer(x, c8, w_ada, b_ada, g_pre_mix, g_post_mix, g_pre_ffn, g_post_ffn,
           w_in, b_fgate, swa_sinks, w_out, w_query, sub_keys, w_u, w_v):
    B, S, D = x.shape
    T = B * S
    scale = HEAD_DIM ** -0.5
    mod = _ada_call(c8, w_ada, b_ada)[:B].reshape(B, N_MOD, D)

    o = 0
    parts = []
    for nh, sc in ((FOX_HEADS, scale), (FOX_HEADS, 1.0), (FOX_HEADS, 1.0)):
        parts.append(_pad_heads_cols(w_in[:, o:o + nh * HEAD_DIM], nh, sc))
        o += nh * HEAD_DIM
    parts.append(jnp.pad(w_in[:, o:o + FOX_HEADS], ((0, 0), (0, HEAD_PAD - FOX_HEADS))))
    o += FOX_HEADS
    for nh, sc in ((SWA_HEADS, scale), (SWA_KV_HEADS, 1.0), (SWA_KV_HEADS, 1.0)):
        parts.append(_pad_heads_cols(w_in[:, o:o + nh * HEAD_DIM], nh, sc))
        o += nh * HEAD_DIM
    w_all = jnp.concatenate(parts, axis=1).astype(BF16)
    bf_pad = jnp.pad(b_fgate, (0, HEAD_PAD - FOX_HEADS)).reshape(1, HEAD_PAD)

    qt, kp, vt, sq, sk, sv = _inproj_call(x, mod, g_pre_mix.reshape(1, D), w_all, bf_pad)
    fo = _fox_call(qt, kp, vt)
    so = _swa_call(swa_sinks, sq, sk, sv)

    nf = FOX_HEADS * HEAD_DIM
    wof = _pad_heads_rows(w_out[:nf], FOX_HEADS).astype(BF16)
    wos = _pad_heads_rows(w_out[nf:], SWA_HEADS).astype(BF16)
    keys = sub_keys.reshape(2 * PEER_HEADS, N_KEYS, PEER_HALF).astype(BF16)
    route = functools.partial(_route_call, fo, so, x, mod, g_post_mix.reshape(1, D), g_pre_ffn.reshape(1, D),
                              wof, wos, w_query.astype(BF16), keys)
    gpf = g_post_ffn.reshape(1, D)

    ub, vb = w_u.astype(BF16), w_v.astype(BF16)
    table = jnp.concatenate([ub.reshape(N_EXPERTS, 8, HEAD_PAD), vb.reshape(N_EXPERTS, 8, HEAD_PAD)], axis=1)

    def tc_experts(idx, h2, gates):
        n = idx.shape[0]
        return _peer_call(idx.reshape(n * PEER_PICKS), table, h2.reshape(n, 8, HEAD_PAD), gates, n).reshape(n, D)

    nba = int(B * SC_SHARE)
    if not nba:
        x1, h2, idx, gates = route()
        y = tc_experts(idx, h2, gates)
        return _final_call(x1, x1, y, y, mod, gpf)

    def bf16_bits(w):
        b = lax.bitcast_convert_type(w, jnp.uint32)
        return (b + jnp.uint32(0x7FFF) + ((b >> 16) & jnp.uint32(1))) >> 16

    table_i32 = lax.bitcast_convert_type((bf16_bits(w_v) << 16) | bf16_bits(w_u), jnp.int32)
    xa, ha, ia, ga = route(b0=0, nb=nba)
    y_sc = _peer_sc_call(ia, table_i32, ha.reshape(nba * S, D), ga, 0)
    mod_b = mod + 0.0 * ga[0, 0]
    xb, hb, ib, gb = _route_call(fo, so, x, mod_b, g_post_mix.reshape(1, D), g_pre_ffn.reshape(1, D),
                                 wof, wos, w_query.astype(BF16), keys, b0=nba, nb=B - nba)
    y_tc = tc_experts(ib, hb, gb)
    return _final_call(xa, xb, y_sc, y_tc, mod, gpf)


def kernel(x, c, w_ada, b_ada, g_pre_mix, g_post_mix, g_pre_ffn, g_post_ffn, w_in, b_fgate, swa_sinks, w_out,
           w_query, sub_keys, w_u, w_v):
    B = x.shape[0]
    c8 = jnp.pad(c, ((0, 8 - B), (0, 0)))
    for l in range(w_ada.shape[0]):
        x = _layer(x, c8, w_ada[l], b_ada[l], g_pre_mix[l], g_post_mix[l], g_pre_ffn[l], g_post_ffn[l],
                   w_in[l], b_fgate[l], swa_sinks[l], w_out[l], w_query[l], sub_keys[l], w_u[l], w_v[l])
    return x
```

```python
import functools

import numpy as np
import jax
import jax.numpy as jnp
from jax import lax
from jax.experimental import pallas as pl
from jax.experimental.pallas import tpu as pltpu
from jax.experimental.pallas import tpu_sc as plsc

F32 = jnp.float32
BF16 = jnp.bfloat16

D_MODEL = 1024
HEAD_DIM = 64
HEAD_PAD = 128
FOX_HEADS = 8
SWA_HEADS = 8
SWA_KV_HEADS = 2
SWA_GROUP = SWA_HEADS // SWA_KV_HEADS
WINDOW = 128
PEER_HEADS = 8
PEER_HALF = 128
N_KEYS = 128
N_EXPERTS = N_KEYS * N_KEYS
PEER_TOPK = 16
PEER_PICKS = PEER_HEADS * PEER_TOPK
N_MOD = 6
RMS_EPS = 1e-6
NEG_INF = -1e30

_F_LANE = HEAD_DIM
_ONE_LANE = HEAD_DIM

ROW_TILE = 1024
FOX_TILE = 1024
SWA_TILE = 512
ROUTE_TILE = 256
PEER_TILE = 8
PEER_PHASES = 4
PEER_UNROLL = 8
PEER_CHAIN = 8
PEER_GROUP = 16
SC_SHARE = 0.47
TABLE_ROWS = 16
TABLE_COLS = PEER_PICKS * TABLE_ROWS

_VMEM_LIMIT = 56 * 1024 * 1024


def _dot(a, b):
    return jnp.dot(a, b, preferred_element_type=F32)


def _dot_nt(a, b):
    return lax.dot_general(a, b, (((1,), (1,)), ((), ())), preferred_element_type=F32)


def _split3(x):
    hi = x.astype(BF16)
    r = x - hi.astype(F32)
    mid = r.astype(BF16)
    lo = (r - mid.astype(F32)).astype(BF16)
    return hi, mid, lo


def _rms(x):
    return x * lax.rsqrt(jnp.mean(x * x, axis=-1, keepdims=True) + RMS_EPS)


def _params(*sem):
    return pltpu.CompilerParams(dimension_semantics=sem, vmem_limit_bytes=_VMEM_LIMIT)


def _ada_kernel(c_ref, w_ref, b_ref, o_ref):
    c = c_ref[...]
    s = (c * jax.nn.sigmoid(c)).astype(BF16)
    o_ref[...] = _dot(s, w_ref[...].astype(BF16)) + b_ref[...]


def _ada_call(c8, w_ada, b_ada):
    n = w_ada.shape[1]
    tn = 1536
    return pl.pallas_call(
        _ada_kernel,
        grid=(n // tn,),
        in_specs=[pl.BlockSpec((8, D_MODEL), lambda j: (0, 0)),
                  pl.BlockSpec((D_MODEL, tn), lambda j: (0, j)),
                  pl.BlockSpec((1, tn), lambda j: (0, j))],
        out_specs=pl.BlockSpec((8, tn), lambda j: (0, j)),
        out_shape=jax.ShapeDtypeStruct((8, n), F32),
        compiler_params=_params("arbitrary"),
        name="ada",
    )(c8, w_ada, b_ada.reshape(1, n))


_NQ = FOX_HEADS * HEAD_PAD
_NS = SWA_HEADS * HEAD_PAD
_NKV = SWA_KV_HEADS * HEAD_PAD
_IN_COLS = 3 * _NQ + HEAD_PAD + _NS + 2 * _NKV


def _inproj_kernel(x_ref, mod_ref, g_ref, w_ref, bf_ref, tri_ref, pq_ref, pk_ref, cst_ref,
                   qt_ref, kp_ref, vt_ref, sq_ref, sk_ref, sv_ref, carry_ref):
    i = pl.program_id(1)

    @pl.when(i == 0)
    def _():
        carry_ref[...] = jnp.zeros_like(carry_ref)

    x = x_ref[0]
    sh1 = mod_ref[0, 0:1, :]
    sc1 = mod_ref[0, 1:2, :]
    h = _rms(x) * g_ref[...] * (1.0 + sc1) + sh1
    proj = _dot(h.astype(BF16), w_ref[...])

    z = proj[:, 3 * _NQ:3 * _NQ + HEAD_PAD] + bf_ref[...]
    ls = jnp.minimum(z, 0.0) - jnp.log(1.0 + jnp.exp(-jnp.abs(z)))
    tri = tri_ref[...]
    hi, mid, lo = _split3(ls)
    fcum = _dot(tri, hi) + _dot(tri, mid) + _dot(tri, lo) + carry_ref[...]
    carry_ref[...] = fcum[fcum.shape[0] - 1:, :]

    fh, fm, fl = _split3(fcum)
    eq = _dot(fh, pq_ref[0]) + _dot(fm, pq_ref[1]) + _dot(fl, pq_ref[2]) + cst_ref[0:1, :]
    ek = _dot(fh, pk_ref[0]) + _dot(fm, pk_ref[1]) + _dot(fl, pk_ref[2]) + cst_ref[1:2, :]
    qt_ref[0, 0] = (proj[:, 0:_NQ] + eq).T.astype(BF16)
    kp_ref[0] = (proj[:, _NQ:2 * _NQ] + ek).astype(BF16)
    vt_ref[0, 0] = (proj[:, 2 * _NQ:3 * _NQ] + cst_ref[2:3, :]).T.astype(BF16)
    o = 3 * _NQ + HEAD_PAD
    sq_ref[0] = proj[:, o:o + _NS].astype(BF16)
    sk_ref[0] = proj[:, o + _NS:o + _NS + _NKV].astype(BF16)
    sv_ref[0] = proj[:, o + _NS + _NKV:o + _NS + 2 * _NKV].astype(BF16)


def _inproj_consts(tr):
    tri = np.tril(np.ones((tr, tr), np.float32))
    pq = np.zeros((3, HEAD_PAD, _NQ), np.float32)
    pk = np.zeros((3, HEAD_PAD, _NQ), np.float32)
    cst = np.zeros((8, _NQ), np.float32)
    for h in range(FOX_HEADS):
        b = h * HEAD_PAD + _F_LANE
        for j in range(3):
            pq[j, h, b + j] = 1.0
            pk[j, h, b + 3 + j] = -1.0
            cst[0, b + 3 + j] = 1.0
            cst[1, b + j] = 1.0
        cst[2, h * HEAD_PAD + _ONE_LANE] = 1.0
    return (jnp.asarray(tri, BF16), jnp.asarray(pq, BF16), jnp.asarray(pk, BF16), jnp.asarray(cst, F32))


def _inproj_call(x, mod, g_pre, w_all, bf_pad):
    B, S, _ = x.shape
    tr = min(ROW_TILE, S)
    n = S // tr
    tri, pq, pk, cst = _inproj_consts(tr)
    row = lambda w: pl.BlockSpec((1, tr, w), lambda b, i: (b, i, 0))
    slab = pl.BlockSpec((1, 1, _NQ, tr), lambda b, i: (b, i, 0, 0))
    full = lambda a: pl.BlockSpec(a.shape, lambda b, i: (0,) * a.ndim)
    outs = [jax.ShapeDtypeStruct((B, n, _NQ, tr), BF16), jax.ShapeDtypeStruct((B, S, _NQ), BF16),
            jax.ShapeDtypeStruct((B, n, _NQ, tr), BF16), jax.ShapeDtypeStruct((B, S, _NS), BF16),
            jax.ShapeDtypeStruct((B, S, _NKV), BF16), jax.ShapeDtypeStruct((B, S, _NKV), BF16)]
    return pl.pallas_call(
        _inproj_kernel,
        grid=(B, n),
        in_specs=[row(D_MODEL),
                  pl.BlockSpec((1, N_MOD, D_MODEL), lambda b, i: (b, 0, 0)),
                  full(g_pre), full(w_all), full(bf_pad), full(tri), full(pq), full(pk), full(cst)],
        out_specs=[slab, row(_NQ), slab, row(_NS), row(_NKV), row(_NKV)],
        out_shape=outs,
        scratch_shapes=[pltpu.VMEM((1, HEAD_PAD), F32)],
        compiler_params=_params("arbitrary", "arbitrary"),
        name="inproj",
    )(x, mod, g_pre, w_all, bf_pad, tri, pq, pk, cst)


def _fox_kernel(qt_ref, k_ref, vt_ref, o_ref, m_ref, acc_ref, sa_ref, sb_ref):
    i = pl.program_id(2)
    t = o_ref.shape[1]
    m_ref[...] = jnp.full_like(m_ref, NEG_INF)
    acc_ref[...] = jnp.zeros_like(acc_ref)
    bufs = (sa_ref, sb_ref)

    def scores(j, dst):
        off = pl.multiple_of(j * t, t)
        dst[...] = _dot(k_ref[0, pl.ds(off, t), :], qt_ref[0, 0])

    def absorb(j, src, masked):
        s = src[...]
        if masked:
            r = lax.broadcasted_iota(jnp.int32, s.shape, 0)
            c = lax.broadcasted_iota(jnp.int32, s.shape, 1)
            s = jnp.where(r <= c, s, NEG_INF)
        m_prev = m_ref[...]
        m_new = jnp.maximum(m_prev, jnp.max(s, axis=0, keepdims=True))
        p = jnp.exp(s - m_new)
        acc_ref[...] = jnp.exp(m_prev - m_new) * acc_ref[...] + _dot(vt_ref[0, j], p.astype(BF16))
        m_ref[...] = m_new

    scores(0, sa_ref)

    def pair(jj, carry):
        j = 2 * jj
        scores(j + 1, sb_ref)
        absorb(j, sa_ref, False)
        scores(j + 2, sa_ref)
        absorb(j + 1, sb_ref, False)
        return carry

    lax.fori_loop(0, i // 2, pair, 0)

    @pl.when(i % 2 == 1)
    def _():
        scores(i, sb_ref)
        absorb(i - 1, sa_ref, False)
        absorb(i, sb_ref, True)

    @pl.when(i % 2 == 0)
    def _():
        absorb(i, sa_ref, True)

    acc = acc_ref[...]
    o_ref[0] = (acc / acc[_ONE_LANE:_ONE_LANE + 1, :]).T.astype(BF16)


def _fox_call(qt, kp, vt):
    B, n, _, t = qt.shape
    S = n * t
    w = HEAD_PAD
    return pl.pallas_call(
        _fox_kernel,
        grid=(B, FOX_HEADS, n),
        in_specs=[pl.BlockSpec((1, 1, w, t), lambda b, h, i: (b, i, h, 0)),
                  pl.BlockSpec((1, S, w), lambda b, h, i: (b, 0, h)),
                  pl.BlockSpec((1, n, w, t), lambda b, h, i: (b, 0, h, 0))],
        out_specs=pl.BlockSpec((1, t, w), lambda b, h, i: (b, i, h)),
        out_shape=jax.ShapeDtypeStruct((B, S, _NQ), BF16),
        scratch_shapes=[pltpu.VMEM((1, t), F32), pltpu.VMEM((HEAD_PAD, t), F32),
                        pltpu.VMEM((t, t), F32), pltpu.VMEM((t, t), F32)],
        compiler_params=_params("arbitrary", "arbitrary", "arbitrary"),
        name="fox",
    )(qt, kp, vt)


def _swa_kernel(sink_ref, q_ref, kc_ref, kp_ref, vc_ref, vp_ref, o_ref):
    i = pl.program_id(1)
    nsub = q_ref.shape[1] // WINDOW
    r = lax.broadcasted_iota(jnp.int32, (WINDOW, 2 * WINDOW), 0)
    j = lax.broadcasted_iota(jnp.int32, (WINDOW, 2 * WINDOW), 1)
    dist = r + WINDOW - j
    valid = (dist >= 0) & (dist < WINDOW)
    distf = dist.astype(F32)
    for qb in range(nsub):
        rows = slice(qb * WINDOW, (qb + 1) * WINDOW)
        if qb == 0:
            ok = valid & ((j >= WINDOW) | (i > 0))
        else:
            ok = valid
        for g in range(SWA_KV_HEADS):
            lanes = slice(g * HEAD_PAD, (g + 1) * HEAD_PAD)
            if qb == 0:
                kprev, vprev = kp_ref[0, :, lanes], vp_ref[0, :, lanes]
            else:
                prev = slice((qb - 1) * WINDOW, qb * WINDOW)
                kprev, vprev = kc_ref[0, prev, lanes], vc_ref[0, prev, lanes]
            kk = jnp.concatenate([kprev, kc_ref[0, rows, lanes]], axis=0)
            vv = jnp.concatenate([vprev, vc_ref[0, rows, lanes]], axis=0)
            for u in range(SWA_GROUP):
                hq = g * SWA_GROUP + u
                slope = 2.0 ** (-8.0 * (hq + 1) / SWA_HEADS)
                q = q_ref[0, rows, hq * HEAD_PAD:(hq + 1) * HEAD_PAD]
                s = _dot_nt(q, kk) - slope * distf
                s = jnp.where(ok, s, NEG_INF)
                sink = sink_ref[hq]
                m = jnp.maximum(jnp.max(s, axis=1, keepdims=True), sink)
                p = jnp.exp(s - m)
                den = jnp.sum(p, axis=1, keepdims=True) + jnp.exp(sink - m)
                o = _dot(p.astype(BF16), vv) / den
                o_ref[0, rows, hq * HEAD_PAD:(hq + 1) * HEAD_PAD] = o.astype(BF16)


def _swa_call(sinks, sq, sk, sv):
    B, S, _ = sq.shape
    t = min(SWA_TILE, S)
    per = t // WINDOW
    cur = lambda w: pl.BlockSpec((1, t, w), lambda b, i: (b, i, 0))
    prv = lambda w: pl.BlockSpec((1, WINDOW, w), lambda b, i: (b, jnp.maximum(i * per - 1, 0), 0))
    return pl.pallas_call(
        _swa_kernel,
        grid=(B, S // t),
        in_specs=[pl.BlockSpec(memory_space=pltpu.SMEM),
                  cur(_NS), cur(_NKV), prv(_NKV), cur(_NKV), prv(_NKV)],
        out_specs=cur(_NS),
        out_shape=jax.ShapeDtypeStruct((B, S, _NS), BF16),
        compiler_params=_params("arbitrary", "arbitrary"),
        name="swa",
    )(sinks, sq, sk, sk, sv, sv)


_NO_ID = 1 << 20


def _topk_rows(s, k, val_ref, idx_ref, ids=None):
    if ids is None:
        ids = lax.broadcasted_iota(jnp.int32, s.shape, 0)
    for r in range(k):
        m = jnp.max(s, axis=0, keepdims=True)
        i = jnp.min(jnp.where(s == m, ids, _NO_ID), axis=0, keepdims=True)
        val_ref[pl.ds(r, 1), :] = m
        idx_ref[pl.ds(r, 1), :] = i
        if r + 1 < k:
            s = jnp.where(ids == i, -jnp.inf, s)


def _cand_counts():
    return [PEER_TOPK // (a + 1) for a in range(PEER_TOPK)]


_CAND_ROWS = 56


def _route_kernel(fo_ref, so_ref, x_ref, mod_ref, gpost_ref, gpre_ref, wof_ref, wos_ref, wq_ref, keys_ref, cid_ref,
                  x1_ref, h2_ref, idx_ref, gate_ref, qs_ref, sv_ref, si_ref, et_ref, gt_ref, cand_ref):
    x = x_ref[0]
    gt1 = mod_ref[0, 2:3, :]
    sh2 = mod_ref[0, 3:4, :]
    sc2 = mod_ref[0, 4:5, :]
    y = _dot(fo_ref[0], wof_ref[...]) + _dot(so_ref[0], wos_ref[...])
    x1 = x + gt1 * (_rms(y) * gpost_ref[...])
    x1_ref[0] = x1
    h2 = _rms(x1) * gpre_ref[...] * (1.0 + sc2) + sh2
    h2_ref[0] = h2
    qp = _dot(h2.astype(BF16), wq_ref[...])
    nhp = 2 * PEER_HEADS
    for hp in range(nhp):
        qs_ref[hp] = qp[:, hp * PEER_HALF:(hp + 1) * PEER_HALF].astype(BF16)

    def half(q, carry):
        for u in range(4):
            hp = 4 * q + u
            sc = _dot_nt(keys_ref[hp], qs_ref[hp])
            _topk_rows(sc, PEER_TOPK, sv_ref.at[hp], si_ref.at[hp])
        return carry

    lax.fori_loop(0, nhp // 4, half, 0)

    counts = _cand_counts()
    used = sum(counts)
    par = cand_ref.shape[0]
    for u in range(par):
        cand_ref[u, pl.ds(used, _CAND_ROWS - used), :] = jnp.full((_CAND_ROWS - used, cand_ref.shape[2]), -jnp.inf, F32)
    cid = cid_ref[...]

    def head(hh, carry):
        for u in range(par):
            h = par * hh + u
            v0, v1 = sv_ref[2 * h], sv_ref[2 * h + 1]
            i0, i1 = si_ref[2 * h], si_ref[2 * h + 1]
            off = 0
            for a, nb in enumerate(counts):
                cand_ref[u, pl.ds(off, nb), :] = v0[a:a + 1, :] + v1[0:nb, :]
                off += nb
            _topk_rows(cand_ref[u], PEER_TOPK, gt_ref.at[h], et_ref.at[h], ids=cid)
            cv, ci = gt_ref[h], et_ref[h]
            ca, cb = ci >> 4, ci & (PEER_TOPK - 1)
            e1 = jnp.zeros_like(ci)
            e2 = jnp.zeros_like(ci)
            for a in range(PEER_TOPK):
                e1 = jnp.where(ca == a, i0[a:a + 1, :], e1)
                e2 = jnp.where(cb == a, i1[a:a + 1, :], e2)
            et_ref[h] = e1 * N_KEYS + e2
            ex = jnp.exp(cv - cv[0:1, :])
            gt_ref[h] = ex / jnp.sum(ex, axis=0, keepdims=True)
        return carry

    lax.fori_loop(0, PEER_HEADS // par, head, 0)
    tt = et_ref.shape[2]
    et = lax.bitcast_convert_type(et_ref[...].reshape(PEER_PICKS, tt), F32)
    idx_ref[...] = lax.bitcast_convert_type(et.T, jnp.int32)
    gate_ref[...] = gt_ref[...].reshape(PEER_PICKS, tt).T


def _route_call(fo, so, x, mod, g_post, g_pre, wof, wos, wq, keys):
    B, S, _ = x.shape
    tt = min(ROUTE_TILE, S)
    per = S // tt
    row = lambda w: pl.BlockSpec((1, tt, w), lambda b, i: (b, i, 0))
    full = lambda a: pl.BlockSpec(a.shape, lambda b, i: (0,) * a.ndim)
    tok = pl.BlockSpec((tt, PEER_PICKS), lambda b, i: (b * per + i, 0))
    flat = [a * PEER_TOPK + b for a, nb in enumerate(_cand_counts()) for b in range(nb)]
    flat += [_NO_ID] * (_CAND_ROWS - len(flat))
    cid = jnp.asarray(np.broadcast_to(np.asarray(flat, np.int32)[:, None], (_CAND_ROWS, tt)))
    return pl.pallas_call(
        _route_kernel,
        grid=(B, per),
        in_specs=[row(_NQ), row(_NS), row(D_MODEL),
                  pl.BlockSpec((1, N_MOD, D_MODEL), lambda b, i: (b, 0, 0)),
                  full(g_post), full(g_pre), full(wof), full(wos), full(wq), full(keys), full(cid)],
        out_specs=[row(D_MODEL), row(D_MODEL), tok, tok],
        out_shape=[jax.ShapeDtypeStruct((B, S, D_MODEL), F32),
                   jax.ShapeDtypeStruct((B, S, D_MODEL), F32),
                   jax.ShapeDtypeStruct((B * S, PEER_PICKS), jnp.int32),
                   jax.ShapeDtypeStruct((B * S, PEER_PICKS), F32)],
        scratch_shapes=[pltpu.VMEM((2 * PEER_HEADS, tt, PEER_HALF), BF16),
                        pltpu.VMEM((2 * PEER_HEADS, PEER_TOPK, tt), F32),
                        pltpu.VMEM((2 * PEER_HEADS, PEER_TOPK, tt), jnp.int32),
                        pltpu.VMEM((PEER_HEADS, PEER_TOPK, tt), jnp.int32),
                        pltpu.VMEM((PEER_HEADS, PEER_TOPK, tt), F32),
                        pltpu.VMEM((2, _CAND_ROWS, tt), F32)],
        compiler_params=_params("arbitrary", "arbitrary"),
        name="route",
    )(fo, so, x, mod, g_post, g_pre, wof, wos, wq, keys, cid)


def _peer_consts():
    half = TABLE_ROWS // 2
    gsum = np.zeros((PEER_GROUP, PEER_GROUP * TABLE_ROWS), np.float32)
    ev = np.zeros((PEER_PICKS, TABLE_COLS), np.float32)
    for j in range(PEER_GROUP):
        gsum[j, j * TABLE_ROWS:j * TABLE_ROWS + half] = 1.0
    for k in range(PEER_PICKS):
        ev[k, k * TABLE_ROWS + half:(k + 1) * TABLE_ROWS] = 1.0
    return jnp.asarray(gsum, BF16), jnp.asarray(ev, BF16)


def _peer_kernel(idx_hbm, tab_hbm, hs_ref, g_ref, gsum_ref, ev_ref, y_ref,
                 buf0, buf1, buf2, buf3, ib0, ib1, wr_ref, sem_g, sem_i):
    s = pl.program_id(0)
    last = pl.num_programs(0) - 1
    tt = PEER_TILE
    npt = tt * PEER_PICKS
    half = TABLE_ROWS // 2
    ngrp = PEER_PICKS // PEER_GROUP
    grows = PEER_GROUP * TABLE_ROWS
    bufs = (buf0, buf1, buf2, buf3)
    ibs = (ib0, ib1)

    def idx_fetch(tile, j):
        return pltpu.make_async_copy(idx_hbm.at[pl.ds(tile * npt, npt)], ibs[j], sem_i.at[j])

    def rows_done(j):
        return pltpu.make_async_copy(tab_hbm.at[pl.ds(0, npt)], bufs[j], sem_g.at[j])

    def issue_rows(ib, buf, sem):
        c = ib[0] >> 31
        for i in range(npt):
            e = ib[i] + c
            pltpu.make_async_copy(tab_hbm.at[e], buf.at[i], sem).start(priority=i % 2)
            if i % PEER_CHAIN == PEER_CHAIN - 1:
                c = e >> 31

    @pl.when(s == 0)
    def _():
        for j in range(2):
            first = idx_fetch(j, j)
            first.start()
            first.wait()

            def body(i, carry):
                pltpu.make_async_copy(tab_hbm.at[ibs[j][i]], bufs[j].at[i], sem_g.at[j]).start()
                return carry

            lax.fori_loop(0, npt, body, 0)
        idx_fetch(2, 0).start()

    row8 = lax.broadcasted_iota(jnp.int32, (half, TABLE_COLS), 0)
    col8 = lax.broadcasted_iota(jnp.int32, (half, TABLE_COLS), 1) % TABLE_ROWS
    mask_v = col8 == row8 + half
    lane = lax.broadcasted_iota(jnp.int32, (PEER_PICKS, HEAD_PAD), 1)
    gsum = gsum_ref[...]

    def evaluate(cur, tok0):
        def ubody(it, at):
            for u in range(PEER_UNROLL):
                t = it * PEER_UNROLL + u
                hrow = hs_ref[tok0 + t]
                h16 = jnp.concatenate([hrow, jnp.zeros_like(hrow)], axis=0).astype(BF16)
                tw = cur[pl.ds(t * PEER_PICKS, PEER_PICKS)]
                prod = (tw * h16[None]).reshape(TABLE_COLS, HEAD_PAD)
                parts = [_dot(gsum, prod[g * grows:(g + 1) * grows]) for g in range(ngrp)]
                z = jnp.sum(jnp.concatenate(parts, axis=0), axis=1, keepdims=True)
                at = jnp.where(lane == t, z, at)
            return at

        at = jnp.zeros((PEER_PICKS, HEAD_PAD), F32)
        for it in range(tt // PEER_UNROLL):
            at = ubody(it, at)
        a = at.T[:tt]
        w = jax.nn.gelu(a) * g_ref[tok0:tok0 + tt, :]
        wr_ref[...] = _dot(w.astype(BF16), ev_ref[...])

        def vbody(it, carry):
            for u in range(PEER_UNROLL):
                t = it * PEER_UNROLL + u
                wrow = jnp.broadcast_to(wr_ref[pl.ds(t, 1), :], (half, TABLE_COLS))
                wexp = jnp.where(mask_v, wrow, 0.0).astype(BF16)
                wb = cur[pl.ds(t * PEER_PICKS, PEER_PICKS)].reshape(TABLE_COLS, HEAD_PAD)
                y_ref[tok0 + t] = _dot(wexp, wb)
            return carry

        for it in range(tt // PEER_UNROLL):
            vbody(it, 0)

    for p in range(PEER_PHASES):
        k = PEER_PHASES * s + p
        idx_fetch(k + 2, p % 2).wait()
        idx_fetch(k + 3, (p + 1) % 2).start()
        rows_done(p).wait()
        issue_rows(ibs[p % 2], bufs[(p + 2) % PEER_PHASES], sem_g.at[(p + 2) % PEER_PHASES])
        evaluate(bufs[p], p * tt)

    @pl.when(s == last)
    def _():
        rows_done(0).wait()
        rows_done(1).wait()
        idx_fetch(0, 0).wait()


def _peer_call(idx_flat, table, hs3, gates, T):
    tt = PEER_TILE
    npt = tt * PEER_PICKS
    assert tt % PEER_UNROLL == 0 and PEER_PICKS % PEER_GROUP == 0 and tt <= HEAD_PAD and PEER_PHASES == 4
    gsum, ev = _peer_consts()
    short = max(0, (T + 3 * tt) * PEER_PICKS - idx_flat.shape[0])
    idx_pad = jnp.pad(idx_flat, (0, short)) if short else idx_flat
    step = PEER_PHASES * tt
    full = lambda a: pl.BlockSpec(a.shape, lambda s: (0,) * a.ndim)
    rows = pltpu.VMEM((npt, TABLE_ROWS, HEAD_PAD), BF16)
    return pl.pallas_call(
        _peer_kernel,
        grid=(T // step,),
        in_specs=[pl.BlockSpec(memory_space=pl.ANY),
                  pl.BlockSpec(memory_space=pl.ANY),
                  pl.BlockSpec((step, 8, HEAD_PAD), lambda s: (s, 0, 0)),
                  pl.BlockSpec((step, PEER_PICKS), lambda s: (s, 0)),
                  full(gsum), full(ev)],
        out_specs=pl.BlockSpec((step, 8, HEAD_PAD), lambda s: (s, 0, 0)),
        out_shape=jax.ShapeDtypeStruct((T, 8, HEAD_PAD), F32),
        scratch_shapes=[rows, rows, rows, rows,
                        pltpu.SMEM((npt,), jnp.int32),
                        pltpu.SMEM((npt,), jnp.int32),
                        pltpu.VMEM((tt, TABLE_COLS), F32),
                        pltpu.SemaphoreType.DMA((PEER_PHASES,)),
                        pltpu.SemaphoreType.DMA((2,))],
        compiler_params=_params("arbitrary"),
        name="peer",
    )(idx_pad, table, hs3, gates, gsum, ev)


SC_LANES = 16
SC_GROUP = 16
SC_UNROLL = 1
_GELU_C = 0.7978845608028654


def _peer_sc_kernel(t1, idx_hbm, tab_hbm, h_hbm, g_hbm, y_hbm,
                    idx_a, idx_b, g_a, g_b, h_a, h_b, o_v, rows_v, sem, sem_in):
    ncores = lax.axis_size("c")
    wid = lax.axis_index("s") * ncores + lax.axis_index("c")
    per = y_hbm.shape[0] // (ncores * lax.axis_size("s"))
    base = t1 + wid * per
    nchunk = D_MODEL // SC_LANES
    ngrp = PEER_PICKS // SC_GROUP
    lane = lax.iota(jnp.int32, SC_LANES)
    sets = ((idx_a, g_a, h_a), (idx_b, g_b, h_b))

    def fetch(t, k):
        return (pltpu.make_async_copy(idx_hbm.at[t], sets[k][0], sem_in.at[k]),
                pltpu.make_async_copy(g_hbm.at[t], sets[k][1], sem_in.at[k]),
                pltpu.make_async_copy(h_hbm.at[t], sets[k][2], sem_in.at[k]))

    def gather(k, g, slot):
        return pltpu.make_async_copy(tab_hbm.at[sets[k][0].at[pl.ds(g * SC_GROUP, SC_GROUP)]], rows_v.at[slot],
                                     sem.at[slot])

    def process(t, k, t_next):
        g_v, h_v = sets[k][1], sets[k][2]

        def zero(c, carry):
            o_v[pl.ds(c * SC_LANES, SC_LANES)] = jnp.zeros((SC_LANES,), F32)
            return carry

        lax.fori_loop(0, nchunk, zero, 0)

        for g in range(ngrp):
            slot = g % 2
            if g + 1 < ngrp:
                gather(k, g + 1, 1 - slot).start()
            else:
                for c in fetch(t_next, 1 - k):
                    c.wait()
                gather(1 - k, 0, 0).start()
            gather(k, g, slot).wait()

            def dot_body(c, accs):
                hv = h_v[pl.ds(c * SC_LANES, SC_LANES)]
                out = []
                for p in range(SC_GROUP):
                    w = rows_v[slot, p, pl.ds(c * SC_LANES, SC_LANES)]
                    u = lax.bitcast_convert_type(w << 16, F32)
                    out.append(accs[p] + u * hv)
                return tuple(out)

            accs = plsc.parallel_loop(0, nchunk, unroll=SC_UNROLL,
                                      carry=tuple(jnp.zeros((SC_LANES,), F32) for _ in range(SC_GROUP)))(dot_body)
            a = jnp.zeros((SC_LANES,), F32)
            for p in range(SC_GROUP):
                a = jnp.where(lane == p, jnp.sum(accs[p]), a)
            z = _GELU_C * (a + 0.044715 * a * a * a)
            th = 1.0 - 2.0 / (jnp.exp(2.0 * z) + 1.0)
            wv = 0.5 * a * (1.0 + th) * g_v[pl.ds(g * SC_GROUP, SC_GROUP)]
            ws = [jnp.sum(jnp.where(lane == p, wv, 0.0)) for p in range(SC_GROUP)]

            def ax_body(c):
                o = o_v[pl.ds(c * SC_LANES, SC_LANES)]
                for p in range(SC_GROUP):
                    w = rows_v[slot, p, pl.ds(c * SC_LANES, SC_LANES)]
                    v = lax.bitcast_convert_type(w & jnp.int32(-65536), F32)
                    o = o + ws[p] * v
                o_v[pl.ds(c * SC_LANES, SC_LANES)] = o

            plsc.parallel_loop(0, nchunk, unroll=SC_UNROLL)(ax_body)

        pltpu.sync_copy(o_v, y_hbm.at[t - t1])

    for c in fetch(base, 0):
        c.start()
    for c in fetch(base, 0):
        c.wait()
    gather(0, 0, 0).start()

    def pair(j, carry):
        t0 = base + 2 * j
        t2 = jnp.minimum(t0 + 2, base + per - 1)
        for c in fetch(t0 + 1, 1):
            c.start()
        process(t0, 0, t0 + 1)
        for c in fetch(t2, 0):
            c.start()
        process(t0 + 1, 1, t2)
        return carry

    lax.fori_loop(0, per // 2, pair, 0)
    gather(0, 0, 0).wait()


def _peer_sc_call(idx2, table_i32, h2, gates, t1):
    ts = h2.shape[0] - t1
    mesh = plsc.VectorSubcoreMesh(core_axis_name="c", subcore_axis_name="s")
    run = pl.kernel(
        functools.partial(_peer_sc_kernel, t1),
        out_type=jax.ShapeDtypeStruct((ts, D_MODEL), F32),
        mesh=mesh,
        scratch_types=[pltpu.VMEM((PEER_PICKS,), jnp.int32), pltpu.VMEM((PEER_PICKS,), jnp.int32),
                       pltpu.VMEM((PEER_PICKS,), F32), pltpu.VMEM((PEER_PICKS,), F32),
                       pltpu.VMEM((D_MODEL,), F32), pltpu.VMEM((D_MODEL,), F32),
                       pltpu.VMEM((D_MODEL,), F32),
                       pltpu.VMEM((2, SC_GROUP, D_MODEL), jnp.int32),
                       pltpu.SemaphoreType.DMA((2,)),
                       pltpu.SemaphoreType.DMA((2,))],
        compiler_params=pltpu.CompilerParams(needs_layout_passes=False),
        name="peer_sc",
    )
    return run(idx2, table_i32, h2, gates)


def _final_kernel(n1, x1_ref, ya_ref, yb_ref, mod_ref, g_ref, o_ref):
    gi = pl.program_id(0) * pl.num_programs(1) + pl.program_id(1)
    y = jnp.where(gi < n1, ya_ref[...], yb_ref[...])
    gt2 = mod_ref[0, 5:6, :]
    o_ref[0] = x1_ref[0] + gt2 * (_rms(y) * g_ref[...])


def _final_call(x1, ya, yb, mod, g_post):
    B, S, _ = x1.shape
    tr = min(ROW_TILE, S)
    per = S // tr
    assert ya.shape[0] % tr == 0 and yb.shape[0] % tr == 0
    n1 = ya.shape[0] // tr
    row = pl.BlockSpec((1, tr, D_MODEL), lambda b, i: (b, i, 0))
    first = pl.BlockSpec((tr, D_MODEL), lambda b, i: (jnp.minimum(b * per + i, n1 - 1), 0))
    rest = pl.BlockSpec((tr, D_MODEL), lambda b, i: (jnp.maximum(b * per + i - n1, 0), 0))
    return pl.pallas_call(
        functools.partial(_final_kernel, n1),
        grid=(B, per),
        in_specs=[row, first, rest, pl.BlockSpec((1, N_MOD, D_MODEL), lambda b, i: (b, 0, 0)),
                  pl.BlockSpec((1, D_MODEL), lambda b, i: (0, 0))],
        out_specs=row,
        out_shape=jax.ShapeDtypeStruct((B, S, D_MODEL), F32),
        compiler_params=_params("arbitrary", "arbitrary"),
        name="final",
    )(x1, ya, yb, mod, g_post)


def _pad_heads_cols(w, nh, scale=1.0):
    k = w.shape[0]
    w = (w * scale).reshape(k, nh, HEAD_DIM)
    return jnp.pad(w, ((0, 0), (0, 0), (0, HEAD_PAD - HEAD_DIM))).reshape(k, nh * HEAD_PAD)


def _pad_heads_rows(w, nh):
    n = w.shape[1]
    w = w.reshape(nh, HEAD_DIM, n)
    return jnp.pad(w, ((0, 0), (0, HEAD_PAD - HEAD_DIM), (0, 0))).reshape(nh * HEAD_PAD, n)


def _layer(x, c8, w_ada, b_ada, g_pre_mix, g_post_mix, g_pre_ffn, g_post_ffn,
           w_in, b_fgate, swa_sinks, w_out, w_query, sub_keys, w_u, w_v):
    B, S, D = x.shape
    T = B * S
    scale = HEAD_DIM ** -0.5
    mod = _ada_call(c8, w_ada, b_ada)[:B].reshape(B, N_MOD, D)

    o = 0
    parts = []
    for nh, sc in ((FOX_HEADS, scale), (FOX_HEADS, 1.0), (FOX_HEADS, 1.0)):
        parts.append(_pad_heads_cols(w_in[:, o:o + nh * HEAD_DIM], nh, sc))
        o += nh * HEAD_DIM
    parts.append(jnp.pad(w_in[:, o:o + FOX_HEADS], ((0, 0), (0, HEAD_PAD - FOX_HEADS))))
    o += FOX_HEADS
    for nh, sc in ((SWA_HEADS, scale), (SWA_KV_HEADS, 1.0), (SWA_KV_HEADS, 1.0)):
        parts.append(_pad_heads_cols(w_in[:, o:o + nh * HEAD_DIM], nh, sc))
        o += nh * HEAD_DIM
    w_all = jnp.concatenate(parts, axis=1).astype(BF16)
    bf_pad = jnp.pad(b_fgate, (0, HEAD_PAD - FOX_HEADS)).reshape(1, HEAD_PAD)

    qt, kp, vt, sq, sk, sv = _inproj_call(x, mod, g_pre_mix.reshape(1, D), w_all, bf_pad)
    fo = _fox_call(qt, kp, vt)
    so = _swa_call(swa_sinks, sq, sk, sv)

    nf = FOX_HEADS * HEAD_DIM
    wof = _pad_heads_rows(w_out[:nf], FOX_HEADS).astype(BF16)
    wos = _pad_heads_rows(w_out[nf:], SWA_HEADS).astype(BF16)
    keys = sub_keys.reshape(2 * PEER_HEADS, N_KEYS, PEER_HALF).astype(BF16)
    x1, h2, idx, gates = _route_call(fo, so, x, mod, g_post_mix.reshape(1, D), g_pre_ffn.reshape(1, D),
                                     wof, wos, w_query.astype(BF16), keys)

    ub, vb = w_u.astype(BF16), w_v.astype(BF16)
    table = jnp.concatenate([ub.reshape(N_EXPERTS, 8, HEAD_PAD), vb.reshape(N_EXPERTS, 8, HEAD_PAD)], axis=1)
    h2f = h2.reshape(T, D)
    tile = min(ROW_TILE, S)
    ts = (int(T * SC_SHARE) // tile) * tile
    t1 = T - ts
    y_tc = _peer_call(idx.reshape(T * PEER_PICKS), table, h2f.reshape(T, 8, HEAD_PAD), gates, t1).reshape(t1, D)
    if not ts:
        return _final_call(x1, y_tc, y_tc, mod, g_post_ffn.reshape(1, D))
    def bf16_bits(w):
        b = lax.bitcast_convert_type(w, jnp.uint32)
        return (b + jnp.uint32(0x7FFF) + ((b >> 16) & jnp.uint32(1))) >> 16

    table_i32 = lax.bitcast_convert_type((bf16_bits(w_v) << 16) | bf16_bits(w_u), jnp.int32)
    y_sc = _peer_sc_call(idx, table_i32, h2f, gates, t1)
    return _final_call(x1, y_tc, y_sc, mod, g_post_ffn.reshape(1, D))


def kernel(x, c, w_ada, b_ada, g_pre_mix, g_post_mix, g_pre_ffn, g_post_ffn, w_in, b_fgate, swa_sinks, w_out,
           w_query, sub_keys, w_u, w_v):
    B = x.shape[0]
    c8 = jnp.pad(c, ((0, 8 - B), (0, 0)))
    for l in range(w_ada.shape[0]):
        x = _layer(x, c8, w_ada[l], b_ada[l], g_pre_mix[l], g_post_mix[l], g_pre_ffn[l], g_post_ffn[l],
                   w_in[l], b_fgate[l], swa_sinks[l], w_out[l], w_query[l], sub_keys[l], w_u[l], w_v[l])
    return x
```

```python
import functools

import numpy as np
import jax
import jax.numpy as jnp
from jax import lax
from jax.experimental import pallas as pl
from jax.experimental.pallas import tpu as pltpu
from jax.experimental.pallas import tpu_sc as plsc

F32 = jnp.float32
BF16 = jnp.bfloat16

D_MODEL = 1024
HEAD_DIM = 64
HEAD_PAD = 128
FOX_HEADS = 8
SWA_HEADS = 8
SWA_KV_HEADS = 2
SWA_GROUP = SWA_HEADS // SWA_KV_HEADS
WINDOW = 128
PEER_HEADS = 8
PEER_HALF = 128
N_KEYS = 128
N_EXPERTS = N_KEYS * N_KEYS
PEER_TOPK = 16
PEER_PICKS = PEER_HEADS * PEER_TOPK
N_MOD = 6
RMS_EPS = 1e-6
NEG_INF = -1e30

_F_LANE = HEAD_DIM
_ONE_LANE = HEAD_DIM

ROW_TILE = 1024
FOX_TILE = 1024
SWA_TILE = 512
ROUTE_TILE = 512
PEER_TILE = 8
PEER_PHASES = 4
PEER_UNROLL = 8
PEER_CHAIN = 8
PEER_GROUP = 16
SC_SHARE = 0.47
TABLE_ROWS = 16
TABLE_COLS = PEER_PICKS * TABLE_ROWS

_VMEM_LIMIT = 56 * 1024 * 1024


def _dot(a, b):
    return jnp.dot(a, b, preferred_element_type=F32)


def _dot_nt(a, b):
    return lax.dot_general(a, b, (((1,), (1,)), ((), ())), preferred_element_type=F32)


def _split3(x):
    hi = x.astype(BF16)
    r = x - hi.astype(F32)
    mid = r.astype(BF16)
    lo = (r - mid.astype(F32)).astype(BF16)
    return hi, mid, lo


def _rms(x):
    return x * lax.rsqrt(jnp.mean(x * x, axis=-1, keepdims=True) + RMS_EPS)


def _params(*sem):
    return pltpu.CompilerParams(dimension_semantics=sem, vmem_limit_bytes=_VMEM_LIMIT)


def _ada_kernel(c_ref, w_ref, b_ref, o_ref):
    c = c_ref[...]
    s = (c * jax.nn.sigmoid(c)).astype(BF16)
    o_ref[...] = _dot(s, w_ref[...].astype(BF16)) + b_ref[...]


def _ada_call(c8, w_ada, b_ada):
    n = w_ada.shape[1]
    tn = 1536
    return pl.pallas_call(
        _ada_kernel,
        grid=(n // tn,),
        in_specs=[pl.BlockSpec((8, D_MODEL), lambda j: (0, 0)),
                  pl.BlockSpec((D_MODEL, tn), lambda j: (0, j)),
                  pl.BlockSpec((1, tn), lambda j: (0, j))],
        out_specs=pl.BlockSpec((8, tn), lambda j: (0, j)),
        out_shape=jax.ShapeDtypeStruct((8, n), F32),
        compiler_params=_params("arbitrary"),
        name="ada",
    )(c8, w_ada, b_ada.reshape(1, n))


_NQ = FOX_HEADS * HEAD_PAD
_NS = SWA_HEADS * HEAD_PAD
_NKV = SWA_KV_HEADS * HEAD_PAD
_IN_COLS = 3 * _NQ + HEAD_PAD + _NS + 2 * _NKV


def _inproj_kernel(x_ref, mod_ref, g_ref, w_ref, bf_ref, tri_ref, pq_ref, pk_ref, cst_ref,
                   qt_ref, kp_ref, vt_ref, sq_ref, sk_ref, sv_ref, carry_ref):
    i = pl.program_id(1)

    @pl.when(i == 0)
    def _():
        carry_ref[...] = jnp.zeros_like(carry_ref)

    x = x_ref[0]
    sh1 = mod_ref[0, 0:1, :]
    sc1 = mod_ref[0, 1:2, :]
    h = _rms(x) * g_ref[...] * (1.0 + sc1) + sh1
    proj = _dot(h.astype(BF16), w_ref[...])

    z = proj[:, 3 * _NQ:3 * _NQ + HEAD_PAD] + bf_ref[...]
    ls = jnp.minimum(z, 0.0) - jnp.log(1.0 + jnp.exp(-jnp.abs(z)))
    tri = tri_ref[...]
    hi, mid, lo = _split3(ls)
    fcum = _dot(tri, hi) + _dot(tri, mid) + _dot(tri, lo) + carry_ref[...]
    carry_ref[...] = fcum[fcum.shape[0] - 1:, :]

    fh, fm, fl = _split3(fcum)
    eq = _dot(fh, pq_ref[0]) + _dot(fm, pq_ref[1]) + _dot(fl, pq_ref[2]) + cst_ref[0:1, :]
    ek = _dot(fh, pk_ref[0]) + _dot(fm, pk_ref[1]) + _dot(fl, pk_ref[2]) + cst_ref[1:2, :]
    qt_ref[0, 0] = (proj[:, 0:_NQ] + eq).T.astype(BF16)
    kp_ref[0] = (proj[:, _NQ:2 * _NQ] + ek).astype(BF16)
    vt_ref[0, 0] = (proj[:, 2 * _NQ:3 * _NQ] + cst_ref[2:3, :]).T.astype(BF16)
    o = 3 * _NQ + HEAD_PAD
    sq_ref[0] = proj[:, o:o + _NS].astype(BF16)
    sk_ref[0] = proj[:, o + _NS:o + _NS + _NKV].astype(BF16)
    sv_ref[0] = proj[:, o + _NS + _NKV:o + _NS + 2 * _NKV].astype(BF16)


def _inproj_consts(tr):
    tri = np.tril(np.ones((tr, tr), np.float32))
    pq = np.zeros((3, HEAD_PAD, _NQ), np.float32)
    pk = np.zeros((3, HEAD_PAD, _NQ), np.float32)
    cst = np.zeros((8, _NQ), np.float32)
    for h in range(FOX_HEADS):
        b = h * HEAD_PAD + _F_LANE
        for j in range(3):
            pq[j, h, b + j] = 1.0
            pk[j, h, b + 3 + j] = -1.0
            cst[0, b + 3 + j] = 1.0
            cst[1, b + j] = 1.0
        cst[2, h * HEAD_PAD + _ONE_LANE] = 1.0
    return (jnp.asarray(tri, BF16), jnp.asarray(pq, BF16), jnp.asarray(pk, BF16), jnp.asarray(cst, F32))


def _inproj_call(x, mod, g_pre, w_all, bf_pad):
    B, S, _ = x.shape
    tr = min(ROW_TILE, S)
    n = S // tr
    tri, pq, pk, cst = _inproj_consts(tr)
    row = lambda w: pl.BlockSpec((1, tr, w), lambda b, i: (b, i, 0))
    slab = pl.BlockSpec((1, 1, _NQ, tr), lambda b, i: (b, i, 0, 0))
    full = lambda a: pl.BlockSpec(a.shape, lambda b, i: (0,) * a.ndim)
    outs = [jax.ShapeDtypeStruct((B, n, _NQ, tr), BF16), jax.ShapeDtypeStruct((B, S, _NQ), BF16),
            jax.ShapeDtypeStruct((B, n, _NQ, tr), BF16), jax.ShapeDtypeStruct((B, S, _NS), BF16),
            jax.ShapeDtypeStruct((B, S, _NKV), BF16), jax.ShapeDtypeStruct((B, S, _NKV), BF16)]
    return pl.pallas_call(
        _inproj_kernel,
        grid=(B, n),
        in_specs=[row(D_MODEL),
                  pl.BlockSpec((1, N_MOD, D_MODEL), lambda b, i: (b, 0, 0)),
                  full(g_pre), full(w_all), full(bf_pad), full(tri), full(pq), full(pk), full(cst)],
        out_specs=[slab, row(_NQ), slab, row(_NS), row(_NKV), row(_NKV)],
        out_shape=outs,
        scratch_shapes=[pltpu.VMEM((1, HEAD_PAD), F32)],
        compiler_params=_params("arbitrary", "arbitrary"),
        name="inproj",
    )(x, mod, g_pre, w_all, bf_pad, tri, pq, pk, cst)


def _fox_kernel(qt_ref, k_ref, vt_ref, o_ref, m_ref, acc_ref, sa_ref, sb_ref):
    i = pl.program_id(2)
    t = o_ref.shape[1]
    m_ref[...] = jnp.full_like(m_ref, NEG_INF)
    acc_ref[...] = jnp.zeros_like(acc_ref)
    bufs = (sa_ref, sb_ref)

    def scores(j, dst):
        off = pl.multiple_of(j * t, t)
        dst[...] = _dot(k_ref[0, pl.ds(off, t), :], qt_ref[0, 0])

    def absorb(j, src, masked):
        s = src[...]
        if masked:
            r = lax.broadcasted_iota(jnp.int32, s.shape, 0)
            c = lax.broadcasted_iota(jnp.int32, s.shape, 1)
            s = jnp.where(r <= c, s, NEG_INF)
        m_prev = m_ref[...]
        m_new = jnp.maximum(m_prev, jnp.max(s, axis=0, keepdims=True))
        p = jnp.exp(s - m_new)
        acc_ref[...] = jnp.exp(m_prev - m_new) * acc_ref[...] + _dot(vt_ref[0, j], p.astype(BF16))
        m_ref[...] = m_new

    scores(0, sa_ref)

    def pair(jj, carry):
        j = 2 * jj
        scores(j + 1, sb_ref)
        absorb(j, sa_ref, False)
        scores(j + 2, sa_ref)
        absorb(j + 1, sb_ref, False)
        return carry

    lax.fori_loop(0, i // 2, pair, 0)

    @pl.when(i % 2 == 1)
    def _():
        scores(i, sb_ref)
        absorb(i - 1, sa_ref, False)
        absorb(i, sb_ref, True)

    @pl.when(i % 2 == 0)
    def _():
        absorb(i, sa_ref, True)

    acc = acc_ref[...]
    o_ref[0] = (acc / acc[_ONE_LANE:_ONE_LANE + 1, :]).T.astype(BF16)


def _fox_call(qt, kp, vt):
    B, n, _, t = qt.shape
    S = n * t
    w = HEAD_PAD
    return pl.pallas_call(
        _fox_kernel,
        grid=(B, FOX_HEADS, n),
        in_specs=[pl.BlockSpec((1, 1, w, t), lambda b, h, i: (b, i, h, 0)),
                  pl.BlockSpec((1, S, w), lambda b, h, i: (b, 0, h)),
                  pl.BlockSpec((1, n, w, t), lambda b, h, i: (b, 0, h, 0))],
        out_specs=pl.BlockSpec((1, t, w), lambda b, h, i: (b, i, h)),
        out_shape=jax.ShapeDtypeStruct((B, S, _NQ), BF16),
        scratch_shapes=[pltpu.VMEM((1, t), F32), pltpu.VMEM((HEAD_PAD, t), F32),
                        pltpu.VMEM((t, t), F32), pltpu.VMEM((t, t), F32)],
        compiler_params=_params("arbitrary", "arbitrary", "arbitrary"),
        name="fox",
    )(qt, kp, vt)


def _swa_kernel(sink_ref, q_ref, kc_ref, kp_ref, vc_ref, vp_ref, o_ref):
    i = pl.program_id(1)
    nsub = q_ref.shape[1] // WINDOW
    r = lax.broadcasted_iota(jnp.int32, (WINDOW, 2 * WINDOW), 0)
    j = lax.broadcasted_iota(jnp.int32, (WINDOW, 2 * WINDOW), 1)
    dist = r + WINDOW - j
    valid = (dist >= 0) & (dist < WINDOW)
    distf = dist.astype(F32)
    for qb in range(nsub):
        rows = slice(qb * WINDOW, (qb + 1) * WINDOW)
        if qb == 0:
            ok = valid & ((j >= WINDOW) | (i > 0))
        else:
            ok = valid
        for g in range(SWA_KV_HEADS):
            lanes = slice(g * HEAD_PAD, (g + 1) * HEAD_PAD)
            if qb == 0:
                kprev, vprev = kp_ref[0, :, lanes], vp_ref[0, :, lanes]
            else:
                prev = slice((qb - 1) * WINDOW, qb * WINDOW)
                kprev, vprev = kc_ref[0, prev, lanes], vc_ref[0, prev, lanes]
            kk = jnp.concatenate([kprev, kc_ref[0, rows, lanes]], axis=0)
            vv = jnp.concatenate([vprev, vc_ref[0, rows, lanes]], axis=0)
            for u in range(SWA_GROUP):
                hq = g * SWA_GROUP + u
                slope = 2.0 ** (-8.0 * (hq + 1) / SWA_HEADS)
                q = q_ref[0, rows, hq * HEAD_PAD:(hq + 1) * HEAD_PAD]
                s = _dot_nt(q, kk) - slope * distf
                s = jnp.where(ok, s, NEG_INF)
                sink = sink_ref[hq]
                m = jnp.maximum(jnp.max(s, axis=1, keepdims=True), sink)
                p = jnp.exp(s - m)
                den = jnp.sum(p, axis=1, keepdims=True) + jnp.exp(sink - m)
                o = _dot(p.astype(BF16), vv) / den
                o_ref[0, rows, hq * HEAD_PAD:(hq + 1) * HEAD_PAD] = o.astype(BF16)


def _swa_call(sinks, sq, sk, sv):
    B, S, _ = sq.shape
    t = min(SWA_TILE, S)
    per = t // WINDOW
    cur = lambda w: pl.BlockSpec((1, t, w), lambda b, i: (b, i, 0))
    prv = lambda w: pl.BlockSpec((1, WINDOW, w), lambda b, i: (b, jnp.maximum(i * per - 1, 0), 0))
    return pl.pallas_call(
        _swa_kernel,
        grid=(B, S // t),
        in_specs=[pl.BlockSpec(memory_space=pltpu.SMEM),
                  cur(_NS), cur(_NKV), prv(_NKV), cur(_NKV), prv(_NKV)],
        out_specs=cur(_NS),
        out_shape=jax.ShapeDtypeStruct((B, S, _NS), BF16),
        compiler_params=_params("arbitrary", "arbitrary"),
        name="swa",
    )(sinks, sq, sk, sk, sv, sv)


_NO_ID = 1 << 20


def _topk_rows(s, k, val_ref, idx_ref, ids=None):
    if ids is None:
        ids = lax.broadcasted_iota(jnp.int32, s.shape, 0)
    for r in range(k):
        m = jnp.max(s, axis=0, keepdims=True)
        i = jnp.min(jnp.where(s == m, ids, _NO_ID), axis=0, keepdims=True)
        val_ref[pl.ds(r, 1), :] = m
        idx_ref[pl.ds(r, 1), :] = i
        if r + 1 < k:
            s = jnp.where(ids == i, -jnp.inf, s)


def _cand_counts():
    return [PEER_TOPK // (a + 1) for a in range(PEER_TOPK)]


_CAND_ROWS = 56


def _route_kernel(fo_ref, so_ref, x_ref, mod_ref, gpost_ref, gpre_ref, wof_ref, wos_ref, wq_ref, keys_ref, cid_ref,
                  x1_ref, h2_ref, idx_ref, gate_ref, qs_ref, sv_ref, si_ref, et_ref, gt_ref, cand_ref):
    x = x_ref[0]
    gt1 = mod_ref[0, 2:3, :]
    sh2 = mod_ref[0, 3:4, :]
    sc2 = mod_ref[0, 4:5, :]
    y = _dot(fo_ref[0], wof_ref[...]) + _dot(so_ref[0], wos_ref[...])
    x1 = x + gt1 * (_rms(y) * gpost_ref[...])
    x1_ref[0] = x1
    h2 = _rms(x1) * gpre_ref[...] * (1.0 + sc2) + sh2
    h2_ref[0] = h2
    qp = _dot(h2.astype(BF16), wq_ref[...])
    nhp = 2 * PEER_HEADS
    for hp in range(nhp):
        qs_ref[hp] = qp[:, hp * PEER_HALF:(hp + 1) * PEER_HALF].astype(BF16)

    def half(q, carry):
        for u in range(4):
            hp = 4 * q + u
            sc = _dot_nt(keys_ref[hp], qs_ref[hp])
            _topk_rows(sc, PEER_TOPK, sv_ref.at[hp], si_ref.at[hp])
        return carry

    lax.fori_loop(0, nhp // 4, half, 0)

    counts = _cand_counts()
    used = sum(counts)
    par = cand_ref.shape[0]
    for u in range(par):
        cand_ref[u, pl.ds(used, _CAND_ROWS - used), :] = jnp.full((_CAND_ROWS - used, cand_ref.shape[2]), -jnp.inf, F32)
    cid = cid_ref[...]

    def head(hh, carry):
        for u in range(par):
            h = par * hh + u
            v0, v1 = sv_ref[2 * h], sv_ref[2 * h + 1]
            i0, i1 = si_ref[2 * h], si_ref[2 * h + 1]
            off = 0
            for a, nb in enumerate(counts):
                cand_ref[u, pl.ds(off, nb), :] = v0[a:a + 1, :] + v1[0:nb, :]
                off += nb
            _topk_rows(cand_ref[u], PEER_TOPK, gt_ref.at[h], et_ref.at[h], ids=cid)
            cv, ci = gt_ref[h], et_ref[h]
            ca, cb = ci >> 4, ci & (PEER_TOPK - 1)
            e1 = jnp.zeros_like(ci)
            e2 = jnp.zeros_like(ci)
            for a in range(PEER_TOPK):
                e1 = jnp.where(ca == a, i0[a:a + 1, :], e1)
                e2 = jnp.where(cb == a, i1[a:a + 1, :], e2)
            et_ref[h] = e1 * N_KEYS + e2
            ex = jnp.exp(cv - cv[0:1, :])
            gt_ref[h] = ex / jnp.sum(ex, axis=0, keepdims=True)
        return carry

    lax.fori_loop(0, PEER_HEADS // par, head, 0)
    tt = et_ref.shape[2]
    et = lax.bitcast_convert_type(et_ref[...].reshape(PEER_PICKS, tt), F32)
    idx_ref[...] = lax.bitcast_convert_type(et.T, jnp.int32)
    gate_ref[...] = gt_ref[...].reshape(PEER_PICKS, tt).T


def _route_call(fo, so, x, mod, g_post, g_pre, wof, wos, wq, keys):
    B, S, _ = x.shape
    tt = min(ROUTE_TILE, S)
    per = S // tt
    row = lambda w: pl.BlockSpec((1, tt, w), lambda b, i: (b, i, 0))
    full = lambda a: pl.BlockSpec(a.shape, lambda b, i: (0,) * a.ndim)
    tok = pl.BlockSpec((tt, PEER_PICKS), lambda b, i: (b * per + i, 0))
    flat = [a * PEER_TOPK + b for a, nb in enumerate(_cand_counts()) for b in range(nb)]
    flat += [_NO_ID] * (_CAND_ROWS - len(flat))
    cid = jnp.asarray(np.broadcast_to(np.asarray(flat, np.int32)[:, None], (_CAND_ROWS, tt)))
    return pl.pallas_call(
        _route_kernel,
        grid=(B, per),
        in_specs=[row(_NQ), row(_NS), row(D_MODEL),
                  pl.BlockSpec((1, N_MOD, D_MODEL), lambda b, i: (b, 0, 0)),
                  full(g_post), full(g_pre), full(wof), full(wos), full(wq), full(keys), full(cid)],
        out_specs=[row(D_MODEL), row(D_MODEL), tok, tok],
        out_shape=[jax.ShapeDtypeStruct((B, S, D_MODEL), F32),
                   jax.ShapeDtypeStruct((B, S, D_MODEL), F32),
                   jax.ShapeDtypeStruct((B * S, PEER_PICKS), jnp.int32),
                   jax.ShapeDtypeStruct((B * S, PEER_PICKS), F32)],
        scratch_shapes=[pltpu.VMEM((2 * PEER_HEADS, tt, PEER_HALF), BF16),
                        pltpu.VMEM((2 * PEER_HEADS, PEER_TOPK, tt), F32),
                        pltpu.VMEM((2 * PEER_HEADS, PEER_TOPK, tt), jnp.int32),
                        pltpu.VMEM((PEER_HEADS, PEER_TOPK, tt), jnp.int32),
                        pltpu.VMEM((PEER_HEADS, PEER_TOPK, tt), F32),
                        pltpu.VMEM((2, _CAND_ROWS, tt), F32)],
        compiler_params=_params("arbitrary", "arbitrary"),
        name="route",
    )(fo, so, x, mod, g_post, g_pre, wof, wos, wq, keys, cid)


def _peer_consts():
    half = TABLE_ROWS // 2
    gsum = np.zeros((PEER_GROUP, PEER_GROUP * TABLE_ROWS), np.float32)
    ev = np.zeros((PEER_PICKS, TABLE_COLS), np.float32)
    for j in range(PEER_GROUP):
        gsum[j, j * TABLE_ROWS:j * TABLE_ROWS + half] = 1.0
    for k in range(PEER_PICKS):
        ev[k, k * TABLE_ROWS + half:(k + 1) * TABLE_ROWS] = 1.0
    return jnp.asarray(gsum, BF16), jnp.asarray(ev, BF16)


def _peer_kernel(idx_hbm, tab_hbm, hs_ref, g_ref, gsum_ref, ev_ref, y_ref,
                 buf0, buf1, buf2, buf3, ib0, ib1, wr_ref, sem_g, sem_i):
    s = pl.program_id(0)
    last = pl.num_programs(0) - 1
    tt = PEER_TILE
    npt = tt * PEER_PICKS
    half = TABLE_ROWS // 2
    ngrp = PEER_PICKS // PEER_GROUP
    grows = PEER_GROUP * TABLE_ROWS
    bufs = (buf0, buf1, buf2, buf3)
    ibs = (ib0, ib1)

    def idx_fetch(tile, j):
        return pltpu.make_async_copy(idx_hbm.at[pl.ds(tile * npt, npt)], ibs[j], sem_i.at[j])

    def rows_done(j):
        return pltpu.make_async_copy(tab_hbm.at[pl.ds(0, npt)], bufs[j], sem_g.at[j])

    def issue_rows(ib, buf, sem):
        c = ib[0] >> 31
        for i in range(npt):
            e = ib[i] + c
            pltpu.make_async_copy(tab_hbm.at[e], buf.at[i], sem).start(priority=i % 2)
            if i % PEER_CHAIN == PEER_CHAIN - 1:
                c = e >> 31

    @pl.when(s == 0)
    def _():
        for j in range(2):
            first = idx_fetch(j, j)
            first.start()
            first.wait()

            def body(i, carry):
                pltpu.make_async_copy(tab_hbm.at[ibs[j][i]], bufs[j].at[i], sem_g.at[j]).start()
                return carry

            lax.fori_loop(0, npt, body, 0)
        idx_fetch(2, 0).start()

    row8 = lax.broadcasted_iota(jnp.int32, (half, TABLE_COLS), 0)
    col8 = lax.broadcasted_iota(jnp.int32, (half, TABLE_COLS), 1) % TABLE_ROWS
    mask_v = col8 == row8 + half
    lane = lax.broadcasted_iota(jnp.int32, (PEER_PICKS, HEAD_PAD), 1)
    gsum = gsum_ref[...]

    def evaluate(cur, tok0):
        def ubody(it, at):
            for u in range(PEER_UNROLL):
                t = it * PEER_UNROLL + u
                hrow = hs_ref[tok0 + t]
                h16 = jnp.concatenate([hrow, jnp.zeros_like(hrow)], axis=0).astype(BF16)
                tw = cur[pl.ds(t * PEER_PICKS, PEER_PICKS)]
                prod = (tw * h16[None]).reshape(TABLE_COLS, HEAD_PAD)
                parts = [_dot(gsum, prod[g * grows:(g + 1) * grows]) for g in range(ngrp)]
                z = jnp.sum(jnp.concatenate(parts, axis=0), axis=1, keepdims=True)
                at = jnp.where(lane == t, z, at)
            return at

        at = jnp.zeros((PEER_PICKS, HEAD_PAD), F32)
        for it in range(tt // PEER_UNROLL):
            at = ubody(it, at)
        a = at.T[:tt]
        w = jax.nn.gelu(a) * g_ref[tok0:tok0 + tt, :]
        wr_ref[...] = _dot(w.astype(BF16), ev_ref[...])

        def vbody(it, carry):
            for u in range(PEER_UNROLL):
                t = it * PEER_UNROLL + u
                wrow = jnp.broadcast_to(wr_ref[pl.ds(t, 1), :], (half, TABLE_COLS))
                wexp = jnp.where(mask_v, wrow, 0.0).astype(BF16)
                wb = cur[pl.ds(t * PEER_PICKS, PEER_PICKS)].reshape(TABLE_COLS, HEAD_PAD)
                y_ref[tok0 + t] = _dot(wexp, wb)
            return carry

        for it in range(tt // PEER_UNROLL):
            vbody(it, 0)

    for p in range(PEER_PHASES):
        k = PEER_PHASES * s + p
        idx_fetch(k + 2, p % 2).wait()
        idx_fetch(k + 3, (p + 1) % 2).start()
        rows_done(p).wait()
        issue_rows(ibs[p % 2], bufs[(p + 2) % PEER_PHASES], sem_g.at[(p + 2) % PEER_PHASES])
        evaluate(bufs[p], p * tt)

    @pl.when(s == last)
    def _():
        rows_done(0).wait()
        rows_done(1).wait()
        idx_fetch(0, 0).wait()


def _peer_call(idx_flat, table, hs3, gates, T):
    tt = PEER_TILE
    npt = tt * PEER_PICKS
    assert tt % PEER_UNROLL == 0 and PEER_PICKS % PEER_GROUP == 0 and tt <= HEAD_PAD and PEER_PHASES == 4
    gsum, ev = _peer_consts()
    short = max(0, (T + 3 * tt) * PEER_PICKS - idx_flat.shape[0])
    idx_pad = jnp.pad(idx_flat, (0, short)) if short else idx_flat
    step = PEER_PHASES * tt
    full = lambda a: pl.BlockSpec(a.shape, lambda s: (0,) * a.ndim)
    rows = pltpu.VMEM((npt, TABLE_ROWS, HEAD_PAD), BF16)
    return pl.pallas_call(
        _peer_kernel,
        grid=(T // step,),
        in_specs=[pl.BlockSpec(memory_space=pl.ANY),
                  pl.BlockSpec(memory_space=pl.ANY),
                  pl.BlockSpec((step, 8, HEAD_PAD), lambda s: (s, 0, 0)),
                  pl.BlockSpec((step, PEER_PICKS), lambda s: (s, 0)),
                  full(gsum), full(ev)],
        out_specs=pl.BlockSpec((step, 8, HEAD_PAD), lambda s: (s, 0, 0)),
        out_shape=jax.ShapeDtypeStruct((T, 8, HEAD_PAD), F32),
        scratch_shapes=[rows, rows, rows, rows,
                        pltpu.SMEM((npt,), jnp.int32),
                        pltpu.SMEM((npt,), jnp.int32),
                        pltpu.VMEM((tt, TABLE_COLS), F32),
                        pltpu.SemaphoreType.DMA((PEER_PHASES,)),
                        pltpu.SemaphoreType.DMA((2,))],
        compiler_params=_params("arbitrary"),
        name="peer",
    )(idx_pad, table, hs3, gates, gsum, ev)


SC_LANES = 16
SC_GROUP = 16
SC_UNROLL = 2
_GELU_C = 0.7978845608028654


def _peer_sc_kernel(t1, idx_hbm, tab_hbm, h_hbm, g_hbm, y_hbm,
                    idx_a, idx_b, g_a, g_b, h_a, h_b, o_v, rows_v, sem, sem_in):
    ncores = lax.axis_size("c")
    wid = lax.axis_index("s") * ncores + lax.axis_index("c")
    per = y_hbm.shape[0] // (ncores * lax.axis_size("s"))
    base = t1 + wid * per
    nchunk = D_MODEL // SC_LANES
    ngrp = PEER_PICKS // SC_GROUP
    lane = lax.iota(jnp.int32, SC_LANES)
    sets = ((idx_a, g_a, h_a), (idx_b, g_b, h_b))

    def fetch(t, k):
        return (pltpu.make_async_copy(idx_hbm.at[t], sets[k][0], sem_in.at[k]),
                pltpu.make_async_copy(g_hbm.at[t], sets[k][1], sem_in.at[k]),
                pltpu.make_async_copy(h_hbm.at[t], sets[k][2], sem_in.at[k]))

    def gather(k, g, slot):
        return pltpu.make_async_copy(tab_hbm.at[sets[k][0].at[pl.ds(g * SC_GROUP, SC_GROUP)]], rows_v.at[slot],
                                     sem.at[slot])

    def process(t, k, t_next):
        g_v, h_v = sets[k][1], sets[k][2]

        def zero(c, carry):
            o_v[pl.ds(c * SC_LANES, SC_LANES)] = jnp.zeros((SC_LANES,), F32)
            return carry

        lax.fori_loop(0, nchunk, zero, 0)

        for g in range(ngrp):
            slot = g % 2
            if g + 1 < ngrp:
                gather(k, g + 1, 1 - slot).start()
            else:
                for c in fetch(t_next, 1 - k):
                    c.wait()
                gather(1 - k, 0, 0).start()
            gather(k, g, slot).wait()

            def dot_body(c, accs):
                hv = h_v[pl.ds(c * SC_LANES, SC_LANES)]
                out = []
                for p in range(SC_GROUP):
                    w = rows_v[slot, p, pl.ds(c * SC_LANES, SC_LANES)]
                    u = lax.bitcast_convert_type(w << 16, F32)
                    out.append(accs[p] + u * hv)
                return tuple(out)

            accs = plsc.parallel_loop(0, nchunk, unroll=SC_UNROLL,
                                      carry=tuple(jnp.zeros((SC_LANES,), F32) for _ in range(SC_GROUP)))(dot_body)
            a = jnp.zeros((SC_LANES,), F32)
            for p in range(SC_GROUP):
                a = jnp.where(lane == p, jnp.sum(accs[p]), a)
            z = _GELU_C * (a + 0.044715 * a * a * a)
            th = 1.0 - 2.0 / (jnp.exp(2.0 * z) + 1.0)
            wv = 0.5 * a * (1.0 + th) * g_v[pl.ds(g * SC_GROUP, SC_GROUP)]
            ws = [jnp.sum(jnp.where(lane == p, wv, 0.0)) for p in range(SC_GROUP)]

            def ax_body(c):
                o = o_v[pl.ds(c * SC_LANES, SC_LANES)]
                for p in range(SC_GROUP):
                    w = rows_v[slot, p, pl.ds(c * SC_LANES, SC_LANES)]
                    v = lax.bitcast_convert_type(w & jnp.int32(-65536), F32)
                    o = o + ws[p] * v
                o_v[pl.ds(c * SC_LANES, SC_LANES)] = o

            plsc.parallel_loop(0, nchunk, unroll=SC_UNROLL)(ax_body)

        pltpu.sync_copy(o_v, y_hbm.at[t - t1])

    for c in fetch(base, 0):
        c.start()
    for c in fetch(base, 0):
        c.wait()
    gather(0, 0, 0).start()

    def pair(j, carry):
        t0 = base + 2 * j
        t2 = jnp.minimum(t0 + 2, base + per - 1)
        for c in fetch(t0 + 1, 1):
            c.start()
        process(t0, 0, t0 + 1)
        for c in fetch(t2, 0):
            c.start()
        process(t0 + 1, 1, t2)
        return carry

    lax.fori_loop(0, per // 2, pair, 0)
    gather(0, 0, 0).wait()


def _peer_sc_call(idx2, table_i32, h2, gates, t1):
    ts = h2.shape[0] - t1
    mesh = plsc.VectorSubcoreMesh(core_axis_name="c", subcore_axis_name="s")
    run = pl.kernel(
        functools.partial(_peer_sc_kernel, t1),
        out_type=jax.ShapeDtypeStruct((ts, D_MODEL), F32),
        mesh=mesh,
        scratch_types=[pltpu.VMEM((PEER_PICKS,), jnp.int32), pltpu.VMEM((PEER_PICKS,), jnp.int32),
                       pltpu.VMEM((PEER_PICKS,), F32), pltpu.VMEM((PEER_PICKS,), F32),
                       pltpu.VMEM((D_MODEL,), F32), pltpu.VMEM((D_MODEL,), F32),
                       pltpu.VMEM((D_MODEL,), F32),
                       pltpu.VMEM((2, SC_GROUP, D_MODEL), jnp.int32),
                       pltpu.SemaphoreType.DMA((2,)),
                       pltpu.SemaphoreType.DMA((2,))],
        compiler_params=pltpu.CompilerParams(needs_layout_passes=False),
        name="peer_sc",
    )
    return run(idx2, table_i32, h2, gates)


def _final_kernel(n1, x1_ref, ya_ref, yb_ref, mod_ref, g_ref, o_ref):
    gi = pl.program_id(0) * pl.num_programs(1) + pl.program_id(1)
    y = jnp.where(gi < n1, ya_ref[...], yb_ref[...])
    gt2 = mod_ref[0, 5:6, :]
    o_ref[0] = x1_ref[0] + gt2 * (_rms(y) * g_ref[...])


def _final_call(x1, ya, yb, mod, g_post):
    B, S, _ = x1.shape
    tr = min(ROW_TILE, S)
    per = S // tr
    assert ya.shape[0] % tr == 0 and yb.shape[0] % tr == 0
    n1 = ya.shape[0] // tr
    row = pl.BlockSpec((1, tr, D_MODEL), lambda b, i: (b, i, 0))
    first = pl.BlockSpec((tr, D_MODEL), lambda b, i: (jnp.minimum(b * per + i, n1 - 1), 0))
    rest = pl.BlockSpec((tr, D_MODEL), lambda b, i: (jnp.maximum(b * per + i - n1, 0), 0))
    return pl.pallas_call(
        functools.partial(_final_kernel, n1),
        grid=(B, per),
        in_specs=[row, first, rest, pl.BlockSpec((1, N_MOD, D_MODEL), lambda b, i: (b, 0, 0)),
                  pl.BlockSpec((1, D_MODEL), lambda b, i: (0, 0))],
        out_specs=row,
        out_shape=jax.ShapeDtypeStruct((B, S, D_MODEL), F32),
        compiler_params=_params("arbitrary", "arbitrary"),
        name="final",
    )(x1, ya, yb, mod, g_post)


def _pad_heads_cols(w, nh, scale=1.0):
    k = w.shape[0]
    w = (w * scale).reshape(k, nh, HEAD_DIM)
    return jnp.pad(w, ((0, 0), (0, 0), (0, HEAD_PAD - HEAD_DIM))).reshape(k, nh * HEAD_PAD)


def _pad_heads_rows(w, nh):
    n = w.shape[1]
    w = w.reshape(nh, HEAD_DIM, n)
    return jnp.pad(w, ((0, 0), (0, HEAD_PAD - HEAD_DIM), (0, 0))).reshape(nh * HEAD_PAD, n)


def _layer(x, c8, w_ada, b_ada, g_pre_mix, g_post_mix, g_pre_ffn, g_post_ffn,
           w_in, b_fgate, swa_sinks, w_out, w_query, sub_keys, w_u, w_v):
    B, S, D = x.shape
    T = B * S
    scale = HEAD_DIM ** -0.5
    mod = _ada_call(c8, w_ada, b_ada)[:B].reshape(B, N_MOD, D)

    o = 0
    parts = []
    for nh, sc in ((FOX_HEADS, scale), (FOX_HEADS, 1.0), (FOX_HEADS, 1.0)):
        parts.append(_pad_heads_cols(w_in[:, o:o + nh * HEAD_DIM], nh, sc))
        o += nh * HEAD_DIM
    parts.append(jnp.pad(w_in[:, o:o + FOX_HEADS], ((0, 0), (0, HEAD_PAD - FOX_HEADS))))
    o += FOX_HEADS
    for nh, sc in ((SWA_HEADS, scale), (SWA_KV_HEADS, 1.0), (SWA_KV_HEADS, 1.0)):
        parts.append(_pad_heads_cols(w_in[:, o:o + nh * HEAD_DIM], nh, sc))
        o += nh * HEAD_DIM
    w_all = jnp.concatenate(parts, axis=1).astype(BF16)
    bf_pad = jnp.pad(b_fgate, (0, HEAD_PAD - FOX_HEADS)).reshape(1, HEAD_PAD)

    qt, kp, vt, sq, sk, sv = _inproj_call(x, mod, g_pre_mix.reshape(1, D), w_all, bf_pad)
    fo = _fox_call(qt, kp, vt)
    so = _swa_call(swa_sinks, sq, sk, sv)

    nf = FOX_HEADS * HEAD_DIM
    wof = _pad_heads_rows(w_out[:nf], FOX_HEADS).astype(BF16)
    wos = _pad_heads_rows(w_out[nf:], SWA_HEADS).astype(BF16)
    keys = sub_keys.reshape(2 * PEER_HEADS, N_KEYS, PEER_HALF).astype(BF16)
    x1, h2, idx, gates = _route_call(fo, so, x, mod, g_post_mix.reshape(1, D), g_pre_ffn.reshape(1, D),
                                     wof, wos, w_query.astype(BF16), keys)

    ub, vb = w_u.astype(BF16), w_v.astype(BF16)
    table = jnp.concatenate([ub.reshape(N_EXPERTS, 8, HEAD_PAD), vb.reshape(N_EXPERTS, 8, HEAD_PAD)], axis=1)
    h2f = h2.reshape(T, D)
    tile = min(ROW_TILE, S)
    ts = (int(T * SC_SHARE) // tile) * tile
    t1 = T - ts
    y_tc = _peer_call(idx.reshape(T * PEER_PICKS), table, h2f.reshape(T, 8, HEAD_PAD), gates, t1).reshape(t1, D)
    if not ts:
        return _final_call(x1, y_tc, y_tc, mod, g_post_ffn.reshape(1, D))
    def bf16_bits(w):
        b = lax.bitcast_convert_type(w, jnp.uint32)
        return (b + jnp.uint32(0x7FFF) + ((b >> 16) & jnp.uint32(1))) >> 16

    table_i32 = lax.bitcast_convert_type((bf16_bits(w_v) << 16) | bf16_bits(w_u), jnp.int32)
    y_sc = _peer_sc_call(idx, table_i32, h2f, gates, t1)
    return _final_call(x1, y_tc, y_sc, mod, g_post_ffn.reshape(1, D))


def kernel(x, c, w_ada, b_ada, g_pre_mix, g_post_mix, g_pre_ffn, g_post_ffn, w_in, b_fgate, swa_sinks, w_out,
           w_query, sub_keys, w_u, w_v):
    B = x.shape[0]
    c8 = jnp.pad(c, ((0, 8 - B), (0, 0)))
    for l in range(w_ada.shape[0]):
        x = _layer(x, c8, w_ada[l], b_ada[l], g_pre_mix[l], g_post_mix[l], g_pre_ffn[l], g_post_ffn[l],
                   w_in[l], b_fgate[l], swa_sinks[l], w_out[l], w_query[l], sub_keys[l], w_u[l], w_v[l])
    return x
```
